```python
import jax
import jax.numpy as jnp
from jax import lax
import numpy as np

D_MODEL = 1024
BATCH = 8
SEQ = 2048
DEPTH = 1

GRID_W = 64
CTX_LEN = 256
MLSTM_HEADS = 4
MLSTM_DH = 128
MLSTM_W = MLSTM_HEADS * MLSTM_DH
MLSTM_CHUNK = 64
QK_CONV = 3
N_GATES = 4 * MLSTM_HEADS
ATTN_HEADS = 8
KV_HEADS = 2
ATTN_DH = 64
ATTN_W = ATTN_HEADS * ATTN_DH
KV_W = KV_HEADS * ATTN_DH
GQA_GROUP = ATTN_HEADS // KV_HEADS
WINDOW = 128
ATTN_BLOCK = 128
ROPE_BASE = 10000.0
ROPE_AXIS_PAIRS = ATTN_DH // 4
MIX_W = MLSTM_W + ATTN_W
IN_SPLITS = (2 * MLSTM_W, 3 * MLSTM_W, 4 * MLSTM_W, 4 * MLSTM_W + N_GATES,
             4 * MLSTM_W + N_GATES + ATTN_W, 4 * MLSTM_W + N_GATES + ATTN_W + KV_W)
IN_COLS = 4 * MLSTM_W + N_GATES + ATTN_W + 2 * KV_W
D_FF = 2816
FFN_CONV = 3
EPS = 1e-6

kernel_name = 'hymba_mlstm_swa_convglu_dit'


def rmsnorm(x, g):
    xf = x.astype(jnp.float32)
    y = xf * lax.rsqrt(jnp.mean(xf * xf, axis=-1, keepdims=True) + EPS)
    return (y * g.astype(jnp.float32)).astype(x.dtype)


def modulate(h, shift, scale):
    return h * (1 + scale) + shift


def dwconv(x, w, b):
    k = w.shape[0]
    y = lax.conv_general_dilated(x, w[:, None, :], window_strides=(1,), padding=[(k // 2, k // 2)],
                                 dimension_numbers=('NWC', 'WIO', 'NWC'),
                                 feature_group_count=x.shape[-1])
    return y + b


def heads_first(t, n_heads):
    b, l, _ = t.shape
    return t.reshape(b, l, n_heads, -1).transpose(0, 2, 1, 3)


def axial_rope(n_tokens):
    rows = n_tokens // GRID_W
    r, col = jnp.meshgrid(jnp.arange(rows, dtype=jnp.float32), jnp.arange(GRID_W, dtype=jnp.float32),
                          indexing='ij')
    inv = ROPE_BASE ** (-jnp.arange(ROPE_AXIS_PAIRS, dtype=jnp.float32) / ROPE_AXIS_PAIRS)
    ang = jnp.stack([r.reshape(-1)[:, None] * inv, col.reshape(-1)[:, None] * inv], axis=1)
    return jnp.cos(ang), jnp.sin(ang)


def apply_rope(x, cos, sin):
    b, l, h, _ = x.shape
    xr = x.astype(jnp.float32).reshape(b, l, h, 2, 2, ROPE_AXIS_PAIRS)
    x0, x1 = xr[..., 0, :], xr[..., 1, :]
    cs, sn = cos[:, None], sin[:, None]
    y = jnp.stack([x0 * cs - x1 * sn, x0 * sn + x1 * cs], axis=-2)
    return y.reshape(b, l, h, ATTN_DH).astype(x.dtype)


def project_tokens(h, w_in, qk_conv_w, qk_conv_b, gate_b):
    b, l, _ = h.shape
    u = h @ w_in
    qk, v_m, o_m, gates, q_a, k_a, v_a = jnp.split(u, IN_SPLITS, axis=-1)
    q_m, k_m = jnp.split(jax.nn.silu(dwconv(qk, qk_conv_w, qk_conv_b)), 2, axis=-1)
    q_m = heads_first(q_m, MLSTM_HEADS).astype(jnp.float32)
    k_m = heads_first(k_m, MLSTM_HEADS).astype(jnp.float32) * (MLSTM_DH ** -0.5)
    v_m = heads_first(v_m, MLSTM_HEADS).astype(jnp.float32)
    g = (gates + gate_b).astype(jnp.float32).reshape(b, l, 4, MLSTM_HEADS).transpose(2, 0, 3, 1)
    gates = (g[0], jax.nn.log_sigmoid(g[1]), g[2], jax.nn.log_sigmoid(g[3]))
    q_a = q_a.reshape(b, l, ATTN_HEADS, ATTN_DH)
    k_a = k_a.reshape(b, l, KV_HEADS, ATTN_DH)
    v_a = v_a.reshape(b, l, KV_HEADS, ATTN_DH)
    return q_m, k_m, v_m, o_m, gates, q_a, k_a, v_a


def init_state(b):
    return (jnp.zeros((b, MLSTM_HEADS, MLSTM_DH, MLSTM_DH), jnp.float32),
            jnp.zeros((b, MLSTM_HEADS, MLSTM_DH), jnp.float32),
            jnp.zeros((b, MLSTM_HEADS), jnp.float32))


def mlstm_scan(q, k, v, log_i, log_f, init, return_h):
    b, h, l, dh = k.shape
    nc = l // MLSTM_CHUNK
    chunk = lambda t: t.reshape((b, h, nc, MLSTM_CHUNK) + t.shape[3:])
    k, v, li, lf = chunk(k), chunk(v), chunk(log_i), chunk(log_f)
    cum = jnp.cumsum(lf, axis=-1)
    cum_end = cum[..., -1]
    g = cum_end[..., None] - cum + li
    m_loc = jnp.max(g, axis=-1)
    w = jnp.exp(g - m_loc[..., None])
    c_loc = jnp.einsum('bhnsd,bhnse->bhnde', w[..., None] * k, v)
    n_loc = jnp.einsum('bhns,bhnsd->bhnd', w, k)

    def step(carry, inp):
        c_st, n_st, m_st = carry
        c_l, n_l, m_l, bt = inp
        m_new = jnp.maximum(bt + m_st, m_l)
        a = jnp.exp(bt + m_st - m_new)
        s = jnp.exp(m_l - m_new)
        return (a[..., None, None] * c_st + s[..., None, None] * c_l,
                a[..., None] * n_st + s[..., None] * n_l, m_new), (c_st, n_st, m_st)

    final, starts = lax.scan(step, init, tuple(jnp.moveaxis(t, 2, 0) for t in (c_loc, n_loc, m_loc, cum_end)))
    if not return_h:
        return None, final
    c0, n0, m0 = (jnp.moveaxis(t, 0, 2) for t in starts)
    q = chunk(q)
    a_log = cum + m0[..., None]
    d_log = cum[..., :, None] - cum[..., None, :] + li[..., None, :]
    order = jnp.tril(jnp.ones((MLSTM_CHUNK, MLSTM_CHUNK), dtype=bool))
    d_log = jnp.where(order, d_log, -jnp.inf)
    m_t = jnp.maximum(a_log, jnp.max(d_log, axis=-1))
    dec = jnp.exp(d_log - m_t[..., None])
    inter = jnp.exp(a_log - m_t)
    s = jnp.einsum('bhntd,bhnsd->bhnts', q, k) * dec
    num = jnp.einsum('bhnts,bhnse->bhnte', s, v) + inter[..., None] * jnp.einsum('bhntd,bhnde->bhnte', q, c0)
    den = jnp.sum(s, axis=-1) + inter * jnp.einsum('bhntd,bhnd->bhnt', q, n0)
    hid = num / jnp.maximum(jnp.abs(den), jnp.exp(-m_t))[..., None]
    return hid.reshape(b, h, l, dh), final


def mlstm_bidirectional(q, k, v, gates, init_f, init_b, return_h):
    li_f, lf_f, li_b, lf_b = gates
    flip = lambda t: jnp.flip(t, axis=2)
    h_f, st_f = mlstm_scan(q, k, v, li_f, lf_f, init_f, return_h)
    h_b, st_b = mlstm_scan(flip(q), flip(k), flip(v), flip(li_b), flip(lf_b), init_b, return_h)
    hid = h_f + flip(h_b) if return_h else None
    return hid, st_f, st_b


def mlstm_merge(h, o, gain):
    b, _, l, _ = h.shape
    hn = h * lax.rsqrt(jnp.mean(h * h, axis=-1, keepdims=True) + EPS)
    hn = hn.transpose(0, 2, 1, 3).reshape(b, l, MLSTM_W) * gain.astype(jnp.float32)
    return (hn * jax.nn.sigmoid(o.astype(jnp.float32))).astype(o.dtype)


def window_attention(q, k, v, k_ctx, v_ctx, sink):
    b, l = q.shape[:2]
    nb = l // ATTN_BLOCK
    qb = q.reshape(b, nb, ATTN_BLOCK, KV_HEADS, GQA_GROUP, ATTN_DH)

    def band(t):
        tb = jnp.pad(t.reshape(b, nb, ATTN_BLOCK, KV_HEADS, ATTN_DH), ((0, 0), (1, 1), (0, 0), (0, 0), (0, 0)))
        return jnp.concatenate([tb[:, :-2], tb[:, 1:-1], tb[:, 2:]], axis=2)

    kb, vb = band(k), band(v)
    n_band = 3 * ATTN_BLOCK
    qpos = jnp.arange(l).reshape(nb, ATTN_BLOCK)
    kpos = qpos[:, :1] - ATTN_BLOCK + jnp.arange(n_band)
    valid = ((jnp.abs(kpos[:, None, :] - qpos[:, :, None]) <= WINDOW)
             & (kpos >= 0)[:, None, :] & (kpos < l)[:, None, :])
    s_band = jnp.einsum('bnqhgd,bnkhd->bnhgqk', qb, kb).astype(jnp.float32)
    s_band = jnp.where(valid[None, :, None, None], s_band, -jnp.inf)
    s_ctx = jnp.einsum('bnqhgd,bchd->bnhgqc', qb, k_ctx).astype(jnp.float32)
    s_sink = jnp.broadcast_to(sink.astype(jnp.float32).reshape(KV_HEADS, GQA_GROUP, 1, 1),
                              s_band.shape[:-1] + (1,))
    p = jax.nn.softmax(jnp.concatenate([s_band, s_ctx, s_sink], axis=-1), axis=-1).astype(v.dtype)
    out = (jnp.einsum('bnhgqk,bnkhd->bnqhgd', p[..., :n_band], vb)
           + jnp.einsum('bnhgqc,bchd->bnqhgd', p[..., n_band:n_band + k_ctx.shape[1]], v_ctx))
    return out.reshape(b, l, ATTN_W)


def context_attention(q, k, v, sink):
    b, n = q.shape[:2]
    qg = q.reshape(b, n, KV_HEADS, GQA_GROUP, ATTN_DH)
    s = jnp.einsum('bqhgd,bkhd->bhgqk', qg, k).astype(jnp.float32)
    s_sink = jnp.broadcast_to(sink.astype(jnp.float32).reshape(KV_HEADS, GQA_GROUP, 1, 1), s.shape[:-1] + (1,))
    p = jax.nn.softmax(jnp.concatenate([s, s_sink], axis=-1), axis=-1)[..., :-1].astype(v.dtype)
    return jnp.einsum('bhgqk,bkhd->bqhgd', p, v).reshape(b, n, ATTN_W)


def conv_glu(h, w_up, conv_w, conv_b, w_down):
    a, val = jnp.split(h @ w_up, 2, axis=-1)
    a = jax.nn.gelu(dwconv(a, conv_w, conv_b))
    return (a * val) @ w_down


def setup_inputs(seed: int = 0) -> dict:
    key = jax.random.key(seed)
    ks = jax.random.split(key, 20)
    nrm = lambda k, shape, s: jax.random.normal(k, shape, jnp.float32) * s
    f_bias = jnp.linspace(3.0, 6.0, MLSTM_HEADS, dtype=jnp.float32)
    zero_h = jnp.zeros((MLSTM_HEADS,), jnp.float32)
    gate_base = jnp.stack([zero_h, f_bias, zero_h, f_bias]).reshape(-1)
    return {
        'x': nrm(ks[0], (BATCH, SEQ, D_MODEL), 1.0),
        'c': nrm(ks[1], (BATCH, D_MODEL), 1.0),
        'ctx': nrm(ks[2], (BATCH, CTX_LEN, D_MODEL), 1.0),
        'c_ctx': nrm(ks[3], (D_MODEL,), 1.0),
        'w_ada': nrm(ks[4], (DEPTH, D_MODEL, 6 * D_MODEL), 0.5 * D_MODEL ** -0.5),
        'b_ada': nrm(ks[5], (DEPTH, 6 * D_MODEL), 0.02),
        'norm_mix': 1.0 + nrm(ks[6], (DEPTH, D_MODEL), 0.02),
        'norm_ffn': 1.0 + nrm(ks[7], (DEPTH, D_MODEL), 0.02),
        'w_in': nrm(ks[8], (DEPTH, D_MODEL, IN_COLS), D_MODEL ** -0.5),
        'gate_b': gate_base[None, :] + nrm(ks[9], (DEPTH, N_GATES), 0.1),
        'qk_conv_w': nrm(ks[10], (DEPTH, QK_CONV, 2 * MLSTM_W), QK_CONV ** -0.5),
        'qk_conv_b': nrm(ks[11], (DEPTH, 2 * MLSTM_W), 0.02),
        'mlstm_norm': 1.0 + nrm(ks[12], (DEPTH, MLSTM_W), 0.02),
        'attn_sink': nrm(ks[13], (DEPTH, ATTN_HEADS), 0.5),
        'w_out': nrm(ks[14], (DEPTH, MIX_W, D_MODEL), MIX_W ** -0.5),
        'w_up': nrm(ks[15], (DEPTH, D_MODEL, 2 * D_FF), D_MODEL ** -0.5),
        'ffn_conv_w': nrm(ks[16], (DEPTH, FFN_CONV, D_FF), FFN_CONV ** -0.5),
        'ffn_conv_b': nrm(ks[17], (DEPTH, D_FF), 0.02),
        'w_down': nrm(ks[18], (DEPTH, D_FF, D_MODEL), D_FF ** -0.5),
        'final_norm': 1.0 + nrm(ks[19], (D_MODEL,), 0.02),
    }


def reference(x, c, ctx, c_ctx, w_ada, b_ada, norm_mix, norm_ffn, w_in, gate_b, qk_conv_w, qk_conv_b,
              mlstm_norm, attn_sink, w_out, w_up, ffn_conv_w, ffn_conv_b, w_down, final_norm):
    cos, sin = axial_rope(x.shape[1])
    attn_scale = ATTN_DH ** -0.5
    for l in range(DEPTH):
        last = l == DEPTH - 1
        sh_a, sc_a, g_a, sh_f, sc_f, g_f = jnp.split(
            (jax.nn.silu(c) @ w_ada[l] + b_ada[l])[:, None, :], 6, axis=-1)
        csh_a, csc_a, cg_a, csh_f, csc_f, cg_f = jnp.split(
            jax.nn.silu(c_ctx) @ w_ada[l] + b_ada[l], 6, axis=-1)

        hc = modulate(rmsnorm(ctx, norm_mix[l]), csh_a, csc_a)
        qm_c, km_c, vm_c, om_c, gates_c, qa_c, ka_c, va_c = project_tokens(
            hc, w_in[l], qk_conv_w[l], qk_conv_b[l], gate_b[l])
        zero = init_state(ctx.shape[0])
        hm_c, st_f, st_b = mlstm_bidirectional(qm_c, km_c, vm_c, gates_c, zero, zero, not last)

        hx = modulate(rmsnorm(x, norm_mix[l]), sh_a, sc_a)
        qm, km, vm, om, gates_x, qa, ka, va = project_tokens(hx, w_in[l], qk_conv_w[l], qk_conv_b[l], gate_b[l])
        hm, _, _ = mlstm_bidirectional(qm, km, vm, gates_x, st_f, st_b, True)
        y_m = mlstm_merge(hm, om, mlstm_norm[l])
        y_a = window_attention(apply_rope(qa, cos, sin) * attn_scale, apply_rope(ka, cos, sin), va,
                               ka_c, va_c, attn_sink[l])
        x = x + g_a * (jnp.concatenate([y_m, y_a], axis=-1) @ w_out[l])
        x = x + g_f * conv_glu(modulate(rmsnorm(x, norm_ffn[l]), sh_f, sc_f),
                               w_up[l], ffn_conv_w[l], ffn_conv_b[l], w_down[l])

        if not last:
            y_mc = mlstm_merge(hm_c, om_c, mlstm_norm[l])
            y_ac = context_attention(qa_c * attn_scale, ka_c, va_c, attn_sink[l])
            ctx = ctx + cg_a * (jnp.concatenate([y_mc, y_ac], axis=-1) @ w_out[l])
            ctx = ctx + cg_f * conv_glu(modulate(rmsnorm(ctx, norm_ffn[l]), csh_f, csc_f),
                                        w_up[l], ffn_conv_w[l], ffn_conv_b[l], w_down[l])
    return rmsnorm(x, final_norm)
```

```python
import functools

import jax
import jax.numpy as jnp
from jax import lax
from jax.experimental import pallas as pl
from jax.experimental.pallas import tpu as pltpu

F32 = jnp.float32
BF16 = jnp.bfloat16

D_MODEL = 1024
GRID_W = 64
MLSTM_HEADS = 4
MLSTM_DH = 128
MLSTM_W = MLSTM_HEADS * MLSTM_DH
N_GATES = 4 * MLSTM_HEADS
ATTN_HEADS = 8
KV_HEADS = 2
ATTN_DH = 64
ATTN_W = ATTN_HEADS * ATTN_DH
KV_W = KV_HEADS * ATTN_DH
GQA_GROUP = ATTN_HEADS // KV_HEADS
WINDOW = 128
ROPE_BASE = 10000.0
ROPE_AXIS_PAIRS = ATTN_DH // 4
D_FF = 2816
EPS = 1e-6

V7X_LANES = 128
V7X_SUBLANES = 8
V7X_VMEM_BYTES = 64 * 1024 * 1024

ROW_TILE = 512
ADA_COL_TILE = 1536
MCHUNK = 128
MAX_CHUNKS = 32
ATTN_BLOCK = 128
BAND = 3 * ATTN_BLOCK
FF_CHUNK = 256
HALO = 2 * V7X_SUBLANES


def _vmem_limit(block_bytes):
    return int(min(V7X_VMEM_BYTES * 7 // 8, 2 * block_bytes + 16 * 1024 * 1024))


def _params(block_bytes, n_axes):
    return pltpu.CompilerParams(dimension_semantics=("arbitrary",) * n_axes,
                                vmem_limit_bytes=_vmem_limit(block_bytes))


def _nbytes(shape, dtype):
    n = 1
    for s in shape:
        n *= s
    return n * jnp.dtype(dtype).itemsize


def _ada_body(c_ref, w_ref, b_ref, o_ref):
    c = c_ref[...]
    s = (c * jax.nn.sigmoid(c)).astype(BF16)
    o_ref[...] = jnp.dot(s, w_ref[...].astype(BF16), preferred_element_type=F32) + b_ref[...]


def _ada(cvec, w_ada, b_ada):
    rows, d = cvec.shape
    n = w_ada.shape[1]
    blk = _nbytes((d, ADA_COL_TILE), F32) + _nbytes((rows, d), F32) + 2 * _nbytes((rows, ADA_COL_TILE), F32)
    return pl.pallas_call(
        _ada_body,
        grid=(n // ADA_COL_TILE,),
        in_specs=[pl.BlockSpec((rows, d), lambda j: (0, 0)),
                  pl.BlockSpec((d, ADA_COL_TILE), lambda j: (0, j)),
                  pl.BlockSpec((1, ADA_COL_TILE), lambda j: (0, j))],
        out_specs=pl.BlockSpec((rows, ADA_COL_TILE), lambda j: (0, j)),
        out_shape=jax.ShapeDtypeStruct((rows, n), F32),
        compiler_params=_params(blk, 1),
        name="ada",
    )(cvec, w_ada, b_ada)


def _norm_modulate(x, gain, shift, scale):
    ms = jnp.mean(x * x, axis=-1, keepdims=True)
    return (x * lax.rsqrt(ms + EPS) * gain) * (1.0 + scale) + shift


def _inproj_body(x_ref, gain_ref, sh_ref, sc_ref, cos_ref, sina_ref, sinb_ref, gb_ref, *refs, kinds):
    n = len(kinds)
    w_refs, o_refs = refs[:n], refs[n:]
    hb = _norm_modulate(x_ref[0], gain_ref[...], sh_ref[0], sc_ref[0]).astype(BF16)
    for kind, w_ref, o_ref in zip(kinds, w_refs, o_refs):
        if kind == "gates":
            u = jnp.dot(hb, w_ref[...], preferred_element_type=F32)
            o_ref[0] = u[:, :N_GATES] + gb_ref[...]
        elif kind == "plain":
            o_ref[0] = jnp.dot(hb, w_ref[...], preferred_element_type=F32).astype(BF16)
        else:
            scale = ATTN_DH ** -0.5 if kind == "rope_q" else 1.0
            width = w_ref.shape[1]
            for j in range(width // V7X_LANES):
                cols = pl.ds(j * V7X_LANES, V7X_LANES)
                u = jnp.dot(hb, w_ref[:, cols], preferred_element_type=F32)
                y = (u * cos_ref[...]
                     + pltpu.roll(u, V7X_LANES - ROPE_AXIS_PAIRS, 1) * sina_ref[...]
                     + pltpu.roll(u, ROPE_AXIS_PAIRS, 1) * sinb_ref[...])
                o_ref[0, :, cols] = (y * scale).astype(BF16)


def _inproj(x, gain, mod3, shift_idx, scale_idx, per_batch_mod, rope, gate_b, weights, kinds, name):
    b, n, d = x.shape
    tm = min(ROW_TILE, n)
    cos, sina, sinb = rope
    mod_spec = lambda piece: pl.BlockSpec(
        (1, 1, d), (lambda bi, i: (bi, 0, piece)) if per_batch_mod else (lambda bi, i: (0, 0, piece)))
    in_specs = [pl.BlockSpec((1, tm, d), lambda bi, i: (bi, i, 0)),
                pl.BlockSpec((1, d), lambda bi, i: (0, 0)),
                mod_spec(shift_idx), mod_spec(scale_idx),
                pl.BlockSpec((tm, V7X_LANES), lambda bi, i: (i, 0)),
                pl.BlockSpec((tm, V7X_LANES), lambda bi, i: (i, 0)),
                pl.BlockSpec((tm, V7X_LANES), lambda bi, i: (i, 0)),
                pl.BlockSpec((1, N_GATES), lambda bi, i: (0, 0))]
    out_specs, out_shapes = [], []
    blk = _nbytes((tm, d), F32) + 3 * _nbytes((tm, V7X_LANES), F32)
    for kind, w in zip(kinds, weights):
        width = N_GATES if kind == "gates" else w.shape[1]
        dt = F32 if kind == "gates" else BF16
        in_specs.append(pl.BlockSpec(w.shape, lambda bi, i: (0, 0)))
        out_specs.append(pl.BlockSpec((1, tm, width), lambda bi, i: (bi, i, 0)))
        out_shapes.append(jax.ShapeDtypeStruct((b, n, width), dt))
        blk += _nbytes(w.shape, BF16) + _nbytes((tm, width), dt) + _nbytes((tm, w.shape[1]), F32)
    return pl.pallas_call(
        functools.partial(_inproj_body, kinds=tuple(kinds)),
        grid=(b, n // tm),
        in_specs=in_specs,
        out_specs=out_specs,
        out_shape=out_shapes,
        compiler_params=_params(blk, 2),
        name=name,
    )(x, gain, mod3, mod3, cos, sina, sinb, gate_b, *weights)


def _conv_silu(x, w, b):
    n = x.shape[0]
    row = lax.broadcasted_iota(jnp.int32, x.shape, 0)
    prev = jnp.where(row == 0, 0.0, pltpu.roll(x, 1, 0))
    nxt = jnp.where(row == n - 1, 0.0, pltpu.roll(x, n - 1, 0))
    y = w[0:1, :] * prev + w[1:2, :] * x + w[2:3, :] * nxt + b
    return y * jax.nn.sigmoid(y)


def _log_sigmoid(x):
    return jnp.minimum(x, 0.0) - jnp.log1p(jnp.exp(-jnp.abs(x)))


def _mlstm_body(q_ref, k_ref, v_ref, o_ref, kc_ref, vc_ref, gt_ref, wq_ref, bq_ref, wk_ref, bk_ref, gain_ref,
                y_ref, qs, ks, kcs, rowt, colt, c0, n0, m0, *, n_ctx_chunks, n_lat_chunks):
    t = MCHUNK
    n_chunks = n_ctx_chunks + n_lat_chunks
    kscale = MLSTM_DH ** -0.5

    qs[...] = _conv_silu(q_ref[0].astype(F32), wq_ref[...], bq_ref[...]).astype(BF16)
    ks[...] = (_conv_silu(k_ref[0].astype(F32), wk_ref[...], bk_ref[...]) * kscale).astype(BF16)
    kcs[...] = (_conv_silu(kc_ref[0].astype(F32), wk_ref[...], bk_ref[...]) * kscale).astype(BF16)

    tbl = gt_ref[0, 0]
    rid = lax.broadcasted_iota(jnp.int32, tbl.shape, 0) // MAX_CHUNKS
    lane = lax.broadcasted_iota(jnp.int32, tbl.shape, 1)
    lf = _log_sigmoid(tbl)
    pre = jnp.where(rid == 1, lf, 0.0)
    suf = jnp.where(rid == 3, lf, 0.0)
    s = 1
    while s < t:
        pre = pre + jnp.where(lane >= s, pltpu.roll(pre, s, 1), 0.0)
        suf = suf + jnp.where(lane < t - s, pltpu.roll(suf, t - s, 1), 0.0)
        s *= 2
    table = jnp.where(rid == 1, pre, jnp.where(rid == 3, suf, tbl))
    rowt[...] = table
    colt[...] = table.T

    def kv_chunk(c):
        if c < n_ctx_chunks:
            rows = pl.ds(c * t, t)
            return kcs[rows, :], vc_ref[0, rows, :]
        rows = pl.ds((c - n_ctx_chunks) * t, t)
        return ks[rows, :], v_ref[0, rows, :]

    def slot(d, forget, c):
        return (2 * d + forget) * MAX_CHUNKS + c

    for d in (0, 1):
        order = list(range(n_chunks)) if d == 0 else (list(range(n_ctx_chunks - 1, -1, -1))
                                                      + list(range(n_chunks - 1, n_ctx_chunks - 1, -1)))
        c_st = jnp.zeros((MLSTM_DH, MLSTM_DH), F32)
        n_st = jnp.zeros((1, MLSTM_DH), F32)
        m_st = jnp.zeros((1, 1), F32)
        for pos, c in enumerate(order):
            if c >= n_ctx_chunks:
                j = c - n_ctx_chunks
                c0[d, j] = c_st.astype(BF16)
                n0[d, j] = n_st
                m0[d, j] = jnp.broadcast_to(m_st, (1, V7X_LANES))
            if pos == n_chunks - 1:
                break
            li_col = colt[:, pl.ds(slot(d, 0, c), 1)]
            cum_col = colt[:, pl.ds(slot(d, 1, c), 1)]
            end_lane = t - 1 if d == 0 else 0
            total = rowt[pl.ds(slot(d, 1, c), 1), pl.ds(end_lane, 1)]
            g = total - cum_col + li_col
            m_loc = jnp.max(g, axis=0, keepdims=True)
            w = jnp.exp(g - m_loc)
            k, v = kv_chunk(c)
            kw = k.astype(F32) * w
            c_loc = lax.dot_general(kw.astype(BF16), v, (((0,), (0,)), ((), ())), preferred_element_type=F32)
            n_loc = jnp.sum(kw, axis=0, keepdims=True)
            m_new = jnp.maximum(total + m_st, m_loc)
            a = jnp.exp(total + m_st - m_new)
            sc = jnp.exp(m_loc - m_new)
            c_st = a * c_st + sc * c_loc
            n_st = a * n_st + sc * n_loc
            m_st = m_new

    ti = lax.broadcasted_iota(jnp.int32, (t, t), 0)
    si = lax.broadcasted_iota(jnp.int32, (t, t), 1)
    gain = gain_ref[...]
    for j in range(n_lat_chunks):
        c = j + n_ctx_chunks
        rows = pl.ds(j * t, t)
        q = qs[rows, :]
        k = ks[rows, :]
        v = v_ref[0, rows, :]
        qf = q.astype(F32)
        s_qk = lax.dot_general(q, k, (((1,), (1,)), ((), ())), preferred_element_type=F32)
        p = None
        hid = None
        for d in (0, 1):
            li_row = rowt[pl.ds(slot(d, 0, c), 1), :]
            cum_row = rowt[pl.ds(slot(d, 1, c), 1), :]
            cum_col = colt[:, pl.ds(slot(d, 1, c), 1)]
            order_ok = (si <= ti) if d == 0 else (si >= ti)
            d_log = jnp.where(order_ok, cum_col - cum_row + li_row, -jnp.inf)
            a_log = cum_col + m0[d, j][:, 0:1]
            m_t = jnp.maximum(a_log, jnp.max(d_log, axis=1, keepdims=True))
            dec = jnp.exp(d_log - m_t)
            inter = jnp.exp(a_log - m_t)
            s_d = s_qk * dec
            qn = jnp.sum(qf * n0[d, j], axis=1, keepdims=True)
            den = jnp.sum(s_d, axis=1, keepdims=True) + inter * qn
            r = 1.0 / jnp.maximum(jnp.abs(den), jnp.exp(-m_t))
            p = s_d * r if p is None else p + s_d * r
            qd = (qf * (inter * r)).astype(BF16)
            h_d = jnp.dot(qd, c0[d, j], preferred_element_type=F32)
            hid = h_d if hid is None else hid + h_d
        hid = hid + jnp.dot(p.astype(BF16), v, preferred_element_type=F32)
        hn = hid * lax.rsqrt(jnp.mean(hid * hid, axis=-1, keepdims=True) + EPS) * gain
        y_ref[0, rows, :] = (hn * jax.nn.sigmoid(o_ref[0, rows, :].astype(F32))).astype(BF16)


def _mlstm(qk, v, o, kc, vc, gate_tbl, conv_w, conv_b, gain):
    b, l, _ = v.shape
    n_ctx = kc.shape[1]
    dh = MLSTM_DH
    nh = MLSTM_HEADS
    n_ctx_chunks, n_lat_chunks = n_ctx // MCHUNK, l // MCHUNK
    seq = lambda col0: pl.BlockSpec((1, l, dh), lambda bi, h: (bi, 0, col0 + h))
    ctx = pl.BlockSpec((1, n_ctx, dh), lambda bi, h: (bi, 0, h))
    blk = (5 * _nbytes((l, dh), BF16) + 2 * _nbytes((n_ctx, dh), BF16) + _nbytes((V7X_LANES, V7X_LANES), F32)
           + 4 * _nbytes((l, dh), F32))
    return pl.pallas_call(
        functools.partial(_mlstm_body, n_ctx_chunks=n_ctx_chunks, n_lat_chunks=n_lat_chunks),
        grid=(b, nh),
        in_specs=[seq(0), seq(nh), seq(0), seq(0), ctx, ctx,
                  pl.BlockSpec((1, 1, 4 * MAX_CHUNKS, MCHUNK), lambda bi, h: (bi, h, 0, 0)),
                  pl.BlockSpec((3, dh), lambda bi, h: (0, h)),
                  pl.BlockSpec((1, dh), lambda bi, h: (0, h)),
                  pl.BlockSpec((3, dh), lambda bi, h: (0, nh + h)),
                  pl.BlockSpec((1, dh), lambda bi, h: (0, nh + h)),
                  pl.BlockSpec((1, dh), lambda bi, h: (0, h))],
        out_specs=pl.BlockSpec((1, l, dh), lambda bi, h: (bi, 0, h)),
        out_shape=jax.ShapeDtypeStruct((b, l, MLSTM_W), BF16),
        scratch_shapes=[pltpu.VMEM((l, dh), BF16), pltpu.VMEM((l, dh), BF16), pltpu.VMEM((n_ctx, dh), BF16),
                        pltpu.VMEM((4 * MAX_CHUNKS, MCHUNK), F32), pltpu.VMEM((MCHUNK, 4 * MAX_CHUNKS), F32),
                        pltpu.VMEM((2, n_lat_chunks, dh, dh), BF16),
                        pltpu.VMEM((2, n_lat_chunks, 1, dh), F32),
                        pltpu.VMEM((2, n_lat_chunks, 1, V7X_LANES), F32)],
        compiler_params=_params(blk, 2),
        name="mlstm",
    )(qk, qk, v, o, kc, vc, gate_tbl, conv_w, conv_b, conv_w, conv_b, gain)


def _attn_body(sink_ref, q_ref, k_ref, v_ref, kc_ref, vc_ref, o_ref, *, seq_len):
    i = pl.program_id(1)
    blk = ATTN_BLOCK
    start = pl.multiple_of(jnp.clip((i - 1) * blk, 0, seq_len - BAND), blk)
    rows = GQA_GROUP * blk
    qpos = i * blk + (lax.broadcasted_iota(jnp.int32, (rows, BAND), 0) & (blk - 1))
    kpos = start + lax.broadcasted_iota(jnp.int32, (rows, BAND), 1)
    valid = jnp.abs(kpos - qpos) <= WINDOW
    nt = (((1,), (1,)), ((), ()))
    for g in range(KV_HEADS):
        kv_cols = pl.ds(g * ATTN_DH, ATTN_DH)
        kb = k_ref[0, pl.ds(start, BAND), kv_cols]
        vb = v_ref[0, pl.ds(start, BAND), kv_cols]
        kc = kc_ref[0, :, kv_cols]
        vc = vc_ref[0, :, kv_cols]
        heads = [g * GQA_GROUP + j for j in range(GQA_GROUP)]
        q = jnp.concatenate([q_ref[0, :, pl.ds(h * ATTN_DH, ATTN_DH)] for h in heads], axis=0)
        s_band = jnp.where(valid, lax.dot_general(q, kb, nt, preferred_element_type=F32), -jnp.inf)
        s_ctx = lax.dot_general(q, kc, nt, preferred_element_type=F32)
        sink = jnp.concatenate([jnp.full((blk, 1), sink_ref[h], F32) for h in heads], axis=0)
        m = jnp.maximum(jnp.maximum(jnp.max(s_band, axis=1, keepdims=True),
                                    jnp.max(s_ctx, axis=1, keepdims=True)), sink)
        p_band = jnp.exp(s_band - m)
        p_ctx = jnp.exp(s_ctx - m)
        den = (jnp.sum(p_band, axis=1, keepdims=True) + jnp.sum(p_ctx, axis=1, keepdims=True)
               + jnp.exp(sink - m))
        out = (jnp.dot(p_band.astype(BF16), vb, preferred_element_type=F32)
               + jnp.dot(p_ctx.astype(BF16), vc, preferred_element_type=F32)) / den
        for j, h in enumerate(heads):
            o_ref[0, :, pl.ds(h * ATTN_DH, ATTN_DH)] = out[j * blk:(j + 1) * blk].astype(BF16)


def _attn(sink, q, k, v, kc, vc):
    b, l, _ = q.shape
    n_ctx = kc.shape[1]
    blk = (2 * _nbytes((ATTN_BLOCK, ATTN_W), BF16) + 2 * _nbytes((l, KV_W), BF16) + 2 * _nbytes((n_ctx, KV_W), BF16)
           + 6 * _nbytes((GQA_GROUP * ATTN_BLOCK, BAND + n_ctx), F32))
    return pl.pallas_call(
        functools.partial(_attn_body, seq_len=l),
        grid=(b, l // ATTN_BLOCK),
        in_specs=[pl.BlockSpec(memory_space=pltpu.SMEM),
                  pl.BlockSpec((1, ATTN_BLOCK, ATTN_W), lambda bi, i: (bi, i, 0)),
                  pl.BlockSpec((1, l, KV_W), lambda bi, i: (bi, 0, 0)),
                  pl.BlockSpec((1, l, KV_W), lambda bi, i: (bi, 0, 0)),
                  pl.BlockSpec((1, n_ctx, KV_W), lambda bi, i: (bi, 0, 0)),
                  pl.BlockSpec((1, n_ctx, KV_W), lambda bi, i: (bi, 0, 0))],
        out_specs=pl.BlockSpec((1, ATTN_BLOCK, ATTN_W), lambda bi, i: (bi, i, 0)),
        out_shape=jax.ShapeDtypeStruct((b, l, ATTN_W), BF16),
        compiler_params=_params(blk, 2),
        name="attn",
    )(sink, q, k, v, kc, vc)


def _outproj_body(ym_ref, ya_ref, x_ref, wm_ref, wa_ref, ga_ref, gain_ref, sh_ref, sc_ref, x1_ref, hf_ref):
    acc = (jnp.dot(ym_ref[0], wm_ref[...], preferred_element_type=F32)
           + jnp.dot(ya_ref[0], wa_ref[...], preferred_element_type=F32))
    x1 = x_ref[0] + ga_ref[0] * acc
    x1_ref[0] = x1
    hf_ref[0] = _norm_modulate(x1, gain_ref[...], sh_ref[0], sc_ref[0]).astype(BF16)


def _outproj(ym, ya, x, wm, wa, mod3, gain):
    b, l, d = x.shape
    tm = ROW_TILE
    mod_spec = lambda piece: pl.BlockSpec((1, 1, d), lambda bi, i: (bi, 0, piece))
    blk = (2 * _nbytes((tm, MLSTM_W), BF16) + 3 * _nbytes((tm, d), F32) + _nbytes((tm, d), BF16)
           + 2 * _nbytes(wm.shape, BF16))
    return pl.pallas_call(
        _outproj_body,
        grid=(b, l // tm),
        in_specs=[pl.BlockSpec((1, tm, MLSTM_W), lambda bi, i: (bi, i, 0)),
                  pl.BlockSpec((1, tm, ATTN_W), lambda bi, i: (bi, i, 0)),
                  pl.BlockSpec((1, tm, d), lambda bi, i: (bi, i, 0)),
                  pl.BlockSpec(wm.shape, lambda bi, i: (0, 0)),
                  pl.BlockSpec(wa.shape, lambda bi, i: (0, 0)),
                  mod_spec(2),
                  pl.BlockSpec((1, d), lambda bi, i: (0, 0)),
                  mod_spec(3), mod_spec(4)],
        out_specs=[pl.BlockSpec((1, tm, d), lambda bi, i: (bi, i, 0)),
                   pl.BlockSpec((1, tm, d), lambda bi, i: (bi, i, 0))],
        out_shape=[jax.ShapeDtypeStruct((b, l, d), F32), jax.ShapeDtypeStruct((b, l, d), BF16)],
        compiler_params=_params(blk, 2),
        name="outproj",
    )(ym, ya, x, wm, wa, mod3, gain, mod3, mod3)


def _gelu_tanh(x):
    return 0.5 * x * (1.0 + jnp.tanh(0.7978845608028654 * (x + 0.044715 * (x * x * x))))


def _ffn_body(h_ref, hp_ref, hn_ref, x1_ref, gf_ref, wup_ref, cw_ref, cb_ref, wdn_ref, fn_ref, o_ref, hs, acc):
    i = pl.program_id(1)
    tm = h_ref.shape[1]
    ext = tm + 2 * HALO
    hs[pl.ds(0, HALO), :] = jnp.where(i > 0, hp_ref[0], jnp.zeros_like(hp_ref[0]))
    hs[pl.ds(HALO, tm), :] = h_ref[0]
    hs[pl.ds(HALO + tm, HALO), :] = jnp.where(i < pl.num_programs(1) - 1, hn_ref[0], jnp.zeros_like(hn_ref[0]))
    for c in range(D_FF // FF_CHUNK):
        cols = pl.ds(c * FF_CHUNK, FF_CHUNK)
        a = jnp.dot(hs[...], wup_ref[:, cols], preferred_element_type=F32)
        val = jnp.dot(h_ref[0], wup_ref[:, pl.ds(D_FF + c * FF_CHUNK, FF_CHUNK)], preferred_element_type=F32)
        a_prev = pltpu.roll(a, 1, 0)[HALO:HALO + tm]
        a_next = pltpu.roll(a, ext - 1, 0)[HALO:HALO + tm]
        conv = (cw_ref[0:1, cols] * a_prev + cw_ref[1:2, cols] * a[HALO:HALO + tm]
                + cw_ref[2:3, cols] * a_next + cb_ref[:, cols])
        gated = (_gelu_tanh(conv) * val).astype(BF16)
        part = jnp.dot(gated, wdn_ref[cols, :], preferred_element_type=F32)
        if c == 0:
            acc[...] = part
        else:
            acc[...] += part
    x2 = x1_ref[0] + gf_ref[0] * acc[...]
    ms = jnp.mean(x2 * x2, axis=-1, keepdims=True)
    o_ref[0] = x2 * lax.rsqrt(ms + EPS) * fn_ref[...]


def _ffn(hf, x1, mod3, w_up, conv_w, conv_b, w_down, final_norm):
    b, l, d = x1.shape
    tm = ROW_TILE
    per = tm // HALO
    last = l // HALO - 1
    blk = (_nbytes((tm, d), BF16) + 2 * _nbytes((tm, d), F32) + _nbytes(w_up.shape, BF16)
           + _nbytes(w_down.shape, BF16) + 4 * _nbytes((tm + 2 * HALO, FF_CHUNK), F32))
    return pl.pallas_call(
        _ffn_body,
        grid=(b, l // tm),
        in_specs=[pl.BlockSpec((1, tm, d), lambda bi, i: (bi, i, 0)),
                  pl.BlockSpec((1, HALO, d), lambda bi, i: (bi, jnp.maximum(i * per - 1, 0), 0)),
                  pl.BlockSpec((1, HALO, d), lambda bi, i: (bi, jnp.minimum((i + 1) * per, last), 0)),
                  pl.BlockSpec((1, tm, d), lambda bi, i: (bi, i, 0)),
                  pl.BlockSpec((1, 1, d), lambda bi, i: (bi, 0, 5)),
                  pl.BlockSpec(w_up.shape, lambda bi, i: (0, 0)),
                  pl.BlockSpec(conv_w.shape, lambda bi, i: (0, 0)),
                  pl.BlockSpec(conv_b.shape, lambda bi, i: (0, 0)),
                  pl.BlockSpec(w_down.shape, lambda bi, i: (0, 0)),
                  pl.BlockSpec((1, d), lambda bi, i: (0, 0))],
        out_specs=pl.BlockSpec((1, tm, d), lambda bi, i: (bi, i, 0)),
        out_shape=jax.ShapeDtypeStruct((b, l, d), F32),
        scratch_shapes=[pltpu.VMEM((tm + 2 * HALO, d), BF16), pltpu.VMEM((tm, d), F32)],
        compiler_params=_params(blk, 2),
        name="ffn",
    )(hf, hf, hf, x1, mod3, w_up, conv_w, conv_b, w_down, final_norm)


def _rope_tables(n_tokens):
    pos = jnp.arange(n_tokens)
    r = (pos // GRID_W).astype(F32)
    c = (pos % GRID_W).astype(F32)
    inv = ROPE_BASE ** (-jnp.arange(ROPE_AXIS_PAIRS, dtype=F32) / ROPE_AXIS_PAIRS)
    ar, ac = r[:, None] * inv, c[:, None] * inv
    zero = jnp.zeros_like(ar)
    cos = jnp.concatenate([jnp.cos(ar), jnp.cos(ar), jnp.cos(ac), jnp.cos(ac)], axis=1)
    sina = jnp.concatenate([-jnp.sin(ar), zero, -jnp.sin(ac), zero], axis=1)
    sinb = jnp.concatenate([zero, jnp.sin(ar), zero, jnp.sin(ac)], axis=1)
    rep = V7X_LANES // ATTN_DH
    return tuple(jnp.tile(tb, (1, rep)) for tb in (cos, sina, sinb))


def _identity_rope(n_tokens):
    return (jnp.ones((n_tokens, V7X_LANES), F32), jnp.zeros((n_tokens, V7X_LANES), F32),
            jnp.zeros((n_tokens, V7X_LANES), F32))


def _gate_table(gates_ctx, gates_x):
    g = jnp.concatenate([gates_ctx, gates_x], axis=1)
    b, n, _ = g.shape
    nc = n // MCHUNK
    g = g.reshape(b, nc, MCHUNK, 4, MLSTM_HEADS).transpose(0, 4, 3, 1, 2)
    g = jnp.pad(g, ((0, 0), (0, 0), (0, 0), (0, MAX_CHUNKS - nc), (0, 0)))
    return g.reshape(b, MLSTM_HEADS, 4 * MAX_CHUNKS, MCHUNK)


def kernel(x, c, ctx, c_ctx, w_ada, b_ada, norm_mix, norm_ffn, w_in, gate_b, qk_conv_w, qk_conv_b, mlstm_norm,
           attn_sink, w_out, w_up, ffn_conv_w, ffn_conv_b, w_down, final_norm):
    b, l, d = x.shape
    n_ctx = ctx.shape[1]
    assert w_ada.shape[0] == 1, "single-layer stack"
    assert l % ROW_TILE == 0 and l % MCHUNK == 0 and n_ctx % MCHUNK == 0 and l >= BAND
    assert (l + n_ctx) // MCHUNK <= MAX_CHUNKS

    rows = -(-(b + 1) // V7X_SUBLANES) * V7X_SUBLANES
    cvec = jnp.zeros((rows, d), F32).at[:b].set(c).at[b].set(c_ctx)
    mod = _ada(cvec, w_ada[0], b_ada[0][None, :])
    mod_x = mod[:b].reshape(b, 1, 6 * d)
    mod_c = mod[b:b + 1].reshape(1, 1, 6 * d)

    w = w_in[0].astype(BF16)
    o0 = 2 * MLSTM_W
    o1, o2 = o0 + MLSTM_W, o0 + 2 * MLSTM_W
    o3 = o2 + N_GATES
    o4, o5 = o3 + ATTN_W, o3 + ATTN_W + KV_W
    w_qk, w_k, w_v, w_o = w[:, :o0], w[:, MLSTM_W:o0], w[:, o0:o1], w[:, o1:o2]
    w_g = jnp.pad(w[:, o2:o3], ((0, 0), (0, V7X_LANES - N_GATES)))
    w_qa, w_ka, w_va = w[:, o3:o4], w[:, o4:o5], w[:, o5:]
    gb = gate_b[0][None, :]
    gain_mix = norm_mix[0][None, :]

    kc_raw, vc, gates_c, ka_c, va_c = _inproj(
        ctx, gain_mix, mod_c, 0, 1, False, _identity_rope(n_ctx), gb,
        [w_k, w_v, w_g, w_ka, w_va], ["plain", "plain", "gates", "plain", "plain"], "inproj_ctx")
    qk_raw, v_m, o_m, gates_x, q_a, k_a, v_a = _inproj(
        x, gain_mix, mod_x, 0, 1, True, _rope_tables(l), gb,
        [w_qk, w_v, w_o, w_g, w_qa, w_ka, w_va],
        ["plain", "plain", "plain", "gates", "rope_q", "rope_k", "plain"], "inproj_x")

    y_m = _mlstm(qk_raw, v_m, o_m, kc_raw, vc, _gate_table(gates_c, gates_x),
                 qk_conv_w[0], qk_conv_b[0][None, :], mlstm_norm[0][None, :])
    y_a = _attn(attn_sink[0], q_a, k_a, v_a, ka_c, va_c)

    wo = w_out[0].astype(BF16)
    x1, hf = _outproj(y_m, y_a, x, wo[:MLSTM_W], wo[MLSTM_W:], mod_x, norm_ffn[0][None, :])
    return _ffn(hf, x1, mod_x, w_up[0].astype(BF16), ffn_conv_w[0], ffn_conv_b[0][None, :],
                w_down[0].astype(BF16), final_norm[None, :])
```

```python
import functools

import jax
import jax.numpy as jnp
from jax import lax
from jax.experimental import pallas as pl
from jax.experimental.pallas import tpu as pltpu

F32 = jnp.float32
BF16 = jnp.bfloat16

D_MODEL = 1024
GRID_W = 64
MLSTM_HEADS = 4
MLSTM_DH = 128
MLSTM_W = MLSTM_HEADS * MLSTM_DH
N_GATES = 4 * MLSTM_HEADS
ATTN_HEADS = 8
KV_HEADS = 2
ATTN_DH = 64
ATTN_W = ATTN_HEADS * ATTN_DH
KV_W = KV_HEADS * ATTN_DH
GQA_GROUP = ATTN_HEADS // KV_HEADS
WINDOW = 128
ROPE_BASE = 10000.0
ROPE_AXIS_PAIRS = ATTN_DH // 4
D_FF = 2816
EPS = 1e-6

V7X_LANES = 128
V7X_SUBLANES = 8
V7X_VMEM_BYTES = 64 * 1024 * 1024

ROW_TILE = 512
ADA_COL_TILE = 1536
MCHUNK = 128
MAX_CHUNKS = 32
ATTN_BLOCK = 128
ATTN_BLOCKS_PER_STEP = 2
BAND = 3 * ATTN_BLOCK
FF_CHUNK = 256
HALO = 2 * V7X_SUBLANES


def _vmem_limit(block_bytes):
    return int(min(V7X_VMEM_BYTES * 7 // 8, 2 * block_bytes + 16 * 1024 * 1024))


def _params(block_bytes, n_axes):
    return pltpu.CompilerParams(dimension_semantics=("arbitrary",) * n_axes,
                                vmem_limit_bytes=_vmem_limit(block_bytes))


def _nbytes(shape, dtype):
    n = 1
    for s in shape:
        n *= s
    return n * jnp.dtype(dtype).itemsize


def _ada_body(c_ref, w_ref, b_ref, o_ref):
    c = c_ref[...]
    s = (c * jax.nn.sigmoid(c)).astype(BF16)
    o_ref[...] = jnp.dot(s, w_ref[...].astype(BF16), preferred_element_type=F32) + b_ref[...]


def _ada(cvec, w_ada, b_ada):
    rows, d = cvec.shape
    n = w_ada.shape[1]
    blk = _nbytes((d, ADA_COL_TILE), F32) + _nbytes((rows, d), F32) + 2 * _nbytes((rows, ADA_COL_TILE), F32)
    return pl.pallas_call(
        _ada_body,
        grid=(n // ADA_COL_TILE,),
        in_specs=[pl.BlockSpec((rows, d), lambda j: (0, 0)),
                  pl.BlockSpec((d, ADA_COL_TILE), lambda j: (0, j)),
                  pl.BlockSpec((1, ADA_COL_TILE), lambda j: (0, j))],
        out_specs=pl.BlockSpec((rows, ADA_COL_TILE), lambda j: (0, j)),
        out_shape=jax.ShapeDtypeStruct((rows, n), F32),
        compiler_params=_params(blk, 1),
        name="ada",
    )(cvec, w_ada, b_ada)


def _norm_modulate(x, gain, shift, scale):
    ms = jnp.mean(x * x, axis=-1, keepdims=True)
    return (x * lax.rsqrt(ms + EPS) * gain) * (1.0 + scale) + shift


def _inproj_body(x_ref, gain_ref, sh_ref, sc_ref, cos_ref, sina_ref, sinb_ref, gb_ref, *refs, kinds):
    n = len(kinds)
    w_refs, o_refs = refs[:n], refs[n:]
    hb = _norm_modulate(x_ref[0], gain_ref[...], sh_ref[0], sc_ref[0]).astype(BF16)
    for kind, w_ref, o_ref in zip(kinds, w_refs, o_refs):
        if kind == "gates":
            u = jnp.dot(hb, w_ref[...], preferred_element_type=F32)
            o_ref[0] = u[:, :N_GATES] + gb_ref[...]
        elif kind == "plain":
            o_ref[0] = jnp.dot(hb, w_ref[...], preferred_element_type=F32).astype(BF16)
        else:
            scale = ATTN_DH ** -0.5 if kind == "rope_q" else 1.0
            width = w_ref.shape[1]
            for j in range(width // V7X_LANES):
                cols = pl.ds(j * V7X_LANES, V7X_LANES)
                u = jnp.dot(hb, w_ref[:, cols], preferred_element_type=F32)
                y = (u * cos_ref[...]
                     + pltpu.roll(u, V7X_LANES - ROPE_AXIS_PAIRS, 1) * sina_ref[...]
                     + pltpu.roll(u, ROPE_AXIS_PAIRS, 1) * sinb_ref[...])
                o_ref[0, :, cols] = (y * scale).astype(BF16)


def _inproj(x, gain, mod3, shift_idx, scale_idx, per_batch_mod, rope, gate_b, weights, kinds, name):
    b, n, d = x.shape
    tm = min(ROW_TILE, n)
    cos, sina, sinb = rope
    mod_spec = lambda piece: pl.BlockSpec(
        (1, 1, d), (lambda bi, i: (bi, 0, piece)) if per_batch_mod else (lambda bi, i: (0, 0, piece)))
    in_specs = [pl.BlockSpec((1, tm, d), lambda bi, i: (bi, i, 0)),
                pl.BlockSpec((1, d), lambda bi, i: (0, 0)),
                mod_spec(shift_idx), mod_spec(scale_idx),
                pl.BlockSpec((tm, V7X_LANES), lambda bi, i: (i, 0)),
                pl.BlockSpec((tm, V7X_LANES), lambda bi, i: (i, 0)),
                pl.BlockSpec((tm, V7X_LANES), lambda bi, i: (i, 0)),
                pl.BlockSpec((1, N_GATES), lambda bi, i: (0, 0))]
    out_specs, out_shapes = [], []
    blk = _nbytes((tm, d), F32) + 3 * _nbytes((tm, V7X_LANES), F32)
    for kind, w in zip(kinds, weights):
        width = N_GATES if kind == "gates" else w.shape[1]
        dt = F32 if kind == "gates" else BF16
        in_specs.append(pl.BlockSpec(w.shape, lambda bi, i: (0, 0)))
        out_specs.append(pl.BlockSpec((1, tm, width), lambda bi, i: (bi, i, 0)))
        out_shapes.append(jax.ShapeDtypeStruct((b, n, width), dt))
        blk += _nbytes(w.shape, BF16) + _nbytes((tm, width), dt) + _nbytes((tm, w.shape[1]), F32)
    return pl.pallas_call(
        functools.partial(_inproj_body, kinds=tuple(kinds)),
        grid=(b, n // tm),
        in_specs=in_specs,
        out_specs=out_specs,
        out_shape=out_shapes,
        compiler_params=_params(blk, 2),
        name=name,
    )(x, gain, mod3, mod3, cos, sina, sinb, gate_b, *weights)


def _conv_silu(x, w, b):
    n = x.shape[0]
    row = lax.broadcasted_iota(jnp.int32, x.shape, 0)
    prev = jnp.where(row == 0, 0.0, pltpu.roll(x, 1, 0))
    nxt = jnp.where(row == n - 1, 0.0, pltpu.roll(x, n - 1, 0))
    y = w[0:1, :] * prev + w[1:2, :] * x + w[2:3, :] * nxt + b
    return y * jax.nn.sigmoid(y)


def _log_sigmoid(x):
    return jnp.minimum(x, 0.0) - jnp.log1p(jnp.exp(-jnp.abs(x)))


def _split3(x):
    hi = x.astype(BF16)
    r1 = x - hi.astype(F32)
    mid = r1.astype(BF16)
    lo = (r1 - mid.astype(F32)).astype(BF16)
    return hi, mid, lo


def _mlstm_body(q_ref, k_ref, v_ref, o_ref, kc_ref, vc_ref, gt_ref, wq_ref, bq_ref, wk_ref, bk_ref, gain_ref,
                y_ref, qs, ks, kcs, wtab, rtab, cumtab, cnl, c0n, cumc, sqk, sds, inter_s, em_s,
                *, n_ctx_chunks, n_lat_chunks):
    t = MCHUNK
    dh = MLSTM_DH
    n_chunks = n_ctx_chunks + n_lat_chunks
    kscale = MLSTM_DH ** -0.5
    nt = (((1,), (1,)), ((), ()))

    qs[...] = _conv_silu(q_ref[0].astype(F32), wq_ref[...], bq_ref[...]).astype(BF16)
    ks[...] = (_conv_silu(k_ref[0].astype(F32), wk_ref[...], bk_ref[...]) * kscale).astype(BF16)
    kcs[...] = (_conv_silu(kc_ref[0].astype(F32), wk_ref[...], bk_ref[...]) * kscale).astype(BF16)

    tbl = gt_ref[0, 0]
    rid = lax.broadcasted_iota(jnp.int32, tbl.shape, 0) // MAX_CHUNKS
    lane = lax.broadcasted_iota(jnp.int32, tbl.shape, 1)
    lf = _log_sigmoid(tbl)
    pre = jnp.where(rid == 1, lf, 0.0)
    suf = jnp.where(rid == 3, lf, 0.0)
    s = 1
    while s < t:
        pre = pre + jnp.where(lane >= s, pltpu.roll(pre, s, 1), 0.0)
        suf = suf + jnp.where(lane < t - s, pltpu.roll(suf, t - s, 1), 0.0)
        s *= 2
    li = (tbl[0:MAX_CHUNKS], tbl[2 * MAX_CHUNKS:3 * MAX_CHUNKS])
    cum = (pre[MAX_CHUNKS:2 * MAX_CHUNKS], suf[3 * MAX_CHUNKS:4 * MAX_CHUNKS])
    total = (cum[0][:, t - 1:t], cum[1][:, 0:1])
    mloc = []
    for d in (0, 1):
        g = total[d] - cum[d] + li[d]
        mloc.append(jnp.max(g, axis=1, keepdims=True))
        wtab[d] = jnp.exp(g - mloc[d])
        rtab[d] = cum[d] - li[d]
        cumtab[d] = cum[d]

    orders = (list(range(n_chunks)),
              list(range(n_ctx_chunks - 1, -1, -1)) + list(range(n_chunks - 1, n_ctx_chunks - 1, -1)))

    a_sc, s_sc, m0 = ({}, {}), ({}, {}), ({}, {})
    for d in (0, 1):
        m_st = jnp.zeros((1, 1), F32)
        for c in orders[d][:-1]:
            m0[d][c] = m_st
            tot_c = total[d][c:c + 1, :]
            ml_c = mloc[d][c:c + 1, :]
            m_new = jnp.maximum(tot_c + m_st, ml_c)
            a_sc[d][c] = jnp.exp(tot_c + m_st - m_new)
            s_sc[d][c] = jnp.exp(ml_c - m_new)
            m_st = m_new
        m0[d][orders[d][-1]] = m_st

    def kv_chunk(c):
        if c < n_ctx_chunks:
            rows = pl.ds(c * t, t)
            return kcs[rows, :], vc_ref[0, rows, :]
        rows = pl.ds((c - n_ctx_chunks) * t, t)
        return ks[rows, :], v_ref[0, rows, :]

    ri = lax.broadcasted_iota(jnp.int32, (t, t), 0)
    ci = lax.broadcasted_iota(jnp.int32, (t, t), 1)
    eye = jnp.where(ri == ci, 1.0, 0.0).astype(BF16)
    ones = jnp.ones((t, dh), BF16)

    for c in range(n_chunks):
        k, v = kv_chunk(c)
        k_t = lax.dot_general(eye, k, nt, preferred_element_type=F32)
        vo = jnp.concatenate([v, ones], axis=1)
        for d in (0, 1):
            if c != orders[d][-1]:
                kw = (k_t * wtab[d, pl.ds(c, 1), :]).astype(BF16)
                cnl[d, c] = jnp.dot(kw, vo, preferred_element_type=F32)

    for d in (0, 1):
        cn = jnp.zeros((dh, 2 * dh), F32)
        for c in orders[d]:
            if c >= n_ctx_chunks:
                c0n[d, c - n_ctx_chunks] = cn.astype(BF16)
            if c != orders[d][-1]:
                cn = a_sc[d][c] * cn + s_sc[d][c] * cnl[d, c]

    eye3 = jnp.concatenate([eye] * 3, axis=1)
    for j in range(n_lat_chunks):
        c = j + n_ctx_chunks
        rows = pl.ds(j * t, t)
        sqk[rows, :] = lax.dot_general(qs[rows, :], ks[rows, :], nt, preferred_element_type=F32)
        for d in (0, 1):
            parts = _split3(cumtab[d, pl.ds(c, 1), :])
            rep = jnp.concatenate([jnp.broadcast_to(p.astype(F32), (t, t)).astype(BF16) for p in parts], axis=1)
            cumc[d, rows, :] = lax.dot_general(eye3, rep, nt, preferred_element_type=F32)

    grp = 2
    gi = lax.broadcasted_iota(jnp.int32, (grp * t, t), 0) & (t - 1)
    gs = lax.broadcasted_iota(jnp.int32, (grp * t, t), 1)
    for d in (0, 1):
        ok = (gs <= gi) if d == 0 else (gs >= gi)
        for j0 in range(0, n_lat_chunks, grp):
            rows = pl.ds(j0 * t, grp * t)
            cs = [j + n_ctx_chunks for j in range(j0, j0 + grp)]
            rb = jnp.concatenate([jnp.broadcast_to(rtab[d, pl.ds(c, 1), :], (t, t)) for c in cs], axis=0)
            m0b = jnp.concatenate([jnp.broadcast_to(m0[d][c], (t, t)) for c in cs], axis=0)
            cum_c = cumc[d, rows, :]
            d_log = jnp.where(ok, cum_c - rb, -jnp.inf)
            a_log = cum_c + m0b
            m_t = jnp.maximum(a_log, jnp.max(d_log, axis=1, keepdims=True))
            sds[d, rows, :] = (sqk[rows, :] * jnp.exp(d_log - m_t)).astype(BF16)
            inter_s[d, rows, :] = jnp.exp(a_log - m_t)
            em_s[d, rows, :] = jnp.exp(-m_t)

    gain = gain_ref[...]
    for j in range(n_lat_chunks):
        rows = pl.ds(j * t, t)
        q = qs[rows, :]
        vo = jnp.concatenate([v_ref[0, rows, :], ones], axis=1)
        hid = None
        for d in (0, 1):
            sv = jnp.dot(sds[d, rows, :], vo, preferred_element_type=F32)
            qc = jnp.dot(q, c0n[d, j], preferred_element_type=F32)
            inter = inter_s[d, rows, :]
            num = sv[:, :dh] + inter * qc[:, :dh]
            den = sv[:, dh:] + inter * qc[:, dh:]
            h_d = num / jnp.maximum(jnp.abs(den), em_s[d, rows, :])
            hid = h_d if hid is None else hid + h_d
        hn = hid * lax.rsqrt(jnp.mean(hid * hid, axis=-1, keepdims=True) + EPS) * gain
        y_ref[0, rows, :] = (hn * jax.nn.sigmoid(o_ref[0, rows, :].astype(F32))).astype(BF16)


def _mlstm(qk, v, o, kc, vc, gate_tbl, conv_w, conv_b, gain):
    b, l, _ = v.shape
    n_ctx = kc.shape[1]
    dh = MLSTM_DH
    nh = MLSTM_HEADS
    n_ctx_chunks, n_lat_chunks = n_ctx // MCHUNK, l // MCHUNK
    seq = lambda col0: pl.BlockSpec((1, l, dh), lambda bi, h: (bi, 0, col0 + h))
    ctx = pl.BlockSpec((1, n_ctx, dh), lambda bi, h: (bi, 0, h))
    n_chunks = n_ctx_chunks + n_lat_chunks
    scratch = [((l, dh), BF16), ((l, dh), BF16), ((n_ctx, dh), BF16),
               ((2, MAX_CHUNKS, MCHUNK), F32), ((2, MAX_CHUNKS, MCHUNK), F32),
               ((2, MAX_CHUNKS, MCHUNK), F32),
               ((2, n_chunks, dh, 2 * dh), F32), ((2, n_lat_chunks, dh, 2 * dh), BF16),
               ((2, l, MCHUNK), F32), ((l, MCHUNK), F32), ((2, l, MCHUNK), BF16),
               ((2, l, MCHUNK), F32), ((2, l, MCHUNK), F32)]
    blk = (5 * _nbytes((l, dh), BF16) + 2 * _nbytes((n_ctx, dh), BF16) + _nbytes((V7X_LANES, V7X_LANES), F32)
           + sum(_nbytes(s, dt) for s, dt in scratch) // 2)
    return pl.pallas_call(
        functools.partial(_mlstm_body, n_ctx_chunks=n_ctx_chunks, n_lat_chunks=n_lat_chunks),
        grid=(b, nh),
        in_specs=[seq(0), seq(nh), seq(0), seq(0), ctx, ctx,
                  pl.BlockSpec((1, 1, 4 * MAX_CHUNKS, MCHUNK), lambda bi, h: (bi, h, 0, 0)),
                  pl.BlockSpec((3, dh), lambda bi, h: (0, h)),
                  pl.BlockSpec((1, dh), lambda bi, h: (0, h)),
                  pl.BlockSpec((3, dh), lambda bi, h: (0, nh + h)),
                  pl.BlockSpec((1, dh), lambda bi, h: (0, nh + h)),
                  pl.BlockSpec((1, dh), lambda bi, h: (0, h))],
        out_specs=pl.BlockSpec((1, l, dh), lambda bi, h: (bi, 0, h)),
        out_shape=jax.ShapeDtypeStruct((b, l, MLSTM_W), BF16),
        scratch_shapes=[pltpu.VMEM(s, dt) for s, dt in scratch],
        compiler_params=_params(blk, 2),
        name="mlstm",
    )(qk, qk, v, o, kc, vc, gate_tbl, conv_w, conv_b, conv_w, conv_b, gain)


def _attn_body(sink_ref, bias_ref, q_ref, k_ref, v_ref, kc_ref, vc_ref, o_ref, *, seq_len):
    blk = ATTN_BLOCK
    n_blocks = seq_len // blk
    nt = (((1,), (1,)), ((), ()))
    ones_b = jnp.ones((BAND, ATTN_DH), BF16)
    ones_c = jnp.ones((kc_ref.shape[1], ATTN_DH), BF16)
    for qb in range(ATTN_BLOCKS_PER_STEP):
        i = pl.program_id(1) * ATTN_BLOCKS_PER_STEP + qb
        start = pl.multiple_of(jnp.clip((i - 1) * blk, 0, seq_len - BAND), blk)
        bias = bias_ref[jnp.where(i == 0, 0, jnp.where(i == n_blocks - 1, 2, 1))]
        q_rows = pl.ds(qb * blk, blk)
        for g in range(KV_HEADS):
            kv_cols = pl.ds(g * ATTN_DH, ATTN_DH)
            kb = k_ref[0, pl.ds(start, BAND), kv_cols]
            vb = jnp.concatenate([v_ref[0, pl.ds(start, BAND), kv_cols], ones_b], axis=1)
            kc = kc_ref[0, :, kv_cols]
            vc = jnp.concatenate([vc_ref[0, :, kv_cols], ones_c], axis=1)
            heads = [g * GQA_GROUP + j for j in range(GQA_GROUP)]
            q = jnp.concatenate([q_ref[0, q_rows, pl.ds(h * ATTN_DH, ATTN_DH)] for h in heads], axis=0)
            s_band = lax.dot_general(q, kb, nt, preferred_element_type=F32) + bias
            s_ctx = lax.dot_general(q, kc, nt, preferred_element_type=F32)
            sink = jnp.concatenate([jnp.full((blk, 1), sink_ref[h], F32) for h in heads], axis=0)
            m = jnp.maximum(jnp.maximum(jnp.max(s_band, axis=1, keepdims=True),
                                        jnp.max(s_ctx, axis=1, keepdims=True)), sink)
            p_band = jnp.exp(s_band - m).astype(BF16)
            p_ctx = jnp.exp(s_ctx - m).astype(BF16)
            acc = (jnp.dot(p_band, vb, preferred_element_type=F32)
                   + jnp.dot(p_ctx, vc, preferred_element_type=F32))
            den = pltpu.roll(acc, ATTN_DH, 1) + jnp.exp(sink - m)
            out = (acc / den)[:, :ATTN_DH]
            for j, h in enumerate(heads):
                o_ref[0, q_rows, pl.ds(h * ATTN_DH, ATTN_DH)] = out[j * blk:(j + 1) * blk].astype(BF16)


def _band_bias(seq_len):
    row = jnp.arange(ATTN_BLOCK)[:, None]
    col = jnp.arange(BAND)[None, :]
    n_blocks = seq_len // ATTN_BLOCK
    tables = []
    for i in (0, 1, n_blocks - 1):
        start = min(max((i - 1) * ATTN_BLOCK, 0), seq_len - BAND)
        ok = jnp.abs(start + col - (i * ATTN_BLOCK + row)) <= WINDOW
        tables.append(jnp.tile(jnp.where(ok, 0.0, -jnp.inf).astype(F32), (GQA_GROUP, 1)))
    return jnp.stack(tables)


def _attn(sink, q, k, v, kc, vc):
    b, l, _ = q.shape
    n_ctx = kc.shape[1]
    rows = ATTN_BLOCKS_PER_STEP * ATTN_BLOCK
    bias = _band_bias(l)
    blk = (2 * _nbytes((rows, ATTN_W), BF16) + 2 * _nbytes((l, KV_W), BF16) + 2 * _nbytes((n_ctx, KV_W), BF16)
           + _nbytes(bias.shape, F32)
           + 2 * KV_HEADS * ATTN_BLOCKS_PER_STEP * _nbytes((GQA_GROUP * ATTN_BLOCK, BAND + n_ctx), F32))
    return pl.pallas_call(
        functools.partial(_attn_body, seq_len=l),
        grid=(b, l // rows),
        in_specs=[pl.BlockSpec(memory_space=pltpu.SMEM),
                  pl.BlockSpec(bias.shape, lambda bi, i: (0, 0, 0)),
                  pl.BlockSpec((1, rows, ATTN_W), lambda bi, i: (bi, i, 0)),
                  pl.BlockSpec((1, l, KV_W), lambda bi, i: (bi, 0, 0)),
                  pl.BlockSpec((1, l, KV_W), lambda bi, i: (bi, 0, 0)),
                  pl.BlockSpec((1, n_ctx, KV_W), lambda bi, i: (bi, 0, 0)),
                  pl.BlockSpec((1, n_ctx, KV_W), lambda bi, i: (bi, 0, 0))],
        out_specs=pl.BlockSpec((1, rows, ATTN_W), lambda bi, i: (bi, i, 0)),
        out_shape=jax.ShapeDtypeStruct((b, l, ATTN_W), BF16),
        compiler_params=_params(blk, 2),
        name="attn",
    )(sink, bias, q, k, v, kc, vc)


def _outproj_body(ym_ref, ya_ref, x_ref, wm_ref, wa_ref, ga_ref, gain_ref, sh_ref, sc_ref, x1_ref, hf_ref):
    acc = (jnp.dot(ym_ref[0], wm_ref[...], preferred_element_type=F32)
           + jnp.dot(ya_ref[0], wa_ref[...], preferred_element_type=F32))
    x1 = x_ref[0] + ga_ref[0] * acc
    x1_ref[0] = x1
    hf_ref[0] = _norm_modulate(x1, gain_ref[...], sh_ref[0], sc_ref[0]).astype(BF16)


def _outproj(ym, ya, x, wm, wa, mod3, gain):
    b, l, d = x.shape
    tm = ROW_TILE
    mod_spec = lambda piece: pl.BlockSpec((1, 1, d), lambda bi, i: (bi, 0, piece))
    blk = (2 * _nbytes((tm, MLSTM_W), BF16) + 3 * _nbytes((tm, d), F32) + _nbytes((tm, d), BF16)
           + 2 * _nbytes(wm.shape, BF16))
    return pl.pallas_call(
        _outproj_body,
        grid=(b, l // tm),
        in_specs=[pl.BlockSpec((1, tm, MLSTM_W), lambda bi, i: (bi, i, 0)),
                  pl.BlockSpec((1, tm, ATTN_W), lambda bi, i: (bi, i, 0)),
                  pl.BlockSpec((1, tm, d), lambda bi, i: (bi, i, 0)),
                  pl.BlockSpec(wm.shape, lambda bi, i: (0, 0)),
                  pl.BlockSpec(wa.shape, lambda bi, i: (0, 0)),
                  mod_spec(2),
                  pl.BlockSpec((1, d), lambda bi, i: (0, 0)),
                  mod_spec(3), mod_spec(4)],
        out_specs=[pl.BlockSpec((1, tm, d), lambda bi, i: (bi, i, 0)),
                   pl.BlockSpec((1, tm, d), lambda bi, i: (bi, i, 0))],
        out_shape=[jax.ShapeDtypeStruct((b, l, d), F32), jax.ShapeDtypeStruct((b, l, d), BF16)],
        compiler_params=_params(blk, 2),
        name="outproj",
    )(ym, ya, x, wm, wa, mod3, gain, mod3, mod3)


def _gelu_tanh(x):
    return 0.5 * x * (1.0 + jnp.tanh(0.7978845608028654 * (x + 0.044715 * (x * x * x))))


def _ffn_body(h_ref, hp_ref, hn_ref, x1_ref, gf_ref, wup_ref, cw_ref, cb_ref, wdn_ref, fn_ref, o_ref, hs, acc):
    i = pl.program_id(1)
    tm = h_ref.shape[1]
    ext = tm + 2 * HALO
    hs[pl.ds(0, HALO), :] = jnp.where(i > 0, hp_ref[0], jnp.zeros_like(hp_ref[0]))
    hs[pl.ds(HALO, tm), :] = h_ref[0]
    hs[pl.ds(HALO + tm, HALO), :] = jnp.where(i < pl.num_programs(1) - 1, hn_ref[0], jnp.zeros_like(hn_ref[0]))
    for c in range(D_FF // FF_CHUNK):
        cols = pl.ds(c * FF_CHUNK, FF_CHUNK)
        a = jnp.dot(hs[...], wup_ref[:, cols], preferred_element_type=F32)
        val = jnp.dot(h_ref[0], wup_ref[:, pl.ds(D_FF + c * FF_CHUNK, FF_CHUNK)], preferred_element_type=F32)
        a_prev = pltpu.roll(a, 1, 0)[HALO:HALO + tm]
        a_next = pltpu.roll(a, ext - 1, 0)[HALO:HALO + tm]
        conv = (cw_ref[0:1, cols] * a_prev + cw_ref[1:2, cols] * a[HALO:HALO + tm]
                + cw_ref[2:3, cols] * a_next + cb_ref[:, cols])
        gated = (_gelu_tanh(conv) * val).astype(BF16)
        part = jnp.dot(gated, wdn_ref[cols, :], preferred_element_type=F32)
        if c == 0:
            acc[...] = part
        else:
            acc[...] += part
    x2 = x1_ref[0] + gf_ref[0] * acc[...]
    ms = jnp.mean(x2 * x2, axis=-1, keepdims=True)
    o_ref[0] = x2 * lax.rsqrt(ms + EPS) * fn_ref[...]


def _ffn(hf, x1, mod3, w_up, conv_w, conv_b, w_down, final_norm):
    b, l, d = x1.shape
    tm = ROW_TILE
    per = tm // HALO
    last = l // HALO - 1
    blk = (_nbytes((tm, d), BF16) + 2 * _nbytes((tm, d), F32) + _nbytes(w_up.shape, BF16)
           + _nbytes(w_down.shape, BF16) + 4 * _nbytes((tm + 2 * HALO, FF_CHUNK), F32))
    return pl.pallas_call(
        _ffn_body,
        grid=(b, l // tm),
        in_specs=[pl.BlockSpec((1, tm, d), lambda bi, i: (bi, i, 0)),
                  pl.BlockSpec((1, HALO, d), lambda bi, i: (bi, jnp.maximum(i * per - 1, 0), 0)),
                  pl.BlockSpec((1, HALO, d), lambda bi, i: (bi, jnp.minimum((i + 1) * per, last), 0)),
                  pl.BlockSpec((1, tm, d), lambda bi, i: (bi, i, 0)),
                  pl.BlockSpec((1, 1, d), lambda bi, i: (bi, 0, 5)),
                  pl.BlockSpec(w_up.shape, lambda bi, i: (0, 0)),
                  pl.BlockSpec(conv_w.shape, lambda bi, i: (0, 0)),
                  pl.BlockSpec(conv_b.shape, lambda bi, i: (0, 0)),
                  pl.BlockSpec(w_down.shape, lambda bi, i: (0, 0)),
                  pl.BlockSpec((1, d), lambda bi, i: (0, 0))],
        out_specs=pl.BlockSpec((1, tm, d), lambda bi, i: (bi, i, 0)),
        out_shape=jax.ShapeDtypeStruct((b, l, d), F32),
        scratch_shapes=[pltpu.VMEM((tm + 2 * HALO, d), BF16), pltpu.VMEM((tm, d), F32)],
        compiler_params=_params(blk, 2),
        name="ffn",
    )(hf, hf, hf, x1, mod3, w_up, conv_w, conv_b, w_down, final_norm)


def _rope_tables(n_tokens):
    pos = jnp.arange(n_tokens)
    r = (pos // GRID_W).astype(F32)
    c = (pos % GRID_W).astype(F32)
    inv = ROPE_BASE ** (-jnp.arange(ROPE_AXIS_PAIRS, dtype=F32) / ROPE_AXIS_PAIRS)
    ar, ac = r[:, None] * inv, c[:, None] * inv
    zero = jnp.zeros_like(ar)
    cos = jnp.concatenate([jnp.cos(ar), jnp.cos(ar), jnp.cos(ac), jnp.cos(ac)], axis=1)
    sina = jnp.concatenate([-jnp.sin(ar), zero, -jnp.sin(ac), zero], axis=1)
    sinb = jnp.concatenate([zero, jnp.sin(ar), zero, jnp.sin(ac)], axis=1)
    rep = V7X_LANES // ATTN_DH
    return tuple(jnp.tile(tb, (1, rep)) for tb in (cos, sina, sinb))


def _identity_rope(n_tokens):
    return (jnp.ones((n_tokens, V7X_LANES), F32), jnp.zeros((n_tokens, V7X_LANES), F32),
            jnp.zeros((n_tokens, V7X_LANES), F32))


def _gate_table(gates_ctx, gates_x):
    g = jnp.concatenate([gates_ctx, gates_x], axis=1)
    b, n, _ = g.shape
    nc = n // MCHUNK
    g = g.reshape(b, nc, MCHUNK, 4, MLSTM_HEADS).transpose(0, 4, 3, 1, 2)
    g = jnp.pad(g, ((0, 0), (0, 0), (0, 0), (0, MAX_CHUNKS - nc), (0, 0)))
    return g.reshape(b, MLSTM_HEADS, 4 * MAX_CHUNKS, MCHUNK)


def kernel(x, c, ctx, c_ctx, w_ada, b_ada, norm_mix, norm_ffn, w_in, gate_b, qk_conv_w, qk_conv_b, mlstm_norm,
           attn_sink, w_out, w_up, ffn_conv_w, ffn_conv_b, w_down, final_norm):
    b, l, d = x.shape
    n_ctx = ctx.shape[1]
    assert w_ada.shape[0] == 1, "single-layer stack"
    assert l % ROW_TILE == 0 and l % MCHUNK == 0 and n_ctx % MCHUNK == 0 and l >= BAND
    assert l % (ATTN_BLOCKS_PER_STEP * ATTN_BLOCK) == 0 and MCHUNK == V7X_LANES
    assert (l + n_ctx) // MCHUNK <= MAX_CHUNKS

    rows = -(-(b + 1) // V7X_SUBLANES) * V7X_SUBLANES
    cvec = jnp.zeros((rows, d), F32).at[:b].set(c).at[b].set(c_ctx)
    mod = _ada(cvec, w_ada[0], b_ada[0][None, :])
    mod_x = mod[:b].reshape(b, 1, 6 * d)
    mod_c = mod[b:b + 1].reshape(1, 1, 6 * d)

    w = w_in[0].astype(BF16)
    o0 = 2 * MLSTM_W
    o1, o2 = o0 + MLSTM_W, o0 + 2 * MLSTM_W
    o3 = o2 + N_GATES
    o4, o5 = o3 + ATTN_W, o3 + ATTN_W + KV_W
    w_qk, w_k, w_v, w_o = w[:, :o0], w[:, MLSTM_W:o0], w[:, o0:o1], w[:, o1:o2]
    w_g = jnp.pad(w[:, o2:o3], ((0, 0), (0, V7X_LANES - N_GATES)))
    w_qa, w_ka, w_va = w[:, o3:o4], w[:, o4:o5], w[:, o5:]
    gb = gate_b[0][None, :]
    gain_mix = norm_mix[0][None, :]

    kc_raw, vc, gates_c, ka_c, va_c = _inproj(
        ctx, gain_mix, mod_c, 0, 1, False, _identity_rope(n_ctx), gb,
        [w_k, w_v, w_g, w_ka, w_va], ["plain", "plain", "gates", "plain", "plain"], "inproj_ctx")
    qk_raw, v_m, o_m, gates_x, q_a, k_a, v_a = _inproj(
        x, gain_mix, mod_x, 0, 1, True, _rope_tables(l), gb,
        [w_qk, w_v, w_o, w_g, w_qa, w_ka, w_va],
        ["plain", "plain", "plain", "gates", "rope_q", "rope_k", "plain"], "inproj_x")

    y_m = _mlstm(qk_raw, v_m, o_m, kc_raw, vc, _gate_table(gates_c, gates_x),
                 qk_conv_w[0], qk_conv_b[0][None, :], mlstm_norm[0][None, :])
    y_a = _attn(attn_sink[0], q_a, k_a, v_a, ka_c, va_c)

    wo = w_out[0].astype(BF16)
    x1, hf = _outproj(y_m, y_a, x, wo[:MLSTM_W], wo[MLSTM_W:], mod_x, norm_ffn[0][None, :])
    return _ffn(hf, x1, mod_x, w_up[0].astype(BF16), ffn_conv_w[0], ffn_conv_b[0][None, :],
                w_down[0].astype(BF16), final_norm[None, :])
```

```python
import functools

import jax
import jax.numpy as jnp
from jax import lax
from jax.experimental import pallas as pl
from jax.experimental.pallas import tpu as pltpu

F32 = jnp.float32
BF16 = jnp.bfloat16

D_MODEL = 1024
GRID_W = 64
MLSTM_HEADS = 4
MLSTM_DH = 128
MLSTM_W = MLSTM_HEADS * MLSTM_DH
N_GATES = 4 * MLSTM_HEADS
ATTN_HEADS = 8
KV_HEADS = 2
ATTN_DH = 64
ATTN_W = ATTN_HEADS * ATTN_DH
KV_W = KV_HEADS * ATTN_DH
GQA_GROUP = ATTN_HEADS // KV_HEADS
WINDOW = 128
ROPE_BASE = 10000.0
ROPE_AXIS_PAIRS = ATTN_DH // 4
D_FF = 2816
EPS = 1e-6
LOG2E = 1.4426950408889634

V7X_LANES = 128
V7X_SUBLANES = 8
V7X_VMEM_BYTES = 64 * 1024 * 1024

ROW_TILE = 512
ADA_COL_TILE = 1536
MCHUNK = 128
MAX_CHUNKS = 32
ATTN_BLOCK = 128
ATTN_BLOCKS_PER_STEP = 2
BAND = 3 * ATTN_BLOCK
FF_CHUNK = 512
HALO = 2 * V7X_SUBLANES


def _vmem_limit(block_bytes):
    return int(min(V7X_VMEM_BYTES * 7 // 8, 2 * block_bytes + 16 * 1024 * 1024))


def _params(block_bytes, n_axes):
    return pltpu.CompilerParams(dimension_semantics=("arbitrary",) * n_axes,
                                vmem_limit_bytes=_vmem_limit(block_bytes))


def _nbytes(shape, dtype):
    n = 1
    for s in shape:
        n *= s
    return n * jnp.dtype(dtype).itemsize


def _ada_body(c_ref, w_ref, b_ref, o_ref):
    c = c_ref[...]
    s = (c * jax.nn.sigmoid(c)).astype(BF16)
    o_ref[...] = jnp.dot(s, w_ref[...].astype(BF16), preferred_element_type=F32) + b_ref[...]


def _ada(cvec, w_ada, b_ada):
    rows, d = cvec.shape
    n = w_ada.shape[1]
    blk = _nbytes((d, ADA_COL_TILE), F32) + _nbytes((rows, d), F32) + 2 * _nbytes((rows, ADA_COL_TILE), F32)
    return pl.pallas_call(
        _ada_body,
        grid=(n // ADA_COL_TILE,),
        in_specs=[pl.BlockSpec((rows, d), lambda j: (0, 0)),
                  pl.BlockSpec((d, ADA_COL_TILE), lambda j: (0, j)),
                  pl.BlockSpec((1, ADA_COL_TILE), lambda j: (0, j))],
        out_specs=pl.BlockSpec((rows, ADA_COL_TILE), lambda j: (0, j)),
        out_shape=jax.ShapeDtypeStruct((rows, n), F32),
        compiler_params=_params(blk, 1),
        name="ada",
    )(cvec, w_ada, b_ada)


def _norm_modulate(x, gain, shift, scale):
    ms = jnp.mean(x * x, axis=-1, keepdims=True)
    return (x * lax.rsqrt(ms + EPS) * gain) * (1.0 + scale) + shift


def _inproj_body(x_ref, gain_ref, sh_ref, sc_ref, cos_ref, sina_ref, sinb_ref, gb_ref, *refs, kinds):
    n = len(kinds)
    w_refs, o_refs = refs[:n], refs[n:]
    hb = _norm_modulate(x_ref[0], gain_ref[...], sh_ref[0], sc_ref[0]).astype(BF16)
    for kind, w_ref, o_ref in zip(kinds, w_refs, o_refs):
        if kind == "gates":
            u = jnp.dot(hb, w_ref[...], preferred_element_type=F32)
            o_ref[0] = u[:, :N_GATES] + gb_ref[...]
        elif kind == "plain":
            o_ref[0] = jnp.dot(hb, w_ref[...], preferred_element_type=F32).astype(BF16)
        else:
            scale = ATTN_DH ** -0.5 * LOG2E if kind == "rope_q" else 1.0
            width = w_ref.shape[1]
            for j in range(width // V7X_LANES):
                cols = pl.ds(j * V7X_LANES, V7X_LANES)
                u = jnp.dot(hb, w_ref[:, cols], preferred_element_type=F32)
                y = (u * cos_ref[...]
                     + pltpu.roll(u, V7X_LANES - ROPE_AXIS_PAIRS, 1) * sina_ref[...]
                     + pltpu.roll(u, ROPE_AXIS_PAIRS, 1) * sinb_ref[...])
                o_ref[0, :, cols] = (y * scale).astype(BF16)


def _inproj(x, gain, mod3, shift_idx, scale_idx, per_batch_mod, rope, gate_b, weights, kinds, name):
    b, n, d = x.shape
    tm = min(ROW_TILE, n)
    cos, sina, sinb = rope
    mod_spec = lambda piece: pl.BlockSpec(
        (1, 1, d), (lambda bi, i: (bi, 0, piece)) if per_batch_mod else (lambda bi, i: (0, 0, piece)))
    in_specs = [pl.BlockSpec((1, tm, d), lambda bi, i: (bi, i, 0)),
                pl.BlockSpec((1, d), lambda bi, i: (0, 0)),
                mod_spec(shift_idx), mod_spec(scale_idx),
                pl.BlockSpec((tm, V7X_LANES), lambda bi, i: (i, 0)),
                pl.BlockSpec((tm, V7X_LANES), lambda bi, i: (i, 0)),
                pl.BlockSpec((tm, V7X_LANES), lambda bi, i: (i, 0)),
                pl.BlockSpec((1, N_GATES), lambda bi, i: (0, 0))]
    out_specs, out_shapes = [], []
    blk = _nbytes((tm, d), F32) + 3 * _nbytes((tm, V7X_LANES), F32)
    for kind, w in zip(kinds, weights):
        width = N_GATES if kind == "gates" else w.shape[1]
        dt = F32 if kind == "gates" else BF16
        in_specs.append(pl.BlockSpec(w.shape, lambda bi, i: (0, 0)))
        out_specs.append(pl.BlockSpec((1, tm, width), lambda bi, i: (bi, i, 0)))
        out_shapes.append(jax.ShapeDtypeStruct((b, n, width), dt))
        blk += _nbytes(w.shape, BF16) + _nbytes((tm, width), dt) + _nbytes((tm, w.shape[1]), F32)
    return pl.pallas_call(
        functools.partial(_inproj_body, kinds=tuple(kinds)),
        grid=(b, n // tm),
        in_specs=in_specs,
        out_specs=out_specs,
        out_shape=out_shapes,
        compiler_params=_params(blk, 2),
        name=name,
    )(x, gain, mod3, mod3, cos, sina, sinb, gate_b, *weights)


def _silu(y):
    h = 0.5 * y
    return h + h * jnp.tanh(h)


def _sigmoid(y):
    return 0.5 + 0.5 * jnp.tanh(0.5 * y)


def _conv_silu_chunks(src_ref, dst_ref, w, b, scale, n_rows):
    t = MCHUNK
    ri = lax.broadcasted_iota(jnp.int32, (t, 2 * t), 0)
    ci = lax.broadcasted_iota(jnp.int32, (t, 2 * t), 1)
    shift_prev = jnp.where(ci == ri + t - 1, 1.0, 0.0).astype(BF16)
    shift_next = jnp.where(ci == ri + 1, 1.0, 0.0).astype(BF16)
    zeros = jnp.zeros((t, src_ref.shape[-1]), BF16)
    n = n_rows // t
    for c in range(n):
        x = src_ref[0, pl.ds(c * t, t), :]
        before = src_ref[0, pl.ds((c - 1) * t, t), :] if c > 0 else zeros
        after = src_ref[0, pl.ds((c + 1) * t, t), :] if c < n - 1 else zeros
        prev = jnp.dot(shift_prev, jnp.concatenate([before, x], axis=0), preferred_element_type=F32)
        nxt = jnp.dot(shift_next, jnp.concatenate([x, after], axis=0), preferred_element_type=F32)
        y = w[0:1, :] * prev + w[1:2, :] * x.astype(F32) + w[2:3, :] * nxt + b
        act = _silu(y) if scale == 1.0 else _silu(y) * scale
        dst_ref[pl.ds(c * t, t), :] = act.astype(BF16)


def _log_sigmoid(x):
    return jnp.minimum(x, 0.0) - jnp.log1p(jnp.exp(-jnp.abs(x)))


def _mlstm_body(q_ref, k_ref, v_ref, o_ref, kc_ref, vc_ref, gt_ref, wq_ref, bq_ref, wk_ref, bk_ref, gain_ref,
                y_ref, qs, ks, kcs, gtab, rtab, cumtab, cnl, c0n, cumc, sqk, sds, inter_s, em_s,
                *, n_ctx_chunks, n_lat_chunks):
    t = MCHUNK
    dh = MLSTM_DH
    n_chunks = n_ctx_chunks + n_lat_chunks
    kscale = MLSTM_DH ** -0.5
    nt = (((1,), (1,)), ((), ()))

    _conv_silu_chunks(q_ref, qs, wq_ref[...], bq_ref[...], 1.0, n_lat_chunks * t)
    _conv_silu_chunks(k_ref, ks, wk_ref[...], bk_ref[...], kscale, n_lat_chunks * t)
    _conv_silu_chunks(kc_ref, kcs, wk_ref[...], bk_ref[...], kscale, n_ctx_chunks * t)

    tbl = gt_ref[0, 0]
    rid = lax.broadcasted_iota(jnp.int32, tbl.shape, 0) // MAX_CHUNKS
    lane = lax.broadcasted_iota(jnp.int32, tbl.shape, 1)
    lf = _log_sigmoid(tbl)
    pre = jnp.where(rid == 1, lf, 0.0)
    suf = jnp.where(rid == 3, lf, 0.0)
    s = 1
    while s < t:
        pre = pre + jnp.where(lane >= s, pltpu.roll(pre, s, 1), 0.0)
        suf = suf + jnp.where(lane < t - s, pltpu.roll(suf, t - s, 1), 0.0)
        s *= 2
    li = (tbl[0:MAX_CHUNKS] * LOG2E, tbl[2 * MAX_CHUNKS:3 * MAX_CHUNKS] * LOG2E)
    cum = (pre[MAX_CHUNKS:2 * MAX_CHUNKS] * LOG2E, suf[3 * MAX_CHUNKS:4 * MAX_CHUNKS] * LOG2E)
    total = (cum[0][:, t - 1:t], cum[1][:, 0:1])
    mloc = []
    for d in (0, 1):
        g = total[d] - cum[d] + li[d]
        mloc.append(jnp.max(g, axis=1, keepdims=True))
        gtab[d] = g
        rtab[d] = cum[d] - li[d]
        cumtab[d] = cum[d]

    orders = (list(range(n_chunks)),
              list(range(n_ctx_chunks - 1, -1, -1)) + list(range(n_chunks - 1, n_ctx_chunks - 1, -1)))

    a_sc, m0, m1 = ({}, {}), ({}, {}), ({}, {})
    for d in (0, 1):
        m_st = jnp.zeros((1, 1), F32)
        for c in orders[d][:-1]:
            m0[d][c] = m_st
            tot_c = total[d][c:c + 1, :]
            m1[d][c] = jnp.maximum(tot_c + m_st, mloc[d][c:c + 1, :])
            a_sc[d][c] = jnp.exp2(tot_c + m_st - m1[d][c])
            m_st = m1[d][c]
        m0[d][orders[d][-1]] = m_st

    def kv_chunk(c):
        if c < n_ctx_chunks:
            rows = pl.ds(c * t, t)
            return kcs[rows, :], vc_ref[0, rows, :]
        rows = pl.ds((c - n_ctx_chunks) * t, t)
        return ks[rows, :], v_ref[0, rows, :]

    ri = lax.broadcasted_iota(jnp.int32, (t, t), 0)
    ci = lax.broadcasted_iota(jnp.int32, (t, t), 1)
    eye = jnp.where(ri == ci, 1.0, 0.0).astype(BF16)
    ones = jnp.ones((t, dh), BF16)

    for c in range(n_chunks):
        k, v = kv_chunk(c)
        k_t = lax.dot_general(eye, k, nt, preferred_element_type=F32)
        vo = jnp.concatenate([v, ones], axis=1)
        for d in (0, 1):
            if c != orders[d][-1]:
                w = jnp.exp2(gtab[d, pl.ds(c, 1), :] - m1[d][c])
                cnl[d, c] = jnp.dot((k_t * w).astype(BF16), vo, preferred_element_type=F32)

    for d in (0, 1):
        cn = jnp.zeros((dh, 2 * dh), F32)
        for c in orders[d]:
            if c >= n_ctx_chunks:
                c0n[d, c - n_ctx_chunks] = cn.astype(BF16)
            if c != orders[d][-1]:
                cn = a_sc[d][c] * cn + cnl[d, c]

    for j in range(n_lat_chunks):
        c = j + n_ctx_chunks
        rows = pl.ds(j * t, t)
        sqk[rows, :] = lax.dot_general(qs[rows, :], ks[rows, :], nt, preferred_element_type=F32)
        for d in (0, 1):
            cumc[d, rows, :] = jnp.broadcast_to(cumtab[d, pl.ds(c, 1), :], (t, t)).T

    grp = 2
    gi = lax.broadcasted_iota(jnp.int32, (grp * t, t), 0) & (t - 1)
    gs = lax.broadcasted_iota(jnp.int32, (grp * t, t), 1)
    for d in (0, 1):
        ok = (gs <= gi) if d == 0 else (gs >= gi)
        for j0 in range(0, n_lat_chunks, grp):
            rows = pl.ds(j0 * t, grp * t)
            cs = [j + n_ctx_chunks for j in range(j0, j0 + grp)]
            rb = jnp.concatenate([jnp.broadcast_to(rtab[d, pl.ds(c, 1), :], (t, t)) for c in cs], axis=0)
            m0b = jnp.concatenate([jnp.broadcast_to(m0[d][c], (t, t)) for c in cs], axis=0)
            cum_c = cumc[d, rows, :]
            d_log = jnp.where(ok, cum_c - rb, -jnp.inf)
            a_log = cum_c + m0b
            m_t = jnp.maximum(a_log, jnp.max(d_log, axis=1, keepdims=True))
            sds[d, rows, :] = (sqk[rows, :] * jnp.exp2(d_log - m_t)).astype(BF16)
            inter_s[d, rows, :] = jnp.exp2(a_log - m_t)
            em_s[d, rows, :] = jnp.exp2(-m_t)

    gain = gain_ref[...]
    for j in range(n_lat_chunks):
        rows = pl.ds(j * t, t)
        q = qs[rows, :]
        vo = jnp.concatenate([v_ref[0, rows, :], ones], axis=1)
        hid = None
        for d in (0, 1):
            sv = jnp.dot(sds[d, rows, :], vo, preferred_element_type=F32)
            qc = jnp.dot(q, c0n[d, j], preferred_element_type=F32)
            inter = inter_s[d, rows, :]
            num = sv[:, :dh] + inter * qc[:, :dh]
            den = sv[:, dh:] + inter * qc[:, dh:]
            h_d = num / jnp.maximum(jnp.abs(den), em_s[d, rows, :])
            hid = h_d if hid is None else hid + h_d
        hn = hid * lax.rsqrt(jnp.mean(hid * hid, axis=-1, keepdims=True) + EPS) * gain
        y_ref[0, rows, :] = (hn * _sigmoid(o_ref[0, rows, :].astype(F32))).astype(BF16)


def _mlstm(qk, v, o, kc, vc, gate_tbl, conv_w, conv_b, gain):
    b, l, _ = v.shape
    n_ctx = kc.shape[1]
    dh = MLSTM_DH
    nh = MLSTM_HEADS
    n_ctx_chunks, n_lat_chunks = n_ctx // MCHUNK, l // MCHUNK
    seq = lambda col0: pl.BlockSpec((1, l, dh), lambda bi, h: (bi, 0, col0 + h))
    ctx = pl.BlockSpec((1, n_ctx, dh), lambda bi, h: (bi, 0, h))
    n_chunks = n_ctx_chunks + n_lat_chunks
    scratch = [((l, dh), BF16), ((l, dh), BF16), ((n_ctx, dh), BF16),
               ((2, MAX_CHUNKS, MCHUNK), F32), ((2, MAX_CHUNKS, MCHUNK), F32),
               ((2, MAX_CHUNKS, MCHUNK), F32),
               ((2, n_chunks, dh, 2 * dh), F32), ((2, n_lat_chunks, dh, 2 * dh), BF16),
               ((2, l, MCHUNK), F32), ((l, MCHUNK), F32), ((2, l, MCHUNK), BF16),
               ((2, l, MCHUNK), F32), ((2, l, MCHUNK), F32)]
    blk = (5 * _nbytes((l, dh), BF16) + 2 * _nbytes((n_ctx, dh), BF16) + _nbytes((V7X_LANES, V7X_LANES), F32)
           + sum(_nbytes(s, dt) for s, dt in scratch) // 2)
    return pl.pallas_call(
        functools.partial(_mlstm_body, n_ctx_chunks=n_ctx_chunks, n_lat_chunks=n_lat_chunks),
        grid=(b, nh),
        in_specs=[seq(0), seq(nh), seq(0), seq(0), ctx, ctx,
                  pl.BlockSpec((1, 1, 4 * MAX_CHUNKS, MCHUNK), lambda bi, h: (bi, h, 0, 0)),
                  pl.BlockSpec((3, dh), lambda bi, h: (0, h)),
                  pl.BlockSpec((1, dh), lambda bi, h: (0, h)),
                  pl.BlockSpec((3, dh), lambda bi, h: (0, nh + h)),
                  pl.BlockSpec((1, dh), lambda bi, h: (0, nh + h)),
                  pl.BlockSpec((1, dh), lambda bi, h: (0, h))],
        out_specs=pl.BlockSpec((1, l, dh), lambda bi, h: (bi, 0, h)),
        out_shape=jax.ShapeDtypeStruct((b, l, MLSTM_W), BF16),
        scratch_shapes=[pltpu.VMEM(s, dt) for s, dt in scratch],
        compiler_params=_params(blk, 2),
        name="mlstm",
    )(qk, qk, v, o, kc, vc, gate_tbl, conv_w, conv_b, conv_w, conv_b, gain)


def _attn_body(sink_ref, bias_ref, q_ref, k_ref, v_ref, kc_ref, vc_ref, o_ref, s_scr, p_scr, e_scr, *, seq_len):
    blk = ATTN_BLOCK
    n_blocks = seq_len // blk
    n_ctx = kc_ref.shape[1]
    rows = GQA_GROUP * blk
    nt = (((1,), (1,)), ((), ()))
    chains = [(qb, g) for qb in range(ATTN_BLOCKS_PER_STEP) for g in range(KV_HEADS)]

    def block_start(qb):
        i = pl.program_id(1) * ATTN_BLOCKS_PER_STEP + qb
        return i, pl.multiple_of(jnp.clip((i - 1) * blk, 0, seq_len - BAND), blk)

    for ci, (qb, g) in enumerate(chains):
        i, start = block_start(qb)
        kv_cols = pl.ds(g * ATTN_DH, ATTN_DH)
        q = jnp.concatenate([q_ref[0, pl.ds(qb * blk, blk), pl.ds((g * GQA_GROUP + j) * ATTN_DH, ATTN_DH)]
                             for j in range(GQA_GROUP)], axis=0)
        bias = bias_ref[jnp.where(i == 0, 0, jnp.where(i == n_blocks - 1, 2, 1))]
        s_scr[ci, :, pl.ds(0, BAND)] = (
            lax.dot_general(q, k_ref[0, pl.ds(start, BAND), kv_cols], nt, preferred_element_type=F32) + bias)
        s_scr[ci, :, pl.ds(BAND, n_ctx)] = lax.dot_general(q, kc_ref[0, :, kv_cols], nt, preferred_element_type=F32)

    for ci, (qb, g) in enumerate(chains):
        s = s_scr[ci]
        sink = jnp.concatenate([jnp.full((blk, V7X_LANES), sink_ref[g * GQA_GROUP + j] * LOG2E, F32)
                                for j in range(GQA_GROUP)], axis=0)
        m = jnp.maximum(jnp.broadcast_to(jnp.max(s, axis=1, keepdims=True), (rows, V7X_LANES)), sink)
        p_scr[ci] = jnp.exp2((s - jnp.tile(m, (1, (BAND + n_ctx) // V7X_LANES))).astype(BF16))
        e_scr[ci] = jnp.exp2(sink - m)

    ones_b = jnp.ones((BAND, ATTN_DH), BF16)
    ones_c = jnp.ones((n_ctx, ATTN_DH), BF16)
    for ci, (qb, g) in enumerate(chains):
        _, start = block_start(qb)
        kv_cols = pl.ds(g * ATTN_DH, ATTN_DH)
        vb = jnp.concatenate([v_ref[0, pl.ds(start, BAND), kv_cols], ones_b], axis=1)
        vc = jnp.concatenate([vc_ref[0, :, kv_cols], ones_c], axis=1)
        acc = (jnp.dot(p_scr[ci, :, pl.ds(0, BAND)], vb, preferred_element_type=F32)
               + jnp.dot(p_scr[ci, :, pl.ds(BAND, n_ctx)], vc, preferred_element_type=F32))
        out = (acc / (pltpu.roll(acc, ATTN_DH, 1) + e_scr[ci]))[:, :ATTN_DH]
        for j in range(GQA_GROUP):
            h = g * GQA_GROUP + j
            o_ref[0, pl.ds(qb * blk, blk), pl.ds(h * ATTN_DH, ATTN_DH)] = out[j * blk:(j + 1) * blk].astype(BF16)


def _band_bias(seq_len):
    row = jnp.arange(ATTN_BLOCK)[:, None]
    col = jnp.arange(BAND)[None, :]
    n_blocks = seq_len // ATTN_BLOCK
    tables = []
    for i in (0, 1, n_blocks - 1):
        start = min(max((i - 1) * ATTN_BLOCK, 0), seq_len - BAND)
        ok = jnp.abs(start + col - (i * ATTN_BLOCK + row)) <= WINDOW
        tables.append(jnp.tile(jnp.where(ok, 0.0, -jnp.inf).astype(F32), (GQA_GROUP, 1)))
    return jnp.stack(tables)


def _attn(sink, q, k, v, kc, vc):
    b, l, _ = q.shape
    n_ctx = kc.shape[1]
    rows = ATTN_BLOCKS_PER_STEP * ATTN_BLOCK
    bias = _band_bias(l)
    n_chains = KV_HEADS * ATTN_BLOCKS_PER_STEP
    stacked = GQA_GROUP * ATTN_BLOCK
    scratch = [((n_chains, stacked, BAND + n_ctx), F32), ((n_chains, stacked, BAND + n_ctx), BF16),
               ((n_chains, stacked, V7X_LANES), F32)]
    blk = (2 * _nbytes((rows, ATTN_W), BF16) + 2 * _nbytes((l, KV_W), BF16) + 2 * _nbytes((n_ctx, KV_W), BF16)
           + _nbytes(bias.shape, F32) + sum(_nbytes(s, dt) for s, dt in scratch) // 2)
    return pl.pallas_call(
        functools.partial(_attn_body, seq_len=l),
        grid=(b, l // rows),
        in_specs=[pl.BlockSpec(memory_space=pltpu.SMEM),
                  pl.BlockSpec(bias.shape, lambda bi, i: (0, 0, 0)),
                  pl.BlockSpec((1, rows, ATTN_W), lambda bi, i: (bi, i, 0)),
                  pl.BlockSpec((1, l, KV_W), lambda bi, i: (bi, 0, 0)),
                  pl.BlockSpec((1, l, KV_W), lambda bi, i: (bi, 0, 0)),
                  pl.BlockSpec((1, n_ctx, KV_W), lambda bi, i: (bi, 0, 0)),
                  pl.BlockSpec((1, n_ctx, KV_W), lambda bi, i: (bi, 0, 0))],
        out_specs=pl.BlockSpec((1, rows, ATTN_W), lambda bi, i: (bi, i, 0)),
        out_shape=jax.ShapeDtypeStruct((b, l, ATTN_W), BF16),
        scratch_shapes=[pltpu.VMEM(s, dt) for s, dt in scratch],
        compiler_params=_params(blk, 2),
        name="attn",
    )(sink, bias, q, k, v, kc, vc)


def _outproj_body(ym_ref, ya_ref, x_ref, wm_ref, wa_ref, ga_ref, gain_ref, sh_ref, sc_ref, x1_ref, hf_ref):
    acc = (jnp.dot(ym_ref[0], wm_ref[...], preferred_element_type=F32)
           + jnp.dot(ya_ref[0], wa_ref[...], preferred_element_type=F32))
    x1 = x_ref[0] + ga_ref[0] * acc
    x1_ref[0] = x1
    hf_ref[0] = _norm_modulate(x1, gain_ref[...], sh_ref[0], sc_ref[0]).astype(BF16)


def _outproj(ym, ya, x, wm, wa, mod3, gain):
    b, l, d = x.shape
    tm = ROW_TILE
    mod_spec = lambda piece: pl.BlockSpec((1, 1, d), lambda bi, i: (bi, 0, piece))
    blk = (2 * _nbytes((tm, MLSTM_W), BF16) + 3 * _nbytes((tm, d), F32) + _nbytes((tm, d), BF16)
           + 2 * _nbytes(wm.shape, BF16))
    return pl.pallas_call(
        _outproj_body,
        grid=(b, l // tm),
        in_specs=[pl.BlockSpec((1, tm, MLSTM_W), lambda bi, i: (bi, i, 0)),
                  pl.BlockSpec((1, tm, ATTN_W), lambda bi, i: (bi, i, 0)),
                  pl.BlockSpec((1, tm, d), lambda bi, i: (bi, i, 0)),
                  pl.BlockSpec(wm.shape, lambda bi, i: (0, 0)),
                  pl.BlockSpec(wa.shape, lambda bi, i: (0, 0)),
                  mod_spec(2),
                  pl.BlockSpec((1, d), lambda bi, i: (0, 0)),
                  mod_spec(3), mod_spec(4)],
        out_specs=[pl.BlockSpec((1, tm, d), lambda bi, i: (bi, i, 0)),
                   pl.BlockSpec((1, tm, d), lambda bi, i: (bi, i, 0))],
        out_shape=[jax.ShapeDtypeStruct((b, l, d), F32), jax.ShapeDtypeStruct((b, l, d), BF16)],
        compiler_params=_params(blk, 2),
        name="outproj",
    )(ym, ya, x, wm, wa, mod3, gain, mod3, mod3)


def _gelu_tanh(x):
    return 0.5 * x * (1.0 + jnp.tanh(0.7978845608028654 * (x + 0.044715 * (x * x * x))))


def _ffn_body(h_ref, hp_ref, hn_ref, x1_ref, gf_ref, wup_ref, cw_ref, cb_ref, wdn_ref, fn_ref, o_ref, hs, acc):
    i = pl.program_id(1)
    tm = h_ref.shape[1]
    ext = tm + 2 * HALO
    hs[pl.ds(0, HALO), :] = jnp.where(i > 0, hp_ref[0], jnp.zeros_like(hp_ref[0]))
    hs[pl.ds(HALO, tm), :] = h_ref[0]
    hs[pl.ds(HALO + tm, HALO), :] = jnp.where(i < pl.num_programs(1) - 1, hn_ref[0], jnp.zeros_like(hn_ref[0]))
    for c, col0 in enumerate(range(0, D_FF, FF_CHUNK)):
        width = min(FF_CHUNK, D_FF - col0)
        cols = pl.ds(col0, width)
        a = jnp.dot(hs[...], wup_ref[:, cols], preferred_element_type=F32)
        val = jnp.dot(h_ref[0], wup_ref[:, pl.ds(D_FF + col0, width)], preferred_element_type=F32)
        a_prev = pltpu.roll(a, 1, 0)[HALO:HALO + tm]
        a_next = pltpu.roll(a, ext - 1, 0)[HALO:HALO + tm]
        conv = (cw_ref[0:1, cols] * a_prev + cw_ref[1:2, cols] * a[HALO:HALO + tm]
                + cw_ref[2:3, cols] * a_next + cb_ref[:, cols])
        gated = (_gelu_tanh(conv) * val).astype(BF16)
        part = jnp.dot(gated, wdn_ref[cols, :], preferred_element_type=F32)
        if c == 0:
            acc[...] = part
        else:
            acc[...] += part
    x2 = x1_ref[0] + gf_ref[0] * acc[...]
    ms = jnp.mean(x2 * x2, axis=-1, keepdims=True)
    o_ref[0] = x2 * lax.rsqrt(ms + EPS) * fn_ref[...]


def _ffn(hf, x1, mod3, w_up, conv_w, conv_b, w_down, final_norm):
    b, l, d = x1.shape
    tm = ROW_TILE
    per = tm // HALO
    last = l // HALO - 1
    blk = (_nbytes((tm, d), BF16) + 2 * _nbytes((tm, d), F32) + _nbytes(w_up.shape, BF16)
           + _nbytes(w_down.shape, BF16) + 4 * _nbytes((tm + 2 * HALO, FF_CHUNK), F32))
    return pl.pallas_call(
        _ffn_body,
        grid=(b, l // tm),
        in_specs=[pl.BlockSpec((1, tm, d), lambda bi, i: (bi, i, 0)),
                  pl.BlockSpec((1, HALO, d), lambda bi, i: (bi, jnp.maximum(i * per - 1, 0), 0)),
                  pl.BlockSpec((1, HALO, d), lambda bi, i: (bi, jnp.minimum((i + 1) * per, last), 0)),
                  pl.BlockSpec((1, tm, d), lambda bi, i: (bi, i, 0)),
                  pl.BlockSpec((1, 1, d), lambda bi, i: (bi, 0, 5)),
                  pl.BlockSpec(w_up.shape, lambda bi, i: (0, 0)),
                  pl.BlockSpec(conv_w.shape, lambda bi, i: (0, 0)),
                  pl.BlockSpec(conv_b.shape, lambda bi, i: (0, 0)),
                  pl.BlockSpec(w_down.shape, lambda bi, i: (0, 0)),
                  pl.BlockSpec((1, d), lambda bi, i: (0, 0))],
        out_specs=pl.BlockSpec((1, tm, d), lambda bi, i: (bi, i, 0)),
        out_shape=jax.ShapeDtypeStruct((b, l, d), F32),
        scratch_shapes=[pltpu.VMEM((tm + 2 * HALO, d), BF16), pltpu.VMEM((tm, d), F32)],
        compiler_params=_params(blk, 2),
        name="ffn",
    )(hf, hf, hf, x1, mod3, w_up, conv_w, conv_b, w_down, final_norm)


def _rope_tables(n_tokens):
    pos = jnp.arange(n_tokens)
    r = (pos // GRID_W).astype(F32)
    c = (pos % GRID_W).astype(F32)
    inv = ROPE_BASE ** (-jnp.arange(ROPE_AXIS_PAIRS, dtype=F32) / ROPE_AXIS_PAIRS)
    ar, ac = r[:, None] * inv, c[:, None] * inv
    zero = jnp.zeros_like(ar)
    cos = jnp.concatenate([jnp.cos(ar), jnp.cos(ar), jnp.cos(ac), jnp.cos(ac)], axis=1)
    sina = jnp.concatenate([-jnp.sin(ar), zero, -jnp.sin(ac), zero], axis=1)
    sinb = jnp.concatenate([zero, jnp.sin(ar), zero, jnp.sin(ac)], axis=1)
    rep = V7X_LANES // ATTN_DH
    return tuple(jnp.tile(tb, (1, rep)) for tb in (cos, sina, sinb))


def _identity_rope(n_tokens):
    return (jnp.ones((n_tokens, V7X_LANES), F32), jnp.zeros((n_tokens, V7X_LANES), F32),
            jnp.zeros((n_tokens, V7X_LANES), F32))


def _gate_table(gates_ctx, gates_x):
    g = jnp.concatenate([gates_ctx, gates_x], axis=1)
    b, n, _ = g.shape
    nc = n // MCHUNK
    g = g.reshape(b, nc, MCHUNK, 4, MLSTM_HEADS).transpose(0, 4, 3, 1, 2)
    g = jnp.pad(g, ((0, 0), (0, 0), (0, 0), (0, MAX_CHUNKS - nc), (0, 0)))
    return g.reshape(b, MLSTM_HEADS, 4 * MAX_CHUNKS, MCHUNK)


def kernel(x, c, ctx, c_ctx, w_ada, b_ada, norm_mix, norm_ffn, w_in, gate_b, qk_conv_w, qk_conv_b, mlstm_norm,
           attn_sink, w_out, w_up, ffn_conv_w, ffn_conv_b, w_down, final_norm):
    b, l, d = x.shape
    n_ctx = ctx.shape[1]
    assert w_ada.shape[0] == 1, "single-layer stack"
    assert l % ROW_TILE == 0 and l % MCHUNK == 0 and n_ctx % MCHUNK == 0 and l >= BAND
    assert l % (ATTN_BLOCKS_PER_STEP * ATTN_BLOCK) == 0 and MCHUNK == V7X_LANES
    assert (l + n_ctx) // MCHUNK <= MAX_CHUNKS

    rows = -(-(b + 1) // V7X_SUBLANES) * V7X_SUBLANES
    cvec = jnp.zeros((rows, d), F32).at[:b].set(c).at[b].set(c_ctx)
    mod = _ada(cvec, w_ada[0], b_ada[0][None, :])
    mod_x = mod[:b].reshape(b, 1, 6 * d)
    mod_c = mod[b:b + 1].reshape(1, 1, 6 * d)

    w = w_in[0].astype(BF16)
    o0 = 2 * MLSTM_W
    o1, o2 = o0 + MLSTM_W, o0 + 2 * MLSTM_W
    o3 = o2 + N_GATES
    o4, o5 = o3 + ATTN_W, o3 + ATTN_W + KV_W
    w_qk, w_k, w_v, w_o = w[:, :o0], w[:, MLSTM_W:o0], w[:, o0:o1], w[:, o1:o2]
    w_g = jnp.pad(w[:, o2:o3], ((0, 0), (0, V7X_LANES - N_GATES)))
    w_qa, w_ka, w_va = w[:, o3:o4], w[:, o4:o5], w[:, o5:]
    gb = gate_b[0][None, :]
    gain_mix = norm_mix[0][None, :]

    kc_raw, vc, gates_c, ka_c, va_c = _inproj(
        ctx, gain_mix, mod_c, 0, 1, False, _identity_rope(n_ctx), gb,
        [w_k, w_v, w_g, w_ka, w_va], ["plain", "plain", "gates", "plain", "plain"], "inproj_ctx")
    qk_raw, v_m, o_m, gates_x, q_a, k_a, v_a = _inproj(
        x, gain_mix, mod_x, 0, 1, True, _rope_tables(l), gb,
        [w_qk, w_v, w_o, w_g, w_qa, w_ka, w_va],
        ["plain", "plain", "plain", "gates", "rope_q", "rope_k", "plain"], "inproj_x")

    y_m = _mlstm(qk_raw, v_m, o_m, kc_raw, vc, _gate_table(gates_c, gates_x),
                 qk_conv_w[0], qk_conv_b[0][None, :], mlstm_norm[0][None, :])
    y_a = _attn(attn_sink[0], q_a, k_a, v_a, ka_c, va_c)

    wo = w_out[0].astype(BF16)
    x1, hf = _outproj(y_m, y_a, x, wo[:MLSTM_W], wo[MLSTM_W:], mod_x, norm_ffn[0][None, :])
    return _ffn(hf, x1, mod_x, w_up[0].astype(BF16), ffn_conv_w[0], ffn_conv_b[0][None, :],
                w_down[0].astype(BF16), final_norm[None, :])
```

```python
import functools

import jax
import jax.numpy as jnp
from jax import lax
from jax.experimental import pallas as pl
from jax.experimental.pallas import tpu as pltpu

F32 = jnp.float32
BF16 = jnp.bfloat16

D_MODEL = 1024
GRID_W = 64
MLSTM_HEADS = 4
MLSTM_DH = 128
MLSTM_W = MLSTM_HEADS * MLSTM_DH
N_GATES = 4 * MLSTM_HEADS
ATTN_HEADS = 8
KV_HEADS = 2
ATTN_DH = 64
ATTN_W = ATTN_HEADS * ATTN_DH
KV_W = KV_HEADS * ATTN_DH
GQA_GROUP = ATTN_HEADS // KV_HEADS
WINDOW = 128
ROPE_BASE = 10000.0
ROPE_AXIS_PAIRS = ATTN_DH // 4
D_FF = 2816
EPS = 1e-6
LOG2E = 1.4426950408889634

V7X_LANES = 128
V7X_SUBLANES = 8
V7X_VMEM_BYTES = 64 * 1024 * 1024

ROW_TILE = 512
ADA_COL_TILE = 1536
MCHUNK = 128
MAX_CHUNKS = 32
ATTN_BLOCK = 128
ATTN_BLOCKS_PER_STEP = 2
BAND = 3 * ATTN_BLOCK
FF_CHUNK = 256
HALO = 2 * V7X_SUBLANES


def _vmem_limit(block_bytes):
    return int(min(V7X_VMEM_BYTES * 7 // 8, 2 * block_bytes + 16 * 1024 * 1024))


def _params(block_bytes, n_axes):
    return pltpu.CompilerParams(dimension_semantics=("arbitrary",) * n_axes,
                                vmem_limit_bytes=_vmem_limit(block_bytes))


def _nbytes(shape, dtype):
    n = 1
    for s in shape:
        n *= s
    return n * jnp.dtype(dtype).itemsize


def _ada_body(c_ref, w_ref, b_ref, o_ref):
    c = c_ref[...]
    s = (c * jax.nn.sigmoid(c)).astype(BF16)
    o_ref[...] = jnp.dot(s, w_ref[...].astype(BF16), preferred_element_type=F32) + b_ref[...]


def _ada(cvec, w_ada, b_ada):
    rows, d = cvec.shape
    n = w_ada.shape[1]
    blk = _nbytes((d, ADA_COL_TILE), F32) + _nbytes((rows, d), F32) + 2 * _nbytes((rows, ADA_COL_TILE), F32)
    return pl.pallas_call(
        _ada_body,
        grid=(n // ADA_COL_TILE,),
        in_specs=[pl.BlockSpec((rows, d), lambda j: (0, 0)),
                  pl.BlockSpec((d, ADA_COL_TILE), lambda j: (0, j)),
                  pl.BlockSpec((1, ADA_COL_TILE), lambda j: (0, j))],
        out_specs=pl.BlockSpec((rows, ADA_COL_TILE), lambda j: (0, j)),
        out_shape=jax.ShapeDtypeStruct((rows, n), F32),
        compiler_params=_params(blk, 1),
        name="ada",
    )(cvec, w_ada, b_ada)


def _norm_modulate(x, gain, shift, scale):
    ms = jnp.mean(x * x, axis=-1, keepdims=True)
    return (x * lax.rsqrt(ms + EPS) * gain) * (1.0 + scale) + shift


def _inproj_body(x_ref, gain_ref, sh_ref, sc_ref, cos_ref, sina_ref, sinb_ref, gb_ref, *refs, kinds):
    n = len(kinds)
    w_refs, o_refs = refs[:n], refs[n:]
    hb = _norm_modulate(x_ref[0], gain_ref[...], sh_ref[0], sc_ref[0]).astype(BF16)
    for kind, w_ref, o_ref in zip(kinds, w_refs, o_refs):
        if kind == "gates":
            u = jnp.dot(hb, w_ref[...], preferred_element_type=F32)
            o_ref[0] = u[:, :N_GATES] + gb_ref[...]
        elif kind == "plain":
            o_ref[0] = jnp.dot(hb, w_ref[...], preferred_element_type=F32).astype(BF16)
        else:
            u_all = jnp.dot(hb, w_ref[...], preferred_element_type=F32)
            n_rot = u_all.shape[1] // V7X_LANES if kind == "rope_q" else KV_W // V7X_LANES
            for j in range(u_all.shape[1] // V7X_LANES):
                cols = pl.ds(j * V7X_LANES, V7X_LANES)
                u = u_all[:, j * V7X_LANES:(j + 1) * V7X_LANES]
                if j < n_rot:
                    u = (u * cos_ref[...]
                         + pltpu.roll(u, V7X_LANES - ROPE_AXIS_PAIRS, 1) * sina_ref[...]
                         + pltpu.roll(u, ROPE_AXIS_PAIRS, 1) * sinb_ref[...])
                    if kind == "rope_q":
                        u = u * (ATTN_DH ** -0.5 * LOG2E)
                o_ref[0, :, cols] = u.astype(BF16)


def _inproj(x, gain, mod3, shift_idx, scale_idx, per_batch_mod, rope, gate_b, weights, kinds, name):
    b, n, d = x.shape
    tm = min(ROW_TILE, n)
    cos, sina, sinb = rope
    mod_spec = lambda piece: pl.BlockSpec(
        (1, 1, d), (lambda bi, i: (bi, 0, piece)) if per_batch_mod else (lambda bi, i: (0, 0, piece)))
    in_specs = [pl.BlockSpec((1, tm, d), lambda bi, i: (bi, i, 0)),
                pl.BlockSpec((1, d), lambda bi, i: (0, 0)),
                mod_spec(shift_idx), mod_spec(scale_idx),
                pl.BlockSpec((tm, V7X_LANES), lambda bi, i: (i, 0)),
                pl.BlockSpec((tm, V7X_LANES), lambda bi, i: (i, 0)),
                pl.BlockSpec((tm, V7X_LANES), lambda bi, i: (i, 0)),
                pl.BlockSpec((1, N_GATES), lambda bi, i: (0, 0))]
    out_specs, out_shapes = [], []
    blk = _nbytes((tm, d), F32) + 3 * _nbytes((tm, V7X_LANES), F32)
    for kind, w in zip(kinds, weights):
        width = N_GATES if kind == "gates" else w.shape[1]
        dt = F32 if kind == "gates" else BF16
        in_specs.append(pl.BlockSpec(w.shape, lambda bi, i: (0, 0)))
        out_specs.append(pl.BlockSpec((1, tm, width), lambda bi, i: (bi, i, 0)))
        out_shapes.append(jax.ShapeDtypeStruct((b, n, width), dt))
        blk += _nbytes(w.shape, BF16) + _nbytes((tm, width), dt) + _nbytes((tm, w.shape[1]), F32)
    return pl.pallas_call(
        functools.partial(_inproj_body, kinds=tuple(kinds)),
        grid=(b, n // tm),
        in_specs=in_specs,
        out_specs=out_specs,
        out_shape=out_shapes,
        compiler_params=_params(blk, 2),
        name=name,
    )(x, gain, mod3, mod3, cos, sina, sinb, gate_b, *weights)


def _silu(y):
    h = 0.5 * y
    return h + h * jnp.tanh(h)


def _sigmoid(y):
    return 0.5 + 0.5 * jnp.tanh(0.5 * y)


def _conv_silu_chunks(src_ref, dst_ref, w, b, scale, n_rows):
    t = MCHUNK
    ri = lax.broadcasted_iota(jnp.int32, (t, 2 * t), 0)
    ci = lax.broadcasted_iota(jnp.int32, (t, 2 * t), 1)
    shift_prev = jnp.where(ci == ri + t - 1, 1.0, 0.0).astype(BF16)
    shift_next = jnp.where(ci == ri + 1, 1.0, 0.0).astype(BF16)
    zeros = jnp.zeros((t, src_ref.shape[-1]), BF16)
    n = n_rows // t
    for c in range(n):
        x = src_ref[0, pl.ds(c * t, t), :]
        before = src_ref[0, pl.ds((c - 1) * t, t), :] if c > 0 else zeros
        after = src_ref[0, pl.ds((c + 1) * t, t), :] if c < n - 1 else zeros
        prev = jnp.dot(shift_prev, jnp.concatenate([before, x], axis=0), preferred_element_type=F32)
        nxt = jnp.dot(shift_next, jnp.concatenate([x, after], axis=0), preferred_element_type=F32)
        y = w[0:1, :] * prev + w[1:2, :] * x.astype(F32) + w[2:3, :] * nxt + b
        act = _silu(y) if scale == 1.0 else _silu(y) * scale
        dst_ref[pl.ds(c * t, t), :] = act.astype(BF16)


def _log_sigmoid(x):
    return jnp.minimum(x, 0.0) - jnp.log1p(jnp.exp(-jnp.abs(x)))


def _mlstm_body(q_ref, k_ref, v_ref, o_ref, kc_ref, vc_ref, gt_ref, wq_ref, bq_ref, wk_ref, bk_ref, gain_ref,
                y_ref, qs, ks, kcs, gtab, rtab, cumtab, cnl, c0n, cumc, sqk, sds, inter_s, em_s,
                *, n_ctx_chunks, n_lat_chunks):
    t = MCHUNK
    dh = MLSTM_DH
    n_chunks = n_ctx_chunks + n_lat_chunks
    kscale = MLSTM_DH ** -0.5
    nt = (((1,), (1,)), ((), ()))

    _conv_silu_chunks(q_ref, qs, wq_ref[...], bq_ref[...], 1.0, n_lat_chunks * t)
    _conv_silu_chunks(k_ref, ks, wk_ref[...], bk_ref[...], kscale, n_lat_chunks * t)
    _conv_silu_chunks(kc_ref, kcs, wk_ref[...], bk_ref[...], kscale, n_ctx_chunks * t)

    tbl = gt_ref[0, 0]
    rid = lax.broadcasted_iota(jnp.int32, tbl.shape, 0) // MAX_CHUNKS
    lane = lax.broadcasted_iota(jnp.int32, tbl.shape, 1)
    lf = _log_sigmoid(tbl)
    pre = jnp.where(rid == 1, lf, 0.0)
    suf = jnp.where(rid == 3, lf, 0.0)
    s = 1
    while s < t:
        pre = pre + jnp.where(lane >= s, pltpu.roll(pre, s, 1), 0.0)
        suf = suf + jnp.where(lane < t - s, pltpu.roll(suf, t - s, 1), 0.0)
        s *= 2
    li = (tbl[0:MAX_CHUNKS] * LOG2E, tbl[2 * MAX_CHUNKS:3 * MAX_CHUNKS] * LOG2E)
    cum = (pre[MAX_CHUNKS:2 * MAX_CHUNKS] * LOG2E, suf[3 * MAX_CHUNKS:4 * MAX_CHUNKS] * LOG2E)
    total = (cum[0][:, t - 1:t], cum[1][:, 0:1])
    mloc = []
    for d in (0, 1):
        g = total[d] - cum[d] + li[d]
        mloc.append(jnp.max(g, axis=1, keepdims=True))
        gtab[d] = g
        rtab[d] = cum[d] - li[d]
        cumtab[d] = cum[d]

    orders = (list(range(n_chunks)),
              list(range(n_ctx_chunks - 1, -1, -1)) + list(range(n_chunks - 1, n_ctx_chunks - 1, -1)))

    a_sc, m0, m1 = ({}, {}), ({}, {}), ({}, {})
    for d in (0, 1):
        m_st = jnp.zeros((1, 1), F32)
        for c in orders[d][:-1]:
            m0[d][c] = m_st
            tot_c = total[d][c:c + 1, :]
            m1[d][c] = jnp.maximum(tot_c + m_st, mloc[d][c:c + 1, :])
            a_sc[d][c] = jnp.exp2(tot_c + m_st - m1[d][c])
            m_st = m1[d][c]
        m0[d][orders[d][-1]] = m_st

    def kv_chunk(c):
        if c < n_ctx_chunks:
            rows = pl.ds(c * t, t)
            return kcs[rows, :], vc_ref[0, rows, :]
        rows = pl.ds((c - n_ctx_chunks) * t, t)
        return ks[rows, :], v_ref[0, rows, :]

    ri = lax.broadcasted_iota(jnp.int32, (t, t), 0)
    ci = lax.broadcasted_iota(jnp.int32, (t, t), 1)
    eye = jnp.where(ri == ci, 1.0, 0.0).astype(BF16)
    ones = jnp.ones((t, dh), BF16)

    for c in range(n_chunks):
        k, v = kv_chunk(c)
        k_t = lax.dot_general(eye, k, nt, preferred_element_type=F32)
        vo = jnp.concatenate([v, ones], axis=1)
        for d in (0, 1):
            if c != orders[d][-1]:
                w = jnp.exp2(gtab[d, pl.ds(c, 1), :] - m1[d][c])
                cnl[d, c] = jnp.dot((k_t * w).astype(BF16), vo, preferred_element_type=F32)

    for d in (0, 1):
        cn = jnp.zeros((dh, 2 * dh), F32)
        for c in orders[d]:
            if c >= n_ctx_chunks:
                c0n[d, c - n_ctx_chunks] = cn.astype(BF16)
            if c != orders[d][-1]:
                cn = a_sc[d][c] * cn + cnl[d, c]

    for j in range(n_lat_chunks):
        c = j + n_ctx_chunks
        rows = pl.ds(j * t, t)
        sqk[rows, :] = lax.dot_general(qs[rows, :], ks[rows, :], nt, preferred_element_type=F32)
        for d in (0, 1):
            cumc[d, rows, :] = jnp.broadcast_to(cumtab[d, pl.ds(c, 1), :], (t, t)).T

    grp = 2
    gi = lax.broadcasted_iota(jnp.int32, (grp * t, t), 0) & (t - 1)
    gs = lax.broadcasted_iota(jnp.int32, (grp * t, t), 1)
    for d in (0, 1):
        ok = (gs <= gi) if d == 0 else (gs >= gi)
        for j0 in range(0, n_lat_chunks, grp):
            rows = pl.ds(j0 * t, grp * t)
            cs = [j + n_ctx_chunks for j in range(j0, j0 + grp)]
            rb = jnp.concatenate([jnp.broadcast_to(rtab[d, pl.ds(c, 1), :], (t, t)) for c in cs], axis=0)
            m0b = jnp.concatenate([jnp.broadcast_to(m0[d][c], (t, t)) for c in cs], axis=0)
            cum_c = cumc[d, rows, :]
            d_log = jnp.where(ok, cum_c - rb, -jnp.inf)
            a_log = cum_c + m0b
            m_t = jnp.maximum(a_log, jnp.max(d_log, axis=1, keepdims=True))
            sds[d, rows, :] = (sqk[rows, :] * jnp.exp2(d_log - m_t)).astype(BF16)
            inter_s[d, rows, :] = jnp.exp2(a_log - m_t)
            em_s[d, rows, :] = jnp.exp2(-m_t)

    gain = gain_ref[...]
    for j in range(n_lat_chunks):
        rows = pl.ds(j * t, t)
        q = qs[rows, :]
        vo = jnp.concatenate([v_ref[0, rows, :], ones], axis=1)
        hid = None
        for d in (0, 1):
            sv = jnp.dot(sds[d, rows, :], vo, preferred_element_type=F32)
            qc = jnp.dot(q, c0n[d, j], preferred_element_type=F32)
            inter = inter_s[d, rows, :]
            num = sv[:, :dh] + inter * qc[:, :dh]
            den = sv[:, dh:] + inter * qc[:, dh:]
            h_d = num / jnp.maximum(jnp.abs(den), em_s[d, rows, :])
            hid = h_d if hid is None else hid + h_d
        hn = hid * lax.rsqrt(jnp.mean(hid * hid, axis=-1, keepdims=True) + EPS) * gain
        y_ref[0, rows, :] = (hn * _sigmoid(o_ref[0, rows, :].astype(F32))).astype(BF16)


def _mlstm(qkvo, kv_ctx, gate_tbl, conv_w, conv_b, gain):
    b, l, _ = qkvo.shape
    n_ctx = kv_ctx.shape[1]
    dh = MLSTM_DH
    nh = MLSTM_HEADS
    n_ctx_chunks, n_lat_chunks = n_ctx // MCHUNK, l // MCHUNK
    seq = lambda col0: pl.BlockSpec((1, l, dh), lambda bi, h: (bi, 0, col0 + h))
    ctx = lambda col0: pl.BlockSpec((1, n_ctx, dh), lambda bi, h: (bi, 0, col0 + h))
    n_chunks = n_ctx_chunks + n_lat_chunks
    scratch = [((l, dh), BF16), ((l, dh), BF16), ((n_ctx, dh), BF16),
               ((2, MAX_CHUNKS, MCHUNK), F32), ((2, MAX_CHUNKS, MCHUNK), F32),
               ((2, MAX_CHUNKS, MCHUNK), F32),
               ((2, n_chunks, dh, 2 * dh), F32), ((2, n_lat_chunks, dh, 2 * dh), BF16),
               ((2, l, MCHUNK), F32), ((l, MCHUNK), F32), ((2, l, MCHUNK), BF16),
               ((2, l, MCHUNK), F32), ((2, l, MCHUNK), F32)]
    blk = (5 * _nbytes((l, dh), BF16) + 2 * _nbytes((n_ctx, dh), BF16) + _nbytes((V7X_LANES, V7X_LANES), F32)
           + sum(_nbytes(s, dt) for s, dt in scratch) // 2)
    return pl.pallas_call(
        functools.partial(_mlstm_body, n_ctx_chunks=n_ctx_chunks, n_lat_chunks=n_lat_chunks),
        grid=(b, nh),
        in_specs=[seq(0), seq(nh), seq(2 * nh), seq(3 * nh), ctx(0), ctx(nh),
                  pl.BlockSpec((1, 1, 4 * MAX_CHUNKS, MCHUNK), lambda bi, h: (bi, h, 0, 0)),
                  pl.BlockSpec((3, dh), lambda bi, h: (0, h)),
                  pl.BlockSpec((1, dh), lambda bi, h: (0, h)),
                  pl.BlockSpec((3, dh), lambda bi, h: (0, nh + h)),
                  pl.BlockSpec((1, dh), lambda bi, h: (0, nh + h)),
                  pl.BlockSpec((1, dh), lambda bi, h: (0, h))],
        out_specs=pl.BlockSpec((1, l, dh), lambda bi, h: (bi, 0, h)),
        out_shape=jax.ShapeDtypeStruct((b, l, MLSTM_W), BF16),
        scratch_shapes=[pltpu.VMEM(s, dt) for s, dt in scratch],
        compiler_params=_params(blk, 2),
        name="mlstm",
    )(qkvo, qkvo, qkvo, qkvo, kv_ctx, kv_ctx, gate_tbl, conv_w, conv_b, conv_w, conv_b, gain)


def _attn_body(sink_ref, bias_ref, q_ref, k_ref, v_ref, kc_ref, vc_ref, o_ref, s_scr, p_scr, e_scr, *, seq_len):
    blk = ATTN_BLOCK
    n_blocks = seq_len // blk
    n_ctx = kc_ref.shape[1]
    rows = GQA_GROUP * blk
    nt = (((1,), (1,)), ((), ()))
    chains = [(qb, g) for qb in range(ATTN_BLOCKS_PER_STEP) for g in range(KV_HEADS)]

    def block_start(qb):
        i = pl.program_id(1) * ATTN_BLOCKS_PER_STEP + qb
        return i, pl.multiple_of(jnp.clip((i - 1) * blk, 0, seq_len - BAND), blk)

    for ci, (qb, g) in enumerate(chains):
        i, start = block_start(qb)
        kv_cols = pl.ds(g * ATTN_DH, ATTN_DH)
        q = jnp.concatenate([q_ref[0, pl.ds(qb * blk, blk), pl.ds((g * GQA_GROUP + j) * ATTN_DH, ATTN_DH)]
                             for j in range(GQA_GROUP)], axis=0)
        bias = bias_ref[jnp.where(i == 0, 0, jnp.where(i == n_blocks - 1, 2, 1))]
        s_scr[ci, :, pl.ds(0, BAND)] = (
            lax.dot_general(q, k_ref[0, pl.ds(start, BAND), kv_cols], nt, preferred_element_type=F32) + bias)
        s_scr[ci, :, pl.ds(BAND, n_ctx)] = lax.dot_general(q, kc_ref[0, :, kv_cols], nt, preferred_element_type=F32)

    for ci, (qb, g) in enumerate(chains):
        s = s_scr[ci]
        sink = jnp.concatenate([jnp.full((blk, V7X_LANES), sink_ref[g * GQA_GROUP + j] * LOG2E, F32)
                                for j in range(GQA_GROUP)], axis=0)
        m = jnp.maximum(jnp.broadcast_to(jnp.max(s, axis=1, keepdims=True), (rows, V7X_LANES)), sink)
        p_scr[ci] = jnp.exp2((s - jnp.tile(m, (1, (BAND + n_ctx) // V7X_LANES))).astype(BF16))
        e_scr[ci] = jnp.exp2(sink - m)

    ones_b = jnp.ones((BAND, ATTN_DH), BF16)
    ones_c = jnp.ones((n_ctx, ATTN_DH), BF16)
    for ci, (qb, g) in enumerate(chains):
        _, start = block_start(qb)
        kv_cols = pl.ds(g * ATTN_DH, ATTN_DH)
        vb = jnp.concatenate([v_ref[0, pl.ds(start, BAND), kv_cols], ones_b], axis=1)
        vc = jnp.concatenate([vc_ref[0, :, kv_cols], ones_c], axis=1)
        acc = (jnp.dot(p_scr[ci, :, pl.ds(0, BAND)], vb, preferred_element_type=F32)
               + jnp.dot(p_scr[ci, :, pl.ds(BAND, n_ctx)], vc, preferred_element_type=F32))
        out = (acc / (pltpu.roll(acc, ATTN_DH, 1) + e_scr[ci]))[:, :ATTN_DH]
        for j in range(GQA_GROUP):
            h = g * GQA_GROUP + j
            o_ref[0, pl.ds(qb * blk, blk), pl.ds(h * ATTN_DH, ATTN_DH)] = out[j * blk:(j + 1) * blk].astype(BF16)


def _band_bias(seq_len):
    row = jnp.arange(ATTN_BLOCK)[:, None]
    col = jnp.arange(BAND)[None, :]
    n_blocks = seq_len // ATTN_BLOCK
    tables = []
    for i in (0, 1, n_blocks - 1):
        start = min(max((i - 1) * ATTN_BLOCK, 0), seq_len - BAND)
        ok = jnp.abs(start + col - (i * ATTN_BLOCK + row)) <= WINDOW
        tables.append(jnp.tile(jnp.where(ok, 0.0, -jnp.inf).astype(F32), (GQA_GROUP, 1)))
    return jnp.stack(tables)


def _attn(sink, q, kv, kvc):
    b, l, _ = q.shape
    n_ctx = kvc.shape[1]
    rows = ATTN_BLOCKS_PER_STEP * ATTN_BLOCK
    bias = _band_bias(l)
    n_chains = KV_HEADS * ATTN_BLOCKS_PER_STEP
    stacked = GQA_GROUP * ATTN_BLOCK
    scratch = [((n_chains, stacked, BAND + n_ctx), F32), ((n_chains, stacked, BAND + n_ctx), BF16),
               ((n_chains, stacked, V7X_LANES), F32)]
    blk = (2 * _nbytes((rows, ATTN_W), BF16) + 2 * _nbytes((l, KV_W), BF16) + 2 * _nbytes((n_ctx, KV_W), BF16)
           + _nbytes(bias.shape, F32) + sum(_nbytes(s, dt) for s, dt in scratch) // 2)
    return pl.pallas_call(
        functools.partial(_attn_body, seq_len=l),
        grid=(b, l // rows),
        in_specs=[pl.BlockSpec(memory_space=pltpu.SMEM),
                  pl.BlockSpec(bias.shape, lambda bi, i: (0, 0, 0)),
                  pl.BlockSpec((1, rows, ATTN_W), lambda bi, i: (bi, i, 0)),
                  pl.BlockSpec((1, l, KV_W), lambda bi, i: (bi, 0, 0)),
                  pl.BlockSpec((1, l, KV_W), lambda bi, i: (bi, 0, 1)),
                  pl.BlockSpec((1, n_ctx, KV_W), lambda bi, i: (bi, 0, 0)),
                  pl.BlockSpec((1, n_ctx, KV_W), lambda bi, i: (bi, 0, 1))],
        out_specs=pl.BlockSpec((1, rows, ATTN_W), lambda bi, i: (bi, i, 0)),
        out_shape=jax.ShapeDtypeStruct((b, l, ATTN_W), BF16),
        scratch_shapes=[pltpu.VMEM(s, dt) for s, dt in scratch],
        compiler_params=_params(blk, 2),
        name="attn",
    )(sink, bias, q, kv, kv, kvc, kvc)


def _outproj_body(ym_ref, ya_ref, x_ref, w_ref, ga_ref, gain_ref, sh_ref, sc_ref, x1_ref, hf_ref):
    y = jnp.concatenate([ym_ref[0], ya_ref[0]], axis=1)
    x1 = x_ref[0] + ga_ref[0] * jnp.dot(y, w_ref[...], preferred_element_type=F32)
    x1_ref[0] = x1
    hf_ref[0] = _norm_modulate(x1, gain_ref[...], sh_ref[0], sc_ref[0]).astype(BF16)


def _outproj(ym, ya, x, w, mod3, gain):
    b, l, d = x.shape
    tm = ROW_TILE
    mod_spec = lambda piece: pl.BlockSpec((1, 1, d), lambda bi, i: (bi, 0, piece))
    blk = (2 * _nbytes((tm, MLSTM_W), BF16) + 3 * _nbytes((tm, d), F32) + _nbytes((tm, d), BF16)
           + _nbytes(w.shape, BF16))
    return pl.pallas_call(
        _outproj_body,
        grid=(b, l // tm),
        in_specs=[pl.BlockSpec((1, tm, MLSTM_W), lambda bi, i: (bi, i, 0)),
                  pl.BlockSpec((1, tm, ATTN_W), lambda bi, i: (bi, i, 0)),
                  pl.BlockSpec((1, tm, d), lambda bi, i: (bi, i, 0)),
                  pl.BlockSpec(w.shape, lambda bi, i: (0, 0)),
                  mod_spec(2),
                  pl.BlockSpec((1, d), lambda bi, i: (0, 0)),
                  mod_spec(3), mod_spec(4)],
        out_specs=[pl.BlockSpec((1, tm, d), lambda bi, i: (bi, i, 0)),
                   pl.BlockSpec((1, tm, d), lambda bi, i: (bi, i, 0))],
        out_shape=[jax.ShapeDtypeStruct((b, l, d), F32), jax.ShapeDtypeStruct((b, l, d), BF16)],
        compiler_params=_params(blk, 2),
        name="outproj",
    )(ym, ya, x, w, mod3, gain, mod3, mod3)


GELU_C1 = 0.7978845608028654
GELU_C2 = GELU_C1 * 0.044715


def _twice_gelu_times(x, val):
    xv = x * val
    return xv + xv * jnp.tanh(x * (GELU_C1 + GELU_C2 * (x * x)))


def _ffn_body(h_ref, hp_ref, hn_ref, x1_ref, gf_ref, wup_ref, cw_ref, cb_ref, wdn_ref, fn_ref, o_ref, hs, acc):
    i = pl.program_id(1)
    tm = h_ref.shape[1]
    ext = tm + 2 * HALO
    hs[pl.ds(0, HALO), :] = jnp.where(i > 0, hp_ref[0], jnp.zeros_like(hp_ref[0]))
    hs[pl.ds(HALO, tm), :] = h_ref[0]
    hs[pl.ds(HALO + tm, HALO), :] = jnp.where(i < pl.num_programs(1) - 1, hn_ref[0], jnp.zeros_like(hn_ref[0]))
    starts = list(range(0, D_FF, FF_CHUNK))

    def up(col0):
        width = min(FF_CHUNK, D_FF - col0)
        return (jnp.dot(hs[...], wup_ref[:, pl.ds(col0, width)], preferred_element_type=F32),
                jnp.dot(h_ref[0], wup_ref[:, pl.ds(D_FF + col0, width)], preferred_element_type=F32))

    nxt = up(starts[0])
    for c, col0 in enumerate(starts):
        cols = pl.ds(col0, min(FF_CHUNK, D_FF - col0))
        a, val = nxt
        if c + 1 < len(starts):
            nxt = up(starts[c + 1])
        a_prev = pltpu.roll(a, 1, 0)[HALO:HALO + tm]
        a_next = pltpu.roll(a, ext - 1, 0)[HALO:HALO + tm]
        conv = (cw_ref[0:1, cols] * a_prev + cw_ref[1:2, cols] * a[HALO:HALO + tm]
                + cw_ref[2:3, cols] * a_next + cb_ref[:, cols])
        part = jnp.dot(_twice_gelu_times(conv, val).astype(BF16), wdn_ref[cols, :], preferred_element_type=F32)
        if c == 0:
            acc[...] = part
        else:
            acc[...] += part
    x2 = x1_ref[0] + (0.5 * gf_ref[0]) * acc[...]
    ms = jnp.mean(x2 * x2, axis=-1, keepdims=True)
    o_ref[0] = x2 * lax.rsqrt(ms + EPS) * fn_ref[...]


def _ffn(hf, x1, mod3, w_up, conv_w, conv_b, w_down, final_norm):
    b, l, d = x1.shape
    tm = ROW_TILE
    per = tm // HALO
    last = l // HALO - 1
    blk = (_nbytes((tm, d), BF16) + 2 * _nbytes((tm, d), F32) + _nbytes(w_up.shape, BF16)
           + _nbytes(w_down.shape, BF16) + 4 * _nbytes((tm + 2 * HALO, FF_CHUNK), F32))
    return pl.pallas_call(
        _ffn_body,
        grid=(b, l // tm),
        in_specs=[pl.BlockSpec((1, tm, d), lambda bi, i: (bi, i, 0)),
                  pl.BlockSpec((1, HALO, d), lambda bi, i: (bi, jnp.maximum(i * per - 1, 0), 0)),
                  pl.BlockSpec((1, HALO, d), lambda bi, i: (bi, jnp.minimum((i + 1) * per, last), 0)),
                  pl.BlockSpec((1, tm, d), lambda bi, i: (bi, i, 0)),
                  pl.BlockSpec((1, 1, d), lambda bi, i: (bi, 0, 5)),
                  pl.BlockSpec(w_up.shape, lambda bi, i: (0, 0)),
                  pl.BlockSpec(conv_w.shape, lambda bi, i: (0, 0)),
                  pl.BlockSpec(conv_b.shape, lambda bi, i: (0, 0)),
                  pl.BlockSpec(w_down.shape, lambda bi, i: (0, 0)),
                  pl.BlockSpec((1, d), lambda bi, i: (0, 0))],
        out_specs=pl.BlockSpec((1, tm, d), lambda bi, i: (bi, i, 0)),
        out_shape=jax.ShapeDtypeStruct((b, l, d), F32),
        scratch_shapes=[pltpu.VMEM((tm + 2 * HALO, d), BF16), pltpu.VMEM((tm, d), F32)],
        compiler_params=_params(blk, 2),
        name="ffn",
    )(hf, hf, hf, x1, mod3, w_up, conv_w, conv_b, w_down, final_norm)


def _rope_tables(n_tokens):
    pos = jnp.arange(n_tokens)
    r = (pos // GRID_W).astype(F32)
    c = (pos % GRID_W).astype(F32)
    inv = ROPE_BASE ** (-jnp.arange(ROPE_AXIS_PAIRS, dtype=F32) / ROPE_AXIS_PAIRS)
    ar, ac = r[:, None] * inv, c[:, None] * inv
    zero = jnp.zeros_like(ar)
    cos = jnp.concatenate([jnp.cos(ar), jnp.cos(ar), jnp.cos(ac), jnp.cos(ac)], axis=1)
    sina = jnp.concatenate([-jnp.sin(ar), zero, -jnp.sin(ac), zero], axis=1)
    sinb = jnp.concatenate([zero, jnp.sin(ar), zero, jnp.sin(ac)], axis=1)
    rep = V7X_LANES // ATTN_DH
    return tuple(jnp.tile(tb, (1, rep)) for tb in (cos, sina, sinb))


def _identity_rope(n_tokens):
    return (jnp.ones((n_tokens, V7X_LANES), F32), jnp.zeros((n_tokens, V7X_LANES), F32),
            jnp.zeros((n_tokens, V7X_LANES), F32))


def _gate_table(gates_ctx, gates_x):
    g = jnp.concatenate([gates_ctx, gates_x], axis=1)
    b, n, _ = g.shape
    nc = n // MCHUNK
    g = g.reshape(b, nc, MCHUNK, 4, MLSTM_HEADS).transpose(0, 4, 3, 1, 2)
    g = jnp.pad(g, ((0, 0), (0, 0), (0, 0), (0, MAX_CHUNKS - nc), (0, 0)))
    return g.reshape(b, MLSTM_HEADS, 4 * MAX_CHUNKS, MCHUNK)


def kernel(x, c, ctx, c_ctx, w_ada, b_ada, norm_mix, norm_ffn, w_in, gate_b, qk_conv_w, qk_conv_b, mlstm_norm,
           attn_sink, w_out, w_up, ffn_conv_w, ffn_conv_b, w_down, final_norm):
    b, l, d = x.shape
    n_ctx = ctx.shape[1]
    assert w_ada.shape[0] == 1, "single-layer stack"
    assert l % ROW_TILE == 0 and l % MCHUNK == 0 and n_ctx % MCHUNK == 0 and l >= BAND
    assert l % (ATTN_BLOCKS_PER_STEP * ATTN_BLOCK) == 0 and MCHUNK == V7X_LANES
    assert (l + n_ctx) // MCHUNK <= MAX_CHUNKS

    rows = -(-(b + 1) // V7X_SUBLANES) * V7X_SUBLANES
    cvec = jnp.zeros((rows, d), F32).at[:b].set(c).at[b].set(c_ctx)
    mod = _ada(cvec, w_ada[0], b_ada[0][None, :])
    mod_x = mod[:b].reshape(b, 1, 6 * d)
    mod_c = mod[b:b + 1].reshape(1, 1, 6 * d)

    w = w_in[0].astype(BF16)
    o1, o2 = 3 * MLSTM_W, 4 * MLSTM_W
    o3 = o2 + N_GATES
    o4 = o3 + ATTN_W
    w_g = jnp.pad(w[:, o2:o3], ((0, 0), (0, V7X_LANES - N_GATES)))
    gb = gate_b[0][None, :]
    gain_mix = norm_mix[0][None, :]

    kv_mc, gates_c, kv_ac = _inproj(
        ctx, gain_mix, mod_c, 0, 1, False, _identity_rope(n_ctx), gb,
        [w[:, MLSTM_W:o1], w_g, w[:, o4:]], ["plain", "gates", "plain"], "inproj_ctx")
    qkvo, gates_x, q_a, kv_a = _inproj(
        x, gain_mix, mod_x, 0, 1, True, _rope_tables(l), gb,
        [w[:, :o2], w_g, w[:, o3:o4], w[:, o4:]], ["plain", "gates", "rope_q", "rope_kv"], "inproj_x")

    y_m = _mlstm(qkvo, kv_mc, _gate_table(gates_c, gates_x),
                 qk_conv_w[0], qk_conv_b[0][None, :], mlstm_norm[0][None, :])
    y_a = _attn(attn_sink[0], q_a, kv_a, kv_ac)

    x1, hf = _outproj(y_m, y_a, x, w_out[0].astype(BF16), mod_x, norm_ffn[0][None, :])
    return _ffn(hf, x1, mod_x, w_up[0].astype(BF16), ffn_conv_w[0], ffn_conv_b[0][None, :],
                w_down[0].astype(BF16), final_norm[None, :])
```

```python
import functools

import jax
import jax.numpy as jnp
from jax import lax
from jax.experimental import pallas as pl
from jax.experimental.pallas import tpu as pltpu

F32 = jnp.float32
BF16 = jnp.bfloat16

D_MODEL = 1024
GRID_W = 64
MLSTM_HEADS = 4
MLSTM_DH = 128
MLSTM_W = MLSTM_HEADS * MLSTM_DH
N_GATES = 4 * MLSTM_HEADS
ATTN_HEADS = 8
KV_HEADS = 2
ATTN_DH = 64
ATTN_W = ATTN_HEADS * ATTN_DH
KV_W = KV_HEADS * ATTN_DH
GQA_GROUP = ATTN_HEADS // KV_HEADS
WINDOW = 128
ROPE_BASE = 10000.0
ROPE_AXIS_PAIRS = ATTN_DH // 4
D_FF = 2816
EPS = 1e-6
LOG2E = 1.4426950408889634

V7X_LANES = 128
V7X_SUBLANES = 8
V7X_VMEM_BYTES = 64 * 1024 * 1024

ROW_TILE = 512
ADA_COL_TILE = 1536
MCHUNK = 128
MAX_CHUNKS = 32
ATTN_BLOCK = 128
ATTN_BLOCKS_PER_STEP = 4
BAND = 3 * ATTN_BLOCK
FF_CHUNK = 512
HALO = 2 * V7X_SUBLANES


def _vmem_limit(block_bytes):
    return int(min(V7X_VMEM_BYTES * 7 // 8, 2 * block_bytes + 16 * 1024 * 1024))


def _params(block_bytes, n_axes):
    return pltpu.CompilerParams(dimension_semantics=("arbitrary",) * n_axes,
                                vmem_limit_bytes=_vmem_limit(block_bytes))


def _nbytes(shape, dtype):
    n = 1
    for s in shape:
        n *= s
    return n * jnp.dtype(dtype).itemsize


def _ada_body(c_ref, w_ref, b_ref, o_ref):
    c = c_ref[...]
    s = (c * jax.nn.sigmoid(c)).astype(BF16)
    o_ref[...] = jnp.dot(s, w_ref[...].astype(BF16), preferred_element_type=F32) + b_ref[...]


def _ada(cvec, w_ada, b_ada):
    rows, d = cvec.shape
    n = w_ada.shape[1]
    blk = _nbytes((d, ADA_COL_TILE), F32) + _nbytes((rows, d), F32) + 2 * _nbytes((rows, ADA_COL_TILE), F32)
    return pl.pallas_call(
        _ada_body,
        grid=(n // ADA_COL_TILE,),
        in_specs=[pl.BlockSpec((rows, d), lambda j: (0, 0)),
                  pl.BlockSpec((d, ADA_COL_TILE), lambda j: (0, j)),
                  pl.BlockSpec((1, ADA_COL_TILE), lambda j: (0, j))],
        out_specs=pl.BlockSpec((rows, ADA_COL_TILE), lambda j: (0, j)),
        out_shape=jax.ShapeDtypeStruct((rows, n), F32),
        compiler_params=_params(blk, 1),
        name="ada",
    )(cvec, w_ada, b_ada)


def _norm_modulate(x, gain, shift, scale):
    ms = jnp.mean(x * x, axis=-1, keepdims=True)
    return (x * lax.rsqrt(ms + EPS) * gain) * (1.0 + scale) + shift


def _inproj_body(x_ref, gain_ref, sh_ref, sc_ref, cos_ref, sina_ref, sinb_ref, gb_ref, *refs, kinds):
    n = len(kinds)
    w_refs, o_refs = refs[:n], refs[n:]
    hb = _norm_modulate(x_ref[0], gain_ref[...], sh_ref[0], sc_ref[0]).astype(BF16)
    for kind, w_ref, o_ref in zip(kinds, w_refs, o_refs):
        if kind == "gates":
            u = jnp.dot(hb, w_ref[...], preferred_element_type=F32)
            o_ref[0] = u[:, :N_GATES] + gb_ref[...]
        elif kind == "plain":
            o_ref[0] = jnp.dot(hb, w_ref[...], preferred_element_type=F32).astype(BF16)
        elif kind == "heads":
            u = jnp.dot(hb, w_ref[...], preferred_element_type=F32)
            for j in range(u.shape[1] // V7X_LANES):
                o_ref[0, j] = u[:, j * V7X_LANES:(j + 1) * V7X_LANES].astype(BF16)
        else:
            u_all = jnp.dot(hb, w_ref[...], preferred_element_type=F32)
            n_rot = u_all.shape[1] // V7X_LANES if kind == "rope_q" else KV_W // V7X_LANES
            for j in range(u_all.shape[1] // V7X_LANES):
                cols = pl.ds(j * V7X_LANES, V7X_LANES)
                u = u_all[:, j * V7X_LANES:(j + 1) * V7X_LANES]
                if j < n_rot:
                    u = (u * cos_ref[...]
                         + pltpu.roll(u, V7X_LANES - ROPE_AXIS_PAIRS, 1) * sina_ref[...]
                         + pltpu.roll(u, ROPE_AXIS_PAIRS, 1) * sinb_ref[...])
                    if kind == "rope_q":
                        u = u * (ATTN_DH ** -0.5 * LOG2E)
                o_ref[0, :, cols] = u.astype(BF16)


def _inproj(x, gain, mod3, shift_idx, scale_idx, per_batch_mod, rope, gate_b, weights, kinds, name):
    b, n, d = x.shape
    tm = min(ROW_TILE, n)
    cos, sina, sinb = rope
    mod_spec = lambda piece: pl.BlockSpec(
        (1, 1, d), (lambda bi, i: (bi, 0, piece)) if per_batch_mod else (lambda bi, i: (0, 0, piece)))
    in_specs = [pl.BlockSpec((1, tm, d), lambda bi, i: (bi, i, 0)),
                pl.BlockSpec((1, d), lambda bi, i: (0, 0)),
                mod_spec(shift_idx), mod_spec(scale_idx),
                pl.BlockSpec((tm, V7X_LANES), lambda bi, i: (i, 0)),
                pl.BlockSpec((tm, V7X_LANES), lambda bi, i: (i, 0)),
                pl.BlockSpec((tm, V7X_LANES), lambda bi, i: (i, 0)),
                pl.BlockSpec((1, N_GATES), lambda bi, i: (0, 0))]
    out_specs, out_shapes = [], []
    blk = _nbytes((tm, d), F32) + 3 * _nbytes((tm, V7X_LANES), F32)
    for kind, w in zip(kinds, weights):
        width = N_GATES if kind == "gates" else w.shape[1]
        dt = F32 if kind == "gates" else BF16
        in_specs.append(pl.BlockSpec(w.shape, lambda bi, i: (0, 0)))
        if kind == "heads":
            groups = width // V7X_LANES
            out_specs.append(pl.BlockSpec((1, groups, tm, V7X_LANES), lambda bi, i: (bi, 0, i, 0)))
            out_shapes.append(jax.ShapeDtypeStruct((b, groups, n, V7X_LANES), dt))
        else:
            out_specs.append(pl.BlockSpec((1, tm, width), lambda bi, i: (bi, i, 0)))
            out_shapes.append(jax.ShapeDtypeStruct((b, n, width), dt))
        blk += _nbytes(w.shape, BF16) + _nbytes((tm, width), dt) + _nbytes((tm, w.shape[1]), F32)
    return pl.pallas_call(
        functools.partial(_inproj_body, kinds=tuple(kinds)),
        grid=(b, n // tm),
        in_specs=in_specs,
        out_specs=out_specs,
        out_shape=out_shapes,
        compiler_params=_params(blk, 2),
        name=name,
    )(x, gain, mod3, mod3, cos, sina, sinb, gate_b, *weights)


def _silu(y):
    h = 0.5 * y
    return h + h * jnp.tanh(h)


def _sigmoid(y):
    return 0.5 + 0.5 * jnp.tanh(0.5 * y)


def _conv_silu_chunks(src_ref, dst_ref, w, b, scale, n_rows):
    t = MCHUNK
    ri = lax.broadcasted_iota(jnp.int32, (t, 2 * t), 0)
    ci = lax.broadcasted_iota(jnp.int32, (t, 2 * t), 1)
    shift_prev = jnp.where(ci == ri + t - 1, 1.0, 0.0).astype(BF16)
    shift_next = jnp.where(ci == ri + 1, 1.0, 0.0).astype(BF16)
    zeros = jnp.zeros((t, src_ref.shape[-1]), BF16)
    n = n_rows // t
    for c in range(n):
        x = src_ref[pl.ds(c * t, t), :]
        before = src_ref[pl.ds((c - 1) * t, t), :] if c > 0 else zeros
        after = src_ref[pl.ds((c + 1) * t, t), :] if c < n - 1 else zeros
        prev = jnp.dot(shift_prev, jnp.concatenate([before, x], axis=0), preferred_element_type=F32)
        nxt = jnp.dot(shift_next, jnp.concatenate([x, after], axis=0), preferred_element_type=F32)
        y = w[0:1, :] * prev + w[1:2, :] * x.astype(F32) + w[2:3, :] * nxt + b
        act = _silu(y) if scale == 1.0 else _silu(y) * scale
        dst_ref[pl.ds(c * t, t), :] = act.astype(BF16)


def _log_sigmoid(x):
    return jnp.minimum(x, 0.0) - jnp.log1p(jnp.exp(-jnp.abs(x)))


def _mlstm_body(q_ref, k_ref, v_ref, o_ref, kc_ref, vc_ref, gt_ref, wq_ref, bq_ref, wk_ref, bk_ref, gain_ref,
                y_ref, qs, ks, kcs, gtab, rtab, cumtab, cnl, c0n, cumc, sqk, sds, inter_s, em_s,
                *, n_ctx_chunks, n_lat_chunks):
    t = MCHUNK
    dh = MLSTM_DH
    n_chunks = n_ctx_chunks + n_lat_chunks
    kscale = MLSTM_DH ** -0.5
    nt = (((1,), (1,)), ((), ()))

    _conv_silu_chunks(q_ref, qs, wq_ref[...], bq_ref[...], 1.0, n_lat_chunks * t)
    _conv_silu_chunks(k_ref, ks, wk_ref[...], bk_ref[...], kscale, n_lat_chunks * t)
    _conv_silu_chunks(kc_ref, kcs, wk_ref[...], bk_ref[...], kscale, n_ctx_chunks * t)

    tbl = gt_ref[0, 0]
    rid = lax.broadcasted_iota(jnp.int32, tbl.shape, 0) // MAX_CHUNKS
    lane = lax.broadcasted_iota(jnp.int32, tbl.shape, 1)
    lf = _log_sigmoid(tbl)
    pre = jnp.where(rid == 1, lf, 0.0)
    suf = jnp.where(rid == 3, lf, 0.0)
    s = 1
    while s < t:
        pre = pre + jnp.where(lane >= s, pltpu.roll(pre, s, 1), 0.0)
        suf = suf + jnp.where(lane < t - s, pltpu.roll(suf, t - s, 1), 0.0)
        s *= 2
    li = (tbl[0:MAX_CHUNKS] * LOG2E, tbl[2 * MAX_CHUNKS:3 * MAX_CHUNKS] * LOG2E)
    cum = (pre[MAX_CHUNKS:2 * MAX_CHUNKS] * LOG2E, suf[3 * MAX_CHUNKS:4 * MAX_CHUNKS] * LOG2E)
    total = (cum[0][:, t - 1:t], cum[1][:, 0:1])
    mloc = []
    for d in (0, 1):
        g = total[d] - cum[d] + li[d]
        mloc.append(jnp.max(g, axis=1, keepdims=True))
        gtab[d] = g
        rtab[d] = cum[d] - li[d]
        cumtab[d] = cum[d]

    orders = (list(range(n_chunks)),
              list(range(n_ctx_chunks - 1, -1, -1)) + list(range(n_chunks - 1, n_ctx_chunks - 1, -1)))

    a_sc, m0, m1 = ({}, {}), ({}, {}), ({}, {})
    for d in (0, 1):
        m_st = jnp.zeros((1, 1), F32)
        for c in orders[d][:-1]:
            m0[d][c] = m_st
            tot_c = total[d][c:c + 1, :]
            m1[d][c] = jnp.maximum(tot_c + m_st, mloc[d][c:c + 1, :])
            a_sc[d][c] = jnp.exp2(tot_c + m_st - m1[d][c])
            m_st = m1[d][c]
        m0[d][orders[d][-1]] = m_st

    def kv_chunk(c):
        if c < n_ctx_chunks:
            rows = pl.ds(c * t, t)
            return kcs[rows, :], vc_ref[rows, :]
        rows = pl.ds((c - n_ctx_chunks) * t, t)
        return ks[rows, :], v_ref[rows, :]

    ri = lax.broadcasted_iota(jnp.int32, (t, t), 0)
    ci = lax.broadcasted_iota(jnp.int32, (t, t), 1)
    eye = jnp.where(ri == ci, 1.0, 0.0).astype(BF16)
    ones = jnp.ones((t, dh), BF16)

    for c in range(n_chunks):
        k, v = kv_chunk(c)
        k_t = lax.dot_general(eye, k, nt, preferred_element_type=F32)
        vo = jnp.concatenate([v, ones], axis=1)
        for d in (0, 1):
            if c != orders[d][-1]:
                w = jnp.exp2(gtab[d, pl.ds(c, 1), :] - m1[d][c])
                cnl[d, c] = jnp.dot((k_t * w).astype(BF16), vo, preferred_element_type=F32)

    for d in (0, 1):
        cn = jnp.zeros((dh, 2 * dh), F32)
        for c in orders[d]:
            if c >= n_ctx_chunks:
                c0n[d, c - n_ctx_chunks] = cn.astype(BF16)
            if c != orders[d][-1]:
                cn = a_sc[d][c] * cn + cnl[d, c]

    for j in range(n_lat_chunks):
        c = j + n_ctx_chunks
        rows = pl.ds(j * t, t)
        sqk[rows, :] = lax.dot_general(qs[rows, :], ks[rows, :], nt, preferred_element_type=F32)
        for d in (0, 1):
            cumc[d, rows, :] = jnp.broadcast_to(cumtab[d, pl.ds(c, 1), :], (t, t)).T

    grp = 2
    gi = lax.broadcasted_iota(jnp.int32, (grp * t, t), 0) & (t - 1)
    gs = lax.broadcasted_iota(jnp.int32, (grp * t, t), 1)
    for d in (0, 1):
        ok = (gs <= gi) if d == 0 else (gs >= gi)
        for j0 in range(0, n_lat_chunks, grp):
            rows = pl.ds(j0 * t, grp * t)
            cs = [j + n_ctx_chunks for j in range(j0, j0 + grp)]
            rb = jnp.concatenate([jnp.broadcast_to(rtab[d, pl.ds(c, 1), :], (t, t)) for c in cs], axis=0)
            m0b = jnp.concatenate([jnp.broadcast_to(m0[d][c], (t, t)) for c in cs], axis=0)
            cum_c = cumc[d, rows, :]
            d_log = jnp.where(ok, cum_c - rb, -jnp.inf)
            a_log = cum_c + m0b
            m_t = jnp.maximum(a_log, jnp.max(d_log, axis=1, keepdims=True))
            sds[d, rows, :] = (sqk[rows, :] * jnp.exp2(d_log - m_t)).astype(BF16)
            inter_s[d, rows, :] = jnp.exp2(a_log - m_t)
            em_s[d, rows, :] = jnp.exp2(-m_t)

    gain = gain_ref[...]
    for j in range(n_lat_chunks):
        rows = pl.ds(j * t, t)
        q = qs[rows, :]
        vo = jnp.concatenate([v_ref[rows, :], ones], axis=1)
        hid = None
        for d in (0, 1):
            sv = jnp.dot(sds[d, rows, :], vo, preferred_element_type=F32)
            qc = jnp.dot(q, c0n[d, j], preferred_element_type=F32)
            inter = inter_s[d, rows, :]
            num = sv[:, :dh] + inter * qc[:, :dh]
            den = sv[:, dh:] + inter * qc[:, dh:]
            h_d = num / jnp.maximum(jnp.abs(den), em_s[d, rows, :])
            hid = h_d if hid is None else hid + h_d
        hn = hid * lax.rsqrt(jnp.mean(hid * hid, axis=-1, keepdims=True) + EPS) * gain
        y_ref[rows, :] = (hn * _sigmoid(o_ref[rows, :].astype(F32))).astype(BF16)


def _mlstm(qkvo, kv_ctx, gate_tbl, conv_w, conv_b, gain):
    b, _, l, _ = qkvo.shape
    n_ctx = kv_ctx.shape[2]
    dh = MLSTM_DH
    nh = MLSTM_HEADS
    n_ctx_chunks, n_lat_chunks = n_ctx // MCHUNK, l // MCHUNK
    seq = lambda slab0: pl.BlockSpec((None, None, l, dh), lambda bi, h: (bi, slab0 + h, 0, 0))
    ctx = lambda slab0: pl.BlockSpec((None, None, n_ctx, dh), lambda bi, h: (bi, slab0 + h, 0, 0))
    n_chunks = n_ctx_chunks + n_lat_chunks
    scratch = [((l, dh), BF16), ((l, dh), BF16), ((n_ctx, dh), BF16),
               ((2, MAX_CHUNKS, MCHUNK), F32), ((2, MAX_CHUNKS, MCHUNK), F32),
               ((2, MAX_CHUNKS, MCHUNK), F32),
               ((2, n_chunks, dh, 2 * dh), F32), ((2, n_lat_chunks, dh, 2 * dh), BF16),
               ((2, l, MCHUNK), F32), ((l, MCHUNK), F32), ((2, l, MCHUNK), BF16),
               ((2, l, MCHUNK), F32), ((2, l, MCHUNK), F32)]
    blk = (5 * _nbytes((l, dh), BF16) + 2 * _nbytes((n_ctx, dh), BF16) + _nbytes((V7X_LANES, V7X_LANES), F32)
           + sum(_nbytes(s, dt) for s, dt in scratch) // 2)
    return pl.pallas_call(
        functools.partial(_mlstm_body, n_ctx_chunks=n_ctx_chunks, n_lat_chunks=n_lat_chunks),
        grid=(b, nh),
        in_specs=[seq(0), seq(nh), seq(2 * nh), seq(3 * nh), ctx(0), ctx(nh),
                  pl.BlockSpec((1, 1, 4 * MAX_CHUNKS, MCHUNK), lambda bi, h: (bi, h, 0, 0)),
                  pl.BlockSpec((3, dh), lambda bi, h: (0, h)),
                  pl.BlockSpec((1, dh), lambda bi, h: (0, h)),
                  pl.BlockSpec((3, dh), lambda bi, h: (0, nh + h)),
                  pl.BlockSpec((1, dh), lambda bi, h: (0, nh + h)),
                  pl.BlockSpec((1, dh), lambda bi, h: (0, h))],
        out_specs=pl.BlockSpec((None, None, l, dh), lambda bi, h: (bi, h, 0, 0)),
        out_shape=jax.ShapeDtypeStruct((b, nh, l, dh), BF16),
        scratch_shapes=[pltpu.VMEM(s, dt) for s, dt in scratch],
        compiler_params=_params(blk, 2),
        name="mlstm",
    )(qkvo, qkvo, qkvo, qkvo, kv_ctx, kv_ctx, gate_tbl, conv_w, conv_b, conv_w, conv_b, gain)


def _attn_body(sink_ref, bias_ref, q_ref, kv_ref, kvc_ref, o_ref, s_scr, p_scr, e_scr, *, seq_len):
    blk = ATTN_BLOCK
    n_blocks = seq_len // blk
    n_ctx = kvc_ref.shape[1]
    rows = GQA_GROUP * blk
    nt = (((1,), (1,)), ((), ()))
    chains = [(qb, g) for qb in range(ATTN_BLOCKS_PER_STEP) for g in range(KV_HEADS)]

    def block_start(qb):
        i = pl.program_id(1) * ATTN_BLOCKS_PER_STEP + qb
        return i, pl.multiple_of(jnp.clip((i - 1) * blk, 0, seq_len - BAND), blk)

    for ci, (qb, g) in enumerate(chains):
        i, start = block_start(qb)
        k_cols = pl.ds(g * ATTN_DH, ATTN_DH)
        q = jnp.concatenate([q_ref[0, pl.ds(qb * blk, blk), pl.ds((g * GQA_GROUP + j) * ATTN_DH, ATTN_DH)]
                             for j in range(GQA_GROUP)], axis=0)
        bias = bias_ref[jnp.where(i == 0, 0, jnp.where(i == n_blocks - 1, 2, 1))]
        s_scr[ci, :, pl.ds(0, BAND)] = (
            lax.dot_general(q, kv_ref[0, pl.ds(start, BAND), k_cols], nt, preferred_element_type=F32) + bias)
        s_scr[ci, :, pl.ds(BAND, n_ctx)] = lax.dot_general(q, kvc_ref[0, :, k_cols], nt, preferred_element_type=F32)

    for ci, (qb, g) in enumerate(chains):
        s = s_scr[ci]
        sink = jnp.concatenate([jnp.full((blk, V7X_LANES), sink_ref[g * GQA_GROUP + j] * LOG2E, F32)
                                for j in range(GQA_GROUP)], axis=0)
        m = jnp.maximum(jnp.broadcast_to(jnp.max(s, axis=1, keepdims=True), (rows, V7X_LANES)), sink)
        p_scr[ci] = jnp.exp2((s - jnp.tile(m, (1, (BAND + n_ctx) // V7X_LANES))).astype(BF16))
        e_scr[ci] = jnp.exp2(sink - m)

    ones_b = jnp.ones((BAND, ATTN_DH), BF16)
    ones_c = jnp.ones((n_ctx, ATTN_DH), BF16)
    for ci, (qb, g) in enumerate(chains):
        _, start = block_start(qb)
        v_cols = pl.ds(KV_W + g * ATTN_DH, ATTN_DH)
        vb = jnp.concatenate([kv_ref[0, pl.ds(start, BAND), v_cols], ones_b], axis=1)
        vc = jnp.concatenate([kvc_ref[0, :, v_cols], ones_c], axis=1)
        acc = (jnp.dot(p_scr[ci, :, pl.ds(0, BAND)], vb, preferred_element_type=F32)
               + jnp.dot(p_scr[ci, :, pl.ds(BAND, n_ctx)], vc, preferred_element_type=F32))
        out = (acc / (pltpu.roll(acc, ATTN_DH, 1) + e_scr[ci]))[:, :ATTN_DH]
        for j in range(GQA_GROUP):
            h = g * GQA_GROUP + j
            o_ref[0, pl.ds(qb * blk, blk), pl.ds(h * ATTN_DH, ATTN_DH)] = out[j * blk:(j + 1) * blk].astype(BF16)


def _band_bias(seq_len):
    row = jnp.arange(ATTN_BLOCK)[:, None]
    col = jnp.arange(BAND)[None, :]
    n_blocks = seq_len // ATTN_BLOCK
    tables = []
    for i in (0, 1, n_blocks - 1):
        start = min(max((i - 1) * ATTN_BLOCK, 0), seq_len - BAND)
        ok = jnp.abs(start + col - (i * ATTN_BLOCK + row)) <= WINDOW
        tables.append(jnp.tile(jnp.where(ok, 0.0, -jnp.inf).astype(F32), (GQA_GROUP, 1)))
    return jnp.stack(tables)


def _attn(sink, q, kv, kvc):
    b, l, _ = q.shape
    n_ctx = kvc.shape[1]
    rows = ATTN_BLOCKS_PER_STEP * ATTN_BLOCK
    bias = _band_bias(l)
    n_chains = KV_HEADS * ATTN_BLOCKS_PER_STEP
    stacked = GQA_GROUP * ATTN_BLOCK
    scratch = [((n_chains, stacked, BAND + n_ctx), F32), ((n_chains, stacked, BAND + n_ctx), BF16),
               ((n_chains, stacked, V7X_LANES), F32)]
    blk = (2 * _nbytes((rows, ATTN_W), BF16) + _nbytes((l, 2 * KV_W), BF16) + _nbytes((n_ctx, 2 * KV_W), BF16)
           + _nbytes(bias.shape, F32) + sum(_nbytes(s, dt) for s, dt in scratch) // 2)
    return pl.pallas_call(
        functools.partial(_attn_body, seq_len=l),
        grid=(b, l // rows),
        in_specs=[pl.BlockSpec(memory_space=pltpu.SMEM),
                  pl.BlockSpec(bias.shape, lambda bi, i: (0, 0, 0)),
                  pl.BlockSpec((1, rows, ATTN_W), lambda bi, i: (bi, i, 0)),
                  pl.BlockSpec((1, l, 2 * KV_W), lambda bi, i: (bi, 0, 0)),
                  pl.BlockSpec((1, n_ctx, 2 * KV_W), lambda bi, i: (bi, 0, 0))],
        out_specs=pl.BlockSpec((1, rows, ATTN_W), lambda bi, i: (bi, i, 0)),
        out_shape=jax.ShapeDtypeStruct((b, l, ATTN_W), BF16),
        scratch_shapes=[pltpu.VMEM(s, dt) for s, dt in scratch],
        compiler_params=_params(blk, 2),
        name="attn",
    )(sink, bias, q, kv, kvc)


def _outproj_body(ym_ref, ya_ref, x_ref, w_ref, ga_ref, gain_ref, sh_ref, sc_ref, x1_ref, hf_ref):
    y = jnp.concatenate([ym_ref[0, h] for h in range(MLSTM_HEADS)] + [ya_ref[0]], axis=1)
    x1 = x_ref[0] + ga_ref[0] * jnp.dot(y, w_ref[...], preferred_element_type=F32)
    x1_ref[0] = x1
    hf_ref[0] = _norm_modulate(x1, gain_ref[...], sh_ref[0], sc_ref[0]).astype(BF16)


def _outproj(ym, ya, x, w, mod3, gain):
    b, l, d = x.shape
    tm = ROW_TILE
    mod_spec = lambda piece: pl.BlockSpec((1, 1, d), lambda bi, i: (bi, 0, piece))
    blk = (2 * _nbytes((tm, MLSTM_W), BF16) + 3 * _nbytes((tm, d), F32) + _nbytes((tm, d), BF16)
           + _nbytes(w.shape, BF16))
    return pl.pallas_call(
        _outproj_body,
        grid=(b, l // tm),
        in_specs=[pl.BlockSpec((1, MLSTM_HEADS, tm, MLSTM_DH), lambda bi, i: (bi, 0, i, 0)),
                  pl.BlockSpec((1, tm, ATTN_W), lambda bi, i: (bi, i, 0)),
                  pl.BlockSpec((1, tm, d), lambda bi, i: (bi, i, 0)),
                  pl.BlockSpec(w.shape, lambda bi, i: (0, 0)),
                  mod_spec(2),
                  pl.BlockSpec((1, d), lambda bi, i: (0, 0)),
                  mod_spec(3), mod_spec(4)],
        out_specs=[pl.BlockSpec((1, tm, d), lambda bi, i: (bi, i, 0)),
                   pl.BlockSpec((1, tm, d), lambda bi, i: (bi, i, 0))],
        out_shape=[jax.ShapeDtypeStruct((b, l, d), F32), jax.ShapeDtypeStruct((b, l, d), BF16)],
        compiler_params=_params(blk, 2),
        name="outproj",
    )(ym, ya, x, w, mod3, gain, mod3, mod3)


GELU_C1 = 0.7978845608028654
GELU_C2 = GELU_C1 * 0.044715


def _twice_gelu_times(x, val):
    xv = x * val
    return xv + xv * jnp.tanh(x * (GELU_C1 + GELU_C2 * (x * x)))


def _ffn_body(h_ref, hp_ref, hn_ref, x1_ref, gf_ref, wup_ref, cw_ref, cb_ref, wdn_ref, fn_ref, o_ref, hs, acc):
    i = pl.program_id(1)
    tm = h_ref.shape[1]
    ext = tm + 2 * HALO
    hs[pl.ds(0, HALO), :] = jnp.where(i > 0, hp_ref[0], jnp.zeros_like(hp_ref[0]))
    hs[pl.ds(HALO, tm), :] = h_ref[0]
    hs[pl.ds(HALO + tm, HALO), :] = jnp.where(i < pl.num_programs(1) - 1, hn_ref[0], jnp.zeros_like(hn_ref[0]))
    starts = list(range(0, D_FF, FF_CHUNK))

    def up(col0):
        width = min(FF_CHUNK, D_FF - col0)
        return (jnp.dot(hs[...], wup_ref[:, pl.ds(col0, width)], preferred_element_type=F32),
                jnp.dot(h_ref[0], wup_ref[:, pl.ds(D_FF + col0, width)], preferred_element_type=F32))

    nxt = up(starts[0])
    for c, col0 in enumerate(starts):
        cols = pl.ds(col0, min(FF_CHUNK, D_FF - col0))
        a, val = nxt
        if c + 1 < len(starts):
            nxt = up(starts[c + 1])
        a_prev = pltpu.roll(a, 1, 0)[HALO:HALO + tm]
        a_next = pltpu.roll(a, ext - 1, 0)[HALO:HALO + tm]
        conv = (cw_ref[0:1, cols] * a_prev + cw_ref[1:2, cols] * a[HALO:HALO + tm]
                + cw_ref[2:3, cols] * a_next + cb_ref[:, cols])
        part = jnp.dot(_twice_gelu_times(conv, val).astype(BF16), wdn_ref[cols, :], preferred_element_type=F32)
        if c == 0:
            acc[...] = part
        else:
            acc[...] += part
    x2 = x1_ref[0] + (0.5 * gf_ref[0]) * acc[...]
    ms = jnp.mean(x2 * x2, axis=-1, keepdims=True)
    o_ref[0] = x2 * lax.rsqrt(ms + EPS) * fn_ref[...]


def _ffn(hf, x1, mod3, w_up, conv_w, conv_b, w_down, final_norm):
    b, l, d = x1.shape
    tm = ROW_TILE
    per = tm // HALO
    last = l // HALO - 1
    blk = (_nbytes((tm, d), BF16) + 2 * _nbytes((tm, d), F32) + _nbytes(w_up.shape, BF16)
           + _nbytes(w_down.shape, BF16) + 4 * _nbytes((tm + 2 * HALO, FF_CHUNK), F32))
    return pl.pallas_call(
        _ffn_body,
        grid=(b, l // tm),
        in_specs=[pl.BlockSpec((1, tm, d), lambda bi, i: (bi, i, 0)),
                  pl.BlockSpec((1, HALO, d), lambda bi, i: (bi, jnp.maximum(i * per - 1, 0), 0)),
                  pl.BlockSpec((1, HALO, d), lambda bi, i: (bi, jnp.minimum((i + 1) * per, last), 0)),
                  pl.BlockSpec((1, tm, d), lambda bi, i: (bi, i, 0)),
                  pl.BlockSpec((1, 1, d), lambda bi, i: (bi, 0, 5)),
                  pl.BlockSpec(w_up.shape, lambda bi, i: (0, 0)),
                  pl.BlockSpec(conv_w.shape, lambda bi, i: (0, 0)),
                  pl.BlockSpec(conv_b.shape, lambda bi, i: (0, 0)),
                  pl.BlockSpec(w_down.shape, lambda bi, i: (0, 0)),
                  pl.BlockSpec((1, d), lambda bi, i: (0, 0))],
        out_specs=pl.BlockSpec((1, tm, d), lambda bi, i: (bi, i, 0)),
        out_shape=jax.ShapeDtypeStruct((b, l, d), F32),
        scratch_shapes=[pltpu.VMEM((tm + 2 * HALO, d), BF16), pltpu.VMEM((tm, d), F32)],
        compiler_params=_params(blk, 2),
        name="ffn",
    )(hf, hf, hf, x1, mod3, w_up, conv_w, conv_b, w_down, final_norm)


def _rope_tables(n_tokens):
    pos = jnp.arange(n_tokens)
    r = (pos // GRID_W).astype(F32)
    c = (pos % GRID_W).astype(F32)
    inv = ROPE_BASE ** (-jnp.arange(ROPE_AXIS_PAIRS, dtype=F32) / ROPE_AXIS_PAIRS)
    ar, ac = r[:, None] * inv, c[:, None] * inv
    zero = jnp.zeros_like(ar)
    cos = jnp.concatenate([jnp.cos(ar), jnp.cos(ar), jnp.cos(ac), jnp.cos(ac)], axis=1)
    sina = jnp.concatenate([-jnp.sin(ar), zero, -jnp.sin(ac), zero], axis=1)
    sinb = jnp.concatenate([zero, jnp.sin(ar), zero, jnp.sin(ac)], axis=1)
    rep = V7X_LANES // ATTN_DH
    return tuple(jnp.tile(tb, (1, rep)) for tb in (cos, sina, sinb))


def _identity_rope(n_tokens):
    return (jnp.ones((n_tokens, V7X_LANES), F32), jnp.zeros((n_tokens, V7X_LANES), F32),
            jnp.zeros((n_tokens, V7X_LANES), F32))


def _gate_table(gates_ctx, gates_x):
    g = jnp.concatenate([gates_ctx, gates_x], axis=1)
    b, n, _ = g.shape
    nc = n // MCHUNK
    g = g.reshape(b, nc, MCHUNK, 4, MLSTM_HEADS).transpose(0, 4, 3, 1, 2)
    g = jnp.pad(g, ((0, 0), (0, 0), (0, 0), (0, MAX_CHUNKS - nc), (0, 0)))
    return g.reshape(b, MLSTM_HEADS, 4 * MAX_CHUNKS, MCHUNK)


def kernel(x, c, ctx, c_ctx, w_ada, b_ada, norm_mix, norm_ffn, w_in, gate_b, qk_conv_w, qk_conv_b, mlstm_norm,
           attn_sink, w_out, w_up, ffn_conv_w, ffn_conv_b, w_down, final_norm):
    b, l, d = x.shape
    n_ctx = ctx.shape[1]
    assert w_ada.shape[0] == 1, "single-layer stack"
    assert l % ROW_TILE == 0 and l % MCHUNK == 0 and n_ctx % MCHUNK == 0 and l >= BAND
    assert l % (ATTN_BLOCKS_PER_STEP * ATTN_BLOCK) == 0 and MCHUNK == V7X_LANES
    assert (l + n_ctx) // MCHUNK <= MAX_CHUNKS

    rows = -(-(b + 1) // V7X_SUBLANES) * V7X_SUBLANES
    cvec = jnp.zeros((rows, d), F32).at[:b].set(c).at[b].set(c_ctx)
    mod = _ada(cvec, w_ada[0], b_ada[0][None, :])
    mod_x = mod[:b].reshape(b, 1, 6 * d)
    mod_c = mod[b:b + 1].reshape(1, 1, 6 * d)

    w = w_in[0].astype(BF16)
    o1, o2 = 3 * MLSTM_W, 4 * MLSTM_W
    o3 = o2 + N_GATES
    o4 = o3 + ATTN_W
    w_g = jnp.pad(w[:, o2:o3], ((0, 0), (0, V7X_LANES - N_GATES)))
    gb = gate_b[0][None, :]
    gain_mix = norm_mix[0][None, :]

    kv_mc, gates_c, kv_ac = _inproj(
        ctx, gain_mix, mod_c, 0, 1, False, _identity_rope(n_ctx), gb,
        [w[:, MLSTM_W:o1], w_g, w[:, o4:]], ["heads", "gates", "plain"], "inproj_ctx")
    qkvo, gates_x, q_a, kv_a = _inproj(
        x, gain_mix, mod_x, 0, 1, True, _rope_tables(l), gb,
        [w[:, :o2], w_g, w[:, o3:o4], w[:, o4:]], ["heads", "gates", "rope_q", "rope_kv"], "inproj_x")

    y_m = _mlstm(qkvo, kv_mc, _gate_table(gates_c, gates_x),
                 qk_conv_w[0], qk_conv_b[0][None, :], mlstm_norm[0][None, :])
    y_a = _attn(attn_sink[0], q_a, kv_a, kv_ac)

    x1, hf = _outproj(y_m, y_a, x, w_out[0].astype(BF16), mod_x, norm_ffn[0][None, :])
    return _ffn(hf, x1, mod_x, w_up[0].astype(BF16), ffn_conv_w[0], ffn_conv_b[0][None, :],
                w_down[0].astype(BF16), final_norm[None, :])
```

```python
import functools

import jax
import jax.numpy as jnp
from jax import lax
from jax.experimental import pallas as pl
from jax.experimental.pallas import tpu as pltpu

F32 = jnp.float32
BF16 = jnp.bfloat16

D_MODEL = 1024
GRID_W = 64
MLSTM_HEADS = 4
MLSTM_DH = 128
MLSTM_W = MLSTM_HEADS * MLSTM_DH
N_GATES = 4 * MLSTM_HEADS
ATTN_HEADS = 8
KV_HEADS = 2
ATTN_DH = 64
ATTN_W = ATTN_HEADS * ATTN_DH
KV_W = KV_HEADS * ATTN_DH
GQA_GROUP = ATTN_HEADS // KV_HEADS
WINDOW = 128
ROPE_BASE = 10000.0
ROPE_AXIS_PAIRS = ATTN_DH // 4
D_FF = 2816
EPS = 1e-6
LOG2E = 1.4426950408889634

V7X_LANES = 128
V7X_SUBLANES = 8
V7X_VMEM_BYTES = 64 * 1024 * 1024

ROW_TILE = 512
ADA_COL_TILE = 1536
MCHUNK = 128
MAX_CHUNKS = 32
ATTN_BLOCK = 128
ATTN_BLOCKS_PER_STEP = 4
BAND = 3 * ATTN_BLOCK
FF_CHUNK = 512
HALO = 2 * V7X_SUBLANES


def _vmem_limit(block_bytes):
    return int(min(V7X_VMEM_BYTES * 7 // 8, 2 * block_bytes + 16 * 1024 * 1024))


def _params(block_bytes, n_axes):
    return pltpu.CompilerParams(dimension_semantics=("arbitrary",) * n_axes,
                                vmem_limit_bytes=_vmem_limit(block_bytes))


def _nbytes(shape, dtype):
    n = 1
    for s in shape:
        n *= s
    return n * jnp.dtype(dtype).itemsize


def _ada_body(c_ref, w_ref, b_ref, o_ref):
    c = c_ref[...]
    s = (c * jax.nn.sigmoid(c)).astype(BF16)
    o_ref[...] = jnp.dot(s, w_ref[...].astype(BF16), preferred_element_type=F32) + b_ref[...]


def _ada(cvec, w_ada, b_ada):
    rows, d = cvec.shape
    n = w_ada.shape[1]
    blk = _nbytes((d, ADA_COL_TILE), F32) + _nbytes((rows, d), F32) + 2 * _nbytes((rows, ADA_COL_TILE), F32)
    return pl.pallas_call(
        _ada_body,
        grid=(n // ADA_COL_TILE,),
        in_specs=[pl.BlockSpec((rows, d), lambda j: (0, 0)),
                  pl.BlockSpec((d, ADA_COL_TILE), lambda j: (0, j)),
                  pl.BlockSpec((1, ADA_COL_TILE), lambda j: (0, j))],
        out_specs=pl.BlockSpec((rows, ADA_COL_TILE), lambda j: (0, j)),
        out_shape=jax.ShapeDtypeStruct((rows, n), F32),
        compiler_params=_params(blk, 1),
        name="ada",
    )(cvec, w_ada, b_ada)


def _norm_modulate(x, gain, shift, scale):
    ms = jnp.mean(x * x, axis=-1, keepdims=True)
    return (x * lax.rsqrt(ms + EPS) * gain) * (1.0 + scale) + shift


def _inproj_body(x_ref, gain_ref, sh_ref, sc_ref, cos_ref, sina_ref, sinb_ref, gb_ref, *refs, kinds):
    n = len(kinds)
    w_refs, o_refs = refs[:n], refs[n:]
    hb = _norm_modulate(x_ref[0], gain_ref[...], sh_ref[0], sc_ref[0]).astype(BF16)
    for kind, w_ref, o_ref in zip(kinds, w_refs, o_refs):
        if kind == "gates":
            u = jnp.dot(hb, w_ref[...], preferred_element_type=F32)
            o_ref[0] = u[:, :N_GATES] + gb_ref[...]
        elif kind == "plain":
            o_ref[0] = jnp.dot(hb, w_ref[...], preferred_element_type=F32).astype(BF16)
        elif kind == "heads":
            u = jnp.dot(hb, w_ref[...], preferred_element_type=F32)
            for j in range(u.shape[1] // V7X_LANES):
                o_ref[0, j] = u[:, j * V7X_LANES:(j + 1) * V7X_LANES].astype(BF16)
        else:
            u_all = jnp.dot(hb, w_ref[...], preferred_element_type=F32)
            n_rot = u_all.shape[1] // V7X_LANES if kind == "rope_q" else KV_W // V7X_LANES
            for j in range(u_all.shape[1] // V7X_LANES):
                cols = pl.ds(j * V7X_LANES, V7X_LANES)
                u = u_all[:, j * V7X_LANES:(j + 1) * V7X_LANES]
                if j < n_rot:
                    u = (u * cos_ref[...]
                         + pltpu.roll(u, V7X_LANES - ROPE_AXIS_PAIRS, 1) * sina_ref[...]
                         + pltpu.roll(u, ROPE_AXIS_PAIRS, 1) * sinb_ref[...])
                    if kind == "rope_q":
                        u = u * (ATTN_DH ** -0.5 * LOG2E)
                o_ref[0, :, cols] = u.astype(BF16)


def _inproj(x, gain, mod3, shift_idx, scale_idx, per_batch_mod, rope, gate_b, weights, kinds, name):
    b, n, d = x.shape
    tm = min(ROW_TILE, n)
    cos, sina, sinb = rope
    mod_spec = lambda piece: pl.BlockSpec(
        (1, 1, d), (lambda bi, i: (bi, 0, piece)) if per_batch_mod else (lambda bi, i: (0, 0, piece)))
    in_specs = [pl.BlockSpec((1, tm, d), lambda bi, i: (bi, i, 0)),
                pl.BlockSpec((1, d), lambda bi, i: (0, 0)),
                mod_spec(shift_idx), mod_spec(scale_idx),
                pl.BlockSpec((tm, V7X_LANES), lambda bi, i: (i, 0)),
                pl.BlockSpec((tm, V7X_LANES), lambda bi, i: (i, 0)),
                pl.BlockSpec((tm, V7X_LANES), lambda bi, i: (i, 0)),
                pl.BlockSpec((1, N_GATES), lambda bi, i: (0, 0))]
    out_specs, out_shapes = [], []
    blk = _nbytes((tm, d), F32) + 3 * _nbytes((tm, V7X_LANES), F32)
    for kind, w in zip(kinds, weights):
        width = N_GATES if kind == "gates" else w.shape[1]
        dt = F32 if kind == "gates" else BF16
        in_specs.append(pl.BlockSpec(w.shape, lambda bi, i: (0, 0)))
        if kind == "heads":
            groups = width // V7X_LANES
            out_specs.append(pl.BlockSpec((1, groups, tm, V7X_LANES), lambda bi, i: (bi, 0, i, 0)))
            out_shapes.append(jax.ShapeDtypeStruct((b, groups, n, V7X_LANES), dt))
        else:
            out_specs.append(pl.BlockSpec((1, tm, width), lambda bi, i: (bi, i, 0)))
            out_shapes.append(jax.ShapeDtypeStruct((b, n, width), dt))
        blk += _nbytes(w.shape, BF16) + _nbytes((tm, width), dt) + _nbytes((tm, w.shape[1]), F32)
    return pl.pallas_call(
        functools.partial(_inproj_body, kinds=tuple(kinds)),
        grid=(b, n // tm),
        in_specs=in_specs,
        out_specs=out_specs,
        out_shape=out_shapes,
        compiler_params=_params(blk, 2),
        name=name,
    )(x, gain, mod3, mod3, cos, sina, sinb, gate_b, *weights)


def _silu(y):
    h = 0.5 * y
    return h + h * jnp.tanh(h)


def _sigmoid(y):
    return 0.5 + 0.5 * jnp.tanh(0.5 * y)


def _conv_silu_chunks(src_ref, dst_ref, w, b, scale, n_rows):
    t = MCHUNK
    ri = lax.broadcasted_iota(jnp.int32, (t, 2 * t), 0)
    ci = lax.broadcasted_iota(jnp.int32, (t, 2 * t), 1)
    shift_prev = jnp.where(ci == ri + t - 1, 1.0, 0.0).astype(BF16)
    shift_next = jnp.where(ci == ri + 1, 1.0, 0.0).astype(BF16)
    zeros = jnp.zeros((t, src_ref.shape[-1]), BF16)
    n = n_rows // t
    for c in range(n):
        x = src_ref[pl.ds(c * t, t), :]
        before = src_ref[pl.ds((c - 1) * t, t), :] if c > 0 else zeros
        after = src_ref[pl.ds((c + 1) * t, t), :] if c < n - 1 else zeros
        prev = jnp.dot(shift_prev, jnp.concatenate([before, x], axis=0), preferred_element_type=F32)
        nxt = jnp.dot(shift_next, jnp.concatenate([x, after], axis=0), preferred_element_type=F32)
        y = w[0:1, :] * prev + w[1:2, :] * x.astype(F32) + w[2:3, :] * nxt + b
        act = _silu(y) if scale == 1.0 else _silu(y) * scale
        dst_ref[pl.ds(c * t, t), :] = act.astype(BF16)


def _log_sigmoid(x):
    return jnp.minimum(x, 0.0) - jnp.log1p(jnp.exp(-jnp.abs(x)))


def _mlstm_body(q_ref, k_ref, v_ref, o_ref, kc_ref, vc_ref, gt_ref, wq_ref, bq_ref, wk_ref, bk_ref, gain_ref,
                y_ref, qs, ks, kcs, gtab, rtab, cumtab, cnl, c0n, cumc, sqk, sds, inter_s, em_s,
                *, n_ctx_chunks, n_lat_chunks):
    t = MCHUNK
    dh = MLSTM_DH
    n_chunks = n_ctx_chunks + n_lat_chunks
    kscale = MLSTM_DH ** -0.5
    nt = (((1,), (1,)), ((), ()))

    _conv_silu_chunks(q_ref, qs, wq_ref[...], bq_ref[...], 1.0, n_lat_chunks * t)
    _conv_silu_chunks(k_ref, ks, wk_ref[...], bk_ref[...], kscale, n_lat_chunks * t)
    _conv_silu_chunks(kc_ref, kcs, wk_ref[...], bk_ref[...], kscale, n_ctx_chunks * t)

    tbl = gt_ref[0, 0]
    rid = lax.broadcasted_iota(jnp.int32, tbl.shape, 0) // MAX_CHUNKS
    lane = lax.broadcasted_iota(jnp.int32, tbl.shape, 1)
    lf = _log_sigmoid(tbl)
    pre = jnp.where(rid == 1, lf, 0.0)
    suf = jnp.where(rid == 3, lf, 0.0)
    s = 1
    while s < t:
        pre = pre + jnp.where(lane >= s, pltpu.roll(pre, s, 1), 0.0)
        suf = suf + jnp.where(lane < t - s, pltpu.roll(suf, t - s, 1), 0.0)
        s *= 2
    li = (tbl[0:MAX_CHUNKS] * LOG2E, tbl[2 * MAX_CHUNKS:3 * MAX_CHUNKS] * LOG2E)
    cum = (pre[MAX_CHUNKS:2 * MAX_CHUNKS] * LOG2E, suf[3 * MAX_CHUNKS:4 * MAX_CHUNKS] * LOG2E)
    total = (cum[0][:, t - 1:t], cum[1][:, 0:1])
    mloc = []
    for d in (0, 1):
        g = total[d] - cum[d] + li[d]
        mloc.append(jnp.max(g, axis=1, keepdims=True))
        gtab[d] = g
        rtab[d] = cum[d] - li[d]
        cumtab[d] = cum[d]

    orders = (list(range(n_chunks)),
              list(range(n_ctx_chunks - 1, -1, -1)) + list(range(n_chunks - 1, n_ctx_chunks - 1, -1)))

    a_sc, m0, m1 = ({}, {}), ({}, {}), ({}, {})
    for d in (0, 1):
        m_st = jnp.zeros((1, 1), F32)
        for c in orders[d][:-1]:
            m0[d][c] = m_st
            tot_c = total[d][c:c + 1, :]
            m1[d][c] = jnp.maximum(tot_c + m_st, mloc[d][c:c + 1, :])
            a_sc[d][c] = jnp.exp2(tot_c + m_st - m1[d][c])
            m_st = m1[d][c]
        m0[d][orders[d][-1]] = m_st

    def kv_chunk(c):
        if c < n_ctx_chunks:
            rows = pl.ds(c * t, t)
            return kcs[rows, :], vc_ref[rows, :]
        rows = pl.ds((c - n_ctx_chunks) * t, t)
        return ks[rows, :], v_ref[rows, :]

    ri = lax.broadcasted_iota(jnp.int32, (t, t), 0)
    ci = lax.broadcasted_iota(jnp.int32, (t, t), 1)
    eye = jnp.where(ri == ci, 1.0, 0.0).astype(BF16)
    ones = jnp.ones((t, dh), BF16)

    for c in range(n_chunks):
        k, v = kv_chunk(c)
        k_t = lax.dot_general(eye, k, nt, preferred_element_type=F32)
        vo = jnp.concatenate([v, ones], axis=1)
        for d in (0, 1):
            if c != orders[d][-1]:
                w = jnp.exp2(gtab[d, pl.ds(c, 1), :] - m1[d][c])
                cnl[d, c] = jnp.dot((k_t * w).astype(BF16), vo, preferred_element_type=F32)

    for d in (0, 1):
        cn = jnp.zeros((dh, 2 * dh), F32)
        for c in orders[d]:
            if c >= n_ctx_chunks:
                c0n[d, c - n_ctx_chunks] = cn.astype(BF16)
            if c != orders[d][-1]:
                cn = a_sc[d][c] * cn + cnl[d, c]

    for j in range(n_lat_chunks):
        c = j + n_ctx_chunks
        rows = pl.ds(j * t, t)
        sqk[rows, :] = lax.dot_general(qs[rows, :], ks[rows, :], nt, preferred_element_type=F32)
        for d in (0, 1):
            cumc[d, rows, :] = jnp.broadcast_to(cumtab[d, pl.ds(c, 1), :], (t, t)).T

    grp = 2
    gi = lax.broadcasted_iota(jnp.int32, (grp * t, t), 0) & (t - 1)
    gs = lax.broadcasted_iota(jnp.int32, (grp * t, t), 1)
    for d in (0, 1):
        ok = (gs <= gi) if d == 0 else (gs >= gi)
        for j0 in range(0, n_lat_chunks, grp):
            rows = pl.ds(j0 * t, grp * t)
            cs = [j + n_ctx_chunks for j in range(j0, j0 + grp)]
            rb = jnp.concatenate([jnp.broadcast_to(rtab[d, pl.ds(c, 1), :], (t, t)) for c in cs], axis=0)
            m0b = jnp.concatenate([jnp.broadcast_to(m0[d][c], (t, t)) for c in cs], axis=0)
            cum_c = cumc[d, rows, :]
            d_log = jnp.where(ok, cum_c - rb, -jnp.inf)
            a_log = cum_c + m0b
            m_t = jnp.maximum(a_log, jnp.max(d_log, axis=1, keepdims=True))
            sds[d, rows, :] = sqk[rows, :].astype(BF16) * jnp.exp2((d_log - m_t).astype(BF16))
            inter_s[d, rows, :] = jnp.exp2(a_log - m_t)
            em_s[d, rows, :] = jnp.exp2(-m_t)

    gain = gain_ref[...]
    for j in range(n_lat_chunks):
        rows = pl.ds(j * t, t)
        q = qs[rows, :]
        vo = jnp.concatenate([v_ref[rows, :], ones], axis=1)
        hid = None
        for d in (0, 1):
            sv = jnp.dot(sds[d, rows, :], vo, preferred_element_type=F32)
            qc = jnp.dot(q, c0n[d, j], preferred_element_type=F32)
            inter = inter_s[d, rows, :]
            num = sv[:, :dh] + inter * qc[:, :dh]
            den = sv[:, dh:] + inter * qc[:, dh:]
            h_d = num / jnp.maximum(jnp.abs(den), em_s[d, rows, :])
            hid = h_d if hid is None else hid + h_d
        hn = hid * lax.rsqrt(jnp.mean(hid * hid, axis=-1, keepdims=True) + EPS) * gain
        y_ref[rows, :] = (hn * _sigmoid(o_ref[rows, :].astype(F32))).astype(BF16)


def _mlstm(qkvo, kv_ctx, gate_tbl, conv_w, conv_b, gain):
    b, _, l, _ = qkvo.shape
    n_ctx = kv_ctx.shape[2]
    dh = MLSTM_DH
    nh = MLSTM_HEADS
    n_ctx_chunks, n_lat_chunks = n_ctx // MCHUNK, l // MCHUNK
    seq = lambda slab0: pl.BlockSpec((None, None, l, dh), lambda bi, h: (bi, slab0 + h, 0, 0))
    ctx = lambda slab0: pl.BlockSpec((None, None, n_ctx, dh), lambda bi, h: (bi, slab0 + h, 0, 0))
    n_chunks = n_ctx_chunks + n_lat_chunks
    scratch = [((l, dh), BF16), ((l, dh), BF16), ((n_ctx, dh), BF16),
               ((2, MAX_CHUNKS, MCHUNK), F32), ((2, MAX_CHUNKS, MCHUNK), F32),
               ((2, MAX_CHUNKS, MCHUNK), F32),
               ((2, n_chunks, dh, 2 * dh), F32), ((2, n_lat_chunks, dh, 2 * dh), BF16),
               ((2, l, MCHUNK), F32), ((l, MCHUNK), F32), ((2, l, MCHUNK), BF16),
               ((2, l, MCHUNK), F32), ((2, l, MCHUNK), F32)]
    blk = (5 * _nbytes((l, dh), BF16) + 2 * _nbytes((n_ctx, dh), BF16) + _nbytes((V7X_LANES, V7X_LANES), F32)
           + sum(_nbytes(s, dt) for s, dt in scratch) // 2)
    return pl.pallas_call(
        functools.partial(_mlstm_body, n_ctx_chunks=n_ctx_chunks, n_lat_chunks=n_lat_chunks),
        grid=(b, nh),
        in_specs=[seq(0), seq(nh), seq(2 * nh), seq(3 * nh), ctx(0), ctx(nh),
                  pl.BlockSpec((1, 1, 4 * MAX_CHUNKS, MCHUNK), lambda bi, h: (bi, h, 0, 0)),
                  pl.BlockSpec((3, dh), lambda bi, h: (0, h)),
                  pl.BlockSpec((1, dh), lambda bi, h: (0, h)),
                  pl.BlockSpec((3, dh), lambda bi, h: (0, nh + h)),
                  pl.BlockSpec((1, dh), lambda bi, h: (0, nh + h)),
                  pl.BlockSpec((1, dh), lambda bi, h: (0, h))],
        out_specs=pl.BlockSpec((None, None, l, dh), lambda bi, h: (bi, h, 0, 0)),
        out_shape=jax.ShapeDtypeStruct((b, nh, l, dh), BF16),
        scratch_shapes=[pltpu.VMEM(s, dt) for s, dt in scratch],
        compiler_params=_params(blk, 2),
        name="mlstm",
    )(qkvo, qkvo, qkvo, qkvo, kv_ctx, kv_ctx, gate_tbl, conv_w, conv_b, conv_w, conv_b, gain)


def _attn_body(sink_ref, bias_ref, q_ref, kv_ref, kvc_ref, o_ref, s_scr, p_scr, e_scr, *, seq_len):
    blk = ATTN_BLOCK
    n_blocks = seq_len // blk
    n_ctx = kvc_ref.shape[1]
    rows = GQA_GROUP * blk
    nt = (((1,), (1,)), ((), ()))
    chains = [(qb, g) for qb in range(ATTN_BLOCKS_PER_STEP) for g in range(KV_HEADS)]

    def block_start(qb):
        i = pl.program_id(1) * ATTN_BLOCKS_PER_STEP + qb
        return i, pl.multiple_of(jnp.clip((i - 1) * blk, 0, seq_len - BAND), blk)

    for ci, (qb, g) in enumerate(chains):
        i, start = block_start(qb)
        k_cols = pl.ds(g * ATTN_DH, ATTN_DH)
        q = jnp.concatenate([q_ref[0, pl.ds(qb * blk, blk), pl.ds((g * GQA_GROUP + j) * ATTN_DH, ATTN_DH)]
                             for j in range(GQA_GROUP)], axis=0)
        bias = bias_ref[jnp.where(i == 0, 0, jnp.where(i == n_blocks - 1, 2, 1))]
        s_scr[ci, :, pl.ds(0, BAND)] = (
            lax.dot_general(q, kv_ref[0, pl.ds(start, BAND), k_cols], nt, preferred_element_type=F32) + bias)
        s_scr[ci, :, pl.ds(BAND, n_ctx)] = lax.dot_general(q, kvc_ref[0, :, k_cols], nt, preferred_element_type=F32)

    for ci, (qb, g) in enumerate(chains):
        s = s_scr[ci]
        sink = jnp.concatenate([jnp.full((blk, V7X_LANES), sink_ref[g * GQA_GROUP + j] * LOG2E, F32)
                                for j in range(GQA_GROUP)], axis=0)
        m = jnp.maximum(jnp.broadcast_to(jnp.max(s, axis=1, keepdims=True), (rows, V7X_LANES)), sink)
        p_scr[ci] = jnp.exp2((s - jnp.tile(m, (1, (BAND + n_ctx) // V7X_LANES))).astype(BF16))
        e_scr[ci] = jnp.exp2(sink - m)

    ones_b = jnp.ones((BAND, ATTN_DH), BF16)
    ones_c = jnp.ones((n_ctx, ATTN_DH), BF16)
    for ci, (qb, g) in enumerate(chains):
        _, start = block_start(qb)
        v_cols = pl.ds(KV_W + g * ATTN_DH, ATTN_DH)
        vb = jnp.concatenate([kv_ref[0, pl.ds(start, BAND), v_cols], ones_b], axis=1)
        vc = jnp.concatenate([kvc_ref[0, :, v_cols], ones_c], axis=1)
        acc = (jnp.dot(p_scr[ci, :, pl.ds(0, BAND)], vb, preferred_element_type=F32)
               + jnp.dot(p_scr[ci, :, pl.ds(BAND, n_ctx)], vc, preferred_element_type=F32))
        out = (acc / (pltpu.roll(acc, ATTN_DH, 1) + e_scr[ci]))[:, :ATTN_DH]
        for j in range(GQA_GROUP):
            h = g * GQA_GROUP + j
            o_ref[0, pl.ds(qb * blk, blk), pl.ds(h * ATTN_DH, ATTN_DH)] = out[j * blk:(j + 1) * blk].astype(BF16)


def _band_bias(seq_len):
    row = jnp.arange(ATTN_BLOCK)[:, None]
    col = jnp.arange(BAND)[None, :]
    n_blocks = seq_len // ATTN_BLOCK
    tables = []
    for i in (0, 1, n_blocks - 1):
        start = min(max((i - 1) * ATTN_BLOCK, 0), seq_len - BAND)
        ok = jnp.abs(start + col - (i * ATTN_BLOCK + row)) <= WINDOW
        tables.append(jnp.tile(jnp.where(ok, 0.0, -jnp.inf).astype(F32), (GQA_GROUP, 1)))
    return jnp.stack(tables)


def _attn(sink, q, kv, kvc):
    b, l, _ = q.shape
    n_ctx = kvc.shape[1]
    rows = ATTN_BLOCKS_PER_STEP * ATTN_BLOCK
    bias = _band_bias(l)
    n_chains = KV_HEADS * ATTN_BLOCKS_PER_STEP
    stacked = GQA_GROUP * ATTN_BLOCK
    scratch = [((n_chains, stacked, BAND + n_ctx), F32), ((n_chains, stacked, BAND + n_ctx), BF16),
               ((n_chains, stacked, V7X_LANES), F32)]
    blk = (2 * _nbytes((rows, ATTN_W), BF16) + _nbytes((l, 2 * KV_W), BF16) + _nbytes((n_ctx, 2 * KV_W), BF16)
           + _nbytes(bias.shape, F32) + sum(_nbytes(s, dt) for s, dt in scratch) // 2)
    return pl.pallas_call(
        functools.partial(_attn_body, seq_len=l),
        grid=(b, l // rows),
        in_specs=[pl.BlockSpec(memory_space=pltpu.SMEM),
                  pl.BlockSpec(bias.shape, lambda bi, i: (0, 0, 0)),
                  pl.BlockSpec((1, rows, ATTN_W), lambda bi, i: (bi, i, 0)),
                  pl.BlockSpec((1, l, 2 * KV_W), lambda bi, i: (bi, 0, 0)),
                  pl.BlockSpec((1, n_ctx, 2 * KV_W), lambda bi, i: (bi, 0, 0))],
        out_specs=pl.BlockSpec((1, rows, ATTN_W), lambda bi, i: (bi, i, 0)),
        out_shape=jax.ShapeDtypeStruct((b, l, ATTN_W), BF16),
        scratch_shapes=[pltpu.VMEM(s, dt) for s, dt in scratch],
        compiler_params=_params(blk, 2),
        name="attn",
    )(sink, bias, q, kv, kvc)


GELU_C1 = 0.7978845608028654
GELU_C2 = GELU_C1 * 0.044715


def _twice_gelu_times(x, val):
    xv = x * val
    return xv + xv * jnp.tanh(x * (GELU_C1 + GELU_C2 * (x * x)))


def _mixffn_body(ym_ref, ymp_ref, ymn_ref, ya_ref, yap_ref, yan_ref, x_ref, xp_ref, xn_ref,
                 wo_ref, ga_ref, gain_ref, sh_ref, sc_ref, gf_ref, wup_ref, cw_ref, cb_ref, wdn_ref, fn_ref,
                 o_ref, ys, hs, x1s, acc):
    i = pl.program_id(1)
    tm = x_ref.shape[1]
    ext = tm + 2 * HALO
    lo, mid, hi = pl.ds(0, HALO), pl.ds(HALO, tm), pl.ds(HALO + tm, HALO)
    for rows, m_ref, a_ref in ((lo, ymp_ref, yap_ref), (mid, ym_ref, ya_ref), (hi, ymn_ref, yan_ref)):
        for h in range(MLSTM_HEADS):
            ys[rows, pl.ds(h * MLSTM_DH, MLSTM_DH)] = m_ref[0, h]
        ys[rows, pl.ds(MLSTM_W, ATTN_W)] = a_ref[0]
    proj = jnp.dot(ys[...], wo_ref[...], preferred_element_type=F32)
    ga = ga_ref[0]

    def mixed(x_rows, proj_rows):
        x1 = x_rows + ga * proj_rows
        return x1, _norm_modulate(x1, gain_ref[...], sh_ref[0], sc_ref[0]).astype(BF16)

    x1, h_mid = mixed(x_ref[0], proj[HALO:HALO + tm])
    x1s[...] = x1
    hs[mid, :] = h_mid
    h_lo = mixed(xp_ref[0], proj[:HALO])[1]
    h_hi = mixed(xn_ref[0], proj[HALO + tm:])[1]
    hs[lo, :] = jnp.where(i > 0, h_lo, jnp.zeros_like(h_lo))
    hs[hi, :] = jnp.where(i < pl.num_programs(1) - 1, h_hi, jnp.zeros_like(h_hi))
    starts = list(range(0, D_FF, FF_CHUNK))

    def up(col0):
        width = min(FF_CHUNK, D_FF - col0)
        return (jnp.dot(hs[...], wup_ref[:, pl.ds(col0, width)], preferred_element_type=F32),
                jnp.dot(hs[mid, :], wup_ref[:, pl.ds(D_FF + col0, width)], preferred_element_type=F32))

    nxt = up(starts[0])
    for c, col0 in enumerate(starts):
        cols = pl.ds(col0, min(FF_CHUNK, D_FF - col0))
        a, val = nxt
        if c + 1 < len(starts):
            nxt = up(starts[c + 1])
        a_prev = pltpu.roll(a, 1, 0)[HALO:HALO + tm]
        a_next = pltpu.roll(a, ext - 1, 0)[HALO:HALO + tm]
        conv = (cw_ref[0:1, cols] * a_prev + cw_ref[1:2, cols] * a[HALO:HALO + tm]
                + cw_ref[2:3, cols] * a_next + cb_ref[:, cols])
        part = jnp.dot(_twice_gelu_times(conv, val).astype(BF16), wdn_ref[cols, :], preferred_element_type=F32)
        if c == 0:
            acc[...] = part
        else:
            acc[...] += part
    x2 = x1s[...] + (0.5 * gf_ref[0]) * acc[...]
    ms = jnp.mean(x2 * x2, axis=-1, keepdims=True)
    o_ref[0] = x2 * lax.rsqrt(ms + EPS) * fn_ref[...]


def _mixffn(ym, ya, x, mod3, w_out, norm_gain, w_up, conv_w, conv_b, w_down, final_norm):
    b, l, d = x.shape
    tm = ROW_TILE
    per = tm // HALO
    last = l // HALO - 1
    ext = tm + 2 * HALO
    tile, before, after = (lambda bi, i: i), (lambda bi, i: jnp.maximum(i * per - 1, 0)), \
        (lambda bi, i: jnp.minimum((i + 1) * per, last))

    def rows3(make):
        return [make(tm, tile), make(HALO, before), make(HALO, after)]

    ym_spec = lambda n, r: pl.BlockSpec((1, MLSTM_HEADS, n, MLSTM_DH), lambda bi, i: (bi, 0, r(bi, i), 0))
    ya_spec = lambda n, r: pl.BlockSpec((1, n, ATTN_W), lambda bi, i: (bi, r(bi, i), 0))
    x_spec = lambda n, r: pl.BlockSpec((1, n, d), lambda bi, i: (bi, r(bi, i), 0))
    mod_spec = lambda piece: pl.BlockSpec((1, 1, d), lambda bi, i: (bi, 0, piece))
    const = lambda arr: pl.BlockSpec(arr.shape, lambda bi, i: (0,) * arr.ndim)
    resident = lambda arr: pl.BlockSpec(arr.shape, lambda bi, i: (0,) * arr.ndim, pipeline_mode=pl.Buffered(1))
    scratch = [((ext, d), BF16), ((ext, d), BF16), ((tm, d), F32), ((tm, d), F32)]
    weights = _nbytes(w_out.shape, BF16) + _nbytes(w_up.shape, BF16) + _nbytes(w_down.shape, BF16)
    blk = (2 * _nbytes((tm, d), F32) + 2 * _nbytes((tm, d), BF16) + 4 * _nbytes((ext, FF_CHUNK), F32)
           + (weights + sum(_nbytes(s, dt) for s, dt in scratch)) // 2)
    return pl.pallas_call(
        _mixffn_body,
        grid=(b, l // tm),
        in_specs=(rows3(ym_spec) + rows3(ya_spec) + rows3(x_spec)
                  + [resident(w_out), mod_spec(2), const(norm_gain), mod_spec(3), mod_spec(4), mod_spec(5),
                     resident(w_up), const(conv_w), const(conv_b), resident(w_down), const(final_norm)]),
        out_specs=pl.BlockSpec((1, tm, d), lambda bi, i: (bi, i, 0)),
        out_shape=jax.ShapeDtypeStruct((b, l, d), F32),
        scratch_shapes=[pltpu.VMEM(s, dt) for s, dt in scratch],
        compiler_params=_params(blk, 2),
        name="mixffn",
    )(ym, ym, ym, ya, ya, ya, x, x, x, w_out, mod3, norm_gain, mod3, mod3, mod3,
      w_up, conv_w, conv_b, w_down, final_norm)


def _rope_tables(n_tokens):
    pos = jnp.arange(n_tokens)
    r = (pos // GRID_W).astype(F32)
    c = (pos % GRID_W).astype(F32)
    inv = ROPE_BASE ** (-jnp.arange(ROPE_AXIS_PAIRS, dtype=F32) / ROPE_AXIS_PAIRS)
    ar, ac = r[:, None] * inv, c[:, None] * inv
    zero = jnp.zeros_like(ar)
    cos = jnp.concatenate([jnp.cos(ar), jnp.cos(ar), jnp.cos(ac), jnp.cos(ac)], axis=1)
    sina = jnp.concatenate([-jnp.sin(ar), zero, -jnp.sin(ac), zero], axis=1)
    sinb = jnp.concatenate([zero, jnp.sin(ar), zero, jnp.sin(ac)], axis=1)
    rep = V7X_LANES // ATTN_DH
    return tuple(jnp.tile(tb, (1, rep)) for tb in (cos, sina, sinb))


def _identity_rope(n_tokens):
    return (jnp.ones((n_tokens, V7X_LANES), F32), jnp.zeros((n_tokens, V7X_LANES), F32),
            jnp.zeros((n_tokens, V7X_LANES), F32))


def _gate_table(gates_ctx, gates_x):
    g = jnp.concatenate([gates_ctx, gates_x], axis=1)
    b, n, _ = g.shape
    nc = n // MCHUNK
    g = g.reshape(b, nc, MCHUNK, 4, MLSTM_HEADS).transpose(0, 4, 3, 1, 2)
    g = jnp.pad(g, ((0, 0), (0, 0), (0, 0), (0, MAX_CHUNKS - nc), (0, 0)))
    return g.reshape(b, MLSTM_HEADS, 4 * MAX_CHUNKS, MCHUNK)


def kernel(x, c, ctx, c_ctx, w_ada, b_ada, norm_mix, norm_ffn, w_in, gate_b, qk_conv_w, qk_conv_b, mlstm_norm,
           attn_sink, w_out, w_up, ffn_conv_w, ffn_conv_b, w_down, final_norm):
    b, l, d = x.shape
    n_ctx = ctx.shape[1]
    assert w_ada.shape[0] == 1, "single-layer stack"
    assert l % ROW_TILE == 0 and l % MCHUNK == 0 and n_ctx % MCHUNK == 0 and l >= BAND
    assert l % (ATTN_BLOCKS_PER_STEP * ATTN_BLOCK) == 0 and MCHUNK == V7X_LANES
    assert (l + n_ctx) // MCHUNK <= MAX_CHUNKS

    rows = -(-(b + 1) // V7X_SUBLANES) * V7X_SUBLANES
    cvec = jnp.zeros((rows, d), F32).at[:b].set(c).at[b].set(c_ctx)
    mod = _ada(cvec, w_ada[0], b_ada[0][None, :])
    mod_x = mod[:b].reshape(b, 1, 6 * d)
    mod_c = mod[b:b + 1].reshape(1, 1, 6 * d)

    w = w_in[0].astype(BF16)
    o1, o2 = 3 * MLSTM_W, 4 * MLSTM_W
    o3 = o2 + N_GATES
    o4 = o3 + ATTN_W
    w_g = jnp.pad(w[:, o2:o3], ((0, 0), (0, V7X_LANES - N_GATES)))
    gb = gate_b[0][None, :]
    gain_mix = norm_mix[0][None, :]

    kv_mc, gates_c, kv_ac = _inproj(
        ctx, gain_mix, mod_c, 0, 1, False, _identity_rope(n_ctx), gb,
        [w[:, MLSTM_W:o1], w_g, w[:, o4:]], ["heads", "gates", "plain"], "inproj_ctx")
    qkvo, gates_x, q_a, kv_a = _inproj(
        x, gain_mix, mod_x, 0, 1, True, _rope_tables(l), gb,
        [w[:, :o2], w_g, w[:, o3:o4], w[:, o4:]], ["heads", "gates", "rope_q", "rope_kv"], "inproj_x")

    y_m = _mlstm(qkvo, kv_mc, _gate_table(gates_c, gates_x),
                 qk_conv_w[0], qk_conv_b[0][None, :], mlstm_norm[0][None, :])
    y_a = _attn(attn_sink[0], q_a, kv_a, kv_ac)

    return _mixffn(y_m, y_a, x, mod_x, w_out[0].astype(BF16), norm_ffn[0][None, :], w_up[0].astype(BF16),
                   ffn_conv_w[0], ffn_conv_b[0][None, :], w_down[0].astype(BF16), final_norm[None, :])
```

```python
import functools

import jax
import jax.numpy as jnp
from jax import lax
from jax.experimental import pallas as pl
from jax.experimental.pallas import tpu as pltpu

F32 = jnp.float32
BF16 = jnp.bfloat16

D_MODEL = 1024
GRID_W = 64
MLSTM_HEADS = 4
MLSTM_DH = 128
MLSTM_W = MLSTM_HEADS * MLSTM_DH
N_GATES = 4 * MLSTM_HEADS
ATTN_HEADS = 8
KV_HEADS = 2
ATTN_DH = 64
ATTN_W = ATTN_HEADS * ATTN_DH
KV_W = KV_HEADS * ATTN_DH
GQA_GROUP = ATTN_HEADS // KV_HEADS
WINDOW = 128
ROPE_BASE = 10000.0
ROPE_AXIS_PAIRS = ATTN_DH // 4
D_FF = 2816
EPS = 1e-6
LOG2E = 1.4426950408889634

V7X_LANES = 128
V7X_SUBLANES = 8
V7X_VMEM_BYTES = 64 * 1024 * 1024

ROW_TILE = 512
ADA_COL_TILE = 1536
MCHUNK = 128
MAX_CHUNKS = 32
ATTN_BLOCK = 128
ATTN_BLOCKS_PER_STEP = 4
BAND = 3 * ATTN_BLOCK
FF_CHUNK = 512
HALO = 2 * V7X_SUBLANES


def _vmem_limit(block_bytes):
    return int(min(V7X_VMEM_BYTES * 7 // 8, 2 * block_bytes + 16 * 1024 * 1024))


def _params(block_bytes, n_axes):
    return pltpu.CompilerParams(dimension_semantics=("arbitrary",) * n_axes,
                                vmem_limit_bytes=_vmem_limit(block_bytes))


def _nbytes(shape, dtype):
    n = 1
    for s in shape:
        n *= s
    return n * jnp.dtype(dtype).itemsize


def _ada_body(c_ref, w_ref, b_ref, o_ref):
    c = c_ref[...]
    s = (c * jax.nn.sigmoid(c)).astype(BF16)
    o_ref[...] = jnp.dot(s, w_ref[...].astype(BF16), preferred_element_type=F32) + b_ref[...]


def _ada(cvec, w_ada, b_ada):
    rows, d = cvec.shape
    n = w_ada.shape[1]
    blk = _nbytes((d, ADA_COL_TILE), F32) + _nbytes((rows, d), F32) + 2 * _nbytes((rows, ADA_COL_TILE), F32)
    return pl.pallas_call(
        _ada_body,
        grid=(n // ADA_COL_TILE,),
        in_specs=[pl.BlockSpec((rows, d), lambda j: (0, 0)),
                  pl.BlockSpec((d, ADA_COL_TILE), lambda j: (0, j)),
                  pl.BlockSpec((1, ADA_COL_TILE), lambda j: (0, j))],
        out_specs=pl.BlockSpec((rows, ADA_COL_TILE), lambda j: (0, j)),
        out_shape=jax.ShapeDtypeStruct((rows, n), F32),
        compiler_params=_params(blk, 1),
        name="ada",
    )(cvec, w_ada, b_ada)


def _norm_modulate(x, gain, shift, scale):
    ms = jnp.mean(x * x, axis=-1, keepdims=True)
    return (x * lax.rsqrt(ms + EPS) * gain) * (1.0 + scale) + shift


def _silu(y):
    h = 0.5 * y
    return h + h * jnp.tanh(h)


def _inproj_body(x_ref, xp_ref, xn_ref, gain_ref, sh_ref, sc_ref, cos_ref, sina_ref, sinb_ref, gb_ref,
                 cw_ref, cb_ref, *refs, kinds):
    n = len(kinds)
    w_refs, o_refs, hs = refs[:n], refs[n:2 * n], refs[2 * n]
    i = pl.program_id(1)
    tm = x_ref.shape[1]
    ext = tm + 2 * HALO
    norm = lambda rows: _norm_modulate(rows, gain_ref[...], sh_ref[0], sc_ref[0]).astype(BF16)
    hb = norm(x_ref[0])
    h_lo, h_hi = norm(xp_ref[0]), norm(xn_ref[0])
    hs[pl.ds(0, HALO), :] = jnp.where(i > 0, h_lo, jnp.zeros_like(h_lo))
    hs[pl.ds(HALO, tm), :] = hb
    hs[pl.ds(HALO + tm, HALO), :] = jnp.where(i < pl.num_programs(1) - 1, h_hi, jnp.zeros_like(h_hi))
    for kind, w_ref, o_ref in zip(kinds, w_refs, o_refs):
        if kind in ("conv_qk", "conv_k"):
            a = jnp.dot(hs[...], w_ref[...], preferred_element_type=F32)
            width = a.shape[1]
            c0 = cw_ref.shape[1] - width
            a_prev = pltpu.roll(a, 1, 0)[HALO:HALO + tm]
            a_next = pltpu.roll(a, ext - 1, 0)[HALO:HALO + tm]
            y = _silu(cw_ref[0:1, c0:] * a_prev + cw_ref[1:2, c0:] * a[HALO:HALO + tm]
                      + cw_ref[2:3, c0:] * a_next + cb_ref[:, c0:])
            first_k = width // V7X_LANES - MLSTM_HEADS
            for j in range(width // V7X_LANES):
                u = y[:, j * V7X_LANES:(j + 1) * V7X_LANES]
                o_ref[0, j] = (u * MLSTM_DH ** -0.5 if j >= first_k else u).astype(BF16)
        elif kind == "gates":
            u = jnp.dot(hb, w_ref[...], preferred_element_type=F32)
            o_ref[0] = u[:, :N_GATES] + gb_ref[...]
        elif kind == "plain":
            o_ref[0] = jnp.dot(hb, w_ref[...], preferred_element_type=F32).astype(BF16)
        elif kind == "heads":
            u = jnp.dot(hb, w_ref[...], preferred_element_type=F32)
            for j in range(u.shape[1] // V7X_LANES):
                o_ref[0, j] = u[:, j * V7X_LANES:(j + 1) * V7X_LANES].astype(BF16)
        else:
            u_all = jnp.dot(hb, w_ref[...], preferred_element_type=F32)
            n_rot = u_all.shape[1] // V7X_LANES if kind == "rope_q" else KV_W // V7X_LANES
            for j in range(u_all.shape[1] // V7X_LANES):
                cols = pl.ds(j * V7X_LANES, V7X_LANES)
                u = u_all[:, j * V7X_LANES:(j + 1) * V7X_LANES]
                if j < n_rot:
                    u = (u * cos_ref[...]
                         + pltpu.roll(u, V7X_LANES - ROPE_AXIS_PAIRS, 1) * sina_ref[...]
                         + pltpu.roll(u, ROPE_AXIS_PAIRS, 1) * sinb_ref[...])
                    if kind == "rope_q":
                        u = u * (ATTN_DH ** -0.5 * LOG2E)
                o_ref[0, :, cols] = u.astype(BF16)


def _inproj(x, gain, mod3, shift_idx, scale_idx, per_batch_mod, rope, gate_b, conv_w, conv_b, weights, kinds, name):
    b, n, d = x.shape
    tm = min(ROW_TILE, n)
    per = tm // HALO
    last = n // HALO - 1
    cos, sina, sinb = rope
    mod_spec = lambda piece: pl.BlockSpec(
        (1, 1, d), (lambda bi, i: (bi, 0, piece)) if per_batch_mod else (lambda bi, i: (0, 0, piece)))
    in_specs = [pl.BlockSpec((1, tm, d), lambda bi, i: (bi, i, 0)),
                pl.BlockSpec((1, HALO, d), lambda bi, i: (bi, jnp.maximum(i * per - 1, 0), 0)),
                pl.BlockSpec((1, HALO, d), lambda bi, i: (bi, jnp.minimum((i + 1) * per, last), 0)),
                pl.BlockSpec((1, d), lambda bi, i: (0, 0)),
                mod_spec(shift_idx), mod_spec(scale_idx),
                pl.BlockSpec((tm, V7X_LANES), lambda bi, i: (i, 0)),
                pl.BlockSpec((tm, V7X_LANES), lambda bi, i: (i, 0)),
                pl.BlockSpec((tm, V7X_LANES), lambda bi, i: (i, 0)),
                pl.BlockSpec((1, N_GATES), lambda bi, i: (0, 0)),
                pl.BlockSpec(conv_w.shape, lambda bi, i: (0, 0)),
                pl.BlockSpec(conv_b.shape, lambda bi, i: (0, 0))]
    out_specs, out_shapes = [], []
    blk = (_nbytes((tm, d), F32) + 3 * _nbytes((tm, V7X_LANES), F32) + _nbytes((tm + 2 * HALO, d), BF16)
           + 4 * _nbytes((tm + 2 * HALO, 2 * MLSTM_W), F32))
    for kind, w in zip(kinds, weights):
        width = N_GATES if kind == "gates" else w.shape[1]
        dt = F32 if kind == "gates" else BF16
        in_specs.append(pl.BlockSpec(w.shape, lambda bi, i: (0, 0)))
        if kind in ("heads", "conv_qk", "conv_k"):
            groups = width // V7X_LANES
            out_specs.append(pl.BlockSpec((1, groups, tm, V7X_LANES), lambda bi, i: (bi, 0, i, 0)))
            out_shapes.append(jax.ShapeDtypeStruct((b, groups, n, V7X_LANES), dt))
        else:
            out_specs.append(pl.BlockSpec((1, tm, width), lambda bi, i: (bi, i, 0)))
            out_shapes.append(jax.ShapeDtypeStruct((b, n, width), dt))
        blk += _nbytes(w.shape, BF16) + _nbytes((tm, width), dt) + _nbytes((tm, w.shape[1]), F32)
    return pl.pallas_call(
        functools.partial(_inproj_body, kinds=tuple(kinds)),
        grid=(b, n // tm),
        in_specs=in_specs,
        out_specs=out_specs,
        out_shape=out_shapes,
        scratch_shapes=[pltpu.VMEM((tm + 2 * HALO, d), BF16)],
        compiler_params=_params(blk, 2),
        name=name,
    )(x, x, x, gain, mod3, mod3, cos, sina, sinb, gate_b, conv_w, conv_b, *weights)


def _sigmoid(y):
    return 0.5 + 0.5 * jnp.tanh(0.5 * y)


def _log_sigmoid(x):
    return jnp.minimum(x, 0.0) - jnp.log1p(jnp.exp(-jnp.abs(x)))


def _mlstm_body(qs, ks, v_ref, o_ref, kcs, vc_ref, gt_ref, gain_ref,
                y_ref, gtab, rtab, cumtab, kts, cnl, c0n, cumc, sqk, sds, inter_s, em_s,
                *, n_ctx_chunks, n_lat_chunks):
    t = MCHUNK
    dh = MLSTM_DH
    n_chunks = n_ctx_chunks + n_lat_chunks
    nt = (((1,), (1,)), ((), ()))

    tbl = gt_ref[0, 0]
    rid = lax.broadcasted_iota(jnp.int32, tbl.shape, 0) // MAX_CHUNKS
    lane = lax.broadcasted_iota(jnp.int32, tbl.shape, 1)
    lf = _log_sigmoid(tbl)
    pre = jnp.where(rid == 1, lf, 0.0)
    suf = jnp.where(rid == 3, lf, 0.0)
    s = 1
    while s < t:
        pre = pre + jnp.where(lane >= s, pltpu.roll(pre, s, 1), 0.0)
        suf = suf + jnp.where(lane < t - s, pltpu.roll(suf, t - s, 1), 0.0)
        s *= 2
    li = (tbl[0:MAX_CHUNKS] * LOG2E, tbl[2 * MAX_CHUNKS:3 * MAX_CHUNKS] * LOG2E)
    cum = (pre[MAX_CHUNKS:2 * MAX_CHUNKS] * LOG2E, suf[3 * MAX_CHUNKS:4 * MAX_CHUNKS] * LOG2E)
    total = (cum[0][:, t - 1:t], cum[1][:, 0:1])
    mloc = []
    for d in (0, 1):
        g = total[d] - cum[d] + li[d]
        mloc.append(jnp.max(g, axis=1, keepdims=True))
        gtab[d] = g
        rtab[d] = cum[d] - li[d]
        cumtab[d] = cum[d]

    orders = (list(range(n_chunks)),
              list(range(n_ctx_chunks - 1, -1, -1)) + list(range(n_chunks - 1, n_ctx_chunks - 1, -1)))

    a_sc, m0, m1 = ({}, {}), ({}, {}), ({}, {})
    for d in (0, 1):
        m_st = jnp.zeros((1, 1), F32)
        for c in orders[d][:-1]:
            m0[d][c] = m_st
            tot_c = total[d][c:c + 1, :]
            m1[d][c] = jnp.maximum(tot_c + m_st, mloc[d][c:c + 1, :])
            a_sc[d][c] = jnp.exp2(tot_c + m_st - m1[d][c])
            m_st = m1[d][c]
        m0[d][orders[d][-1]] = m_st

    def kv_chunk(c):
        if c < n_ctx_chunks:
            rows = pl.ds(c * t, t)
            return kcs[rows, :], vc_ref[rows, :]
        rows = pl.ds((c - n_ctx_chunks) * t, t)
        return ks[rows, :], v_ref[rows, :]

    ri = lax.broadcasted_iota(jnp.int32, (t, t), 0)
    ci = lax.broadcasted_iota(jnp.int32, (t, t), 1)
    eye = jnp.where(ri == ci, 1.0, 0.0).astype(BF16)
    ones = jnp.ones((t, dh), BF16)

    for c in range(n_chunks):
        kts[c] = lax.dot_general(eye, kv_chunk(c)[0], nt, preferred_element_type=F32)
    for c in range(n_chunks):
        vo = jnp.concatenate([kv_chunk(c)[1], ones], axis=1)
        for d in (0, 1):
            if c != orders[d][-1]:
                w = jnp.exp2(gtab[d, pl.ds(c, 1), :] - m1[d][c])
                cnl[d, c] = jnp.dot((kts[c] * w).astype(BF16), vo, preferred_element_type=F32)

    for d in (0, 1):
        cn = jnp.zeros((dh, 2 * dh), F32)
        for c in orders[d]:
            if c >= n_ctx_chunks:
                c0n[d, c - n_ctx_chunks] = cn.astype(BF16)
            if c != orders[d][-1]:
                cn = a_sc[d][c] * cn + cnl[d, c]

    for j in range(n_lat_chunks):
        c = j + n_ctx_chunks
        rows = pl.ds(j * t, t)
        sqk[rows, :] = lax.dot_general(qs[rows, :], ks[rows, :], nt, preferred_element_type=F32)
        for d in (0, 1):
            cumc[d, rows, :] = jnp.broadcast_to(cumtab[d, pl.ds(c, 1), :], (t, t)).T

    grp = 2
    gi = lax.broadcasted_iota(jnp.int32, (grp * t, t), 0) & (t - 1)
    gs = lax.broadcasted_iota(jnp.int32, (grp * t, t), 1)
    for d in (0, 1):
        ok = (gs <= gi) if d == 0 else (gs >= gi)
        for j0 in range(0, n_lat_chunks, grp):
            rows = pl.ds(j0 * t, grp * t)
            cs = [j + n_ctx_chunks for j in range(j0, j0 + grp)]
            rb = jnp.concatenate([jnp.broadcast_to(rtab[d, pl.ds(c, 1), :], (t, t)) for c in cs], axis=0)
            m0b = jnp.concatenate([jnp.broadcast_to(m0[d][c], (t, t)) for c in cs], axis=0)
            cum_c = cumc[d, rows, :]
            d_log = jnp.where(ok, cum_c - rb, -jnp.inf)
            a_log = cum_c + m0b
            m_t = jnp.maximum(a_log, jnp.max(d_log, axis=1, keepdims=True))
            sds[d, rows, :] = sqk[rows, :].astype(BF16) * jnp.exp2((d_log - m_t).astype(BF16))
            inter_s[d, rows, :] = jnp.exp2(a_log - m_t)
            em_s[d, rows, :] = jnp.exp2(-m_t)

    gain = gain_ref[...]
    for j in range(n_lat_chunks):
        rows = pl.ds(j * t, t)
        q = qs[rows, :]
        vo = jnp.concatenate([v_ref[rows, :], ones], axis=1)
        hid = None
        for d in (0, 1):
            sv = jnp.dot(sds[d, rows, :], vo, preferred_element_type=F32)
            qc = jnp.dot(q, c0n[d, j], preferred_element_type=F32)
            inter = inter_s[d, rows, :]
            num = sv[:, :dh] + inter * qc[:, :dh]
            den = sv[:, dh:] + inter * qc[:, dh:]
            h_d = num / jnp.maximum(jnp.abs(den), em_s[d, rows, :])
            hid = h_d if hid is None else hid + h_d
        hn = hid * lax.rsqrt(jnp.mean(hid * hid, axis=-1, keepdims=True) + EPS) * gain
        y_ref[rows, :] = (hn * _sigmoid(o_ref[rows, :].astype(F32))).astype(BF16)


def _mlstm(qk, vo, k_ctx, v_ctx, gate_tbl, gain):
    b, _, l, _ = qk.shape
    n_ctx = k_ctx.shape[2]
    dh = MLSTM_DH
    nh = MLSTM_HEADS
    n_ctx_chunks, n_lat_chunks = n_ctx // MCHUNK, l // MCHUNK
    seq = lambda slab0: pl.BlockSpec((None, None, l, dh), lambda bi, h: (bi, slab0 + h, 0, 0))
    ctx = lambda slab0: pl.BlockSpec((None, None, n_ctx, dh), lambda bi, h: (bi, slab0 + h, 0, 0))
    n_chunks = n_ctx_chunks + n_lat_chunks
    scratch = [((2, MAX_CHUNKS, MCHUNK), F32), ((2, MAX_CHUNKS, MCHUNK), F32),
               ((2, MAX_CHUNKS, MCHUNK), F32), ((n_chunks, dh, MCHUNK), F32),
               ((2, n_chunks, dh, 2 * dh), F32), ((2, n_lat_chunks, dh, 2 * dh), BF16),
               ((2, l, MCHUNK), F32), ((l, MCHUNK), F32), ((2, l, MCHUNK), BF16),
               ((2, l, MCHUNK), F32), ((2, l, MCHUNK), F32)]
    blk = (5 * _nbytes((l, dh), BF16) + 2 * _nbytes((n_ctx, dh), BF16) + _nbytes((V7X_LANES, V7X_LANES), F32)
           + sum(_nbytes(s, dt) for s, dt in scratch) // 2)
    return pl.pallas_call(
        functools.partial(_mlstm_body, n_ctx_chunks=n_ctx_chunks, n_lat_chunks=n_lat_chunks),
        grid=(b, nh),
        in_specs=[seq(0), seq(nh), seq(0), seq(nh), ctx(0), ctx(0),
                  pl.BlockSpec((1, 1, 4 * MAX_CHUNKS, MCHUNK), lambda bi, h: (bi, h, 0, 0)),
                  pl.BlockSpec((1, dh), lambda bi, h: (0, h))],
        out_specs=pl.BlockSpec((None, None, l, dh), lambda bi, h: (bi, h, 0, 0)),
        out_shape=jax.ShapeDtypeStruct((b, nh, l, dh), BF16),
        scratch_shapes=[pltpu.VMEM(s, dt) for s, dt in scratch],
        compiler_params=_params(blk, 2),
        name="mlstm",
    )(qk, qk, vo, vo, k_ctx, v_ctx, gate_tbl, gain)


def _attn_body(sink_ref, bias_ref, q_ref, kv_ref, kvc_ref, o_ref, s_scr, p_scr, e_scr, *, seq_len):
    blk = ATTN_BLOCK
    n_blocks = seq_len // blk
    n_ctx = kvc_ref.shape[1]
    rows = GQA_GROUP * blk
    nt = (((1,), (1,)), ((), ()))
    chains = [(qb, g) for qb in range(ATTN_BLOCKS_PER_STEP) for g in range(KV_HEADS)]

    def block_start(qb):
        i = pl.program_id(1) * ATTN_BLOCKS_PER_STEP + qb
        return i, pl.multiple_of(jnp.clip((i - 1) * blk, 0, seq_len - BAND), blk)

    for ci, (qb, g) in enumerate(chains):
        i, start = block_start(qb)
        k_cols = pl.ds(g * ATTN_DH, ATTN_DH)
        q = jnp.concatenate([q_ref[0, pl.ds(qb * blk, blk), pl.ds((g * GQA_GROUP + j) * ATTN_DH, ATTN_DH)]
                             for j in range(GQA_GROUP)], axis=0)
        bias = bias_ref[jnp.where(i == 0, 0, jnp.where(i == n_blocks - 1, 2, 1))]
        s_scr[ci, :, pl.ds(0, BAND)] = (
            lax.dot_general(q, kv_ref[0, pl.ds(start, BAND), k_cols], nt, preferred_element_type=F32) + bias)
        s_scr[ci, :, pl.ds(BAND, n_ctx)] = lax.dot_general(q, kvc_ref[0, :, k_cols], nt, preferred_element_type=F32)

    for ci, (qb, g) in enumerate(chains):
        s = s_scr[ci]
        sink = jnp.concatenate([jnp.full((blk, V7X_LANES), sink_ref[g * GQA_GROUP + j] * LOG2E, F32)
                                for j in range(GQA_GROUP)], axis=0)
        m = jnp.maximum(jnp.broadcast_to(jnp.max(s, axis=1, keepdims=True), (rows, V7X_LANES)), sink)
        p_scr[ci] = jnp.exp2((s - jnp.tile(m, (1, (BAND + n_ctx) // V7X_LANES))).astype(BF16))
        e_scr[ci] = jnp.exp2(sink - m)

    ones_b = jnp.ones((BAND, ATTN_DH), BF16)
    ones_c = jnp.ones((n_ctx, ATTN_DH), BF16)
    for ci, (qb, g) in enumerate(chains):
        _, start = block_start(qb)
        v_cols = pl.ds(KV_W + g * ATTN_DH, ATTN_DH)
        vb = jnp.concatenate([kv_ref[0, pl.ds(start, BAND), v_cols], ones_b], axis=1)
        vc = jnp.concatenate([kvc_ref[0, :, v_cols], ones_c], axis=1)
        acc = (jnp.dot(p_scr[ci, :, pl.ds(0, BAND)], vb, preferred_element_type=F32)
               + jnp.dot(p_scr[ci, :, pl.ds(BAND, n_ctx)], vc, preferred_element_type=F32))
        out = (acc / (pltpu.roll(acc, ATTN_DH, 1) + e_scr[ci]))[:, :ATTN_DH]
        for j in range(GQA_GROUP):
            h = g * GQA_GROUP + j
            o_ref[0, pl.ds(qb * blk, blk), pl.ds(h * ATTN_DH, ATTN_DH)] = out[j * blk:(j + 1) * blk].astype(BF16)


def _band_bias(seq_len):
    row = jnp.arange(ATTN_BLOCK)[:, None]
    col = jnp.arange(BAND)[None, :]
    n_blocks = seq_len // ATTN_BLOCK
    tables = []
    for i in (0, 1, n_blocks - 1):
        start = min(max((i - 1) * ATTN_BLOCK, 0), seq_len - BAND)
        ok = jnp.abs(start + col - (i * ATTN_BLOCK + row)) <= WINDOW
        tables.append(jnp.tile(jnp.where(ok, 0.0, -jnp.inf).astype(F32), (GQA_GROUP, 1)))
    return jnp.stack(tables)


def _attn(sink, q, kv, kvc):
    b, l, _ = q.shape
    n_ctx = kvc.shape[1]
    rows = ATTN_BLOCKS_PER_STEP * ATTN_BLOCK
    bias = _band_bias(l)
    n_chains = KV_HEADS * ATTN_BLOCKS_PER_STEP
    stacked = GQA_GROUP * ATTN_BLOCK
    scratch = [((n_chains, stacked, BAND + n_ctx), F32), ((n_chains, stacked, BAND + n_ctx), BF16),
               ((n_chains, stacked, V7X_LANES), F32)]
    blk = (2 * _nbytes((rows, ATTN_W), BF16) + _nbytes((l, 2 * KV_W), BF16) + _nbytes((n_ctx, 2 * KV_W), BF16)
           + _nbytes(bias.shape, F32) + sum(_nbytes(s, dt) for s, dt in scratch) // 2)
    return pl.pallas_call(
        functools.partial(_attn_body, seq_len=l),
        grid=(b, l // rows),
        in_specs=[pl.BlockSpec(memory_space=pltpu.SMEM),
                  pl.BlockSpec(bias.shape, lambda bi, i: (0, 0, 0)),
                  pl.BlockSpec((1, rows, ATTN_W), lambda bi, i: (bi, i, 0)),
                  pl.BlockSpec((1, l, 2 * KV_W), lambda bi, i: (bi, 0, 0)),
                  pl.BlockSpec((1, n_ctx, 2 * KV_W), lambda bi, i: (bi, 0, 0))],
        out_specs=pl.BlockSpec((1, rows, ATTN_W), lambda bi, i: (bi, i, 0)),
        out_shape=jax.ShapeDtypeStruct((b, l, ATTN_W), BF16),
        scratch_shapes=[pltpu.VMEM(s, dt) for s, dt in scratch],
        compiler_params=_params(blk, 2),
        name="attn",
    )(sink, bias, q, kv, kvc)


GELU_C1 = 0.7978845608028654
GELU_C2 = GELU_C1 * 0.044715


def _twice_gelu_times(x, val):
    xv = x * val
    return xv + xv * jnp.tanh(x * (GELU_C1 + GELU_C2 * (x * x)))


def _mixffn_body(ym_ref, ymp_ref, ymn_ref, ya_ref, yap_ref, yan_ref, x_ref, xp_ref, xn_ref,
                 wo_ref, ga_ref, gain_ref, sh_ref, sc_ref, gf_ref, wup_ref, cw_ref, cb_ref, wdn_ref, fn_ref,
                 o_ref, ys, hs, x1s, acc):
    i = pl.program_id(1)
    tm = x_ref.shape[1]
    ext = tm + 2 * HALO
    lo, mid, hi = pl.ds(0, HALO), pl.ds(HALO, tm), pl.ds(HALO + tm, HALO)
    for rows, m_ref, a_ref in ((lo, ymp_ref, yap_ref), (mid, ym_ref, ya_ref), (hi, ymn_ref, yan_ref)):
        for h in range(MLSTM_HEADS):
            ys[rows, pl.ds(h * MLSTM_DH, MLSTM_DH)] = m_ref[0, h]
        ys[rows, pl.ds(MLSTM_W, ATTN_W)] = a_ref[0]
    proj = jnp.dot(ys[...], wo_ref[...], preferred_element_type=F32)
    ga = ga_ref[0]

    def mixed(x_rows, proj_rows):
        x1 = x_rows + ga * proj_rows
        return x1, _norm_modulate(x1, gain_ref[...], sh_ref[0], sc_ref[0]).astype(BF16)

    x1, h_mid = mixed(x_ref[0], proj[HALO:HALO + tm])
    x1s[...] = x1
    hs[mid, :] = h_mid
    h_lo = mixed(xp_ref[0], proj[:HALO])[1]
    h_hi = mixed(xn_ref[0], proj[HALO + tm:])[1]
    hs[lo, :] = jnp.where(i > 0, h_lo, jnp.zeros_like(h_lo))
    hs[hi, :] = jnp.where(i < pl.num_programs(1) - 1, h_hi, jnp.zeros_like(h_hi))
    starts = list(range(0, D_FF, FF_CHUNK))

    def up(col0):
        width = min(FF_CHUNK, D_FF - col0)
        return (jnp.dot(hs[...], wup_ref[:, pl.ds(col0, width)], preferred_element_type=F32),
                jnp.dot(hs[mid, :], wup_ref[:, pl.ds(D_FF + col0, width)], preferred_element_type=F32))

    nxt = up(starts[0])
    for c, col0 in enumerate(starts):
        cols = pl.ds(col0, min(FF_CHUNK, D_FF - col0))
        a, val = nxt
        if c + 1 < len(starts):
            nxt = up(starts[c + 1])
        a_prev = pltpu.roll(a, 1, 0)[HALO:HALO + tm]
        a_next = pltpu.roll(a, ext - 1, 0)[HALO:HALO + tm]
        conv = (cw_ref[0:1, cols] * a_prev + cw_ref[1:2, cols] * a[HALO:HALO + tm]
                + cw_ref[2:3, cols] * a_next + cb_ref[:, cols])
        part = jnp.dot(_twice_gelu_times(conv, val).astype(BF16), wdn_ref[cols, :], preferred_element_type=F32)
        if c == 0:
            acc[...] = part
        else:
            acc[...] += part
    x2 = x1s[...] + (0.5 * gf_ref[0]) * acc[...]
    ms = jnp.mean(x2 * x2, axis=-1, keepdims=True)
    o_ref[0] = x2 * lax.rsqrt(ms + EPS) * fn_ref[...]


def _mixffn(ym, ya, x, mod3, w_out, norm_gain, w_up, conv_w, conv_b, w_down, final_norm):
    b, l, d = x.shape
    tm = ROW_TILE
    per = tm // HALO
    last = l // HALO - 1
    ext = tm + 2 * HALO
    tile, before, after = (lambda bi, i: i), (lambda bi, i: jnp.maximum(i * per - 1, 0)), \
        (lambda bi, i: jnp.minimum((i + 1) * per, last))

    def rows3(make):
        return [make(tm, tile), make(HALO, before), make(HALO, after)]

    ym_spec = lambda n, r: pl.BlockSpec((1, MLSTM_HEADS, n, MLSTM_DH), lambda bi, i: (bi, 0, r(bi, i), 0))
    ya_spec = lambda n, r: pl.BlockSpec((1, n, ATTN_W), lambda bi, i: (bi, r(bi, i), 0))
    x_spec = lambda n, r: pl.BlockSpec((1, n, d), lambda bi, i: (bi, r(bi, i), 0))
    mod_spec = lambda piece: pl.BlockSpec((1, 1, d), lambda bi, i: (bi, 0, piece))
    const = lambda arr: pl.BlockSpec(arr.shape, lambda bi, i: (0,) * arr.ndim)
    resident = lambda arr: pl.BlockSpec(arr.shape, lambda bi, i: (0,) * arr.ndim, pipeline_mode=pl.Buffered(1))
    scratch = [((ext, d), BF16), ((ext, d), BF16), ((tm, d), F32), ((tm, d), F32)]
    weights = _nbytes(w_out.shape, BF16) + _nbytes(w_up.shape, BF16) + _nbytes(w_down.shape, BF16)
    blk = (2 * _nbytes((tm, d), F32) + 2 * _nbytes((tm, d), BF16) + 4 * _nbytes((ext, FF_CHUNK), F32)
           + (weights + sum(_nbytes(s, dt) for s, dt in scratch)) // 2)
    return pl.pallas_call(
        _mixffn_body,
        grid=(b, l // tm),
        in_specs=(rows3(ym_spec) + rows3(ya_spec) + rows3(x_spec)
                  + [resident(w_out), mod_spec(2), const(norm_gain), mod_spec(3), mod_spec(4), mod_spec(5),
                     resident(w_up), const(conv_w), const(conv_b), resident(w_down), const(final_norm)]),
        out_specs=pl.BlockSpec((1, tm, d), lambda bi, i: (bi, i, 0)),
        out_shape=jax.ShapeDtypeStruct((b, l, d), F32),
        scratch_shapes=[pltpu.VMEM(s, dt) for s, dt in scratch],
        compiler_params=_params(blk, 2),
        name="mixffn",
    )(ym, ym, ym, ya, ya, ya, x, x, x, w_out, mod3, norm_gain, mod3, mod3, mod3,
      w_up, conv_w, conv_b, w_down, final_norm)


def _rope_tables(n_tokens):
    pos = jnp.arange(n_tokens)
    r = (pos // GRID_W).astype(F32)
    c = (pos % GRID_W).astype(F32)
    inv = ROPE_BASE ** (-jnp.arange(ROPE_AXIS_PAIRS, dtype=F32) / ROPE_AXIS_PAIRS)
    ar, ac = r[:, None] * inv, c[:, None] * inv
    zero = jnp.zeros_like(ar)
    cos = jnp.concatenate([jnp.cos(ar), jnp.cos(ar), jnp.cos(ac), jnp.cos(ac)], axis=1)
    sina = jnp.concatenate([-jnp.sin(ar), zero, -jnp.sin(ac), zero], axis=1)
    sinb = jnp.concatenate([zero, jnp.sin(ar), zero, jnp.sin(ac)], axis=1)
    rep = V7X_LANES // ATTN_DH
    return tuple(jnp.tile(tb, (1, rep)) for tb in (cos, sina, sinb))


def _identity_rope(n_tokens):
    return (jnp.ones((n_tokens, V7X_LANES), F32), jnp.zeros((n_tokens, V7X_LANES), F32),
            jnp.zeros((n_tokens, V7X_LANES), F32))


def _gate_table(gates_ctx, gates_x):
    g = jnp.concatenate([gates_ctx, gates_x], axis=1)
    b, n, _ = g.shape
    nc = n // MCHUNK
    g = g.reshape(b, nc, MCHUNK, 4, MLSTM_HEADS).transpose(0, 4, 3, 1, 2)
    g = jnp.pad(g, ((0, 0), (0, 0), (0, 0), (0, MAX_CHUNKS - nc), (0, 0)))
    return g.reshape(b, MLSTM_HEADS, 4 * MAX_CHUNKS, MCHUNK)


def kernel(x, c, ctx, c_ctx, w_ada, b_ada, norm_mix, norm_ffn, w_in, gate_b, qk_conv_w, qk_conv_b, mlstm_norm,
           attn_sink, w_out, w_up, ffn_conv_w, ffn_conv_b, w_down, final_norm):
    b, l, d = x.shape
    n_ctx = ctx.shape[1]
    assert w_ada.shape[0] == 1, "single-layer stack"
    assert l % ROW_TILE == 0 and l % MCHUNK == 0 and n_ctx % MCHUNK == 0 and l >= BAND
    assert l % (ATTN_BLOCKS_PER_STEP * ATTN_BLOCK) == 0 and MCHUNK == V7X_LANES
    assert (l + n_ctx) // MCHUNK <= MAX_CHUNKS

    rows = -(-(b + 1) // V7X_SUBLANES) * V7X_SUBLANES
    cvec = jnp.zeros((rows, d), F32).at[:b].set(c).at[b].set(c_ctx)
    mod = _ada(cvec, w_ada[0], b_ada[0][None, :])
    mod_x = mod[:b].reshape(b, 1, 6 * d)
    mod_c = mod[b:b + 1].reshape(1, 1, 6 * d)

    w = w_in[0].astype(BF16)
    o1, o2 = 3 * MLSTM_W, 4 * MLSTM_W
    o3 = o2 + N_GATES
    o4 = o3 + ATTN_W
    w_g = jnp.pad(w[:, o2:o3], ((0, 0), (0, V7X_LANES - N_GATES)))
    gb = gate_b[0][None, :]
    gain_mix = norm_mix[0][None, :]

    o0 = 2 * MLSTM_W
    conv = (qk_conv_w[0], qk_conv_b[0][None, :])
    k_mc, v_mc, gates_c, kv_ac = _inproj(
        ctx, gain_mix, mod_c, 0, 1, False, _identity_rope(n_ctx), gb, *conv,
        [w[:, MLSTM_W:o0], w[:, o0:o1], w_g, w[:, o4:]], ["conv_k", "heads", "gates", "plain"], "inproj_ctx")
    qk_m, vo_m, gates_x, q_a, kv_a = _inproj(
        x, gain_mix, mod_x, 0, 1, True, _rope_tables(l), gb, *conv,
        [w[:, :o0], w[:, o0:o2], w_g, w[:, o3:o4], w[:, o4:]],
        ["conv_qk", "heads", "gates", "rope_q", "rope_kv"], "inproj_x")

    y_m = _mlstm(qk_m, vo_m, k_mc, v_mc, _gate_table(gates_c, gates_x), mlstm_norm[0][None, :])
    y_a = _attn(attn_sink[0], q_a, kv_a, kv_ac)

    return _mixffn(y_m, y_a, x, mod_x, w_out[0].astype(BF16), norm_ffn[0][None, :], w_up[0].astype(BF16),
                   ffn_conv_w[0], ffn_conv_b[0][None, :], w_down[0].astype(BF16), final_norm[None, :])
```

```python
import functools

import jax
import jax.numpy as jnp
from jax import lax
from jax.experimental import pallas as pl
from jax.experimental.pallas import tpu as pltpu

F32 = jnp.float32
BF16 = jnp.bfloat16

D_MODEL = 1024
GRID_W = 64
MLSTM_HEADS = 4
MLSTM_DH = 128
MLSTM_W = MLSTM_HEADS * MLSTM_DH
N_GATES = 4 * MLSTM_HEADS
ATTN_HEADS = 8
KV_HEADS = 2
ATTN_DH = 64
ATTN_W = ATTN_HEADS * ATTN_DH
KV_W = KV_HEADS * ATTN_DH
GQA_GROUP = ATTN_HEADS // KV_HEADS
WINDOW = 128
ROPE_BASE = 10000.0
ROPE_AXIS_PAIRS = ATTN_DH // 4
D_FF = 2816
EPS = 1e-6
LOG2E = 1.4426950408889634

V7X_LANES = 128
V7X_SUBLANES = 8
V7X_VMEM_BYTES = 64 * 1024 * 1024

ROW_TILE = 512
ADA_COL_TILE = 1536
MCHUNK = 128
MAX_CHUNKS = 32
ATTN_BLOCK = 128
ATTN_BLOCKS_PER_STEP = 4
BAND = 3 * ATTN_BLOCK
FF_CHUNK = 512
HALO = 2 * V7X_SUBLANES


def _vmem_limit(block_bytes):
    return int(min(V7X_VMEM_BYTES * 7 // 8, 2 * block_bytes + 16 * 1024 * 1024))


def _params(block_bytes, n_axes):
    return pltpu.CompilerParams(dimension_semantics=("arbitrary",) * n_axes,
                                vmem_limit_bytes=_vmem_limit(block_bytes))


def _nbytes(shape, dtype):
    n = 1
    for s in shape:
        n *= s
    return n * jnp.dtype(dtype).itemsize


def _ada_body(c_ref, w_ref, b_ref, o_ref):
    c = c_ref[...]
    s = (c * jax.nn.sigmoid(c)).astype(BF16)
    o_ref[...] = jnp.dot(s, w_ref[...].astype(BF16), preferred_element_type=F32) + b_ref[...]


def _ada(cvec, w_ada, b_ada):
    rows, d = cvec.shape
    n = w_ada.shape[1]
    blk = _nbytes((d, ADA_COL_TILE), F32) + _nbytes((rows, d), F32) + 2 * _nbytes((rows, ADA_COL_TILE), F32)
    return pl.pallas_call(
        _ada_body,
        grid=(n // ADA_COL_TILE,),
        in_specs=[pl.BlockSpec((rows, d), lambda j: (0, 0)),
                  pl.BlockSpec((d, ADA_COL_TILE), lambda j: (0, j)),
                  pl.BlockSpec((1, ADA_COL_TILE), lambda j: (0, j))],
        out_specs=pl.BlockSpec((rows, ADA_COL_TILE), lambda j: (0, j)),
        out_shape=jax.ShapeDtypeStruct((rows, n), F32),
        compiler_params=_params(blk, 1),
        name="ada",
    )(cvec, w_ada, b_ada)


def _norm_modulate(x, gain, shift, scale):
    ms = jnp.mean(x * x, axis=-1, keepdims=True)
    return (x * lax.rsqrt(ms + EPS) * gain) * (1.0 + scale) + shift


def _silu(y):
    h = 0.5 * y
    return h + h * jnp.tanh(h)


def _inproj_body(x_ref, xp_ref, xn_ref, gain_ref, sh_ref, sc_ref, cos_ref, sina_ref, sinb_ref, gb_ref,
                 cw_ref, cb_ref, win_ref, wg_ref, *refs, groups):
    n = len(groups)
    o_refs, hs, w_scr = refs[:n], refs[n], refs[n + 1:]
    kinds = [g[0] for g in groups]
    w_refs, k = [], 0
    for kind, _, _ in groups:
        w_refs.append(wg_ref if kind == "gates" else w_scr[k])
        k += kind != "gates"

    @pl.when((pl.program_id(0) == 0) & (pl.program_id(1) == 0))
    def _():
        for (kind, col0, width), w_bf in zip(groups, w_refs):
            if kind != "gates":
                w_bf[...] = win_ref[:, pl.ds(col0, width)].astype(BF16)

    i = pl.program_id(1)
    tm = x_ref.shape[1]
    ext = tm + 2 * HALO
    norm = lambda rows: _norm_modulate(rows, gain_ref[...], sh_ref[0], sc_ref[0]).astype(BF16)
    hb = norm(x_ref[0])
    h_lo, h_hi = norm(xp_ref[0]), norm(xn_ref[0])
    hs[pl.ds(0, HALO), :] = jnp.where(i > 0, h_lo, jnp.zeros_like(h_lo))
    hs[pl.ds(HALO, tm), :] = hb
    hs[pl.ds(HALO + tm, HALO), :] = jnp.where(i < pl.num_programs(1) - 1, h_hi, jnp.zeros_like(h_hi))
    for kind, w_ref, o_ref in zip(kinds, w_refs, o_refs):
        if kind in ("conv_qk", "conv_k"):
            a = jnp.dot(hs[...], w_ref[...], preferred_element_type=F32)
            width = a.shape[1]
            c0 = cw_ref.shape[1] - width
            a_prev = pltpu.roll(a, 1, 0)[HALO:HALO + tm]
            a_next = pltpu.roll(a, ext - 1, 0)[HALO:HALO + tm]
            y = _silu(cw_ref[0:1, c0:] * a_prev + cw_ref[1:2, c0:] * a[HALO:HALO + tm]
                      + cw_ref[2:3, c0:] * a_next + cb_ref[:, c0:])
            first_k = width // V7X_LANES - MLSTM_HEADS
            for j in range(width // V7X_LANES):
                u = y[:, j * V7X_LANES:(j + 1) * V7X_LANES]
                o_ref[0, j] = (u * MLSTM_DH ** -0.5 if j >= first_k else u).astype(BF16)
        elif kind == "gates":
            o_ref[0] = lax.dot_general(w_ref[...], hb, (((1,), (1,)), ((), ())),
                                       preferred_element_type=F32) + gb_ref[...]
        elif kind == "plain":
            o_ref[0] = jnp.dot(hb, w_ref[...], preferred_element_type=F32).astype(BF16)
        elif kind == "heads":
            u = jnp.dot(hb, w_ref[...], preferred_element_type=F32)
            for j in range(u.shape[1] // V7X_LANES):
                o_ref[0, j] = u[:, j * V7X_LANES:(j + 1) * V7X_LANES].astype(BF16)
        else:
            u_all = jnp.dot(hb, w_ref[...], preferred_element_type=F32)
            n_rot = u_all.shape[1] // V7X_LANES if kind == "rope_q" else KV_W // V7X_LANES
            for j in range(u_all.shape[1] // V7X_LANES):
                cols = pl.ds(j * V7X_LANES, V7X_LANES)
                u = u_all[:, j * V7X_LANES:(j + 1) * V7X_LANES]
                if j < n_rot:
                    u = (u * cos_ref[...]
                         + pltpu.roll(u, V7X_LANES - ROPE_AXIS_PAIRS, 1) * sina_ref[...]
                         + pltpu.roll(u, ROPE_AXIS_PAIRS, 1) * sinb_ref[...])
                    if kind == "rope_q":
                        u = u * (ATTN_DH ** -0.5 * LOG2E)
                o_ref[0, :, cols] = u.astype(BF16)


def _inproj(x, gain, mod3, shift_idx, scale_idx, per_batch_mod, rope, gate_b, conv_w, conv_b, w_in, w_gates_t,
            groups, name):
    b, n, d = x.shape
    tm = min(ROW_TILE, n)
    per = tm // HALO
    last = n // HALO - 1
    cos, sina, sinb = rope
    mod_spec = lambda piece: pl.BlockSpec(
        (1, 1, d), (lambda bi, i: (bi, 0, piece)) if per_batch_mod else (lambda bi, i: (0, 0, piece)))
    in_specs = [pl.BlockSpec((1, tm, d), lambda bi, i: (bi, i, 0)),
                pl.BlockSpec((1, HALO, d), lambda bi, i: (bi, jnp.maximum(i * per - 1, 0), 0)),
                pl.BlockSpec((1, HALO, d), lambda bi, i: (bi, jnp.minimum((i + 1) * per, last), 0)),
                pl.BlockSpec((1, d), lambda bi, i: (0, 0)),
                mod_spec(shift_idx), mod_spec(scale_idx),
                pl.BlockSpec((tm, V7X_LANES), lambda bi, i: (i, 0)),
                pl.BlockSpec((tm, V7X_LANES), lambda bi, i: (i, 0)),
                pl.BlockSpec((tm, V7X_LANES), lambda bi, i: (i, 0)),
                pl.BlockSpec((N_GATES, 1), lambda bi, i: (0, 0)),
                pl.BlockSpec(conv_w.shape, lambda bi, i: (0, 0)),
                pl.BlockSpec(conv_b.shape, lambda bi, i: (0, 0)),
                pl.BlockSpec(w_in.shape, lambda bi, i: (0, 0), pipeline_mode=pl.Buffered(1)),
                pl.BlockSpec(w_gates_t.shape, lambda bi, i: (0, 0))]
    out_specs, out_shapes = [], []
    scratch = [((tm + 2 * HALO, d), BF16)]
    blk = (_nbytes((tm, d), F32) + 3 * _nbytes((tm, V7X_LANES), F32) + 4 * _nbytes((tm + 2 * HALO, 2 * MLSTM_W), F32)
           + _nbytes(w_in.shape, F32) // 2)
    for kind, _, width in groups:
        if kind == "gates":
            out_specs.append(pl.BlockSpec((1, N_GATES, tm), lambda bi, i: (bi, 0, i)))
            out_shapes.append(jax.ShapeDtypeStruct((b, N_GATES, n), F32))
            continue
        scratch.append(((d, width), BF16))
        if kind in ("heads", "conv_qk", "conv_k"):
            slabs = width // V7X_LANES
            out_specs.append(pl.BlockSpec((1, slabs, tm, V7X_LANES), lambda bi, i: (bi, 0, i, 0)))
            out_shapes.append(jax.ShapeDtypeStruct((b, slabs, n, V7X_LANES), BF16))
        else:
            out_specs.append(pl.BlockSpec((1, tm, width), lambda bi, i: (bi, i, 0)))
            out_shapes.append(jax.ShapeDtypeStruct((b, n, width), BF16))
        blk += _nbytes((tm, width), BF16) + _nbytes((tm, width), F32)
    blk += sum(_nbytes(s, dt) for s, dt in scratch) // 2
    return pl.pallas_call(
        functools.partial(_inproj_body, groups=tuple(groups)),
        grid=(b, n // tm),
        in_specs=in_specs,
        out_specs=out_specs,
        out_shape=out_shapes,
        scratch_shapes=[pltpu.VMEM(s, dt) for s, dt in scratch],
        compiler_params=_params(blk, 2),
        name=name,
    )(x, x, x, gain, mod3, mod3, cos, sina, sinb, gate_b, conv_w, conv_b, w_in, w_gates_t)


def _sigmoid(y):
    return 0.5 + 0.5 * jnp.tanh(0.5 * y)


def _log_sigmoid(x):
    return jnp.minimum(x, 0.0) - jnp.log1p(jnp.exp(-jnp.abs(x)))


def _mlstm_body(qs, ks, v_ref, o_ref, kcs, vc_ref, gt_ref, gain_ref,
                y_ref, gtab, rtab, cumtab, kts, cnl, c0n, cumc, sqk, sds, inter_s, em_s,
                *, n_ctx_chunks, n_lat_chunks):
    t = MCHUNK
    dh = MLSTM_DH
    n_chunks = n_ctx_chunks + n_lat_chunks
    nt = (((1,), (1,)), ((), ()))

    tbl = gt_ref[0, 0]
    rid = lax.broadcasted_iota(jnp.int32, tbl.shape, 0) // MAX_CHUNKS
    lane = lax.broadcasted_iota(jnp.int32, tbl.shape, 1)
    lf = _log_sigmoid(tbl)
    pre = jnp.where(rid == 1, lf, 0.0)
    suf = jnp.where(rid == 3, lf, 0.0)
    s = 1
    while s < t:
        pre = pre + jnp.where(lane >= s, pltpu.roll(pre, s, 1), 0.0)
        suf = suf + jnp.where(lane < t - s, pltpu.roll(suf, t - s, 1), 0.0)
        s *= 2
    li = (tbl[0:MAX_CHUNKS] * LOG2E, tbl[2 * MAX_CHUNKS:3 * MAX_CHUNKS] * LOG2E)
    cum = (pre[MAX_CHUNKS:2 * MAX_CHUNKS] * LOG2E, suf[3 * MAX_CHUNKS:4 * MAX_CHUNKS] * LOG2E)
    total = (cum[0][:, t - 1:t], cum[1][:, 0:1])
    mloc = []
    for d in (0, 1):
        g = total[d] - cum[d] + li[d]
        mloc.append(jnp.max(g, axis=1, keepdims=True))
        gtab[d] = g
        rtab[d] = cum[d] - li[d]
        cumtab[d] = cum[d]

    orders = (list(range(n_chunks)),
              list(range(n_ctx_chunks - 1, -1, -1)) + list(range(n_chunks - 1, n_ctx_chunks - 1, -1)))

    a_sc, m0, m1 = ({}, {}), ({}, {}), ({}, {})
    for d in (0, 1):
        m_st = jnp.zeros((1, 1), F32)
        for c in orders[d][:-1]:
            m0[d][c] = m_st
            tot_c = total[d][c:c + 1, :]
            m1[d][c] = jnp.maximum(tot_c + m_st, mloc[d][c:c + 1, :])
            a_sc[d][c] = jnp.exp2(tot_c + m_st - m1[d][c])
            m_st = m1[d][c]
        m0[d][orders[d][-1]] = m_st

    def kv_chunk(c):
        if c < n_ctx_chunks:
            rows = pl.ds(c * t, t)
            return kcs[rows, :], vc_ref[rows, :]
        rows = pl.ds((c - n_ctx_chunks) * t, t)
        return ks[rows, :], v_ref[rows, :]

    ri = lax.broadcasted_iota(jnp.int32, (t, t), 0)
    ci = lax.broadcasted_iota(jnp.int32, (t, t), 1)
    eye = jnp.where(ri == ci, 1.0, 0.0).astype(BF16)
    ones = jnp.ones((t, dh), BF16)

    for c in range(n_chunks):
        kts[c] = lax.dot_general(eye, kv_chunk(c)[0], nt, preferred_element_type=F32)
    for c in range(n_chunks):
        vo = jnp.concatenate([kv_chunk(c)[1], ones], axis=1)
        for d in (0, 1):
            if c != orders[d][-1]:
                w = jnp.exp2(gtab[d, pl.ds(c, 1), :] - m1[d][c])
                cnl[d, c] = jnp.dot((kts[c] * w).astype(BF16), vo, preferred_element_type=F32)

    for d in (0, 1):
        cn = jnp.zeros((dh, 2 * dh), F32)
        for c in orders[d]:
            if c >= n_ctx_chunks:
                c0n[d, c - n_ctx_chunks] = cn.astype(BF16)
            if c != orders[d][-1]:
                cn = a_sc[d][c] * cn + cnl[d, c]

    for j in range(n_lat_chunks):
        c = j + n_ctx_chunks
        rows = pl.ds(j * t, t)
        sqk[rows, :] = lax.dot_general(qs[rows, :], ks[rows, :], nt, preferred_element_type=F32)
        for d in (0, 1):
            cumc[d, rows, :] = jnp.broadcast_to(cumtab[d, pl.ds(c, 1), :], (t, t)).T

    grp = 2
    gi = lax.broadcasted_iota(jnp.int32, (grp * t, t), 0) & (t - 1)
    gs = lax.broadcasted_iota(jnp.int32, (grp * t, t), 1)
    for d in (0, 1):
        ok = (gs <= gi) if d == 0 else (gs >= gi)
        for j0 in range(0, n_lat_chunks, grp):
            rows = pl.ds(j0 * t, grp * t)
            cs = [j + n_ctx_chunks for j in range(j0, j0 + grp)]
            rb = jnp.concatenate([jnp.broadcast_to(rtab[d, pl.ds(c, 1), :], (t, t)) for c in cs], axis=0)
            m0b = jnp.concatenate([jnp.broadcast_to(m0[d][c], (t, t)) for c in cs], axis=0)
            cum_c = cumc[d, rows, :]
            d_log = jnp.where(ok, cum_c - rb, -jnp.inf)
            a_log = cum_c + m0b
            m_t = jnp.maximum(a_log, jnp.max(d_log, axis=1, keepdims=True))
            sds[d, rows, :] = sqk[rows, :].astype(BF16) * jnp.exp2((d_log - m_t).astype(BF16))
            inter_s[d, rows, :] = jnp.exp2(a_log - m_t)
            em_s[d, rows, :] = jnp.exp2(-m_t)

    gain = gain_ref[...]
    for j in range(n_lat_chunks):
        rows = pl.ds(j * t, t)
        q = qs[rows, :]
        vo = jnp.concatenate([v_ref[rows, :], ones], axis=1)
        hid = None
        for d in (0, 1):
            sv = jnp.dot(sds[d, rows, :], vo, preferred_element_type=F32)
            qc = jnp.dot(q, c0n[d, j], preferred_element_type=F32)
            inter = inter_s[d, rows, :]
            num = sv[:, :dh] + inter * qc[:, :dh]
            den = sv[:, dh:] + inter * qc[:, dh:]
            h_d = num / jnp.maximum(jnp.abs(den), em_s[d, rows, :])
            hid = h_d if hid is None else hid + h_d
        hn = hid * lax.rsqrt(jnp.mean(hid * hid, axis=-1, keepdims=True) + EPS) * gain
        y_ref[rows, :] = (hn * _sigmoid(o_ref[rows, :].astype(F32))).astype(BF16)


def _mlstm(qk, vo, k_ctx, v_ctx, gate_tbl, gain):
    b, _, l, _ = qk.shape
    n_ctx = k_ctx.shape[2]
    dh = MLSTM_DH
    nh = MLSTM_HEADS
    n_ctx_chunks, n_lat_chunks = n_ctx // MCHUNK, l // MCHUNK
    seq = lambda slab0: pl.BlockSpec((None, None, l, dh), lambda bi, h: (bi, slab0 + h, 0, 0))
    ctx = lambda slab0: pl.BlockSpec((None, None, n_ctx, dh), lambda bi, h: (bi, slab0 + h, 0, 0))
    n_chunks = n_ctx_chunks + n_lat_chunks
    scratch = [((2, MAX_CHUNKS, MCHUNK), F32), ((2, MAX_CHUNKS, MCHUNK), F32),
               ((2, MAX_CHUNKS, MCHUNK), F32), ((n_chunks, dh, MCHUNK), F32),
               ((2, n_chunks, dh, 2 * dh), F32), ((2, n_lat_chunks, dh, 2 * dh), BF16),
               ((2, l, MCHUNK), F32), ((l, MCHUNK), F32), ((2, l, MCHUNK), BF16),
               ((2, l, MCHUNK), F32), ((2, l, MCHUNK), F32)]
    blk = (5 * _nbytes((l, dh), BF16) + 2 * _nbytes((n_ctx, dh), BF16) + _nbytes((V7X_LANES, V7X_LANES), F32)
           + sum(_nbytes(s, dt) for s, dt in scratch) // 2)
    return pl.pallas_call(
        functools.partial(_mlstm_body, n_ctx_chunks=n_ctx_chunks, n_lat_chunks=n_lat_chunks),
        grid=(b, nh),
        in_specs=[seq(0), seq(nh), seq(0), seq(nh), ctx(0), ctx(0),
                  pl.BlockSpec((1, 1, 4 * MAX_CHUNKS, MCHUNK), lambda bi, h: (bi, h, 0, 0)),
                  pl.BlockSpec((1, dh), lambda bi, h: (0, h))],
        out_specs=pl.BlockSpec((None, None, l, dh), lambda bi, h: (bi, h, 0, 0)),
        out_shape=jax.ShapeDtypeStruct((b, nh, l, dh), BF16),
        scratch_shapes=[pltpu.VMEM(s, dt) for s, dt in scratch],
        compiler_params=_params(blk, 2),
        name="mlstm",
    )(qk, qk, vo, vo, k_ctx, v_ctx, gate_tbl, gain)


def _attn_body(sink_ref, bias_ref, q_ref, kv_ref, kvc_ref, o_ref, s_scr, p_scr, e_scr, *, seq_len):
    blk = ATTN_BLOCK
    n_blocks = seq_len // blk
    n_ctx = kvc_ref.shape[1]
    rows = GQA_GROUP * blk
    nt = (((1,), (1,)), ((), ()))
    chains = [(qb, g) for qb in range(ATTN_BLOCKS_PER_STEP) for g in range(KV_HEADS)]

    def block_start(qb):
        i = pl.program_id(1) * ATTN_BLOCKS_PER_STEP + qb
        return i, pl.multiple_of(jnp.clip((i - 1) * blk, 0, seq_len - BAND), blk)

    for ci, (qb, g) in enumerate(chains):
        i, start = block_start(qb)
        k_cols = pl.ds(g * ATTN_DH, ATTN_DH)
        q = jnp.concatenate([q_ref[0, pl.ds(qb * blk, blk), pl.ds((g * GQA_GROUP + j) * ATTN_DH, ATTN_DH)]
                             for j in range(GQA_GROUP)], axis=0)
        bias = bias_ref[jnp.where(i == 0, 0, jnp.where(i == n_blocks - 1, 2, 1))]
        s_scr[ci, :, pl.ds(0, BAND)] = (
            lax.dot_general(q, kv_ref[0, pl.ds(start, BAND), k_cols], nt, preferred_element_type=F32) + bias)
        s_scr[ci, :, pl.ds(BAND, n_ctx)] = lax.dot_general(q, kvc_ref[0, :, k_cols], nt, preferred_element_type=F32)

    for ci, (qb, g) in enumerate(chains):
        s = s_scr[ci]
        sink = jnp.concatenate([jnp.full((blk, V7X_LANES), sink_ref[g * GQA_GROUP + j] * LOG2E, F32)
                                for j in range(GQA_GROUP)], axis=0)
        m = jnp.maximum(jnp.broadcast_to(jnp.max(s, axis=1, keepdims=True), (rows, V7X_LANES)), sink)
        p_scr[ci] = jnp.exp2((s - jnp.tile(m, (1, (BAND + n_ctx) // V7X_LANES))).astype(BF16))
        e_scr[ci] = jnp.exp2(sink - m)

    ones_b = jnp.ones((BAND, ATTN_DH), BF16)
    ones_c = jnp.ones((n_ctx, ATTN_DH), BF16)
    for ci, (qb, g) in enumerate(chains):
        _, start = block_start(qb)
        v_cols = pl.ds(KV_W + g * ATTN_DH, ATTN_DH)
        vb = jnp.concatenate([kv_ref[0, pl.ds(start, BAND), v_cols], ones_b], axis=1)
        vc = jnp.concatenate([kvc_ref[0, :, v_cols], ones_c], axis=1)
        acc = (jnp.dot(p_scr[ci, :, pl.ds(0, BAND)], vb, preferred_element_type=F32)
               + jnp.dot(p_scr[ci, :, pl.ds(BAND, n_ctx)], vc, preferred_element_type=F32))
        out = (acc / (pltpu.roll(acc, ATTN_DH, 1) + e_scr[ci]))[:, :ATTN_DH]
        for j in range(GQA_GROUP):
            h = g * GQA_GROUP + j
            o_ref[0, pl.ds(qb * blk, blk), pl.ds(h * ATTN_DH, ATTN_DH)] = out[j * blk:(j + 1) * blk].astype(BF16)


def _band_bias(seq_len):
    row = jnp.arange(ATTN_BLOCK)[:, None]
    col = jnp.arange(BAND)[None, :]
    n_blocks = seq_len // ATTN_BLOCK
    tables = []
    for i in (0, 1, n_blocks - 1):
        start = min(max((i - 1) * ATTN_BLOCK, 0), seq_len - BAND)
        ok = jnp.abs(start + col - (i * ATTN_BLOCK + row)) <= WINDOW
        tables.append(jnp.tile(jnp.where(ok, 0.0, -jnp.inf).astype(F32), (GQA_GROUP, 1)))
    return jnp.stack(tables)


def _attn(sink, q, kv, kvc):
    b, l, _ = q.shape
    n_ctx = kvc.shape[1]
    rows = ATTN_BLOCKS_PER_STEP * ATTN_BLOCK
    bias = _band_bias(l)
    n_chains = KV_HEADS * ATTN_BLOCKS_PER_STEP
    stacked = GQA_GROUP * ATTN_BLOCK
    scratch = [((n_chains, stacked, BAND + n_ctx), F32), ((n_chains, stacked, BAND + n_ctx), BF16),
               ((n_chains, stacked, V7X_LANES), F32)]
    blk = (2 * _nbytes((rows, ATTN_W), BF16) + _nbytes((l, 2 * KV_W), BF16) + _nbytes((n_ctx, 2 * KV_W), BF16)
           + _nbytes(bias.shape, F32) + sum(_nbytes(s, dt) for s, dt in scratch) // 2)
    return pl.pallas_call(
        functools.partial(_attn_body, seq_len=l),
        grid=(b, l // rows),
        in_specs=[pl.BlockSpec(memory_space=pltpu.SMEM),
                  pl.BlockSpec(bias.shape, lambda bi, i: (0, 0, 0)),
                  pl.BlockSpec((1, rows, ATTN_W), lambda bi, i: (bi, i, 0)),
                  pl.BlockSpec((1, l, 2 * KV_W), lambda bi, i: (bi, 0, 0)),
                  pl.BlockSpec((1, n_ctx, 2 * KV_W), lambda bi, i: (bi, 0, 0))],
        out_specs=pl.BlockSpec((1, rows, ATTN_W), lambda bi, i: (bi, i, 0)),
        out_shape=jax.ShapeDtypeStruct((b, l, ATTN_W), BF16),
        scratch_shapes=[pltpu.VMEM(s, dt) for s, dt in scratch],
        compiler_params=_params(blk, 2),
        name="attn",
    )(sink, bias, q, kv, kvc)


GELU_C1 = 0.7978845608028654
GELU_C2 = GELU_C1 * 0.044715


def _twice_gelu_times(x, val):
    xv = x * val
    return xv + xv * jnp.tanh(x * (GELU_C1 + GELU_C2 * (x * x)))


def _mixffn_body(ym_ref, ymp_ref, ymn_ref, ya_ref, yap_ref, yan_ref, x_ref, xp_ref, xn_ref,
                 wo_ref, ga_ref, gain_ref, sh_ref, sc_ref, gf_ref, wup_ref, cw_ref, cb_ref, wdn_ref, fn_ref,
                 o_ref, ys, hs, x1s, acc):
    i = pl.program_id(1)
    tm = x_ref.shape[1]
    ext = tm + 2 * HALO
    lo, mid, hi = pl.ds(0, HALO), pl.ds(HALO, tm), pl.ds(HALO + tm, HALO)
    for rows, m_ref, a_ref in ((lo, ymp_ref, yap_ref), (mid, ym_ref, ya_ref), (hi, ymn_ref, yan_ref)):
        for h in range(MLSTM_HEADS):
            ys[rows, pl.ds(h * MLSTM_DH, MLSTM_DH)] = m_ref[0, h]
        ys[rows, pl.ds(MLSTM_W, ATTN_W)] = a_ref[0]
    proj = jnp.dot(ys[...], wo_ref[...], preferred_element_type=F32)
    ga = ga_ref[0]

    def mixed(x_rows, proj_rows):
        x1 = x_rows + ga * proj_rows
        return x1, _norm_modulate(x1, gain_ref[...], sh_ref[0], sc_ref[0]).astype(BF16)

    x1, h_mid = mixed(x_ref[0], proj[HALO:HALO + tm])
    x1s[...] = x1
    hs[mid, :] = h_mid
    h_lo = mixed(xp_ref[0], proj[:HALO])[1]
    h_hi = mixed(xn_ref[0], proj[HALO + tm:])[1]
    hs[lo, :] = jnp.where(i > 0, h_lo, jnp.zeros_like(h_lo))
    hs[hi, :] = jnp.where(i < pl.num_programs(1) - 1, h_hi, jnp.zeros_like(h_hi))
    starts = list(range(0, D_FF, FF_CHUNK))

    def up(col0):
        width = min(FF_CHUNK, D_FF - col0)
        return (jnp.dot(hs[...], wup_ref[:, pl.ds(col0, width)], preferred_element_type=F32),
                jnp.dot(hs[mid, :], wup_ref[:, pl.ds(D_FF + col0, width)], preferred_element_type=F32))

    nxt = up(starts[0])
    for c, col0 in enumerate(starts):
        cols = pl.ds(col0, min(FF_CHUNK, D_FF - col0))
        a, val = nxt
        if c + 1 < len(starts):
            nxt = up(starts[c + 1])
        a_prev = pltpu.roll(a, 1, 0)[HALO:HALO + tm]
        a_next = pltpu.roll(a, ext - 1, 0)[HALO:HALO + tm]
        conv = (cw_ref[0:1, cols] * a_prev + cw_ref[1:2, cols] * a[HALO:HALO + tm]
                + cw_ref[2:3, cols] * a_next + cb_ref[:, cols])
        part = jnp.dot(_twice_gelu_times(conv, val).astype(BF16), wdn_ref[cols, :], preferred_element_type=F32)
        if c == 0:
            acc[...] = part
        else:
            acc[...] += part
    x2 = x1s[...] + (0.5 * gf_ref[0]) * acc[...]
    ms = jnp.mean(x2 * x2, axis=-1, keepdims=True)
    o_ref[0] = x2 * lax.rsqrt(ms + EPS) * fn_ref[...]


def _mixffn(ym, ya, x, mod3, w_out, norm_gain, w_up, conv_w, conv_b, w_down, final_norm):
    b, l, d = x.shape
    tm = ROW_TILE
    per = tm // HALO
    last = l // HALO - 1
    ext = tm + 2 * HALO
    tile, before, after = (lambda bi, i: i), (lambda bi, i: jnp.maximum(i * per - 1, 0)), \
        (lambda bi, i: jnp.minimum((i + 1) * per, last))

    def rows3(make):
        return [make(tm, tile), make(HALO, before), make(HALO, after)]

    ym_spec = lambda n, r: pl.BlockSpec((1, MLSTM_HEADS, n, MLSTM_DH), lambda bi, i: (bi, 0, r(bi, i), 0))
    ya_spec = lambda n, r: pl.BlockSpec((1, n, ATTN_W), lambda bi, i: (bi, r(bi, i), 0))
    x_spec = lambda n, r: pl.BlockSpec((1, n, d), lambda bi, i: (bi, r(bi, i), 0))
    mod_spec = lambda piece: pl.BlockSpec((1, 1, d), lambda bi, i: (bi, 0, piece))
    const = lambda arr: pl.BlockSpec(arr.shape, lambda bi, i: (0,) * arr.ndim)
    resident = lambda arr: pl.BlockSpec(arr.shape, lambda bi, i: (0,) * arr.ndim, pipeline_mode=pl.Buffered(1))
    scratch = [((ext, d), BF16), ((ext, d), BF16), ((tm, d), F32), ((tm, d), F32)]
    weights = _nbytes(w_out.shape, BF16) + _nbytes(w_up.shape, BF16) + _nbytes(w_down.shape, BF16)
    blk = (2 * _nbytes((tm, d), F32) + 2 * _nbytes((tm, d), BF16) + 4 * _nbytes((ext, FF_CHUNK), F32)
           + (weights + sum(_nbytes(s, dt) for s, dt in scratch)) // 2)
    return pl.pallas_call(
        _mixffn_body,
        grid=(b, l // tm),
        in_specs=(rows3(ym_spec) + rows3(ya_spec) + rows3(x_spec)
                  + [resident(w_out), mod_spec(2), const(norm_gain), mod_spec(3), mod_spec(4), mod_spec(5),
                     resident(w_up), const(conv_w), const(conv_b), resident(w_down), const(final_norm)]),
        out_specs=pl.BlockSpec((1, tm, d), lambda bi, i: (bi, i, 0)),
        out_shape=jax.ShapeDtypeStruct((b, l, d), F32),
        scratch_shapes=[pltpu.VMEM(s, dt) for s, dt in scratch],
        compiler_params=_params(blk, 2),
        name="mixffn",
    )(ym, ym, ym, ya, ya, ya, x, x, x, w_out, mod3, norm_gain, mod3, mod3, mod3,
      w_up, conv_w, conv_b, w_down, final_norm)


def _rope_tables(n_tokens):
    pos = jnp.arange(n_tokens)
    r = (pos // GRID_W).astype(F32)
    c = (pos % GRID_W).astype(F32)
    inv = ROPE_BASE ** (-jnp.arange(ROPE_AXIS_PAIRS, dtype=F32) / ROPE_AXIS_PAIRS)
    ar, ac = r[:, None] * inv, c[:, None] * inv
    zero = jnp.zeros_like(ar)
    cos = jnp.concatenate([jnp.cos(ar), jnp.cos(ar), jnp.cos(ac), jnp.cos(ac)], axis=1)
    sina = jnp.concatenate([-jnp.sin(ar), zero, -jnp.sin(ac), zero], axis=1)
    sinb = jnp.concatenate([zero, jnp.sin(ar), zero, jnp.sin(ac)], axis=1)
    rep = V7X_LANES // ATTN_DH
    return tuple(jnp.tile(tb, (1, rep)) for tb in (cos, sina, sinb))


def _identity_rope(n_tokens):
    return (jnp.ones((n_tokens, V7X_LANES), F32), jnp.zeros((n_tokens, V7X_LANES), F32),
            jnp.zeros((n_tokens, V7X_LANES), F32))


def _gate_table(gates_ctx, gates_x):
    g = jnp.concatenate([gates_ctx, gates_x], axis=2)
    b, _, n = g.shape
    nc = n // MCHUNK
    g = g.reshape(b, 4, MLSTM_HEADS, nc, MCHUNK).transpose(0, 2, 1, 3, 4)
    g = jnp.pad(g, ((0, 0), (0, 0), (0, 0), (0, MAX_CHUNKS - nc), (0, 0)))
    return g.reshape(b, MLSTM_HEADS, 4 * MAX_CHUNKS, MCHUNK)


def kernel(x, c, ctx, c_ctx, w_ada, b_ada, norm_mix, norm_ffn, w_in, gate_b, qk_conv_w, qk_conv_b, mlstm_norm,
           attn_sink, w_out, w_up, ffn_conv_w, ffn_conv_b, w_down, final_norm):
    b, l, d = x.shape
    n_ctx = ctx.shape[1]
    assert w_ada.shape[0] == 1, "single-layer stack"
    assert l % ROW_TILE == 0 and l % MCHUNK == 0 and n_ctx % MCHUNK == 0 and l >= BAND
    assert l % (ATTN_BLOCKS_PER_STEP * ATTN_BLOCK) == 0 and MCHUNK == V7X_LANES
    assert (l + n_ctx) // MCHUNK <= MAX_CHUNKS

    rows = -(-(b + 1) // V7X_SUBLANES) * V7X_SUBLANES
    cvec = jnp.zeros((rows, d), F32).at[:b].set(c).at[b].set(c_ctx)
    mod = _ada(cvec, w_ada[0], b_ada[0][None, :])
    mod_x = mod[:b].reshape(b, 1, 6 * d)
    mod_c = mod[b:b + 1].reshape(1, 1, 6 * d)

    w = w_in[0]
    o0, o1, o2 = 2 * MLSTM_W, 3 * MLSTM_W, 4 * MLSTM_W
    o3 = o2 + N_GATES
    o4 = o3 + ATTN_W
    w_gt = w[:, o2:o3].T.astype(BF16)
    gb = gate_b[0][:, None]
    gain_mix = norm_mix[0][None, :]
    conv = (qk_conv_w[0], qk_conv_b[0][None, :])
    k_mc, v_mc, gates_c, kv_ac = _inproj(
        ctx, gain_mix, mod_c, 0, 1, False, _identity_rope(n_ctx), gb, *conv, w, w_gt,
        [("conv_k", MLSTM_W, MLSTM_W), ("heads", o0, MLSTM_W), ("gates", o2, N_GATES), ("plain", o4, 2 * KV_W)],
        "inproj_ctx")
    qk_m, vo_m, gates_x, q_a, kv_a = _inproj(
        x, gain_mix, mod_x, 0, 1, True, _rope_tables(l), gb, *conv, w, w_gt,
        [("conv_qk", 0, o0), ("heads", o0, o0), ("gates", o2, N_GATES), ("rope_q", o3, ATTN_W),
         ("rope_kv", o4, 2 * KV_W)], "inproj_x")

    y_m = _mlstm(qk_m, vo_m, k_mc, v_mc, _gate_table(gates_c, gates_x), mlstm_norm[0][None, :])
    y_a = _attn(attn_sink[0], q_a, kv_a, kv_ac)

    return _mixffn(y_m, y_a, x, mod_x, w_out[0].astype(BF16), norm_ffn[0][None, :], w_up[0].astype(BF16),
                   ffn_conv_w[0], ffn_conv_b[0][None, :], w_down[0].astype(BF16), final_norm[None, :])
```

```python
import functools

import jax
import jax.numpy as jnp
from jax import lax
from jax.experimental import pallas as pl
from jax.experimental.pallas import tpu as pltpu

F32 = jnp.float32
BF16 = jnp.bfloat16

D_MODEL = 1024
GRID_W = 64
MLSTM_HEADS = 4
MLSTM_DH = 128
MLSTM_W = MLSTM_HEADS * MLSTM_DH
N_GATES = 4 * MLSTM_HEADS
ATTN_HEADS = 8
KV_HEADS = 2
ATTN_DH = 64
ATTN_W = ATTN_HEADS * ATTN_DH
KV_W = KV_HEADS * ATTN_DH
GQA_GROUP = ATTN_HEADS // KV_HEADS
WINDOW = 128
ROPE_BASE = 10000.0
ROPE_AXIS_PAIRS = ATTN_DH // 4
D_FF = 2816
EPS = 1e-6
LOG2E = 1.4426950408889634

V7X_LANES = 128
V7X_SUBLANES = 8
V7X_VMEM_BYTES = 64 * 1024 * 1024

ROW_TILE = 512
ADA_COL_TILE = 1536
MCHUNK = 128
MAX_CHUNKS = 32
ATTN_BLOCK = 128
ATTN_BLOCKS_PER_STEP = 4
BAND = 3 * ATTN_BLOCK
FF_CHUNK = 512
HALO = 2 * V7X_SUBLANES


def _vmem_limit(block_bytes):
    return int(min(V7X_VMEM_BYTES * 7 // 8, 2 * block_bytes + 16 * 1024 * 1024))


def _params(block_bytes, n_axes):
    return pltpu.CompilerParams(dimension_semantics=("arbitrary",) * n_axes,
                                vmem_limit_bytes=_vmem_limit(block_bytes))


def _nbytes(shape, dtype):
    n = 1
    for s in shape:
        n *= s
    return n * jnp.dtype(dtype).itemsize


def _ada_body(c_ref, w_ref, b_ref, o_ref):
    c = c_ref[...]
    s = (c * jax.nn.sigmoid(c)).astype(BF16)
    o_ref[...] = jnp.dot(s, w_ref[...].astype(BF16), preferred_element_type=F32) + b_ref[...]


def _ada(cvec, w_ada, b_ada):
    rows, d = cvec.shape
    n = w_ada.shape[1]
    blk = _nbytes((d, ADA_COL_TILE), F32) + _nbytes((rows, d), F32) + 2 * _nbytes((rows, ADA_COL_TILE), F32)
    return pl.pallas_call(
        _ada_body,
        grid=(n // ADA_COL_TILE,),
        in_specs=[pl.BlockSpec((rows, d), lambda j: (0, 0)),
                  pl.BlockSpec((d, ADA_COL_TILE), lambda j: (0, j)),
                  pl.BlockSpec((1, ADA_COL_TILE), lambda j: (0, j))],
        out_specs=pl.BlockSpec((rows, ADA_COL_TILE), lambda j: (0, j)),
        out_shape=jax.ShapeDtypeStruct((rows, n), F32),
        compiler_params=_params(blk, 1),
        name="ada",
    )(cvec, w_ada, b_ada)


def _norm_modulate(x, gain, shift, scale):
    ms = jnp.mean(x * x, axis=-1, keepdims=True)
    return (x * lax.rsqrt(ms + EPS) * gain) * (1.0 + scale) + shift


def _silu(y):
    h = 0.5 * y
    return h + h * jnp.tanh(h)


def _inproj_body(x_ref, xp_ref, xn_ref, gain_ref, sh_ref, sc_ref, cos_ref, sina_ref, sinb_ref, gb_ref,
                 cw_ref, cb_ref, win_ref, *refs, groups):
    n = len(groups)
    o_refs, hs, w_refs = refs[:n], refs[n], refs[n + 1:]
    kinds = [g[0] for g in groups]

    @pl.when((pl.program_id(0) == 0) & (pl.program_id(1) == 0))
    def _():
        for (_, col0, width), w_bf in zip(groups, w_refs):
            w_bf[...] = win_ref[pl.ds(col0, width), :].astype(BF16)

    proj = lambda lhs, w_bf: lax.dot_general(lhs, w_bf[...], (((1,), (1,)), ((), ())), preferred_element_type=F32)

    i = pl.program_id(1)
    tm = x_ref.shape[1]
    ext = tm + 2 * HALO
    norm = lambda rows: _norm_modulate(rows, gain_ref[...], sh_ref[0], sc_ref[0]).astype(BF16)
    hb = norm(x_ref[0])
    h_lo, h_hi = norm(xp_ref[0]), norm(xn_ref[0])
    hs[pl.ds(0, HALO), :] = jnp.where(i > 0, h_lo, jnp.zeros_like(h_lo))
    hs[pl.ds(HALO, tm), :] = hb
    hs[pl.ds(HALO + tm, HALO), :] = jnp.where(i < pl.num_programs(1) - 1, h_hi, jnp.zeros_like(h_hi))
    for kind, w_ref, o_ref in zip(kinds, w_refs, o_refs):
        if kind in ("conv_qk", "conv_k"):
            a = proj(hs[...], w_ref)
            width = a.shape[1]
            c0 = cw_ref.shape[1] - width
            a_prev = pltpu.roll(a, 1, 0)[HALO:HALO + tm]
            a_next = pltpu.roll(a, ext - 1, 0)[HALO:HALO + tm]
            y = _silu(cw_ref[0:1, c0:] * a_prev + cw_ref[1:2, c0:] * a[HALO:HALO + tm]
                      + cw_ref[2:3, c0:] * a_next + cb_ref[:, c0:])
            first_k = width // V7X_LANES - MLSTM_HEADS
            for j in range(width // V7X_LANES):
                u = y[:, j * V7X_LANES:(j + 1) * V7X_LANES]
                o_ref[0, j] = (u * MLSTM_DH ** -0.5 if j >= first_k else u).astype(BF16)
        elif kind == "gates":
            o_ref[0] = lax.dot_general(w_ref[...], hb, (((1,), (1,)), ((), ())),
                                       preferred_element_type=F32) + gb_ref[...]
        elif kind == "plain":
            o_ref[0] = proj(hb, w_ref).astype(BF16)
        elif kind == "heads":
            u = proj(hb, w_ref)
            for j in range(u.shape[1] // V7X_LANES):
                o_ref[0, j] = u[:, j * V7X_LANES:(j + 1) * V7X_LANES].astype(BF16)
        else:
            u_all = proj(hb, w_ref)
            n_rot = u_all.shape[1] // V7X_LANES if kind == "rope_q" else KV_W // V7X_LANES
            for j in range(u_all.shape[1] // V7X_LANES):
                cols = pl.ds(j * V7X_LANES, V7X_LANES)
                u = u_all[:, j * V7X_LANES:(j + 1) * V7X_LANES]
                if j < n_rot:
                    u = (u * cos_ref[...]
                         + pltpu.roll(u, V7X_LANES - ROPE_AXIS_PAIRS, 1) * sina_ref[...]
                         + pltpu.roll(u, ROPE_AXIS_PAIRS, 1) * sinb_ref[...])
                    if kind == "rope_q":
                        u = u * (ATTN_DH ** -0.5 * LOG2E)
                o_ref[0, :, cols] = u.astype(BF16)


def _inproj(x, gain, mod3, shift_idx, scale_idx, per_batch_mod, rope, gate_b, conv_w, conv_b, w_in, groups, name):
    b, n, d = x.shape
    tm = min(ROW_TILE, n)
    per = tm // HALO
    last = n // HALO - 1
    cos, sina, sinb = rope
    mod_spec = lambda piece: pl.BlockSpec(
        (1, 1, d), (lambda bi, i: (bi, 0, piece)) if per_batch_mod else (lambda bi, i: (0, 0, piece)))
    in_specs = [pl.BlockSpec((1, tm, d), lambda bi, i: (bi, i, 0)),
                pl.BlockSpec((1, HALO, d), lambda bi, i: (bi, jnp.maximum(i * per - 1, 0), 0)),
                pl.BlockSpec((1, HALO, d), lambda bi, i: (bi, jnp.minimum((i + 1) * per, last), 0)),
                pl.BlockSpec((1, d), lambda bi, i: (0, 0)),
                mod_spec(shift_idx), mod_spec(scale_idx),
                pl.BlockSpec((tm, V7X_LANES), lambda bi, i: (i, 0)),
                pl.BlockSpec((tm, V7X_LANES), lambda bi, i: (i, 0)),
                pl.BlockSpec((tm, V7X_LANES), lambda bi, i: (i, 0)),
                pl.BlockSpec((N_GATES, 1), lambda bi, i: (0, 0)),
                pl.BlockSpec(conv_w.shape, lambda bi, i: (0, 0)),
                pl.BlockSpec(conv_b.shape, lambda bi, i: (0, 0)),
                pl.BlockSpec((None,) + w_in.shape[1:], lambda bi, i: (0, 0, 0), pipeline_mode=pl.Buffered(1))]
    out_specs, out_shapes = [], []
    scratch = [((tm + 2 * HALO, d), BF16)]
    blk = (_nbytes((tm, d), F32) + 3 * _nbytes((tm, V7X_LANES), F32) + 4 * _nbytes((tm + 2 * HALO, 2 * MLSTM_W), F32)
           + _nbytes(w_in.shape, F32) // 2)
    for kind, _, width in groups:
        if kind == "gates":
            scratch.append(((width, d), BF16))
            out_specs.append(pl.BlockSpec((1, N_GATES, tm), lambda bi, i: (bi, 0, i)))
            out_shapes.append(jax.ShapeDtypeStruct((b, N_GATES, n), F32))
            continue
        scratch.append(((width, d), BF16))
        if kind in ("heads", "conv_qk", "conv_k"):
            slabs = width // V7X_LANES
            out_specs.append(pl.BlockSpec((1, slabs, tm, V7X_LANES), lambda bi, i: (bi, 0, i, 0)))
            out_shapes.append(jax.ShapeDtypeStruct((b, slabs, n, V7X_LANES), BF16))
        else:
            out_specs.append(pl.BlockSpec((1, tm, width), lambda bi, i: (bi, i, 0)))
            out_shapes.append(jax.ShapeDtypeStruct((b, n, width), BF16))
        blk += _nbytes((tm, width), BF16) + _nbytes((tm, width), F32)
    blk += sum(_nbytes(s, dt) for s, dt in scratch) // 2
    return pl.pallas_call(
        functools.partial(_inproj_body, groups=tuple(groups)),
        grid=(b, n // tm),
        in_specs=in_specs,
        out_specs=out_specs,
        out_shape=out_shapes,
        scratch_shapes=[pltpu.VMEM(s, dt) for s, dt in scratch],
        compiler_params=_params(blk, 2),
        name=name,
    )(x, x, x, gain, mod3, mod3, cos, sina, sinb, gate_b, conv_w, conv_b, w_in)


def _sigmoid(y):
    return 0.5 + 0.5 * jnp.tanh(0.5 * y)


def _log_sigmoid(x):
    return jnp.minimum(x, 0.0) - jnp.log1p(jnp.exp(-jnp.abs(x)))


def _mlstm_body(qs, ks, v_ref, o_ref, kcs, vc_ref, gt_ref, gain_ref,
                y_ref, gtab, rtab, cumtab, kts, cnl, c0n, cumc, sqk, sds, inter_s, em_s, svq,
                *, n_ctx_chunks, n_lat_chunks):
    t = MCHUNK
    dh = MLSTM_DH
    n_chunks = n_ctx_chunks + n_lat_chunks
    nt = (((1,), (1,)), ((), ()))

    tbl = gt_ref[0, 0]
    rid = lax.broadcasted_iota(jnp.int32, tbl.shape, 0) // MAX_CHUNKS
    lane = lax.broadcasted_iota(jnp.int32, tbl.shape, 1)
    lf = _log_sigmoid(tbl)
    pre = jnp.where(rid == 1, lf, 0.0)
    suf = jnp.where(rid == 3, lf, 0.0)
    s = 1
    while s < t:
        pre = pre + jnp.where(lane >= s, pltpu.roll(pre, s, 1), 0.0)
        suf = suf + jnp.where(lane < t - s, pltpu.roll(suf, t - s, 1), 0.0)
        s *= 2
    li = (tbl[0:MAX_CHUNKS] * LOG2E, tbl[2 * MAX_CHUNKS:3 * MAX_CHUNKS] * LOG2E)
    cum = (pre[MAX_CHUNKS:2 * MAX_CHUNKS] * LOG2E, suf[3 * MAX_CHUNKS:4 * MAX_CHUNKS] * LOG2E)
    total = (cum[0][:, t - 1:t], cum[1][:, 0:1])
    mloc = []
    for d in (0, 1):
        g = total[d] - cum[d] + li[d]
        mloc.append(jnp.max(g, axis=1, keepdims=True))
        gtab[d] = g
        rtab[d] = cum[d] - li[d]
        cumtab[d] = cum[d]

    orders = (list(range(n_chunks)),
              list(range(n_ctx_chunks - 1, -1, -1)) + list(range(n_chunks - 1, n_ctx_chunks - 1, -1)))

    a_sc, m0, m1 = ({}, {}), ({}, {}), ({}, {})
    for d in (0, 1):
        m_st = jnp.zeros((1, 1), F32)
        for c in orders[d][:-1]:
            m0[d][c] = m_st
            tot_c = total[d][c:c + 1, :]
            m1[d][c] = jnp.maximum(tot_c + m_st, mloc[d][c:c + 1, :])
            a_sc[d][c] = jnp.exp2(tot_c + m_st - m1[d][c])
            m_st = m1[d][c]
        m0[d][orders[d][-1]] = m_st

    def kv_chunk(c):
        if c < n_ctx_chunks:
            rows = pl.ds(c * t, t)
            return kcs[rows, :], vc_ref[rows, :]
        rows = pl.ds((c - n_ctx_chunks) * t, t)
        return ks[rows, :], v_ref[rows, :]

    ri = lax.broadcasted_iota(jnp.int32, (t, t), 0)
    ci = lax.broadcasted_iota(jnp.int32, (t, t), 1)
    eye = jnp.where(ri == ci, 1.0, 0.0).astype(BF16)
    ones = jnp.ones((t, dh), BF16)

    for c in range(n_chunks):
        kts[c] = lax.dot_general(eye, kv_chunk(c)[0], nt, preferred_element_type=F32)
    for c in range(n_chunks):
        vo = jnp.concatenate([kv_chunk(c)[1], ones], axis=1)
        for d in (0, 1):
            if c != orders[d][-1]:
                w = jnp.exp2(gtab[d, pl.ds(c, 1), :] - m1[d][c])
                cnl[d, c] = jnp.dot((kts[c] * w).astype(BF16), vo, preferred_element_type=F32)

    for d in (0, 1):
        cn = jnp.zeros((dh, 2 * dh), F32)
        for c in orders[d]:
            if c >= n_ctx_chunks:
                c0n[d, c - n_ctx_chunks] = cn.astype(BF16)
            if c != orders[d][-1]:
                cn = a_sc[d][c] * cn + cnl[d, c]

    for j in range(n_lat_chunks):
        c = j + n_ctx_chunks
        rows = pl.ds(j * t, t)
        sqk[rows, :] = lax.dot_general(qs[rows, :], ks[rows, :], nt, preferred_element_type=F32)
        for d in (0, 1):
            cumc[d, rows, :] = jnp.broadcast_to(cumtab[d, pl.ds(c, 1), :], (t, t)).T

    grp = 2
    gi = lax.broadcasted_iota(jnp.int32, (grp * t, t), 0) & (t - 1)
    gs = lax.broadcasted_iota(jnp.int32, (grp * t, t), 1)
    for d in (0, 1):
        ok = (gs <= gi) if d == 0 else (gs >= gi)
        for j0 in range(0, n_lat_chunks, grp):
            rows = pl.ds(j0 * t, grp * t)
            cs = [j + n_ctx_chunks for j in range(j0, j0 + grp)]
            rb = jnp.concatenate([jnp.broadcast_to(rtab[d, pl.ds(c, 1), :], (t, t)) for c in cs], axis=0)
            m0b = jnp.concatenate([jnp.broadcast_to(m0[d][c], (t, t)) for c in cs], axis=0)
            cum_c = cumc[d, rows, :]
            d_log = jnp.where(ok, cum_c - rb, -jnp.inf)
            a_log = cum_c + m0b
            m_t = jnp.maximum(a_log, jnp.max(d_log, axis=1, keepdims=True))
            sds[d, rows, :] = sqk[rows, :].astype(BF16) * jnp.exp2((d_log - m_t).astype(BF16))
            inter_s[d, rows, :] = jnp.exp2(a_log - m_t)
            em_s[d, rows, :] = jnp.exp2(-m_t)

    for j in range(n_lat_chunks):
        rows = pl.ds(j * t, t)
        q = qs[rows, :]
        vo = jnp.concatenate([v_ref[rows, :], ones], axis=1)
        for d in (0, 1):
            svq[d, rows, :] = (jnp.dot(sds[d, rows, :], vo, preferred_element_type=F32)
                               + jnp.tile(inter_s[d, rows, :], (1, 2))
                               * jnp.dot(q, c0n[d, j], preferred_element_type=F32))

    gain = gain_ref[...]
    grp = 4
    for j0 in range(0, n_lat_chunks, grp):
        rows = pl.ds(j0 * t, grp * t)
        hid = None
        for d in (0, 1):
            h_d = svq[d, rows, pl.ds(0, dh)] / jnp.maximum(jnp.abs(svq[d, rows, pl.ds(dh, dh)]), em_s[d, rows, :])
            hid = h_d if hid is None else hid + h_d
        hn = hid * lax.rsqrt(jnp.mean(hid * hid, axis=-1, keepdims=True) + EPS) * gain
        y_ref[rows, :] = (hn * _sigmoid(o_ref[rows, :].astype(F32))).astype(BF16)


def _mlstm(qk, vo, k_ctx, v_ctx, gate_tbl, gain):
    b, _, l, _ = qk.shape
    n_ctx = k_ctx.shape[2]
    dh = MLSTM_DH
    nh = MLSTM_HEADS
    n_ctx_chunks, n_lat_chunks = n_ctx // MCHUNK, l // MCHUNK
    seq = lambda slab0: pl.BlockSpec((None, None, l, dh), lambda bi, h: (bi, slab0 + h, 0, 0))
    ctx = lambda slab0: pl.BlockSpec((None, None, n_ctx, dh), lambda bi, h: (bi, slab0 + h, 0, 0))
    n_chunks = n_ctx_chunks + n_lat_chunks
    scratch = [((2, MAX_CHUNKS, MCHUNK), F32), ((2, MAX_CHUNKS, MCHUNK), F32),
               ((2, MAX_CHUNKS, MCHUNK), F32), ((n_chunks, dh, MCHUNK), F32),
               ((2, n_chunks, dh, 2 * dh), F32), ((2, n_lat_chunks, dh, 2 * dh), BF16),
               ((2, l, MCHUNK), F32), ((l, MCHUNK), F32), ((2, l, MCHUNK), BF16),
               ((2, l, MCHUNK), F32), ((2, l, MCHUNK), F32), ((2, l, 2 * dh), F32)]
    blk = (5 * _nbytes((l, dh), BF16) + 2 * _nbytes((n_ctx, dh), BF16) + _nbytes((V7X_LANES, V7X_LANES), F32)
           + sum(_nbytes(s, dt) for s, dt in scratch) // 2)
    return pl.pallas_call(
        functools.partial(_mlstm_body, n_ctx_chunks=n_ctx_chunks, n_lat_chunks=n_lat_chunks),
        grid=(b, nh),
        in_specs=[seq(0), seq(nh), seq(0), seq(nh), ctx(0), ctx(0),
                  pl.BlockSpec((1, 1, 4 * MAX_CHUNKS, MCHUNK), lambda bi, h: (bi, h, 0, 0)),
                  pl.BlockSpec((1, dh), lambda bi, h: (0, h))],
        out_specs=pl.BlockSpec((None, None, l, dh), lambda bi, h: (bi, h, 0, 0)),
        out_shape=jax.ShapeDtypeStruct((b, nh, l, dh), BF16),
        scratch_shapes=[pltpu.VMEM(s, dt) for s, dt in scratch],
        compiler_params=_params(blk, 2),
        name="mlstm",
    )(qk, qk, vo, vo, k_ctx, v_ctx, gate_tbl, gain)


def _attn_body(sink_ref, bias_ref, q_ref, kv_ref, kvc_ref, o_ref, s_scr, p_scr, e_scr, *, seq_len):
    blk = ATTN_BLOCK
    n_blocks = seq_len // blk
    n_ctx = kvc_ref.shape[1]
    rows = GQA_GROUP * blk
    nt = (((1,), (1,)), ((), ()))
    chains = [(qb, g) for qb in range(ATTN_BLOCKS_PER_STEP) for g in range(KV_HEADS)]

    def block_start(qb):
        i = pl.program_id(1) * ATTN_BLOCKS_PER_STEP + qb
        return i, pl.multiple_of(jnp.clip((i - 1) * blk, 0, seq_len - BAND), blk)

    for ci, (qb, g) in enumerate(chains):
        i, start = block_start(qb)
        k_cols = pl.ds(g * ATTN_DH, ATTN_DH)
        q = jnp.concatenate([q_ref[0, pl.ds(qb * blk, blk), pl.ds((g * GQA_GROUP + j) * ATTN_DH, ATTN_DH)]
                             for j in range(GQA_GROUP)], axis=0)
        bias = bias_ref[jnp.where(i == 0, 0, jnp.where(i == n_blocks - 1, 2, 1))]
        s_scr[ci, :, pl.ds(0, BAND)] = (
            lax.dot_general(q, kv_ref[0, pl.ds(start, BAND), k_cols], nt, preferred_element_type=F32) + bias)
        s_scr[ci, :, pl.ds(BAND, n_ctx)] = lax.dot_general(q, kvc_ref[0, :, k_cols], nt, preferred_element_type=F32)

    for ci, (qb, g) in enumerate(chains):
        s = s_scr[ci]
        sink = jnp.concatenate([jnp.full((blk, V7X_LANES), sink_ref[g * GQA_GROUP + j] * LOG2E, F32)
                                for j in range(GQA_GROUP)], axis=0)
        m = jnp.maximum(jnp.broadcast_to(jnp.max(s, axis=1, keepdims=True), (rows, V7X_LANES)), sink)
        p_scr[ci] = jnp.exp2((s - jnp.tile(m, (1, (BAND + n_ctx) // V7X_LANES))).astype(BF16))
        e_scr[ci] = jnp.exp2(sink - m)

    ones_b = jnp.ones((BAND, ATTN_DH), BF16)
    ones_c = jnp.ones((n_ctx, ATTN_DH), BF16)
    for ci, (qb, g) in enumerate(chains):
        _, start = block_start(qb)
        v_cols = pl.ds(KV_W + g * ATTN_DH, ATTN_DH)
        vb = jnp.concatenate([kv_ref[0, pl.ds(start, BAND), v_cols], ones_b], axis=1)
        vc = jnp.concatenate([kvc_ref[0, :, v_cols], ones_c], axis=1)
        acc = (jnp.dot(p_scr[ci, :, pl.ds(0, BAND)], vb, preferred_element_type=F32)
               + jnp.dot(p_scr[ci, :, pl.ds(BAND, n_ctx)], vc, preferred_element_type=F32))
        out = (acc / (pltpu.roll(acc, ATTN_DH, 1) + e_scr[ci]))[:, :ATTN_DH]
        for j in range(GQA_GROUP):
            h = g * GQA_GROUP + j
            o_ref[0, pl.ds(qb * blk, blk), pl.ds(h * ATTN_DH, ATTN_DH)] = out[j * blk:(j + 1) * blk].astype(BF16)


def _band_bias(seq_len):
    row = jnp.arange(ATTN_BLOCK)[:, None]
    col = jnp.arange(BAND)[None, :]
    n_blocks = seq_len // ATTN_BLOCK
    tables = []
    for i in (0, 1, n_blocks - 1):
        start = min(max((i - 1) * ATTN_BLOCK, 0), seq_len - BAND)
        ok = jnp.abs(start + col - (i * ATTN_BLOCK + row)) <= WINDOW
        tables.append(jnp.tile(jnp.where(ok, 0.0, -jnp.inf).astype(F32), (GQA_GROUP, 1)))
    return jnp.stack(tables)


def _attn(sink, q, kv, kvc):
    b, l, _ = q.shape
    n_ctx = kvc.shape[1]
    rows = ATTN_BLOCKS_PER_STEP * ATTN_BLOCK
    bias = _band_bias(l)
    n_chains = KV_HEADS * ATTN_BLOCKS_PER_STEP
    stacked = GQA_GROUP * ATTN_BLOCK
    scratch = [((n_chains, stacked, BAND + n_ctx), F32), ((n_chains, stacked, BAND + n_ctx), BF16),
               ((n_chains, stacked, V7X_LANES), F32)]
    blk = (2 * _nbytes((rows, ATTN_W), BF16) + _nbytes((l, 2 * KV_W), BF16) + _nbytes((n_ctx, 2 * KV_W), BF16)
           + _nbytes(bias.shape, F32) + sum(_nbytes(s, dt) for s, dt in scratch) // 2)
    return pl.pallas_call(
        functools.partial(_attn_body, seq_len=l),
        grid=(b, l // rows),
        in_specs=[pl.BlockSpec(memory_space=pltpu.SMEM),
                  pl.BlockSpec(bias.shape, lambda bi, i: (0, 0, 0)),
                  pl.BlockSpec((1, rows, ATTN_W), lambda bi, i: (bi, i, 0)),
                  pl.BlockSpec((1, l, 2 * KV_W), lambda bi, i: (bi, 0, 0)),
                  pl.BlockSpec((1, n_ctx, 2 * KV_W), lambda bi, i: (bi, 0, 0))],
        out_specs=pl.BlockSpec((1, rows, ATTN_W), lambda bi, i: (bi, i, 0)),
        out_shape=jax.ShapeDtypeStruct((b, l, ATTN_W), BF16),
        scratch_shapes=[pltpu.VMEM(s, dt) for s, dt in scratch],
        compiler_params=_params(blk, 2),
        name="attn",
    )(sink, bias, q, kv, kvc)


GELU_C1 = 0.7978845608028654
GELU_C2 = GELU_C1 * 0.044715


def _twice_gelu_times(x, val):
    xv = x * val
    return xv + xv * jnp.tanh(x * (GELU_C1 + GELU_C2 * (x * x)))


def _mixffn_body(ym_ref, ymp_ref, ymn_ref, ya_ref, yap_ref, yan_ref, x_ref, xp_ref, xn_ref,
                 wo_ref, ga_ref, gain_ref, sh_ref, sc_ref, gf_ref, wup_ref, cw_ref, cb_ref, wdn_ref, fn_ref,
                 o_ref, ys, hs, x1s, acc):
    i = pl.program_id(1)
    tm = x_ref.shape[1]
    ext = tm + 2 * HALO
    lo, mid, hi = pl.ds(0, HALO), pl.ds(HALO, tm), pl.ds(HALO + tm, HALO)
    for rows, m_ref, a_ref in ((lo, ymp_ref, yap_ref), (mid, ym_ref, ya_ref), (hi, ymn_ref, yan_ref)):
        for h in range(MLSTM_HEADS):
            ys[rows, pl.ds(h * MLSTM_DH, MLSTM_DH)] = m_ref[0, h]
        ys[rows, pl.ds(MLSTM_W, ATTN_W)] = a_ref[0]
    proj = jnp.dot(ys[...], wo_ref[...], preferred_element_type=F32)
    ga = ga_ref[0]

    def mixed(x_rows, proj_rows):
        x1 = x_rows + ga * proj_rows
        return x1, _norm_modulate(x1, gain_ref[...], sh_ref[0], sc_ref[0]).astype(BF16)

    x1, h_mid = mixed(x_ref[0], proj[HALO:HALO + tm])
    x1s[...] = x1
    hs[mid, :] = h_mid
    h_lo = mixed(xp_ref[0], proj[:HALO])[1]
    h_hi = mixed(xn_ref[0], proj[HALO + tm:])[1]
    hs[lo, :] = jnp.where(i > 0, h_lo, jnp.zeros_like(h_lo))
    hs[hi, :] = jnp.where(i < pl.num_programs(1) - 1, h_hi, jnp.zeros_like(h_hi))
    starts = list(range(0, D_FF, FF_CHUNK))

    def up(col0):
        width = min(FF_CHUNK, D_FF - col0)
        return (jnp.dot(hs[...], wup_ref[:, pl.ds(col0, width)], preferred_element_type=F32),
                jnp.dot(hs[mid, :], wup_ref[:, pl.ds(D_FF + col0, width)], preferred_element_type=F32))

    nxt = up(starts[0])
    for c, col0 in enumerate(starts):
        cols = pl.ds(col0, min(FF_CHUNK, D_FF - col0))
        a, val = nxt
        if c + 1 < len(starts):
            nxt = up(starts[c + 1])
        a_prev = pltpu.roll(a, 1, 0)[HALO:HALO + tm]
        a_next = pltpu.roll(a, ext - 1, 0)[HALO:HALO + tm]
        conv = (cw_ref[0:1, cols] * a_prev + cw_ref[1:2, cols] * a[HALO:HALO + tm]
                + cw_ref[2:3, cols] * a_next + cb_ref[:, cols])
        part = jnp.dot(_twice_gelu_times(conv, val).astype(BF16), wdn_ref[cols, :], preferred_element_type=F32)
        if c == 0:
            acc[...] = part
        else:
            acc[...] += part
    x2 = x1s[...] + (0.5 * gf_ref[0]) * acc[...]
    ms = jnp.mean(x2 * x2, axis=-1, keepdims=True)
    o_ref[0] = x2 * lax.rsqrt(ms + EPS) * fn_ref[...]


def _mixffn(ym, ya, x, mod3, w_out, norm_gain, w_up, conv_w, conv_b, w_down, final_norm):
    b, l, d = x.shape
    tm = ROW_TILE
    per = tm // HALO
    last = l // HALO - 1
    ext = tm + 2 * HALO
    tile, before, after = (lambda bi, i: i), (lambda bi, i: jnp.maximum(i * per - 1, 0)), \
        (lambda bi, i: jnp.minimum((i + 1) * per, last))

    def rows3(make):
        return [make(tm, tile), make(HALO, before), make(HALO, after)]

    ym_spec = lambda n, r: pl.BlockSpec((1, MLSTM_HEADS, n, MLSTM_DH), lambda bi, i: (bi, 0, r(bi, i), 0))
    ya_spec = lambda n, r: pl.BlockSpec((1, n, ATTN_W), lambda bi, i: (bi, r(bi, i), 0))
    x_spec = lambda n, r: pl.BlockSpec((1, n, d), lambda bi, i: (bi, r(bi, i), 0))
    mod_spec = lambda piece: pl.BlockSpec((1, 1, d), lambda bi, i: (bi, 0, piece))
    const = lambda arr: pl.BlockSpec(arr.shape, lambda bi, i: (0,) * arr.ndim)
    resident = lambda arr: pl.BlockSpec(arr.shape, lambda bi, i: (0,) * arr.ndim, pipeline_mode=pl.Buffered(1))
    scratch = [((ext, d), BF16), ((ext, d), BF16), ((tm, d), F32), ((tm, d), F32)]
    weights = _nbytes(w_out.shape, BF16) + _nbytes(w_up.shape, BF16) + _nbytes(w_down.shape, BF16)
    blk = (2 * _nbytes((tm, d), F32) + 2 * _nbytes((tm, d), BF16) + 4 * _nbytes((ext, FF_CHUNK), F32)
           + (weights + sum(_nbytes(s, dt) for s, dt in scratch)) // 2)
    return pl.pallas_call(
        _mixffn_body,
        grid=(b, l // tm),
        in_specs=(rows3(ym_spec) + rows3(ya_spec) + rows3(x_spec)
                  + [resident(w_out), mod_spec(2), const(norm_gain), mod_spec(3), mod_spec(4), mod_spec(5),
                     resident(w_up), const(conv_w), const(conv_b), resident(w_down), const(final_norm)]),
        out_specs=pl.BlockSpec((1, tm, d), lambda bi, i: (bi, i, 0)),
        out_shape=jax.ShapeDtypeStruct((b, l, d), F32),
        scratch_shapes=[pltpu.VMEM(s, dt) for s, dt in scratch],
        compiler_params=_params(blk, 2),
        name="mixffn",
    )(ym, ym, ym, ya, ya, ya, x, x, x, w_out, mod3, norm_gain, mod3, mod3, mod3,
      w_up, conv_w, conv_b, w_down, final_norm)


def _rope_tables(n_tokens):
    pos = jnp.arange(n_tokens)
    r = (pos // GRID_W).astype(F32)
    c = (pos % GRID_W).astype(F32)
    inv = ROPE_BASE ** (-jnp.arange(ROPE_AXIS_PAIRS, dtype=F32) / ROPE_AXIS_PAIRS)
    ar, ac = r[:, None] * inv, c[:, None] * inv
    zero = jnp.zeros_like(ar)
    cos = jnp.concatenate([jnp.cos(ar), jnp.cos(ar), jnp.cos(ac), jnp.cos(ac)], axis=1)
    sina = jnp.concatenate([-jnp.sin(ar), zero, -jnp.sin(ac), zero], axis=1)
    sinb = jnp.concatenate([zero, jnp.sin(ar), zero, jnp.sin(ac)], axis=1)
    rep = V7X_LANES // ATTN_DH
    return tuple(jnp.tile(tb, (1, rep)) for tb in (cos, sina, sinb))


def _identity_rope(n_tokens):
    return (jnp.ones((n_tokens, V7X_LANES), F32), jnp.zeros((n_tokens, V7X_LANES), F32),
            jnp.zeros((n_tokens, V7X_LANES), F32))


def _gate_table(gates_ctx, gates_x):
    g = jnp.concatenate([gates_ctx, gates_x], axis=2)
    b, _, n = g.shape
    nc = n // MCHUNK
    g = g.reshape(b, 4, MLSTM_HEADS, nc, MCHUNK).transpose(0, 2, 1, 3, 4)
    g = jnp.pad(g, ((0, 0), (0, 0), (0, 0), (0, MAX_CHUNKS - nc), (0, 0)))
    return g.reshape(b, MLSTM_HEADS, 4 * MAX_CHUNKS, MCHUNK)


def kernel(x, c, ctx, c_ctx, w_ada, b_ada, norm_mix, norm_ffn, w_in, gate_b, qk_conv_w, qk_conv_b, mlstm_norm,
           attn_sink, w_out, w_up, ffn_conv_w, ffn_conv_b, w_down, final_norm):
    b, l, d = x.shape
    n_ctx = ctx.shape[1]
    assert w_ada.shape[0] == 1, "single-layer stack"
    assert l % ROW_TILE == 0 and l % MCHUNK == 0 and n_ctx % MCHUNK == 0 and l >= BAND
    assert l % (ATTN_BLOCKS_PER_STEP * ATTN_BLOCK) == 0 and MCHUNK == V7X_LANES
    assert (l + n_ctx) // MCHUNK <= MAX_CHUNKS

    rows = -(-(b + 1) // V7X_SUBLANES) * V7X_SUBLANES
    cvec = jnp.zeros((rows, d), F32).at[:b].set(c).at[b].set(c_ctx)
    mod = _ada(cvec, w_ada[0], b_ada[0][None, :])
    mod_x = mod[:b].reshape(b, 1, 6 * d)
    mod_c = mod[b:b + 1].reshape(1, 1, 6 * d)

    w_in = jnp.swapaxes(w_in, 1, 2)
    o0, o1, o2 = 2 * MLSTM_W, 3 * MLSTM_W, 4 * MLSTM_W
    o3 = o2 + N_GATES
    o4 = o3 + ATTN_W
    gb = gate_b[0][:, None]
    gain_mix = norm_mix[0][None, :]
    conv = (qk_conv_w[0], qk_conv_b[0][None, :])
    k_mc, v_mc, gates_c, kv_ac = _inproj(
        ctx, gain_mix, mod_c, 0, 1, False, _identity_rope(n_ctx), gb, *conv, w_in,
        [("conv_k", MLSTM_W, MLSTM_W), ("heads", o0, MLSTM_W), ("gates", o2, N_GATES), ("plain", o4, 2 * KV_W)],
        "inproj_ctx")
    qk_m, vo_m, gates_x, q_a, kv_a = _inproj(
        x, gain_mix, mod_x, 0, 1, True, _rope_tables(l), gb, *conv, w_in,
        [("conv_qk", 0, o0), ("heads", o0, o0), ("gates", o2, N_GATES), ("rope_q", o3, ATTN_W),
         ("rope_kv", o4, 2 * KV_W)], "inproj_x")

    y_m = _mlstm(qk_m, vo_m, k_mc, v_mc, _gate_table(gates_c, gates_x), mlstm_norm[0][None, :])
    y_a = _attn(attn_sink[0], q_a, kv_a, kv_ac)

    return _mixffn(y_m, y_a, x, mod_x, w_out[0].astype(BF16), norm_ffn[0][None, :], w_up[0].astype(BF16),
                   ffn_conv_w[0], ffn_conv_b[0][None, :], w_down[0].astype(BF16), final_norm[None, :])
```

```python
import functools

import jax
import jax.numpy as jnp
from jax import lax
from jax.experimental import pallas as pl
from jax.experimental.pallas import tpu as pltpu

F32 = jnp.float32
BF16 = jnp.bfloat16

D_MODEL = 1024
GRID_W = 64
MLSTM_HEADS = 4
MLSTM_DH = 128
MLSTM_W = MLSTM_HEADS * MLSTM_DH
N_GATES = 4 * MLSTM_HEADS
ATTN_HEADS = 8
KV_HEADS = 2
ATTN_DH = 64
ATTN_W = ATTN_HEADS * ATTN_DH
KV_W = KV_HEADS * ATTN_DH
GQA_GROUP = ATTN_HEADS // KV_HEADS
WINDOW = 128
ROPE_BASE = 10000.0
ROPE_AXIS_PAIRS = ATTN_DH // 4
D_FF = 2816
EPS = 1e-6
LOG2E = 1.4426950408889634

V7X_LANES = 128
V7X_SUBLANES = 8
V7X_VMEM_BYTES = 64 * 1024 * 1024

ROW_TILE = 512
ADA_COL_TILE = 1536
MCHUNK = 128
MAX_CHUNKS = 32
ATTN_BLOCK = 128
ATTN_BLOCKS_PER_STEP = 4
BAND = 3 * ATTN_BLOCK
FF_CHUNK = 512
HALO = 2 * V7X_SUBLANES


def _vmem_limit(block_bytes):
    return int(min(V7X_VMEM_BYTES * 7 // 8, 2 * block_bytes + 16 * 1024 * 1024))


def _params(block_bytes, n_axes):
    return pltpu.CompilerParams(dimension_semantics=("arbitrary",) * n_axes,
                                vmem_limit_bytes=_vmem_limit(block_bytes))


def _nbytes(shape, dtype):
    n = 1
    for s in shape:
        n *= s
    return n * jnp.dtype(dtype).itemsize


def _ada_body(c_ref, w_ref, b_ref, o_ref):
    c = c_ref[...]
    s = (c * jax.nn.sigmoid(c)).astype(BF16)
    o_ref[...] = jnp.dot(s, w_ref[...].astype(BF16), preferred_element_type=F32) + b_ref[...]


def _ada(cvec, w_ada, b_ada):
    rows, d = cvec.shape
    n = w_ada.shape[1]
    blk = _nbytes((d, ADA_COL_TILE), F32) + _nbytes((rows, d), F32) + 2 * _nbytes((rows, ADA_COL_TILE), F32)
    return pl.pallas_call(
        _ada_body,
        grid=(n // ADA_COL_TILE,),
        in_specs=[pl.BlockSpec((rows, d), lambda j: (0, 0)),
                  pl.BlockSpec((d, ADA_COL_TILE), lambda j: (0, j)),
                  pl.BlockSpec((1, ADA_COL_TILE), lambda j: (0, j))],
        out_specs=pl.BlockSpec((rows, ADA_COL_TILE), lambda j: (0, j)),
        out_shape=jax.ShapeDtypeStruct((rows, n), F32),
        compiler_params=_params(blk, 1),
        name="ada",
    )(cvec, w_ada, b_ada)


def _norm_modulate(x, gain, shift, scale):
    ms = jnp.mean(x * x, axis=-1, keepdims=True)
    return (x * lax.rsqrt(ms + EPS) * gain) * (1.0 + scale) + shift


def _silu(y):
    h = 0.5 * y
    return h + h * jnp.tanh(h)


def _inproj_body(x_ref, xp_ref, xn_ref, gain_ref, sh_ref, sc_ref, cos_ref, sina_ref, sinb_ref, gb_ref,
                 cw_ref, cb_ref, win_ref, *refs, groups):
    n = len(groups)
    o_refs, hs, w_refs = refs[:n], refs[n], refs[n + 1:]
    kinds = [g[0] for g in groups]

    @pl.when((pl.program_id(0) == 0) & (pl.program_id(1) == 0))
    def _():
        for (_, col0, width), w_bf in zip(groups, w_refs):
            w_bf[...] = win_ref[pl.ds(col0, width), :].astype(BF16)

    proj = lambda lhs, w_bf: lax.dot_general(lhs, w_bf[...], (((1,), (1,)), ((), ())), preferred_element_type=F32)

    i = pl.program_id(1)
    tm = x_ref.shape[1]
    ext = tm + 2 * HALO
    norm = lambda rows: _norm_modulate(rows, gain_ref[...], sh_ref[0], sc_ref[0]).astype(BF16)
    hb = norm(x_ref[0])
    h_lo, h_hi = norm(xp_ref[0]), norm(xn_ref[0])
    hs[pl.ds(0, HALO), :] = jnp.where(i > 0, h_lo, jnp.zeros_like(h_lo))
    hs[pl.ds(HALO, tm), :] = hb
    hs[pl.ds(HALO + tm, HALO), :] = jnp.where(i < pl.num_programs(1) - 1, h_hi, jnp.zeros_like(h_hi))
    for kind, w_ref, o_ref in zip(kinds, w_refs, o_refs):
        if kind in ("conv_qk", "conv_k"):
            a = proj(hs[...], w_ref)
            width = a.shape[1]
            c0 = cw_ref.shape[1] - width
            a_prev = pltpu.roll(a, 1, 0)[HALO:HALO + tm]
            a_next = pltpu.roll(a, ext - 1, 0)[HALO:HALO + tm]
            y = _silu(cw_ref[0:1, c0:] * a_prev + cw_ref[1:2, c0:] * a[HALO:HALO + tm]
                      + cw_ref[2:3, c0:] * a_next + cb_ref[:, c0:])
            first_k = width // V7X_LANES - MLSTM_HEADS
            for j in range(width // V7X_LANES):
                u = y[:, j * V7X_LANES:(j + 1) * V7X_LANES]
                o_ref[0, j] = (u * MLSTM_DH ** -0.5 if j >= first_k else u).astype(BF16)
        elif kind == "gates":
            o_ref[0] = lax.dot_general(w_ref[...], hb, (((1,), (1,)), ((), ())),
                                       preferred_element_type=F32) + gb_ref[...]
        elif kind == "plain":
            o_ref[0] = proj(hb, w_ref).astype(BF16)
        elif kind == "heads":
            u = proj(hb, w_ref)
            for j in range(u.shape[1] // V7X_LANES):
                o_ref[0, j] = u[:, j * V7X_LANES:(j + 1) * V7X_LANES].astype(BF16)
        else:
            u_all = proj(hb, w_ref)
            n_rot = u_all.shape[1] // V7X_LANES if kind == "rope_q" else KV_W // V7X_LANES
            for j in range(u_all.shape[1] // V7X_LANES):
                cols = pl.ds(j * V7X_LANES, V7X_LANES)
                u = u_all[:, j * V7X_LANES:(j + 1) * V7X_LANES]
                if j < n_rot:
                    u = (u * cos_ref[...]
                         + pltpu.roll(u, V7X_LANES - ROPE_AXIS_PAIRS, 1) * sina_ref[...]
                         + pltpu.roll(u, ROPE_AXIS_PAIRS, 1) * sinb_ref[...])
                    if kind == "rope_q":
                        u = u * (ATTN_DH ** -0.5 * LOG2E)
                o_ref[0, :, cols] = u.astype(BF16)


def _inproj(x, gain, mod3, shift_idx, scale_idx, per_batch_mod, rope, gate_b, conv_w, conv_b, w_in, groups, name):
    b, n, d = x.shape
    tm = min(ROW_TILE, n)
    per = tm // HALO
    last = n // HALO - 1
    cos, sina, sinb = rope
    mod_spec = lambda piece: pl.BlockSpec(
        (1, 1, d), (lambda bi, i: (bi, 0, piece)) if per_batch_mod else (lambda bi, i: (0, 0, piece)))
    in_specs = [pl.BlockSpec((1, tm, d), lambda bi, i: (bi, i, 0)),
                pl.BlockSpec((1, HALO, d), lambda bi, i: (bi, jnp.maximum(i * per - 1, 0), 0)),
                pl.BlockSpec((1, HALO, d), lambda bi, i: (bi, jnp.minimum((i + 1) * per, last), 0)),
                pl.BlockSpec((1, d), lambda bi, i: (0, 0)),
                mod_spec(shift_idx), mod_spec(scale_idx),
                pl.BlockSpec((tm, V7X_LANES), lambda bi, i: (i, 0)),
                pl.BlockSpec((tm, V7X_LANES), lambda bi, i: (i, 0)),
                pl.BlockSpec((tm, V7X_LANES), lambda bi, i: (i, 0)),
                pl.BlockSpec((N_GATES, 1), lambda bi, i: (0, 0)),
                pl.BlockSpec(conv_w.shape, lambda bi, i: (0, 0)),
                pl.BlockSpec(conv_b.shape, lambda bi, i: (0, 0)),
                pl.BlockSpec((None,) + w_in.shape[1:], lambda bi, i: (0, 0, 0), pipeline_mode=pl.Buffered(1))]
    out_specs, out_shapes = [], []
    scratch = [((tm + 2 * HALO, d), BF16)]
    blk = (_nbytes((tm, d), F32) + 3 * _nbytes((tm, V7X_LANES), F32) + 4 * _nbytes((tm + 2 * HALO, 2 * MLSTM_W), F32)
           + _nbytes(w_in.shape, F32) // 2)
    for kind, _, width in groups:
        if kind == "gates":
            scratch.append(((width, d), BF16))
            out_specs.append(pl.BlockSpec((1, N_GATES, tm), lambda bi, i: (bi, 0, i)))
            out_shapes.append(jax.ShapeDtypeStruct((b, N_GATES, n), F32))
            continue
        scratch.append(((width, d), BF16))
        if kind in ("heads", "conv_qk", "conv_k"):
            slabs = width // V7X_LANES
            out_specs.append(pl.BlockSpec((1, slabs, tm, V7X_LANES), lambda bi, i: (bi, 0, i, 0)))
            out_shapes.append(jax.ShapeDtypeStruct((b, slabs, n, V7X_LANES), BF16))
        else:
            out_specs.append(pl.BlockSpec((1, tm, width), lambda bi, i: (bi, i, 0)))
            out_shapes.append(jax.ShapeDtypeStruct((b, n, width), BF16))
        blk += _nbytes((tm, width), BF16) + _nbytes((tm, width), F32)
    blk += sum(_nbytes(s, dt) for s, dt in scratch) // 2
    return pl.pallas_call(
        functools.partial(_inproj_body, groups=tuple(groups)),
        grid=(b, n // tm),
        in_specs=in_specs,
        out_specs=out_specs,
        out_shape=out_shapes,
        scratch_shapes=[pltpu.VMEM(s, dt) for s, dt in scratch],
        compiler_params=_params(blk, 2),
        name=name,
    )(x, x, x, gain, mod3, mod3, cos, sina, sinb, gate_b, conv_w, conv_b, w_in)


def _sigmoid(y):
    return 0.5 + 0.5 * jnp.tanh(0.5 * y)


def _log_sigmoid(x):
    return jnp.minimum(x, 0.0) - jnp.log1p(jnp.exp(-jnp.abs(x)))


def _mlstm_body(qs, ks, v_ref, o_ref, kcs, vc_ref, gt_ref, gain_ref,
                y_ref, gtab, rtab, cumtab, kts, cnl, c0n, cumc, sqk, sds, inter_s, em_s, svq,
                *, n_ctx_chunks, n_lat_chunks):
    t = MCHUNK
    dh = MLSTM_DH
    n_chunks = n_ctx_chunks + n_lat_chunks
    nt = (((1,), (1,)), ((), ()))

    tbl = gt_ref[0, 0]
    rid = lax.broadcasted_iota(jnp.int32, tbl.shape, 0) // MAX_CHUNKS
    lane = lax.broadcasted_iota(jnp.int32, tbl.shape, 1)
    lf = _log_sigmoid(tbl)
    pre = jnp.where(rid == 1, lf, 0.0)
    suf = jnp.where(rid == 3, lf, 0.0)
    s = 1
    while s < t:
        pre = pre + jnp.where(lane >= s, pltpu.roll(pre, s, 1), 0.0)
        suf = suf + jnp.where(lane < t - s, pltpu.roll(suf, t - s, 1), 0.0)
        s *= 2
    li = (tbl[0:MAX_CHUNKS] * LOG2E, tbl[2 * MAX_CHUNKS:3 * MAX_CHUNKS] * LOG2E)
    cum = (pre[MAX_CHUNKS:2 * MAX_CHUNKS] * LOG2E, suf[3 * MAX_CHUNKS:4 * MAX_CHUNKS] * LOG2E)
    total = (cum[0][:, t - 1:t], cum[1][:, 0:1])
    mloc = []
    for d in (0, 1):
        g = total[d] - cum[d] + li[d]
        mloc.append(jnp.max(g, axis=1, keepdims=True))
        gtab[d] = g
        rtab[d] = cum[d] - li[d]
        cumtab[d] = cum[d]

    orders = (list(range(n_chunks)),
              list(range(n_ctx_chunks - 1, -1, -1)) + list(range(n_chunks - 1, n_ctx_chunks - 1, -1)))

    a_sc, m0, m1 = ({}, {}), ({}, {}), ({}, {})
    for d in (0, 1):
        m_st = jnp.zeros((1, 1), F32)
        for c in orders[d][:-1]:
            m0[d][c] = m_st
            tot_c = total[d][c:c + 1, :]
            m1[d][c] = jnp.maximum(tot_c + m_st, mloc[d][c:c + 1, :])
            a_sc[d][c] = jnp.exp2(tot_c + m_st - m1[d][c])
            m_st = m1[d][c]
        m0[d][orders[d][-1]] = m_st

    def kv_chunk(c):
        if c < n_ctx_chunks:
            rows = pl.ds(c * t, t)
            return kcs[rows, :], vc_ref[rows, :]
        rows = pl.ds((c - n_ctx_chunks) * t, t)
        return ks[rows, :], v_ref[rows, :]

    ri = lax.broadcasted_iota(jnp.int32, (t, t), 0)
    ci = lax.broadcasted_iota(jnp.int32, (t, t), 1)
    eye = jnp.where(ri == ci, 1.0, 0.0).astype(BF16)
    ones = jnp.ones((t, dh), BF16)

    for c in range(n_chunks):
        kts[c] = lax.dot_general(eye, kv_chunk(c)[0], nt, preferred_element_type=F32)
    for c in range(n_chunks):
        vo = jnp.concatenate([kv_chunk(c)[1], ones], axis=1)
        for d in (0, 1):
            if c != orders[d][-1]:
                w = jnp.exp2(gtab[d, pl.ds(c, 1), :] - m1[d][c])
                cnl[d, c] = jnp.dot((kts[c] * w).astype(BF16), vo, preferred_element_type=F32)

    for d in (0, 1):
        cn = jnp.zeros((dh, 2 * dh), F32)
        for c in orders[d]:
            if c >= n_ctx_chunks:
                c0n[d, c - n_ctx_chunks] = cn.astype(BF16)
            if c != orders[d][-1]:
                cn = a_sc[d][c] * cn + cnl[d, c]

    for j in range(n_lat_chunks):
        c = j + n_ctx_chunks
        rows = pl.ds(j * t, t)
        sqk[rows, :] = lax.dot_general(qs[rows, :], ks[rows, :], nt, preferred_element_type=F32)
        for d in (0, 1):
            cumc[d, rows, :] = jnp.broadcast_to(cumtab[d, pl.ds(c, 1), :], (t, t)).T

    grp = 2
    gi = lax.broadcasted_iota(jnp.int32, (grp * t, t), 0) & (t - 1)
    gs = lax.broadcasted_iota(jnp.int32, (grp * t, t), 1)
    for d in (0, 1):
        ok = (gs <= gi) if d == 0 else (gs >= gi)
        for j0 in range(0, n_lat_chunks, grp):
            rows = pl.ds(j0 * t, grp * t)
            cs = [j + n_ctx_chunks for j in range(j0, j0 + grp)]
            rb = jnp.concatenate([jnp.broadcast_to(rtab[d, pl.ds(c, 1), :], (t, t)) for c in cs], axis=0)
            m0b = jnp.concatenate([jnp.broadcast_to(m0[d][c], (t, t)) for c in cs], axis=0)
            cum_c = cumc[d, rows, :]
            d_log = jnp.where(ok, cum_c - rb, -jnp.inf)
            a_log = cum_c + m0b
            m_t = jnp.maximum(a_log, jnp.max(d_log, axis=1, keepdims=True))
            sds[d, rows, :] = sqk[rows, :].astype(BF16) * jnp.exp2((d_log - m_t).astype(BF16))
            inter_s[d, rows, :] = jnp.exp2(a_log - m_t)
            em_s[d, rows, :] = jnp.exp2(-m_t)

    for j in range(n_lat_chunks):
        rows = pl.ds(j * t, t)
        q = qs[rows, :]
        vo = jnp.concatenate([v_ref[rows, :], ones], axis=1)
        for d in (0, 1):
            svq[d, rows, :] = (jnp.dot(sds[d, rows, :], vo, preferred_element_type=F32)
                               + jnp.tile(inter_s[d, rows, :], (1, 2))
                               * jnp.dot(q, c0n[d, j], preferred_element_type=F32))

    gain = gain_ref[...]
    grp = 4
    for j0 in range(0, n_lat_chunks, grp):
        rows = pl.ds(j0 * t, grp * t)
        hid = None
        for d in (0, 1):
            h_d = svq[d, rows, pl.ds(0, dh)] / jnp.maximum(jnp.abs(svq[d, rows, pl.ds(dh, dh)]), em_s[d, rows, :])
            hid = h_d if hid is None else hid + h_d
        hn = hid * lax.rsqrt(jnp.mean(hid * hid, axis=-1, keepdims=True) + EPS) * gain
        y_ref[rows, :] = (hn * _sigmoid(o_ref[rows, :].astype(F32))).astype(BF16)


def _mlstm(qk, vo, k_ctx, v_ctx, gate_tbl, gain):
    b, _, l, _ = qk.shape
    n_ctx = k_ctx.shape[2]
    dh = MLSTM_DH
    nh = MLSTM_HEADS
    n_ctx_chunks, n_lat_chunks = n_ctx // MCHUNK, l // MCHUNK
    seq = lambda slab0: pl.BlockSpec((None, None, l, dh), lambda bi, h: (bi, slab0 + h, 0, 0))
    ctx = lambda slab0: pl.BlockSpec((None, None, n_ctx, dh), lambda bi, h: (bi, slab0 + h, 0, 0))
    n_chunks = n_ctx_chunks + n_lat_chunks
    scratch = [((2, MAX_CHUNKS, MCHUNK), F32), ((2, MAX_CHUNKS, MCHUNK), F32),
               ((2, MAX_CHUNKS, MCHUNK), F32), ((n_chunks, dh, MCHUNK), F32),
               ((2, n_chunks, dh, 2 * dh), F32), ((2, n_lat_chunks, dh, 2 * dh), BF16),
               ((2, l, MCHUNK), F32), ((l, MCHUNK), F32), ((2, l, MCHUNK), BF16),
               ((2, l, MCHUNK), F32), ((2, l, MCHUNK), F32), ((2, l, 2 * dh), F32)]
    blk = (5 * _nbytes((l, dh), BF16) + 2 * _nbytes((n_ctx, dh), BF16) + _nbytes((V7X_LANES, V7X_LANES), F32)
           + sum(_nbytes(s, dt) for s, dt in scratch) // 2)
    return pl.pallas_call(
        functools.partial(_mlstm_body, n_ctx_chunks=n_ctx_chunks, n_lat_chunks=n_lat_chunks),
        grid=(b, nh),
        in_specs=[seq(0), seq(nh), seq(0), seq(nh), ctx(0), ctx(0),
                  pl.BlockSpec((1, 1, 4 * MAX_CHUNKS, MCHUNK), lambda bi, h: (bi, h, 0, 0)),
                  pl.BlockSpec((1, dh), lambda bi, h: (0, h))],
        out_specs=pl.BlockSpec((None, None, l, dh), lambda bi, h: (bi, h, 0, 0)),
        out_shape=jax.ShapeDtypeStruct((b, nh, l, dh), BF16),
        scratch_shapes=[pltpu.VMEM(s, dt) for s, dt in scratch],
        compiler_params=_params(blk, 2),
        name="mlstm",
    )(qk, qk, vo, vo, k_ctx, v_ctx, gate_tbl, gain)


def _attn_body(sink_ref, bias_ref, q_ref, kv_ref, kvc_ref, o_ref, s_scr, p_scr, e_scr, *, seq_len):
    blk = ATTN_BLOCK
    n_blocks = seq_len // blk
    n_ctx = kvc_ref.shape[1]
    rows = GQA_GROUP * blk
    nt = (((1,), (1,)), ((), ()))
    chains = [(qb, g) for qb in range(ATTN_BLOCKS_PER_STEP) for g in range(KV_HEADS)]

    def block_start(qb):
        i = pl.program_id(1) * ATTN_BLOCKS_PER_STEP + qb
        return i, pl.multiple_of(jnp.clip((i - 1) * blk, 0, seq_len - BAND), blk)

    for ci, (qb, g) in enumerate(chains):
        i, start = block_start(qb)
        k_cols = pl.ds(g * ATTN_DH, ATTN_DH)
        q = jnp.concatenate([q_ref[0, pl.ds(qb * blk, blk), pl.ds((g * GQA_GROUP + j) * ATTN_DH, ATTN_DH)]
                             for j in range(GQA_GROUP)], axis=0)
        bias = bias_ref[jnp.where(i == 0, 0, jnp.where(i == n_blocks - 1, 2, 1))]
        s_scr[ci, :, pl.ds(0, BAND)] = (
            lax.dot_general(q, kv_ref[0, pl.ds(start, BAND), k_cols], nt, preferred_element_type=F32) + bias)
        s_scr[ci, :, pl.ds(BAND, n_ctx)] = lax.dot_general(q, kvc_ref[0, :, k_cols], nt, preferred_element_type=F32)

    for ci, (qb, g) in enumerate(chains):
        s = s_scr[ci]
        sink = jnp.concatenate([jnp.full((blk, V7X_LANES), sink_ref[g * GQA_GROUP + j] * LOG2E, F32)
                                for j in range(GQA_GROUP)], axis=0)
        m = jnp.maximum(jnp.broadcast_to(jnp.max(s, axis=1, keepdims=True), (rows, V7X_LANES)), sink)
        p_scr[ci] = jnp.exp2((s - jnp.tile(m, (1, (BAND + n_ctx) // V7X_LANES))).astype(BF16))
        e_scr[ci] = jnp.exp2(sink - m)

    ones_b = jnp.ones((BAND, ATTN_DH), BF16)
    ones_c = jnp.ones((n_ctx, ATTN_DH), BF16)
    for ci, (qb, g) in enumerate(chains):
        _, start = block_start(qb)
        v_cols = pl.ds(KV_W + g * ATTN_DH, ATTN_DH)
        vb = jnp.concatenate([kv_ref[0, pl.ds(start, BAND), v_cols], ones_b], axis=1)
        vc = jnp.concatenate([kvc_ref[0, :, v_cols], ones_c], axis=1)
        acc = (jnp.dot(p_scr[ci, :, pl.ds(0, BAND)], vb, preferred_element_type=F32)
               + jnp.dot(p_scr[ci, :, pl.ds(BAND, n_ctx)], vc, preferred_element_type=F32))
        out = (acc / (pltpu.roll(acc, ATTN_DH, 1) + e_scr[ci]))[:, :ATTN_DH]
        for j in range(GQA_GROUP):
            h = g * GQA_GROUP + j
            o_ref[0, pl.ds(qb * blk, blk), pl.ds(h * ATTN_DH, ATTN_DH)] = out[j * blk:(j + 1) * blk].astype(BF16)


def _band_bias(seq_len):
    row = jnp.arange(ATTN_BLOCK)[:, None]
    col = jnp.arange(BAND)[None, :]
    n_blocks = seq_len // ATTN_BLOCK
    tables = []
    for i in (0, 1, n_blocks - 1):
        start = min(max((i - 1) * ATTN_BLOCK, 0), seq_len - BAND)
        ok = jnp.abs(start + col - (i * ATTN_BLOCK + row)) <= WINDOW
        tables.append(jnp.tile(jnp.where(ok, 0.0, -jnp.inf).astype(F32), (GQA_GROUP, 1)))
    return jnp.stack(tables)


def _attn(sink, q, kv, kvc):
    b, l, _ = q.shape
    n_ctx = kvc.shape[1]
    rows = ATTN_BLOCKS_PER_STEP * ATTN_BLOCK
    bias = _band_bias(l)
    n_chains = KV_HEADS * ATTN_BLOCKS_PER_STEP
    stacked = GQA_GROUP * ATTN_BLOCK
    scratch = [((n_chains, stacked, BAND + n_ctx), F32), ((n_chains, stacked, BAND + n_ctx), BF16),
               ((n_chains, stacked, V7X_LANES), F32)]
    blk = (2 * _nbytes((rows, ATTN_W), BF16) + _nbytes((l, 2 * KV_W), BF16) + _nbytes((n_ctx, 2 * KV_W), BF16)
           + _nbytes(bias.shape, F32) + sum(_nbytes(s, dt) for s, dt in scratch) // 2)
    return pl.pallas_call(
        functools.partial(_attn_body, seq_len=l),
        grid=(b, l // rows),
        in_specs=[pl.BlockSpec(memory_space=pltpu.SMEM),
                  pl.BlockSpec(bias.shape, lambda bi, i: (0, 0, 0)),
                  pl.BlockSpec((1, rows, ATTN_W), lambda bi, i: (bi, i, 0)),
                  pl.BlockSpec((1, l, 2 * KV_W), lambda bi, i: (bi, 0, 0)),
                  pl.BlockSpec((1, n_ctx, 2 * KV_W), lambda bi, i: (bi, 0, 0))],
        out_specs=pl.BlockSpec((1, rows, ATTN_W), lambda bi, i: (bi, i, 0)),
        out_shape=jax.ShapeDtypeStruct((b, l, ATTN_W), BF16),
        scratch_shapes=[pltpu.VMEM(s, dt) for s, dt in scratch],
        compiler_params=_params(blk, 2),
        name="attn",
    )(sink, bias, q, kv, kvc)


GELU_C1 = 0.7978845608028654
GELU_C2 = GELU_C1 * 0.044715


def _twice_gelu(x):
    return x + x * jnp.tanh(x * (GELU_C1 + GELU_C2 * (x * x)))


def _mixffn_body(ym_ref, ymp_ref, ymn_ref, ya_ref, yap_ref, yan_ref, x_ref, xp_ref, xn_ref,
                 wo_ref, ga_ref, gain_ref, sh_ref, sc_ref, gf_ref, wup_ref, cw_ref, cb_ref, wdn_ref, fn_ref,
                 o_ref, ys, hs, x1s, acc):
    i = pl.program_id(1)
    tm = x_ref.shape[1]
    ext = tm + 2 * HALO
    lo, mid, hi = pl.ds(0, HALO), pl.ds(HALO, tm), pl.ds(HALO + tm, HALO)
    for rows, m_ref, a_ref in ((lo, ymp_ref, yap_ref), (mid, ym_ref, ya_ref), (hi, ymn_ref, yan_ref)):
        for h in range(MLSTM_HEADS):
            ys[rows, pl.ds(h * MLSTM_DH, MLSTM_DH)] = m_ref[0, h]
        ys[rows, pl.ds(MLSTM_W, ATTN_W)] = a_ref[0]
    proj = jnp.dot(ys[...], wo_ref[...], preferred_element_type=F32)
    ga = ga_ref[0]

    def mixed(x_rows, proj_rows):
        x1 = x_rows + ga * proj_rows
        return x1, _norm_modulate(x1, gain_ref[...], sh_ref[0], sc_ref[0]).astype(BF16)

    x1, h_mid = mixed(x_ref[0], proj[HALO:HALO + tm])
    x1s[...] = x1
    hs[mid, :] = h_mid
    h_lo = mixed(xp_ref[0], proj[:HALO])[1]
    h_hi = mixed(xn_ref[0], proj[HALO + tm:])[1]
    hs[lo, :] = jnp.where(i > 0, h_lo, jnp.zeros_like(h_lo))
    hs[hi, :] = jnp.where(i < pl.num_programs(1) - 1, h_hi, jnp.zeros_like(h_hi))
    starts = list(range(0, D_FF, FF_CHUNK))

    def up_a(col0):
        return jnp.dot(hs[...], wup_ref[:, pl.ds(col0, min(FF_CHUNK, D_FF - col0))], preferred_element_type=F32)

    nxt = up_a(starts[0])
    for c, col0 in enumerate(starts):
        width = min(FF_CHUNK, D_FF - col0)
        cols = pl.ds(col0, width)
        a = nxt
        if c + 1 < len(starts):
            nxt = up_a(starts[c + 1])
        a_prev = pltpu.roll(a, 1, 0)[HALO:HALO + tm]
        a_next = pltpu.roll(a, ext - 1, 0)[HALO:HALO + tm]
        conv = (cw_ref[0:1, cols] * a_prev + cw_ref[1:2, cols] * a[HALO:HALO + tm]
                + cw_ref[2:3, cols] * a_next + cb_ref[:, cols])
        g2 = _twice_gelu(conv)
        val = jnp.dot(hs[mid, :], wup_ref[:, pl.ds(D_FF + col0, width)], preferred_element_type=F32)
        part = jnp.dot((g2 * val).astype(BF16), wdn_ref[cols, :], preferred_element_type=F32)
        if c == 0:
            acc[...] = part
        else:
            acc[...] += part
    x2 = x1s[...] + (0.5 * gf_ref[0]) * acc[...]
    ms = jnp.mean(x2 * x2, axis=-1, keepdims=True)
    o_ref[0] = x2 * lax.rsqrt(ms + EPS) * fn_ref[...]


def _mixffn(ym, ya, x, mod3, w_out, norm_gain, w_up, conv_w, conv_b, w_down, final_norm):
    b, l, d = x.shape
    tm = ROW_TILE
    per = tm // HALO
    last = l // HALO - 1
    ext = tm + 2 * HALO
    tile, before, after = (lambda bi, i: i), (lambda bi, i: jnp.maximum(i * per - 1, 0)), \
        (lambda bi, i: jnp.minimum((i + 1) * per, last))

    def rows3(make):
        return [make(tm, tile), make(HALO, before), make(HALO, after)]

    ym_spec = lambda n, r: pl.BlockSpec((1, MLSTM_HEADS, n, MLSTM_DH), lambda bi, i: (bi, 0, r(bi, i), 0))
    ya_spec = lambda n, r: pl.BlockSpec((1, n, ATTN_W), lambda bi, i: (bi, r(bi, i), 0))
    x_spec = lambda n, r: pl.BlockSpec((1, n, d), lambda bi, i: (bi, r(bi, i), 0))
    mod_spec = lambda piece: pl.BlockSpec((1, 1, d), lambda bi, i: (bi, 0, piece))
    const = lambda arr: pl.BlockSpec(arr.shape, lambda bi, i: (0,) * arr.ndim)
    resident = lambda arr: pl.BlockSpec(arr.shape, lambda bi, i: (0,) * arr.ndim, pipeline_mode=pl.Buffered(1))
    scratch = [((ext, d), BF16), ((ext, d), BF16), ((tm, d), F32), ((tm, d), F32)]
    weights = _nbytes(w_out.shape, BF16) + _nbytes(w_up.shape, BF16) + _nbytes(w_down.shape, BF16)
    blk = (2 * _nbytes((tm, d), F32) + 2 * _nbytes((tm, d), BF16) + 4 * _nbytes((ext, FF_CHUNK), F32)
           + (weights + sum(_nbytes(s, dt) for s, dt in scratch)) // 2)
    return pl.pallas_call(
        _mixffn_body,
        grid=(b, l // tm),
        in_specs=(rows3(ym_spec) + rows3(ya_spec) + rows3(x_spec)
                  + [resident(w_out), mod_spec(2), const(norm_gain), mod_spec(3), mod_spec(4), mod_spec(5),
                     resident(w_up), const(conv_w), const(conv_b), resident(w_down), const(final_norm)]),
        out_specs=pl.BlockSpec((1, tm, d), lambda bi, i: (bi, i, 0)),
        out_shape=jax.ShapeDtypeStruct((b, l, d), F32),
        scratch_shapes=[pltpu.VMEM(s, dt) for s, dt in scratch],
        compiler_params=_params(blk, 2),
        name="mixffn",
    )(ym, ym, ym, ya, ya, ya, x, x, x, w_out, mod3, norm_gain, mod3, mod3, mod3,
      w_up, conv_w, conv_b, w_down, final_norm)


def _rope_tables(n_tokens):
    pos = jnp.arange(n_tokens)
    r = (pos // GRID_W).astype(F32)
    c = (pos % GRID_W).astype(F32)
    inv = ROPE_BASE ** (-jnp.arange(ROPE_AXIS_PAIRS, dtype=F32) / ROPE_AXIS_PAIRS)
    ar, ac = r[:, None] * inv, c[:, None] * inv
    zero = jnp.zeros_like(ar)
    cos = jnp.concatenate([jnp.cos(ar), jnp.cos(ar), jnp.cos(ac), jnp.cos(ac)], axis=1)
    sina = jnp.concatenate([-jnp.sin(ar), zero, -jnp.sin(ac), zero], axis=1)
    sinb = jnp.concatenate([zero, jnp.sin(ar), zero, jnp.sin(ac)], axis=1)
    rep = V7X_LANES // ATTN_DH
    return tuple(jnp.tile(tb, (1, rep)) for tb in (cos, sina, sinb))


def _identity_rope(n_tokens):
    return (jnp.ones((n_tokens, V7X_LANES), F32), jnp.zeros((n_tokens, V7X_LANES), F32),
            jnp.zeros((n_tokens, V7X_LANES), F32))


def _gate_table(gates_ctx, gates_x):
    g = jnp.concatenate([gates_ctx, gates_x], axis=2)
    b, _, n = g.shape
    nc = n // MCHUNK
    g = g.reshape(b, 4, MLSTM_HEADS, nc, MCHUNK).transpose(0, 2, 1, 3, 4)
    g = jnp.pad(g, ((0, 0), (0, 0), (0, 0), (0, MAX_CHUNKS - nc), (0, 0)))
    return g.reshape(b, MLSTM_HEADS, 4 * MAX_CHUNKS, MCHUNK)


def kernel(x, c, ctx, c_ctx, w_ada, b_ada, norm_mix, norm_ffn, w_in, gate_b, qk_conv_w, qk_conv_b, mlstm_norm,
           attn_sink, w_out, w_up, ffn_conv_w, ffn_conv_b, w_down, final_norm):
    b, l, d = x.shape
    n_ctx = ctx.shape[1]
    assert w_ada.shape[0] == 1, "single-layer stack"
    assert l % ROW_TILE == 0 and l % MCHUNK == 0 and n_ctx % MCHUNK == 0 and l >= BAND
    assert l % (ATTN_BLOCKS_PER_STEP * ATTN_BLOCK) == 0 and MCHUNK == V7X_LANES
    assert (l + n_ctx) // MCHUNK <= MAX_CHUNKS

    rows = -(-(b + 1) // V7X_SUBLANES) * V7X_SUBLANES
    cvec = jnp.zeros((rows, d), F32).at[:b].set(c).at[b].set(c_ctx)
    mod = _ada(cvec, w_ada[0], b_ada[0][None, :])
    mod_x = mod[:b].reshape(b, 1, 6 * d)
    mod_c = mod[b:b + 1].reshape(1, 1, 6 * d)

    w_in = jnp.swapaxes(w_in, 1, 2)
    o0, o1, o2 = 2 * MLSTM_W, 3 * MLSTM_W, 4 * MLSTM_W
    o3 = o2 + N_GATES
    o4 = o3 + ATTN_W
    gb = gate_b[0][:, None]
    gain_mix = norm_mix[0][None, :]
    conv = (qk_conv_w[0], qk_conv_b[0][None, :])
    k_mc, v_mc, gates_c, kv_ac = _inproj(
        ctx, gain_mix, mod_c, 0, 1, False, _identity_rope(n_ctx), gb, *conv, w_in,
        [("conv_k", MLSTM_W, MLSTM_W), ("heads", o0, MLSTM_W), ("gates", o2, N_GATES), ("plain", o4, 2 * KV_W)],
        "inproj_ctx")
    qk_m, vo_m, gates_x, q_a, kv_a = _inproj(
        x, gain_mix, mod_x, 0, 1, True, _rope_tables(l), gb, *conv, w_in,
        [("conv_qk", 0, o0), ("heads", o0, o0), ("gates", o2, N_GATES), ("rope_q", o3, ATTN_W),
         ("rope_kv", o4, 2 * KV_W)], "inproj_x")

    y_m = _mlstm(qk_m, vo_m, k_mc, v_mc, _gate_table(gates_c, gates_x), mlstm_norm[0][None, :])
    y_a = _attn(attn_sink[0], q_a, kv_a, kv_ac)

    return _mixffn(y_m, y_a, x, mod_x, w_out[0].astype(BF16), norm_ffn[0][None, :], w_up[0].astype(BF16),
                   ffn_conv_w[0], ffn_conv_b[0][None, :], w_down[0].astype(BF16), final_norm[None, :])
```

```python
import functools

import jax
import jax.numpy as jnp
from jax import lax
from jax.experimental import pallas as pl
from jax.experimental.pallas import tpu as pltpu

F32 = jnp.float32
BF16 = jnp.bfloat16

D_MODEL = 1024
GRID_W = 64
MLSTM_HEADS = 4
MLSTM_DH = 128
MLSTM_W = MLSTM_HEADS * MLSTM_DH
N_GATES = 4 * MLSTM_HEADS
ATTN_HEADS = 8
KV_HEADS = 2
ATTN_DH = 64
ATTN_W = ATTN_HEADS * ATTN_DH
KV_W = KV_HEADS * ATTN_DH
GQA_GROUP = ATTN_HEADS // KV_HEADS
WINDOW = 128
ROPE_BASE = 10000.0
ROPE_AXIS_PAIRS = ATTN_DH // 4
D_FF = 2816
EPS = 1e-6
LOG2E = 1.4426950408889634

V7X_LANES = 128
V7X_SUBLANES = 8
V7X_VMEM_BYTES = 64 * 1024 * 1024

ROW_TILE = 512
ADA_COL_TILE = 1536
PROJ_CHUNK = 512
MCHUNK = 128
MAX_CHUNKS = 32
ATTN_BLOCK = 128
ATTN_BLOCKS_PER_STEP = 4
BAND = 3 * ATTN_BLOCK
FF_CHUNK = 512
HALO = 2 * V7X_SUBLANES


def _vmem_limit(block_bytes):
    return int(min(V7X_VMEM_BYTES * 7 // 8, 2 * block_bytes + 16 * 1024 * 1024))


def _params(block_bytes, n_axes):
    return pltpu.CompilerParams(dimension_semantics=("arbitrary",) * n_axes,
                                vmem_limit_bytes=_vmem_limit(block_bytes))


def _nbytes(shape, dtype):
    n = 1
    for s in shape:
        n *= s
    return n * jnp.dtype(dtype).itemsize


def _ada_body(c_ref, w_ref, b_ref, o_ref):
    c = c_ref[...]
    s = (c * jax.nn.sigmoid(c)).astype(BF16)
    o_ref[...] = jnp.dot(s, w_ref[...].astype(BF16), preferred_element_type=F32) + b_ref[...]


def _ada(cvec, w_ada, b_ada):
    rows, d = cvec.shape
    n = w_ada.shape[1]
    blk = _nbytes((d, ADA_COL_TILE), F32) + _nbytes((rows, d), F32) + 2 * _nbytes((rows, ADA_COL_TILE), F32)
    return pl.pallas_call(
        _ada_body,
        grid=(n // ADA_COL_TILE,),
        in_specs=[pl.BlockSpec((rows, d), lambda j: (0, 0)),
                  pl.BlockSpec((d, ADA_COL_TILE), lambda j: (0, j)),
                  pl.BlockSpec((1, ADA_COL_TILE), lambda j: (0, j))],
        out_specs=pl.BlockSpec((rows, ADA_COL_TILE), lambda j: (0, j)),
        out_shape=jax.ShapeDtypeStruct((rows, n), F32),
        compiler_params=_params(blk, 1),
        name="ada",
    )(cvec, w_ada, b_ada)


def _norm_modulate(x, gain, shift, scale):
    ms = jnp.mean(x * x, axis=-1, keepdims=True)
    return (x * lax.rsqrt(ms + EPS) * gain) * (1.0 + scale) + shift


def _silu(y):
    h = 0.5 * y
    return h + h * jnp.tanh(h)


def _inproj_body(x_ref, xp_ref, xn_ref, gain_ref, sh_ref, sc_ref, cos_ref, sina_ref, sinb_ref, gb_ref,
                 cw_ref, cb_ref, win_ref, *refs, groups):
    n = len(groups)
    o_refs, hs, w_refs = refs[:n], refs[n], refs[n + 1:]
    kinds = [g[0] for g in groups]

    @pl.when((pl.program_id(0) == 0) & (pl.program_id(1) == 0))
    def _():
        for (_, col0, width), w_bf in zip(groups, w_refs):
            w_bf[...] = win_ref[pl.ds(col0, width), :].astype(BF16)

    proj = lambda lhs, w_bf: lax.dot_general(lhs, w_bf[...], (((1,), (1,)), ((), ())), preferred_element_type=F32)

    i = pl.program_id(1)
    tm = x_ref.shape[1]
    ext = tm + 2 * HALO
    norm = lambda rows: _norm_modulate(rows, gain_ref[...], sh_ref[0], sc_ref[0]).astype(BF16)
    hb = norm(x_ref[0])
    h_lo, h_hi = norm(xp_ref[0]), norm(xn_ref[0])
    hs[pl.ds(0, HALO), :] = jnp.where(i > 0, h_lo, jnp.zeros_like(h_lo))
    hs[pl.ds(HALO, tm), :] = hb
    hs[pl.ds(HALO + tm, HALO), :] = jnp.where(i < pl.num_programs(1) - 1, h_hi, jnp.zeros_like(h_hi))
    def conv_item(w_ref, o_ref, s0, c0, first_k):
        def epilogue(a):
            cc = pl.ds(c0 + s0, PROJ_CHUNK)
            a_prev = pltpu.roll(a, 1, 0)[HALO:HALO + tm]
            a_next = pltpu.roll(a, ext - 1, 0)[HALO:HALO + tm]
            y = _silu(cw_ref[0:1, cc] * a_prev + cw_ref[1:2, cc] * a[HALO:HALO + tm]
                      + cw_ref[2:3, cc] * a_next + cb_ref[:, cc])
            for j in range(PROJ_CHUNK // V7X_LANES):
                slab = s0 // V7X_LANES + j
                u = y[:, j * V7X_LANES:(j + 1) * V7X_LANES]
                o_ref[0, slab] = (u * MLSTM_DH ** -0.5 if slab >= first_k else u).astype(BF16)
        return (lambda: proj(hs[...], w_ref.at[pl.ds(s0, PROJ_CHUNK)])), epilogue

    def heads_item(w_ref, o_ref, s0):
        def epilogue(u):
            for j in range(PROJ_CHUNK // V7X_LANES):
                o_ref[0, s0 // V7X_LANES + j] = u[:, j * V7X_LANES:(j + 1) * V7X_LANES].astype(BF16)
        return (lambda: proj(hb, w_ref.at[pl.ds(s0, PROJ_CHUNK)])), epilogue

    def gates_item(w_ref, o_ref):
        def epilogue(u):
            o_ref[0] = u + gb_ref[...]
        return (lambda: lax.dot_general(w_ref[...], hb, (((1,), (1,)), ((), ())), preferred_element_type=F32)), epilogue

    def rope_item(w_ref, o_ref, kind):
        def epilogue(u_all):
            n_groups = u_all.shape[1] // V7X_LANES
            n_rot = {"rope_q": n_groups, "rope_kv": KV_W // V7X_LANES, "plain": 0}[kind]
            for j in range(n_groups):
                u = u_all[:, j * V7X_LANES:(j + 1) * V7X_LANES]
                if j < n_rot:
                    u = (u * cos_ref[...]
                         + pltpu.roll(u, V7X_LANES - ROPE_AXIS_PAIRS, 1) * sina_ref[...]
                         + pltpu.roll(u, ROPE_AXIS_PAIRS, 1) * sinb_ref[...])
                    if kind == "rope_q":
                        u = u * (ATTN_DH ** -0.5 * LOG2E)
                o_ref[0, :, pl.ds(j * V7X_LANES, V7X_LANES)] = u.astype(BF16)
        return (lambda: proj(hb, w_ref)), epilogue

    items = []
    for kind, w_ref, o_ref in zip(kinds, w_refs, o_refs):
        if kind in ("conv_qk", "conv_k"):
            width = w_ref.shape[0]
            c0 = cw_ref.shape[1] - width
            items += [conv_item(w_ref, o_ref, s0, c0, width // V7X_LANES - MLSTM_HEADS)
                      for s0 in range(0, width, PROJ_CHUNK)]
        elif kind == "heads":
            items += [heads_item(w_ref, o_ref, s0) for s0 in range(0, w_ref.shape[0], PROJ_CHUNK)]
        elif kind == "gates":
            items.append(gates_item(w_ref, o_ref))
        else:
            items.append(rope_item(w_ref, o_ref, kind))
    for matmul, epilogue in items:
        epilogue(matmul())


def _inproj(x, gain, mod3, shift_idx, scale_idx, per_batch_mod, rope, gate_b, conv_w, conv_b, w_in, groups, name):
    b, n, d = x.shape
    tm = min(ROW_TILE, n)
    per = tm // HALO
    last = n // HALO - 1
    cos, sina, sinb = rope
    mod_spec = lambda piece: pl.BlockSpec(
        (1, 1, d), (lambda bi, i: (bi, 0, piece)) if per_batch_mod else (lambda bi, i: (0, 0, piece)))
    in_specs = [pl.BlockSpec((1, tm, d), lambda bi, i: (bi, i, 0)),
                pl.BlockSpec((1, HALO, d), lambda bi, i: (bi, jnp.maximum(i * per - 1, 0), 0)),
                pl.BlockSpec((1, HALO, d), lambda bi, i: (bi, jnp.minimum((i + 1) * per, last), 0)),
                pl.BlockSpec((1, d), lambda bi, i: (0, 0)),
                mod_spec(shift_idx), mod_spec(scale_idx),
                pl.BlockSpec((tm, V7X_LANES), lambda bi, i: (i, 0)),
                pl.BlockSpec((tm, V7X_LANES), lambda bi, i: (i, 0)),
                pl.BlockSpec((tm, V7X_LANES), lambda bi, i: (i, 0)),
                pl.BlockSpec((N_GATES, 1), lambda bi, i: (0, 0)),
                pl.BlockSpec(conv_w.shape, lambda bi, i: (0, 0)),
                pl.BlockSpec(conv_b.shape, lambda bi, i: (0, 0)),
                pl.BlockSpec((None,) + w_in.shape[1:], lambda bi, i: (0, 0, 0), pipeline_mode=pl.Buffered(1))]
    out_specs, out_shapes = [], []
    scratch = [((tm + 2 * HALO, d), BF16)]
    blk = (_nbytes((tm, d), F32) + 3 * _nbytes((tm, V7X_LANES), F32) + 4 * _nbytes((tm + 2 * HALO, 2 * MLSTM_W), F32)
           + _nbytes(w_in.shape, F32) // 2)
    for kind, _, width in groups:
        if kind == "gates":
            scratch.append(((width, d), BF16))
            out_specs.append(pl.BlockSpec((1, N_GATES, tm), lambda bi, i: (bi, 0, i)))
            out_shapes.append(jax.ShapeDtypeStruct((b, N_GATES, n), F32))
            continue
        scratch.append(((width, d), BF16))
        if kind in ("heads", "conv_qk", "conv_k"):
            slabs = width // V7X_LANES
            out_specs.append(pl.BlockSpec((1, slabs, tm, V7X_LANES), lambda bi, i: (bi, 0, i, 0)))
            out_shapes.append(jax.ShapeDtypeStruct((b, slabs, n, V7X_LANES), BF16))
        else:
            out_specs.append(pl.BlockSpec((1, tm, width), lambda bi, i: (bi, i, 0)))
            out_shapes.append(jax.ShapeDtypeStruct((b, n, width), BF16))
        blk += _nbytes((tm, width), BF16) + _nbytes((tm, width), F32)
    blk += sum(_nbytes(s, dt) for s, dt in scratch) // 2
    return pl.pallas_call(
        functools.partial(_inproj_body, groups=tuple(groups)),
        grid=(b, n // tm),
        in_specs=in_specs,
        out_specs=out_specs,
        out_shape=out_shapes,
        scratch_shapes=[pltpu.VMEM(s, dt) for s, dt in scratch],
        compiler_params=_params(blk, 2),
        name=name,
    )(x, x, x, gain, mod3, mod3, cos, sina, sinb, gate_b, conv_w, conv_b, w_in)


def _sigmoid(y):
    return 0.5 + 0.5 * jnp.tanh(0.5 * y)


def _log_sigmoid(x):
    return jnp.minimum(x, 0.0) - jnp.log1p(jnp.exp(-jnp.abs(x)))


def _mlstm_body(qs, ks, v_ref, o_ref, kcs, vc_ref, gt_ref, gain_ref,
                y_ref, gtab, rtab, cumtab, kts, cnl, c0n, cumc, sqk, sds, inter_s, em_s, svq,
                *, n_ctx_chunks, n_lat_chunks):
    t = MCHUNK
    dh = MLSTM_DH
    n_chunks = n_ctx_chunks + n_lat_chunks
    nt = (((1,), (1,)), ((), ()))

    tbl = gt_ref[0, 0]
    rid = lax.broadcasted_iota(jnp.int32, tbl.shape, 0) // MAX_CHUNKS
    lane = lax.broadcasted_iota(jnp.int32, tbl.shape, 1)
    lf = _log_sigmoid(tbl)
    pre = jnp.where(rid == 1, lf, 0.0)
    suf = jnp.where(rid == 3, lf, 0.0)
    s = 1
    while s < t:
        pre = pre + jnp.where(lane >= s, pltpu.roll(pre, s, 1), 0.0)
        suf = suf + jnp.where(lane < t - s, pltpu.roll(suf, t - s, 1), 0.0)
        s *= 2
    li = (tbl[0:MAX_CHUNKS] * LOG2E, tbl[2 * MAX_CHUNKS:3 * MAX_CHUNKS] * LOG2E)
    cum = (pre[MAX_CHUNKS:2 * MAX_CHUNKS] * LOG2E, suf[3 * MAX_CHUNKS:4 * MAX_CHUNKS] * LOG2E)
    total = (cum[0][:, t - 1:t], cum[1][:, 0:1])
    mloc = []
    for d in (0, 1):
        g = total[d] - cum[d] + li[d]
        mloc.append(jnp.max(g, axis=1, keepdims=True))
        gtab[d] = g
        rtab[d] = cum[d] - li[d]
        cumtab[d] = cum[d]

    orders = (list(range(n_chunks)),
              list(range(n_ctx_chunks - 1, -1, -1)) + list(range(n_chunks - 1, n_ctx_chunks - 1, -1)))

    a_sc, m0, m1 = ({}, {}), ({}, {}), ({}, {})
    for d in (0, 1):
        m_st = jnp.zeros((1, 1), F32)
        for c in orders[d][:-1]:
            m0[d][c] = m_st
            tot_c = total[d][c:c + 1, :]
            m1[d][c] = jnp.maximum(tot_c + m_st, mloc[d][c:c + 1, :])
            a_sc[d][c] = jnp.exp2(tot_c + m_st - m1[d][c])
            m_st = m1[d][c]
        m0[d][orders[d][-1]] = m_st

    def kv_chunk(c):
        if c < n_ctx_chunks:
            rows = pl.ds(c * t, t)
            return kcs[rows, :], vc_ref[rows, :]
        rows = pl.ds((c - n_ctx_chunks) * t, t)
        return ks[rows, :], v_ref[rows, :]

    ri = lax.broadcasted_iota(jnp.int32, (t, t), 0)
    ci = lax.broadcasted_iota(jnp.int32, (t, t), 1)
    eye = jnp.where(ri == ci, 1.0, 0.0).astype(BF16)
    ones = jnp.ones((t, dh), BF16)

    for c in range(n_chunks):
        kts[c] = lax.dot_general(eye, kv_chunk(c)[0], nt, preferred_element_type=F32)
    for c in range(n_chunks):
        vo = jnp.concatenate([kv_chunk(c)[1], ones], axis=1)
        for d in (0, 1):
            if c != orders[d][-1]:
                w = jnp.exp2(gtab[d, pl.ds(c, 1), :] - m1[d][c])
                cnl[d, c] = jnp.dot((kts[c] * w).astype(BF16), vo, preferred_element_type=F32)

    for d in (0, 1):
        cn = jnp.zeros((dh, 2 * dh), F32)
        for c in orders[d]:
            if c >= n_ctx_chunks:
                c0n[d, c - n_ctx_chunks] = cn.astype(BF16)
            if c != orders[d][-1]:
                cn = a_sc[d][c] * cn + cnl[d, c]

    for j in range(n_lat_chunks):
        c = j + n_ctx_chunks
        rows = pl.ds(j * t, t)
        sqk[rows, :] = lax.dot_general(qs[rows, :], ks[rows, :], nt, preferred_element_type=F32)
        for d in (0, 1):
            cumc[d, rows, :] = jnp.broadcast_to(cumtab[d, pl.ds(c, 1), :], (t, t)).T

    grp = 2
    gi = lax.broadcasted_iota(jnp.int32, (grp * t, t), 0) & (t - 1)
    gs = lax.broadcasted_iota(jnp.int32, (grp * t, t), 1)
    for d in (0, 1):
        ok = (gs <= gi) if d == 0 else (gs >= gi)
        for j0 in range(0, n_lat_chunks, grp):
            rows = pl.ds(j0 * t, grp * t)
            cs = [j + n_ctx_chunks for j in range(j0, j0 + grp)]
            rb = jnp.concatenate([jnp.broadcast_to(rtab[d, pl.ds(c, 1), :], (t, t)) for c in cs], axis=0)
            m0b = jnp.concatenate([jnp.broadcast_to(m0[d][c], (t, t)) for c in cs], axis=0)
            cum_c = cumc[d, rows, :]
            d_log = jnp.where(ok, cum_c - rb, -jnp.inf)
            a_log = cum_c + m0b
            m_t = jnp.maximum(a_log, jnp.max(d_log, axis=1, keepdims=True))
            sds[d, rows, :] = sqk[rows, :].astype(BF16) * jnp.exp2((d_log - m_t).astype(BF16))
            inter_s[d, rows, :] = jnp.exp2(a_log - m_t)
            em_s[d, rows, :] = jnp.exp2(-m_t)

    for j in range(n_lat_chunks):
        rows = pl.ds(j * t, t)
        qf = qs[rows, :].astype(F32)
        vo = jnp.concatenate([v_ref[rows, :], ones], axis=1)
        for d in (0, 1):
            lhs = jnp.concatenate([sds[d, rows, :], (qf * inter_s[d, rows, :]).astype(BF16)], axis=1)
            rhs = jnp.concatenate([vo, c0n[d, j]], axis=0)
            svq[d, rows, :] = jnp.dot(lhs, rhs, preferred_element_type=F32)

    gain = gain_ref[...]
    grp = 4
    for j0 in range(0, n_lat_chunks, grp):
        rows = pl.ds(j0 * t, grp * t)
        hid = None
        for d in (0, 1):
            h_d = svq[d, rows, pl.ds(0, dh)] / jnp.maximum(jnp.abs(svq[d, rows, pl.ds(dh, dh)]), em_s[d, rows, :])
            hid = h_d if hid is None else hid + h_d
        hn = hid * lax.rsqrt(jnp.mean(hid * hid, axis=-1, keepdims=True) + EPS) * gain
        y_ref[rows, :] = (hn * _sigmoid(o_ref[rows, :].astype(F32))).astype(BF16)


def _mlstm(qk, vo, k_ctx, v_ctx, gate_tbl, gain):
    b, _, l, _ = qk.shape
    n_ctx = k_ctx.shape[2]
    dh = MLSTM_DH
    nh = MLSTM_HEADS
    n_ctx_chunks, n_lat_chunks = n_ctx // MCHUNK, l // MCHUNK
    seq = lambda slab0: pl.BlockSpec((None, None, l, dh), lambda bi, h: (bi, slab0 + h, 0, 0))
    ctx = lambda slab0: pl.BlockSpec((None, None, n_ctx, dh), lambda bi, h: (bi, slab0 + h, 0, 0))
    n_chunks = n_ctx_chunks + n_lat_chunks
    scratch = [((2, MAX_CHUNKS, MCHUNK), F32), ((2, MAX_CHUNKS, MCHUNK), F32),
               ((2, MAX_CHUNKS, MCHUNK), F32), ((n_chunks, dh, MCHUNK), F32),
               ((2, n_chunks, dh, 2 * dh), F32), ((2, n_lat_chunks, dh, 2 * dh), BF16),
               ((2, l, MCHUNK), F32), ((l, MCHUNK), F32), ((2, l, MCHUNK), BF16),
               ((2, l, MCHUNK), F32), ((2, l, MCHUNK), F32), ((2, l, 2 * dh), F32)]
    blk = (5 * _nbytes((l, dh), BF16) + 2 * _nbytes((n_ctx, dh), BF16) + _nbytes((V7X_LANES, V7X_LANES), F32)
           + sum(_nbytes(s, dt) for s, dt in scratch) // 2)
    return pl.pallas_call(
        functools.partial(_mlstm_body, n_ctx_chunks=n_ctx_chunks, n_lat_chunks=n_lat_chunks),
        grid=(b, nh),
        in_specs=[seq(0), seq(nh), seq(0), seq(nh), ctx(0), ctx(0),
                  pl.BlockSpec((1, 1, 4 * MAX_CHUNKS, MCHUNK), lambda bi, h: (bi, h, 0, 0)),
                  pl.BlockSpec((1, dh), lambda bi, h: (0, h))],
        out_specs=pl.BlockSpec((None, None, l, dh), lambda bi, h: (bi, h, 0, 0)),
        out_shape=jax.ShapeDtypeStruct((b, nh, l, dh), BF16),
        scratch_shapes=[pltpu.VMEM(s, dt) for s, dt in scratch],
        compiler_params=_params(blk, 2),
        name="mlstm",
    )(qk, qk, vo, vo, k_ctx, v_ctx, gate_tbl, gain)


def _attn_body(sink_ref, bias_ref, q_ref, kv_ref, kvc_ref, o_ref, s_scr, p_scr, e_scr, *, seq_len):
    blk = ATTN_BLOCK
    n_blocks = seq_len // blk
    n_ctx = kvc_ref.shape[1]
    rows = GQA_GROUP * blk
    nt = (((1,), (1,)), ((), ()))
    chains = [(qb, g) for qb in range(ATTN_BLOCKS_PER_STEP) for g in range(KV_HEADS)]

    def block_start(qb):
        i = pl.program_id(1) * ATTN_BLOCKS_PER_STEP + qb
        return i, pl.multiple_of(jnp.clip((i - 1) * blk, 0, seq_len - BAND), blk)

    for ci, (qb, g) in enumerate(chains):
        i, start = block_start(qb)
        k_cols = pl.ds(g * ATTN_DH, ATTN_DH)
        q = jnp.concatenate([q_ref[0, pl.ds(qb * blk, blk), pl.ds((g * GQA_GROUP + j) * ATTN_DH, ATTN_DH)]
                             for j in range(GQA_GROUP)], axis=0)
        bias = bias_ref[jnp.where(i == 0, 0, jnp.where(i == n_blocks - 1, 2, 1))]
        s_scr[ci, :, pl.ds(0, BAND)] = (
            lax.dot_general(q, kv_ref[0, pl.ds(start, BAND), k_cols], nt, preferred_element_type=F32) + bias)
        s_scr[ci, :, pl.ds(BAND, n_ctx)] = lax.dot_general(q, kvc_ref[0, :, k_cols], nt, preferred_element_type=F32)

    for ci, (qb, g) in enumerate(chains):
        s = s_scr[ci]
        sink = jnp.concatenate([jnp.full((blk, V7X_LANES), sink_ref[g * GQA_GROUP + j] * LOG2E, F32)
                                for j in range(GQA_GROUP)], axis=0)
        m = jnp.maximum(jnp.broadcast_to(jnp.max(s, axis=1, keepdims=True), (rows, V7X_LANES)), sink)
        p_scr[ci] = jnp.exp2((s - jnp.tile(m, (1, (BAND + n_ctx) // V7X_LANES))).astype(BF16))
        e_scr[ci] = jnp.exp2(sink - m)

    ones_b = jnp.ones((BAND, ATTN_DH), BF16)
    ones_c = jnp.ones((n_ctx, ATTN_DH), BF16)
    for ci, (qb, g) in enumerate(chains):
        _, start = block_start(qb)
        v_cols = pl.ds(KV_W + g * ATTN_DH, ATTN_DH)
        vb = jnp.concatenate([kv_ref[0, pl.ds(start, BAND), v_cols], ones_b], axis=1)
        vc = jnp.concatenate([kvc_ref[0, :, v_cols], ones_c], axis=1)
        acc = (jnp.dot(p_scr[ci, :, pl.ds(0, BAND)], vb, preferred_element_type=F32)
               + jnp.dot(p_scr[ci, :, pl.ds(BAND, n_ctx)], vc, preferred_element_type=F32))
        out = (acc / (pltpu.roll(acc, ATTN_DH, 1) + e_scr[ci]))[:, :ATTN_DH]
        for j in range(GQA_GROUP):
            h = g * GQA_GROUP + j
            o_ref[0, pl.ds(qb * blk, blk), pl.ds(h * ATTN_DH, ATTN_DH)] = out[j * blk:(j + 1) * blk].astype(BF16)


def _band_bias(seq_len):
    row = jnp.arange(ATTN_BLOCK)[:, None]
    col = jnp.arange(BAND)[None, :]
    n_blocks = seq_len // ATTN_BLOCK
    tables = []
    for i in (0, 1, n_blocks - 1):
        start = min(max((i - 1) * ATTN_BLOCK, 0), seq_len - BAND)
        ok = jnp.abs(start + col - (i * ATTN_BLOCK + row)) <= WINDOW
        tables.append(jnp.tile(jnp.where(ok, 0.0, -jnp.inf).astype(F32), (GQA_GROUP, 1)))
    return jnp.stack(tables)


def _attn(sink, q, kv, kvc):
    b, l, _ = q.shape
    n_ctx = kvc.shape[1]
    rows = ATTN_BLOCKS_PER_STEP * ATTN_BLOCK
    bias = _band_bias(l)
    n_chains = KV_HEADS * ATTN_BLOCKS_PER_STEP
    stacked = GQA_GROUP * ATTN_BLOCK
    scratch = [((n_chains, stacked, BAND + n_ctx), F32), ((n_chains, stacked, BAND + n_ctx), BF16),
               ((n_chains, stacked, V7X_LANES), F32)]
    blk = (2 * _nbytes((rows, ATTN_W), BF16) + _nbytes((l, 2 * KV_W), BF16) + _nbytes((n_ctx, 2 * KV_W), BF16)
           + _nbytes(bias.shape, F32) + sum(_nbytes(s, dt) for s, dt in scratch) // 2)
    return pl.pallas_call(
        functools.partial(_attn_body, seq_len=l),
        grid=(b, l // rows),
        in_specs=[pl.BlockSpec(memory_space=pltpu.SMEM),
                  pl.BlockSpec(bias.shape, lambda bi, i: (0, 0, 0)),
                  pl.BlockSpec((1, rows, ATTN_W), lambda bi, i: (bi, i, 0)),
                  pl.BlockSpec((1, l, 2 * KV_W), lambda bi, i: (bi, 0, 0)),
                  pl.BlockSpec((1, n_ctx, 2 * KV_W), lambda bi, i: (bi, 0, 0))],
        out_specs=pl.BlockSpec((1, rows, ATTN_W), lambda bi, i: (bi, i, 0)),
        out_shape=jax.ShapeDtypeStruct((b, l, ATTN_W), BF16),
        scratch_shapes=[pltpu.VMEM(s, dt) for s, dt in scratch],
        compiler_params=_params(blk, 2),
        name="attn",
    )(sink, bias, q, kv, kvc)


GELU_C1 = 0.7978845608028654
GELU_C2 = GELU_C1 * 0.044715


def _twice_gelu(x):
    return x + x * jnp.tanh(x * (GELU_C1 + GELU_C2 * (x * x)))


def _mixffn_body(ym_ref, ymp_ref, ymn_ref, ya_ref, yap_ref, yan_ref, x_ref, xp_ref, xn_ref,
                 wo_ref, ga_ref, gain_ref, sh_ref, sc_ref, gf_ref, wup_ref, cw_ref, cb_ref, wdn_ref, fn_ref,
                 o_ref, ys, hs, x1s, gated):
    i = pl.program_id(1)
    tm = x_ref.shape[1]
    ext = tm + 2 * HALO
    lo, mid, hi = pl.ds(0, HALO), pl.ds(HALO, tm), pl.ds(HALO + tm, HALO)
    for rows, m_ref, a_ref in ((lo, ymp_ref, yap_ref), (mid, ym_ref, ya_ref), (hi, ymn_ref, yan_ref)):
        for h in range(MLSTM_HEADS):
            ys[rows, pl.ds(h * MLSTM_DH, MLSTM_DH)] = m_ref[0, h]
        ys[rows, pl.ds(MLSTM_W, ATTN_W)] = a_ref[0]
    proj = jnp.dot(ys[...], wo_ref[...], preferred_element_type=F32)
    ga = ga_ref[0]

    def mixed(x_rows, proj_rows):
        x1 = x_rows + ga * proj_rows
        return x1, _norm_modulate(x1, gain_ref[...], sh_ref[0], sc_ref[0]).astype(BF16)

    x1, h_mid = mixed(x_ref[0], proj[HALO:HALO + tm])
    x1s[...] = x1
    hs[mid, :] = h_mid
    h_lo = mixed(xp_ref[0], proj[:HALO])[1]
    h_hi = mixed(xn_ref[0], proj[HALO + tm:])[1]
    hs[lo, :] = jnp.where(i > 0, h_lo, jnp.zeros_like(h_lo))
    hs[hi, :] = jnp.where(i < pl.num_programs(1) - 1, h_hi, jnp.zeros_like(h_hi))
    starts = list(range(0, D_FF, FF_CHUNK))

    def up_a(col0):
        return jnp.dot(hs[...], wup_ref[:, pl.ds(col0, min(FF_CHUNK, D_FF - col0))], preferred_element_type=F32)

    nxt = up_a(starts[0])
    for c, col0 in enumerate(starts):
        width = min(FF_CHUNK, D_FF - col0)
        cols = pl.ds(col0, width)
        a = nxt
        if c + 1 < len(starts):
            nxt = up_a(starts[c + 1])
        a_prev = pltpu.roll(a, 1, 0)[HALO:HALO + tm]
        a_next = pltpu.roll(a, ext - 1, 0)[HALO:HALO + tm]
        conv = (cw_ref[0:1, cols] * a_prev + cw_ref[1:2, cols] * a[HALO:HALO + tm]
                + cw_ref[2:3, cols] * a_next + cb_ref[:, cols])
        g2 = _twice_gelu(conv)
        val = jnp.dot(hs[mid, :], wup_ref[:, pl.ds(D_FF + col0, width)], preferred_element_type=F32)
        gated[:, cols] = (g2 * val).astype(BF16)
    x2 = x1s[...] + (0.5 * gf_ref[0]) * jnp.dot(gated[...], wdn_ref[...], preferred_element_type=F32)
    ms = jnp.mean(x2 * x2, axis=-1, keepdims=True)
    o_ref[0] = x2 * lax.rsqrt(ms + EPS) * fn_ref[...]


def _mixffn(ym, ya, x, mod3, w_out, norm_gain, w_up, conv_w, conv_b, w_down, final_norm):
    b, l, d = x.shape
    tm = ROW_TILE
    per = tm // HALO
    last = l // HALO - 1
    ext = tm + 2 * HALO
    tile, before, after = (lambda bi, i: i), (lambda bi, i: jnp.maximum(i * per - 1, 0)), \
        (lambda bi, i: jnp.minimum((i + 1) * per, last))

    def rows3(make):
        return [make(tm, tile), make(HALO, before), make(HALO, after)]

    ym_spec = lambda n, r: pl.BlockSpec((1, MLSTM_HEADS, n, MLSTM_DH), lambda bi, i: (bi, 0, r(bi, i), 0))
    ya_spec = lambda n, r: pl.BlockSpec((1, n, ATTN_W), lambda bi, i: (bi, r(bi, i), 0))
    x_spec = lambda n, r: pl.BlockSpec((1, n, d), lambda bi, i: (bi, r(bi, i), 0))
    mod_spec = lambda piece: pl.BlockSpec((1, 1, d), lambda bi, i: (bi, 0, piece))
    const = lambda arr: pl.BlockSpec(arr.shape, lambda bi, i: (0,) * arr.ndim)
    resident = lambda arr: pl.BlockSpec(arr.shape, lambda bi, i: (0,) * arr.ndim, pipeline_mode=pl.Buffered(1))
    scratch = [((ext, d), BF16), ((ext, d), BF16), ((tm, d), F32), ((tm, D_FF), BF16)]
    weights = _nbytes(w_out.shape, BF16) + _nbytes(w_up.shape, BF16) + _nbytes(w_down.shape, BF16)
    blk = (2 * _nbytes((tm, d), F32) + 2 * _nbytes((tm, d), BF16) + 4 * _nbytes((ext, FF_CHUNK), F32)
           + (weights + sum(_nbytes(s, dt) for s, dt in scratch)) // 2)
    return pl.pallas_call(
        _mixffn_body,
        grid=(b, l // tm),
        in_specs=(rows3(ym_spec) + rows3(ya_spec) + rows3(x_spec)
                  + [resident(w_out), mod_spec(2), const(norm_gain), mod_spec(3), mod_spec(4), mod_spec(5),
                     resident(w_up), const(conv_w), const(conv_b), resident(w_down), const(final_norm)]),
        out_specs=pl.BlockSpec((1, tm, d), lambda bi, i: (bi, i, 0)),
        out_shape=jax.ShapeDtypeStruct((b, l, d), F32),
        scratch_shapes=[pltpu.VMEM(s, dt) for s, dt in scratch],
        compiler_params=_params(blk, 2),
        name="mixffn",
    )(ym, ym, ym, ya, ya, ya, x, x, x, w_out, mod3, norm_gain, mod3, mod3, mod3,
      w_up, conv_w, conv_b, w_down, final_norm)


def _rope_tables(n_tokens):
    pos = jnp.arange(n_tokens)
    r = (pos // GRID_W).astype(F32)
    c = (pos % GRID_W).astype(F32)
    inv = ROPE_BASE ** (-jnp.arange(ROPE_AXIS_PAIRS, dtype=F32) / ROPE_AXIS_PAIRS)
    ar, ac = r[:, None] * inv, c[:, None] * inv
    zero = jnp.zeros_like(ar)
    cos = jnp.concatenate([jnp.cos(ar), jnp.cos(ar), jnp.cos(ac), jnp.cos(ac)], axis=1)
    sina = jnp.concatenate([-jnp.sin(ar), zero, -jnp.sin(ac), zero], axis=1)
    sinb = jnp.concatenate([zero, jnp.sin(ar), zero, jnp.sin(ac)], axis=1)
    rep = V7X_LANES // ATTN_DH
    return tuple(jnp.tile(tb, (1, rep)) for tb in (cos, sina, sinb))


def _identity_rope(n_tokens):
    return (jnp.ones((n_tokens, V7X_LANES), F32), jnp.zeros((n_tokens, V7X_LANES), F32),
            jnp.zeros((n_tokens, V7X_LANES), F32))


def _gate_table(gates_ctx, gates_x):
    g = jnp.concatenate([gates_ctx, gates_x], axis=2)
    b, _, n = g.shape
    nc = n // MCHUNK
    g = g.reshape(b, 4, MLSTM_HEADS, nc, MCHUNK).transpose(0, 2, 1, 3, 4)
    g = jnp.pad(g, ((0, 0), (0, 0), (0, 0), (0, MAX_CHUNKS - nc), (0, 0)))
    return g.reshape(b, MLSTM_HEADS, 4 * MAX_CHUNKS, MCHUNK)


def kernel(x, c, ctx, c_ctx, w_ada, b_ada, norm_mix, norm_ffn, w_in, gate_b, qk_conv_w, qk_conv_b, mlstm_norm,
           attn_sink, w_out, w_up, ffn_conv_w, ffn_conv_b, w_down, final_norm):
    b, l, d = x.shape
    n_ctx = ctx.shape[1]
    assert w_ada.shape[0] == 1, "single-layer stack"
    assert l % ROW_TILE == 0 and l % MCHUNK == 0 and n_ctx % MCHUNK == 0 and l >= BAND
    assert l % (ATTN_BLOCKS_PER_STEP * ATTN_BLOCK) == 0 and MCHUNK == V7X_LANES
    assert (l + n_ctx) // MCHUNK <= MAX_CHUNKS

    rows = -(-(b + 1) // V7X_SUBLANES) * V7X_SUBLANES
    cvec = jnp.zeros((rows, d), F32).at[:b].set(c).at[b].set(c_ctx)
    mod = _ada(cvec, w_ada[0], b_ada[0][None, :])
    mod_x = mod[:b].reshape(b, 1, 6 * d)
    mod_c = mod[b:b + 1].reshape(1, 1, 6 * d)

    w_in = jnp.swapaxes(w_in, 1, 2)
    o0, o1, o2 = 2 * MLSTM_W, 3 * MLSTM_W, 4 * MLSTM_W
    o3 = o2 + N_GATES
    o4 = o3 + ATTN_W
    gb = gate_b[0][:, None]
    gain_mix = norm_mix[0][None, :]
    conv = (qk_conv_w[0], qk_conv_b[0][None, :])
    k_mc, v_mc, gates_c, kv_ac = _inproj(
        ctx, gain_mix, mod_c, 0, 1, False, _identity_rope(n_ctx), gb, *conv, w_in,
        [("conv_k", MLSTM_W, MLSTM_W), ("heads", o0, MLSTM_W), ("gates", o2, N_GATES), ("plain", o4, 2 * KV_W)],
        "inproj_ctx")
    qk_m, vo_m, gates_x, q_a, kv_a = _inproj(
        x, gain_mix, mod_x, 0, 1, True, _rope_tables(l), gb, *conv, w_in,
        [("conv_qk", 0, o0), ("heads", o0, o0), ("gates", o2, N_GATES), ("rope_q", o3, ATTN_W),
         ("rope_kv", o4, 2 * KV_W)], "inproj_x")

    y_m = _mlstm(qk_m, vo_m, k_mc, v_mc, _gate_table(gates_c, gates_x), mlstm_norm[0][None, :])
    y_a = _attn(attn_sink[0], q_a, kv_a, kv_ac)

    return _mixffn(y_m, y_a, x, mod_x, w_out[0].astype(BF16), norm_ffn[0][None, :], w_up[0].astype(BF16),
                   ffn_conv_w[0], ffn_conv_b[0][None, :], w_down[0].astype(BF16), final_norm[None, :])
```

```python
import functools

import jax
import jax.numpy as jnp
from jax import lax
from jax.experimental import pallas as pl
from jax.experimental.pallas import tpu as pltpu

F32 = jnp.float32
BF16 = jnp.bfloat16

D_MODEL = 1024
GRID_W = 64
MLSTM_HEADS = 4
MLSTM_DH = 128
MLSTM_W = MLSTM_HEADS * MLSTM_DH
N_GATES = 4 * MLSTM_HEADS
ATTN_HEADS = 8
KV_HEADS = 2
ATTN_DH = 64
ATTN_W = ATTN_HEADS * ATTN_DH
KV_W = KV_HEADS * ATTN_DH
GQA_GROUP = ATTN_HEADS // KV_HEADS
WINDOW = 128
ROPE_BASE = 10000.0
ROPE_AXIS_PAIRS = ATTN_DH // 4
D_FF = 2816
EPS = 1e-6
LOG2E = 1.4426950408889634

V7X_LANES = 128
V7X_SUBLANES = 8
V7X_VMEM_BYTES = 64 * 1024 * 1024

ROW_TILE = 512
ADA_COL_TILE = 1536
PROJ_CHUNK = 512
WEIGHT_STAGE = 256
MCHUNK = 128
MAX_CHUNKS = 32
ATTN_BLOCK = 128
ATTN_BLOCKS_PER_STEP = 4
BAND = 3 * ATTN_BLOCK
FF_CHUNK = 512
HALO = 2 * V7X_SUBLANES


def _vmem_limit(block_bytes):
    return int(min(V7X_VMEM_BYTES * 7 // 8, 2 * block_bytes + 16 * 1024 * 1024))


def _params(block_bytes, n_axes):
    return pltpu.CompilerParams(dimension_semantics=("arbitrary",) * n_axes,
                                vmem_limit_bytes=_vmem_limit(block_bytes))


def _nbytes(shape, dtype):
    n = 1
    for s in shape:
        n *= s
    return n * jnp.dtype(dtype).itemsize


def _ada_body(c_ref, w_ref, b_ref, o_ref):
    c = c_ref[...]
    s = (c * jax.nn.sigmoid(c)).astype(BF16)
    o_ref[...] = jnp.dot(s, w_ref[...].astype(BF16), preferred_element_type=F32) + b_ref[...]


def _ada(cvec, w_ada, b_ada):
    rows, d = cvec.shape
    n = w_ada.shape[1]
    blk = _nbytes((d, ADA_COL_TILE), F32) + _nbytes((rows, d), F32) + 2 * _nbytes((rows, ADA_COL_TILE), F32)
    return pl.pallas_call(
        _ada_body,
        grid=(n // ADA_COL_TILE,),
        in_specs=[pl.BlockSpec((rows, d), lambda j: (0, 0)),
                  pl.BlockSpec((d, ADA_COL_TILE), lambda j: (0, j)),
                  pl.BlockSpec((1, ADA_COL_TILE), lambda j: (0, j))],
        out_specs=pl.BlockSpec((rows, ADA_COL_TILE), lambda j: (0, j)),
        out_shape=jax.ShapeDtypeStruct((rows, n), F32),
        compiler_params=_params(blk, 1),
        name="ada",
    )(cvec, w_ada, b_ada)


def _norm_modulate(x, gain, shift, scale):
    ms = jnp.mean(x * x, axis=-1, keepdims=True)
    return (x * lax.rsqrt(ms + EPS) * gain) * (1.0 + scale) + shift


def _silu(y):
    h = 0.5 * y
    return h + h * jnp.tanh(h)


def _inproj_body(x_ref, xp_ref, xn_ref, gain_ref, sh_ref, sc_ref, cos_ref, sina_ref, sinb_ref, gb_ref,
                 cw_ref, cb_ref, win_ref, *refs, groups):
    n = len(groups)
    o_refs, hs, w_refs = refs[:n], refs[n], refs[n + 1:]
    kinds = [g[0] for g in groups]

    @pl.when((pl.program_id(0) == 0) & (pl.program_id(1) == 0))
    def _():
        for (_, col0, width), w_bf in zip(groups, w_refs):
            w_bf[...] = win_ref[pl.ds(col0, width), :].astype(BF16)

    proj = lambda lhs, w_bf: lax.dot_general(lhs, w_bf[...], (((1,), (1,)), ((), ())), preferred_element_type=F32)

    i = pl.program_id(1)
    tm = x_ref.shape[1]
    ext = tm + 2 * HALO
    norm = lambda rows: _norm_modulate(rows, gain_ref[...], sh_ref[0], sc_ref[0]).astype(BF16)
    hb = norm(x_ref[0])
    h_lo, h_hi = norm(xp_ref[0]), norm(xn_ref[0])
    hs[pl.ds(0, HALO), :] = jnp.where(i > 0, h_lo, jnp.zeros_like(h_lo))
    hs[pl.ds(HALO, tm), :] = hb
    hs[pl.ds(HALO + tm, HALO), :] = jnp.where(i < pl.num_programs(1) - 1, h_hi, jnp.zeros_like(h_hi))
    def conv_item(w_ref, o_ref, s0, c0, first_k):
        def epilogue(a):
            cc = pl.ds(c0 + s0, PROJ_CHUNK)
            a_prev = pltpu.roll(a, 1, 0)[HALO:HALO + tm]
            a_next = pltpu.roll(a, ext - 1, 0)[HALO:HALO + tm]
            y = _silu(cw_ref[0:1, cc] * a_prev + cw_ref[1:2, cc] * a[HALO:HALO + tm]
                      + cw_ref[2:3, cc] * a_next + cb_ref[:, cc])
            for j in range(PROJ_CHUNK // V7X_LANES):
                slab = s0 // V7X_LANES + j
                u = y[:, j * V7X_LANES:(j + 1) * V7X_LANES]
                o_ref[0, slab] = (u * MLSTM_DH ** -0.5 if slab >= first_k else u).astype(BF16)
        return (lambda: proj(hs[...], w_ref.at[pl.ds(s0, PROJ_CHUNK)])), epilogue

    def heads_item(w_ref, o_ref, s0):
        def epilogue(u):
            for j in range(PROJ_CHUNK // V7X_LANES):
                o_ref[0, s0 // V7X_LANES + j] = u[:, j * V7X_LANES:(j + 1) * V7X_LANES].astype(BF16)
        return (lambda: proj(hb, w_ref.at[pl.ds(s0, PROJ_CHUNK)])), epilogue

    def gates_item(w_ref, o_ref):
        def epilogue(u):
            o_ref[0] = u + gb_ref[...]
        return (lambda: lax.dot_general(w_ref[...], hb, (((1,), (1,)), ((), ())), preferred_element_type=F32)), epilogue

    def rope_item(w_ref, o_ref, kind):
        def epilogue(u_all):
            n_groups = u_all.shape[1] // V7X_LANES
            n_rot = {"rope_q": n_groups, "rope_kv": KV_W // V7X_LANES, "plain": 0}[kind]
            for j in range(n_groups):
                u = u_all[:, j * V7X_LANES:(j + 1) * V7X_LANES]
                if j < n_rot:
                    u = (u * cos_ref[...]
                         + pltpu.roll(u, V7X_LANES - ROPE_AXIS_PAIRS, 1) * sina_ref[...]
                         + pltpu.roll(u, ROPE_AXIS_PAIRS, 1) * sinb_ref[...])
                    if kind == "rope_q":
                        u = u * (ATTN_DH ** -0.5 * LOG2E)
                o_ref[0, :, pl.ds(j * V7X_LANES, V7X_LANES)] = u.astype(BF16)
        return (lambda: proj(hb, w_ref)), epilogue

    items = []
    for kind, w_ref, o_ref in zip(kinds, w_refs, o_refs):
        if kind in ("conv_qk", "conv_k"):
            width = w_ref.shape[0]
            c0 = cw_ref.shape[1] - width
            items += [conv_item(w_ref, o_ref, s0, c0, width // V7X_LANES - MLSTM_HEADS)
                      for s0 in range(0, width, PROJ_CHUNK)]
        elif kind == "heads":
            items += [heads_item(w_ref, o_ref, s0) for s0 in range(0, w_ref.shape[0], PROJ_CHUNK)]
        elif kind == "gates":
            items.append(gates_item(w_ref, o_ref))
        else:
            items.append(rope_item(w_ref, o_ref, kind))
    for matmul, epilogue in items:
        epilogue(matmul())


def _inproj(x, gain, mod3, shift_idx, scale_idx, per_batch_mod, rope, gate_b, conv_w, conv_b, w_in, groups, name):
    b, n, d = x.shape
    tm = min(ROW_TILE, n)
    per = tm // HALO
    last = n // HALO - 1
    cos, sina, sinb = rope
    mod_spec = lambda piece: pl.BlockSpec(
        (1, 1, d), (lambda bi, i: (bi, 0, piece)) if per_batch_mod else (lambda bi, i: (0, 0, piece)))
    in_specs = [pl.BlockSpec((1, tm, d), lambda bi, i: (bi, i, 0)),
                pl.BlockSpec((1, HALO, d), lambda bi, i: (bi, jnp.maximum(i * per - 1, 0), 0)),
                pl.BlockSpec((1, HALO, d), lambda bi, i: (bi, jnp.minimum((i + 1) * per, last), 0)),
                pl.BlockSpec((1, d), lambda bi, i: (0, 0)),
                mod_spec(shift_idx), mod_spec(scale_idx),
                pl.BlockSpec((tm, V7X_LANES), lambda bi, i: (i, 0)),
                pl.BlockSpec((tm, V7X_LANES), lambda bi, i: (i, 0)),
                pl.BlockSpec((tm, V7X_LANES), lambda bi, i: (i, 0)),
                pl.BlockSpec((N_GATES, 1), lambda bi, i: (0, 0)),
                pl.BlockSpec(conv_w.shape, lambda bi, i: (0, 0)),
                pl.BlockSpec(conv_b.shape, lambda bi, i: (0, 0)),
                pl.BlockSpec((None,) + w_in.shape[1:], lambda bi, i: (0, 0, 0), pipeline_mode=pl.Buffered(1))]
    out_specs, out_shapes = [], []
    scratch = [((tm + 2 * HALO, d), BF16)]
    blk = (_nbytes((tm, d), F32) + 3 * _nbytes((tm, V7X_LANES), F32) + 4 * _nbytes((tm + 2 * HALO, 2 * MLSTM_W), F32)
           + _nbytes(w_in.shape, F32) // 2)
    for kind, _, width in groups:
        if kind == "gates":
            scratch.append(((width, d), BF16))
            out_specs.append(pl.BlockSpec((1, N_GATES, tm), lambda bi, i: (bi, 0, i)))
            out_shapes.append(jax.ShapeDtypeStruct((b, N_GATES, n), F32))
            continue
        scratch.append(((width, d), BF16))
        if kind in ("heads", "conv_qk", "conv_k"):
            slabs = width // V7X_LANES
            out_specs.append(pl.BlockSpec((1, slabs, tm, V7X_LANES), lambda bi, i: (bi, 0, i, 0)))
            out_shapes.append(jax.ShapeDtypeStruct((b, slabs, n, V7X_LANES), BF16))
        else:
            out_specs.append(pl.BlockSpec((1, tm, width), lambda bi, i: (bi, i, 0)))
            out_shapes.append(jax.ShapeDtypeStruct((b, n, width), BF16))
        blk += _nbytes((tm, width), BF16) + _nbytes((tm, width), F32)
    blk += sum(_nbytes(s, dt) for s, dt in scratch) // 2
    return pl.pallas_call(
        functools.partial(_inproj_body, groups=tuple(groups)),
        grid=(b, n // tm),
        in_specs=in_specs,
        out_specs=out_specs,
        out_shape=out_shapes,
        scratch_shapes=[pltpu.VMEM(s, dt) for s, dt in scratch],
        compiler_params=_params(blk, 2),
        name=name,
    )(x, x, x, gain, mod3, mod3, cos, sina, sinb, gate_b, conv_w, conv_b, w_in)


def _sigmoid(y):
    return 0.5 + 0.5 * jnp.tanh(0.5 * y)


def _log_sigmoid(x):
    return jnp.minimum(x, 0.0) - jnp.log1p(jnp.exp(-jnp.abs(x)))


def _mlstm_body(qs, ks, v_ref, o_ref, kcs, vc_ref, gt_ref, gain_ref,
                y_ref, gtab, rtab, cumtab, kts, cnl, c0n, cumc, sqk, sds, inter_s, em_s, svq,
                *, n_ctx_chunks, n_lat_chunks):
    t = MCHUNK
    dh = MLSTM_DH
    n_chunks = n_ctx_chunks + n_lat_chunks
    nt = (((1,), (1,)), ((), ()))

    tbl = gt_ref[0, 0]
    rid = lax.broadcasted_iota(jnp.int32, tbl.shape, 0) // MAX_CHUNKS
    lane = lax.broadcasted_iota(jnp.int32, tbl.shape, 1)
    lf = _log_sigmoid(tbl)
    pre = jnp.where(rid == 1, lf, 0.0)
    suf = jnp.where(rid == 3, lf, 0.0)
    s = 1
    while s < t:
        pre = pre + jnp.where(lane >= s, pltpu.roll(pre, s, 1), 0.0)
        suf = suf + jnp.where(lane < t - s, pltpu.roll(suf, t - s, 1), 0.0)
        s *= 2
    li = (tbl[0:MAX_CHUNKS] * LOG2E, tbl[2 * MAX_CHUNKS:3 * MAX_CHUNKS] * LOG2E)
    cum = (pre[MAX_CHUNKS:2 * MAX_CHUNKS] * LOG2E, suf[3 * MAX_CHUNKS:4 * MAX_CHUNKS] * LOG2E)
    total = (cum[0][:, t - 1:t], cum[1][:, 0:1])
    mloc = []
    for d in (0, 1):
        g = total[d] - cum[d] + li[d]
        mloc.append(jnp.max(g, axis=1, keepdims=True))
        gtab[d] = g
        rtab[d] = cum[d] - li[d]
        cumtab[d] = cum[d]

    orders = (list(range(n_chunks)),
              list(range(n_ctx_chunks - 1, -1, -1)) + list(range(n_chunks - 1, n_ctx_chunks - 1, -1)))

    a_sc, m0, m1 = ({}, {}), ({}, {}), ({}, {})
    for d in (0, 1):
        m_st = jnp.zeros((1, 1), F32)
        for c in orders[d][:-1]:
            m0[d][c] = m_st
            tot_c = total[d][c:c + 1, :]
            m1[d][c] = jnp.maximum(tot_c + m_st, mloc[d][c:c + 1, :])
            a_sc[d][c] = jnp.exp2(tot_c + m_st - m1[d][c])
            m_st = m1[d][c]
        m0[d][orders[d][-1]] = m_st

    def kv_chunk(c):
        if c < n_ctx_chunks:
            rows = pl.ds(c * t, t)
            return kcs[rows, :], vc_ref[rows, :]
        rows = pl.ds((c - n_ctx_chunks) * t, t)
        return ks[rows, :], v_ref[rows, :]

    ri = lax.broadcasted_iota(jnp.int32, (t, t), 0)
    ci = lax.broadcasted_iota(jnp.int32, (t, t), 1)
    eye = jnp.where(ri == ci, 1.0, 0.0).astype(BF16)
    ones = jnp.ones((t, dh), BF16)

    for c in range(n_chunks):
        kts[c] = lax.dot_general(eye, kv_chunk(c)[0], nt, preferred_element_type=F32)
    for c in range(n_chunks):
        vo = jnp.concatenate([kv_chunk(c)[1], ones], axis=1)
        for d in (0, 1):
            if c != orders[d][-1]:
                w = jnp.exp2(gtab[d, pl.ds(c, 1), :] - m1[d][c])
                cnl[d, c] = jnp.dot((kts[c] * w).astype(BF16), vo, preferred_element_type=F32)

    for d in (0, 1):
        cn = jnp.zeros((dh, 2 * dh), F32)
        for c in orders[d]:
            if c >= n_ctx_chunks:
                c0n[d, c - n_ctx_chunks] = cn.astype(BF16)
            if c != orders[d][-1]:
                cn = a_sc[d][c] * cn + cnl[d, c]

    for j in range(n_lat_chunks):
        c = j + n_ctx_chunks
        rows = pl.ds(j * t, t)
        sqk[rows, :] = lax.dot_general(qs[rows, :], ks[rows, :], nt, preferred_element_type=F32)
        for d in (0, 1):
            cumc[d, rows, :] = jnp.broadcast_to(cumtab[d, pl.ds(c, 1), :], (t, t)).T

    grp = 2
    gi = lax.broadcasted_iota(jnp.int32, (grp * t, t), 0) & (t - 1)
    gs = lax.broadcasted_iota(jnp.int32, (grp * t, t), 1)
    for d in (0, 1):
        ok = (gs <= gi) if d == 0 else (gs >= gi)
        for j0 in range(0, n_lat_chunks, grp):
            rows = pl.ds(j0 * t, grp * t)
            cs = [j + n_ctx_chunks for j in range(j0, j0 + grp)]
            rb = jnp.concatenate([jnp.broadcast_to(rtab[d, pl.ds(c, 1), :], (t, t)) for c in cs], axis=0)
            m0b = jnp.concatenate([jnp.broadcast_to(m0[d][c], (t, t)) for c in cs], axis=0)
            cum_c = cumc[d, rows, :]
            d_log = jnp.where(ok, cum_c - rb, -jnp.inf)
            a_log = cum_c + m0b
            m_t = jnp.maximum(a_log, jnp.max(d_log, axis=1, keepdims=True))
            sds[d, rows, :] = sqk[rows, :].astype(BF16) * jnp.exp2((d_log - m_t).astype(BF16))
            inter_s[d, rows, :] = jnp.exp2(a_log - m_t)
            em_s[d, rows, :] = jnp.exp2(-m_t)

    for j in range(n_lat_chunks):
        rows = pl.ds(j * t, t)
        qf = qs[rows, :].astype(F32)
        vo = jnp.concatenate([v_ref[rows, :], ones], axis=1)
        for d in (0, 1):
            lhs = jnp.concatenate([sds[d, rows, :], (qf * inter_s[d, rows, :]).astype(BF16)], axis=1)
            rhs = jnp.concatenate([vo, c0n[d, j]], axis=0)
            svq[d, rows, :] = jnp.dot(lhs, rhs, preferred_element_type=F32)

    gain = gain_ref[...]
    grp = 4
    for j0 in range(0, n_lat_chunks, grp):
        rows = pl.ds(j0 * t, grp * t)
        hid = None
        for d in (0, 1):
            h_d = svq[d, rows, pl.ds(0, dh)] / jnp.maximum(jnp.abs(svq[d, rows, pl.ds(dh, dh)]), em_s[d, rows, :])
            hid = h_d if hid is None else hid + h_d
        hn = hid * lax.rsqrt(jnp.mean(hid * hid, axis=-1, keepdims=True) + EPS) * gain
        y_ref[rows, :] = (hn * _sigmoid(o_ref[rows, :].astype(F32))).astype(BF16)


def _mlstm(qk, vo, k_ctx, v_ctx, gate_tbl, gain):
    b, _, l, _ = qk.shape
    n_ctx = k_ctx.shape[2]
    dh = MLSTM_DH
    nh = MLSTM_HEADS
    n_ctx_chunks, n_lat_chunks = n_ctx // MCHUNK, l // MCHUNK
    seq = lambda slab0: pl.BlockSpec((None, None, l, dh), lambda bi, h: (bi, slab0 + h, 0, 0))
    ctx = lambda slab0: pl.BlockSpec((None, None, n_ctx, dh), lambda bi, h: (bi, slab0 + h, 0, 0))
    n_chunks = n_ctx_chunks + n_lat_chunks
    scratch = [((2, MAX_CHUNKS, MCHUNK), F32), ((2, MAX_CHUNKS, MCHUNK), F32),
               ((2, MAX_CHUNKS, MCHUNK), F32), ((n_chunks, dh, MCHUNK), F32),
               ((2, n_chunks, dh, 2 * dh), F32), ((2, n_lat_chunks, dh, 2 * dh), BF16),
               ((2, l, MCHUNK), F32), ((l, MCHUNK), F32), ((2, l, MCHUNK), BF16),
               ((2, l, MCHUNK), F32), ((2, l, MCHUNK), F32), ((2, l, 2 * dh), F32)]
    blk = (5 * _nbytes((l, dh), BF16) + 2 * _nbytes((n_ctx, dh), BF16) + _nbytes((V7X_LANES, V7X_LANES), F32)
           + sum(_nbytes(s, dt) for s, dt in scratch) // 2)
    return pl.pallas_call(
        functools.partial(_mlstm_body, n_ctx_chunks=n_ctx_chunks, n_lat_chunks=n_lat_chunks),
        grid=(b, nh),
        in_specs=[seq(0), seq(nh), seq(0), seq(nh), ctx(0), ctx(0),
                  pl.BlockSpec((1, 1, 4 * MAX_CHUNKS, MCHUNK), lambda bi, h: (bi, h, 0, 0)),
                  pl.BlockSpec((1, dh), lambda bi, h: (0, h))],
        out_specs=pl.BlockSpec((None, None, l, dh), lambda bi, h: (bi, h, 0, 0)),
        out_shape=jax.ShapeDtypeStruct((b, nh, l, dh), BF16),
        scratch_shapes=[pltpu.VMEM(s, dt) for s, dt in scratch],
        compiler_params=_params(blk, 2),
        name="mlstm",
    )(qk, qk, vo, vo, k_ctx, v_ctx, gate_tbl, gain)


def _attn_body(sink_ref, bias_ref, q_ref, kv_ref, kvc_ref, o_ref, s_scr, p_scr, e_scr, *, seq_len):
    blk = ATTN_BLOCK
    n_blocks = seq_len // blk
    n_ctx = kvc_ref.shape[1]
    rows = GQA_GROUP * blk
    nt = (((1,), (1,)), ((), ()))
    chains = [(qb, g) for qb in range(ATTN_BLOCKS_PER_STEP) for g in range(KV_HEADS)]

    def block_start(qb):
        i = pl.program_id(1) * ATTN_BLOCKS_PER_STEP + qb
        return i, pl.multiple_of(jnp.clip((i - 1) * blk, 0, seq_len - BAND), blk)

    for ci, (qb, g) in enumerate(chains):
        i, start = block_start(qb)
        k_cols = pl.ds(g * ATTN_DH, ATTN_DH)
        q = jnp.concatenate([q_ref[0, pl.ds(qb * blk, blk), pl.ds((g * GQA_GROUP + j) * ATTN_DH, ATTN_DH)]
                             for j in range(GQA_GROUP)], axis=0)
        bias = bias_ref[jnp.where(i == 0, 0, jnp.where(i == n_blocks - 1, 2, 1))]
        s_scr[ci, :, pl.ds(0, BAND)] = (
            lax.dot_general(q, kv_ref[0, pl.ds(start, BAND), k_cols], nt, preferred_element_type=F32) + bias)
        s_scr[ci, :, pl.ds(BAND, n_ctx)] = lax.dot_general(q, kvc_ref[0, :, k_cols], nt, preferred_element_type=F32)

    for ci, (qb, g) in enumerate(chains):
        s = s_scr[ci]
        sink = jnp.concatenate([jnp.full((blk, V7X_LANES), sink_ref[g * GQA_GROUP + j] * LOG2E, F32)
                                for j in range(GQA_GROUP)], axis=0)
        m = jnp.maximum(jnp.broadcast_to(jnp.max(s, axis=1, keepdims=True), (rows, V7X_LANES)), sink)
        p_scr[ci] = jnp.exp2((s - jnp.tile(m, (1, (BAND + n_ctx) // V7X_LANES))).astype(BF16))
        e_scr[ci] = jnp.exp2(sink - m)

    ones_b = jnp.ones((BAND, ATTN_DH), BF16)
    ones_c = jnp.ones((n_ctx, ATTN_DH), BF16)
    for ci, (qb, g) in enumerate(chains):
        _, start = block_start(qb)
        v_cols = pl.ds(KV_W + g * ATTN_DH, ATTN_DH)
        vb = jnp.concatenate([kv_ref[0, pl.ds(start, BAND), v_cols], ones_b], axis=1)
        vc = jnp.concatenate([kvc_ref[0, :, v_cols], ones_c], axis=1)
        acc = (jnp.dot(p_scr[ci, :, pl.ds(0, BAND)], vb, preferred_element_type=F32)
               + jnp.dot(p_scr[ci, :, pl.ds(BAND, n_ctx)], vc, preferred_element_type=F32))
        out = (acc / (pltpu.roll(acc, ATTN_DH, 1) + e_scr[ci]))[:, :ATTN_DH]
        for j in range(GQA_GROUP):
            h = g * GQA_GROUP + j
            o_ref[0, pl.ds(qb * blk, blk), pl.ds(h * ATTN_DH, ATTN_DH)] = out[j * blk:(j + 1) * blk].astype(BF16)


def _band_bias(seq_len):
    row = jnp.arange(ATTN_BLOCK)[:, None]
    col = jnp.arange(BAND)[None, :]
    n_blocks = seq_len // ATTN_BLOCK
    tables = []
    for i in (0, 1, n_blocks - 1):
        start = min(max((i - 1) * ATTN_BLOCK, 0), seq_len - BAND)
        ok = jnp.abs(start + col - (i * ATTN_BLOCK + row)) <= WINDOW
        tables.append(jnp.tile(jnp.where(ok, 0.0, -jnp.inf).astype(F32), (GQA_GROUP, 1)))
    return jnp.stack(tables)


def _attn(sink, q, kv, kvc):
    b, l, _ = q.shape
    n_ctx = kvc.shape[1]
    rows = ATTN_BLOCKS_PER_STEP * ATTN_BLOCK
    bias = _band_bias(l)
    n_chains = KV_HEADS * ATTN_BLOCKS_PER_STEP
    stacked = GQA_GROUP * ATTN_BLOCK
    scratch = [((n_chains, stacked, BAND + n_ctx), F32), ((n_chains, stacked, BAND + n_ctx), BF16),
               ((n_chains, stacked, V7X_LANES), F32)]
    blk = (2 * _nbytes((rows, ATTN_W), BF16) + _nbytes((l, 2 * KV_W), BF16) + _nbytes((n_ctx, 2 * KV_W), BF16)
           + _nbytes(bias.shape, F32) + sum(_nbytes(s, dt) for s, dt in scratch) // 2)
    return pl.pallas_call(
        functools.partial(_attn_body, seq_len=l),
        grid=(b, l // rows),
        in_specs=[pl.BlockSpec(memory_space=pltpu.SMEM),
                  pl.BlockSpec(bias.shape, lambda bi, i: (0, 0, 0)),
                  pl.BlockSpec((1, rows, ATTN_W), lambda bi, i: (bi, i, 0)),
                  pl.BlockSpec((1, l, 2 * KV_W), lambda bi, i: (bi, 0, 0)),
                  pl.BlockSpec((1, n_ctx, 2 * KV_W), lambda bi, i: (bi, 0, 0))],
        out_specs=pl.BlockSpec((1, rows, ATTN_W), lambda bi, i: (bi, i, 0)),
        out_shape=jax.ShapeDtypeStruct((b, l, ATTN_W), BF16),
        scratch_shapes=[pltpu.VMEM(s, dt) for s, dt in scratch],
        compiler_params=_params(blk, 2),
        name="attn",
    )(sink, bias, q, kv, kvc)


GELU_C1 = 0.7978845608028654
GELU_C2 = GELU_C1 * 0.044715


def _twice_gelu(x):
    return x + x * jnp.tanh(x * (GELU_C1 + GELU_C2 * (x * x)))


def _stage_cast(chunks, stage, sem, store):
    copies = [pltpu.make_async_copy(src, stage.at[k % 2], sem.at[k % 2]) for k, src in enumerate(chunks)]
    copies[0].start()
    for k, copy in enumerate(copies):
        if k + 1 < len(copies):
            copies[k + 1].start()
        copy.wait()
        store(k, stage[k % 2])


def _mixffn_body(ym_ref, ymp_ref, ymn_ref, ya_ref, yap_ref, yan_ref, x_ref, xp_ref, xn_ref,
                 wo_ref, ga_ref, gain_ref, sh_ref, sc_ref, gf_ref, wup_hbm, cw_ref, cb_ref, wdn_hbm, fn_ref,
                 o_ref, ys, hs, x1s, gated, wup_ref, wdn_ref, stage_up, stage_dn, sem_up, sem_dn):
    @pl.when((pl.program_id(0) == 0) & (pl.program_id(1) == 0))
    def _():
        wc, wr = stage_up.shape[2], stage_dn.shape[1]

        def put_up(k, chunk):
            wup_ref[:, pl.ds(k * wc, wc)] = chunk.astype(BF16)

        def put_dn(k, chunk):
            wdn_ref[pl.ds(k * wr, wr), :] = chunk.astype(BF16)

        _stage_cast([wup_hbm.at[0, :, pl.ds(k * wc, wc)] for k in range(wup_ref.shape[1] // wc)],
                    stage_up, sem_up, put_up)
        _stage_cast([wdn_hbm.at[0, pl.ds(k * wr, wr), :] for k in range(wdn_ref.shape[0] // wr)],
                    stage_dn, sem_dn, put_dn)

    i = pl.program_id(1)
    tm = x_ref.shape[1]
    ext = tm + 2 * HALO
    lo, mid, hi = pl.ds(0, HALO), pl.ds(HALO, tm), pl.ds(HALO + tm, HALO)
    for rows, m_ref, a_ref in ((lo, ymp_ref, yap_ref), (mid, ym_ref, ya_ref), (hi, ymn_ref, yan_ref)):
        for h in range(MLSTM_HEADS):
            ys[rows, pl.ds(h * MLSTM_DH, MLSTM_DH)] = m_ref[0, h]
        ys[rows, pl.ds(MLSTM_W, ATTN_W)] = a_ref[0]
    proj = jnp.dot(ys[...], wo_ref[...], preferred_element_type=F32)
    ga = ga_ref[0]

    def mixed(x_rows, proj_rows):
        x1 = x_rows + ga * proj_rows
        return x1, _norm_modulate(x1, gain_ref[...], sh_ref[0], sc_ref[0]).astype(BF16)

    x1, h_mid = mixed(x_ref[0], proj[HALO:HALO + tm])
    x1s[...] = x1
    hs[mid, :] = h_mid
    h_lo = mixed(xp_ref[0], proj[:HALO])[1]
    h_hi = mixed(xn_ref[0], proj[HALO + tm:])[1]
    hs[lo, :] = jnp.where(i > 0, h_lo, jnp.zeros_like(h_lo))
    hs[hi, :] = jnp.where(i < pl.num_programs(1) - 1, h_hi, jnp.zeros_like(h_hi))
    starts = list(range(0, D_FF, FF_CHUNK))

    def up_a(col0):
        return jnp.dot(hs[...], wup_ref[:, pl.ds(col0, min(FF_CHUNK, D_FF - col0))], preferred_element_type=F32)

    nxt = up_a(starts[0])
    for c, col0 in enumerate(starts):
        width = min(FF_CHUNK, D_FF - col0)
        cols = pl.ds(col0, width)
        a = nxt
        if c + 1 < len(starts):
            nxt = up_a(starts[c + 1])
        a_prev = pltpu.roll(a, 1, 0)[HALO:HALO + tm]
        a_next = pltpu.roll(a, ext - 1, 0)[HALO:HALO + tm]
        conv = (cw_ref[0:1, cols] * a_prev + cw_ref[1:2, cols] * a[HALO:HALO + tm]
                + cw_ref[2:3, cols] * a_next + cb_ref[:, cols])
        g2 = _twice_gelu(conv)
        val = jnp.dot(hs[mid, :], wup_ref[:, pl.ds(D_FF + col0, width)], preferred_element_type=F32)
        gated[:, cols] = (g2 * val).astype(BF16)
    x2 = x1s[...] + (0.5 * gf_ref[0]) * jnp.dot(gated[...], wdn_ref[...], preferred_element_type=F32)
    ms = jnp.mean(x2 * x2, axis=-1, keepdims=True)
    o_ref[0] = x2 * lax.rsqrt(ms + EPS) * fn_ref[...]


def _mixffn(ym, ya, x, mod3, w_out, norm_gain, w_up, conv_w, conv_b, w_down, final_norm):
    b, l, d = x.shape
    tm = ROW_TILE
    per = tm // HALO
    last = l // HALO - 1
    ext = tm + 2 * HALO
    tile, before, after = (lambda bi, i: i), (lambda bi, i: jnp.maximum(i * per - 1, 0)), \
        (lambda bi, i: jnp.minimum((i + 1) * per, last))

    def rows3(make):
        return [make(tm, tile), make(HALO, before), make(HALO, after)]

    ym_spec = lambda n, r: pl.BlockSpec((1, MLSTM_HEADS, n, MLSTM_DH), lambda bi, i: (bi, 0, r(bi, i), 0))
    ya_spec = lambda n, r: pl.BlockSpec((1, n, ATTN_W), lambda bi, i: (bi, r(bi, i), 0))
    x_spec = lambda n, r: pl.BlockSpec((1, n, d), lambda bi, i: (bi, r(bi, i), 0))
    mod_spec = lambda piece: pl.BlockSpec((1, 1, d), lambda bi, i: (bi, 0, piece))
    const = lambda arr: pl.BlockSpec(arr.shape, lambda bi, i: (0,) * arr.ndim)
    resident = lambda arr: pl.BlockSpec(arr.shape, lambda bi, i: (0,) * arr.ndim, pipeline_mode=pl.Buffered(1))
    hbm = pl.BlockSpec(memory_space=pl.ANY)
    scratch = [((ext, d), BF16), ((ext, d), BF16), ((tm, d), F32), ((tm, D_FF), BF16),
               (w_up.shape[1:], BF16), (w_down.shape[1:], BF16),
               ((2, d, WEIGHT_STAGE), F32), ((2, WEIGHT_STAGE, d), F32)]
    blk = (2 * _nbytes((tm, d), F32) + 2 * _nbytes((tm, d), BF16) + 4 * _nbytes((ext, FF_CHUNK), F32)
           + (_nbytes(w_out.shape, BF16) + sum(_nbytes(s, dt) for s, dt in scratch)) // 2)
    return pl.pallas_call(
        _mixffn_body,
        grid=(b, l // tm),
        in_specs=(rows3(ym_spec) + rows3(ya_spec) + rows3(x_spec)
                  + [resident(w_out), mod_spec(2), const(norm_gain), mod_spec(3), mod_spec(4), mod_spec(5),
                     hbm, const(conv_w), const(conv_b), hbm, const(final_norm)]),
        out_specs=pl.BlockSpec((1, tm, d), lambda bi, i: (bi, i, 0)),
        out_shape=jax.ShapeDtypeStruct((b, l, d), F32),
        scratch_shapes=([pltpu.VMEM(s, dt) for s, dt in scratch]
                        + [pltpu.SemaphoreType.DMA((2,)), pltpu.SemaphoreType.DMA((2,))]),
        compiler_params=_params(blk, 2),
        name="mixffn",
    )(ym, ym, ym, ya, ya, ya, x, x, x, w_out, mod3, norm_gain, mod3, mod3, mod3,
      w_up, conv_w, conv_b, w_down, final_norm)


def _rope_tables(n_tokens):
    pos = jnp.arange(n_tokens)
    r = (pos // GRID_W).astype(F32)
    c = (pos % GRID_W).astype(F32)
    inv = ROPE_BASE ** (-jnp.arange(ROPE_AXIS_PAIRS, dtype=F32) / ROPE_AXIS_PAIRS)
    ar, ac = r[:, None] * inv, c[:, None] * inv
    zero = jnp.zeros_like(ar)
    cos = jnp.concatenate([jnp.cos(ar), jnp.cos(ar), jnp.cos(ac), jnp.cos(ac)], axis=1)
    sina = jnp.concatenate([-jnp.sin(ar), zero, -jnp.sin(ac), zero], axis=1)
    sinb = jnp.concatenate([zero, jnp.sin(ar), zero, jnp.sin(ac)], axis=1)
    rep = V7X_LANES // ATTN_DH
    return tuple(jnp.tile(tb, (1, rep)) for tb in (cos, sina, sinb))


def _identity_rope(n_tokens):
    return (jnp.ones((n_tokens, V7X_LANES), F32), jnp.zeros((n_tokens, V7X_LANES), F32),
            jnp.zeros((n_tokens, V7X_LANES), F32))


def _gate_table(gates_ctx, gates_x):
    g = jnp.concatenate([gates_ctx, gates_x], axis=2)
    b, _, n = g.shape
    nc = n // MCHUNK
    g = g.reshape(b, 4, MLSTM_HEADS, nc, MCHUNK).transpose(0, 2, 1, 3, 4)
    g = jnp.pad(g, ((0, 0), (0, 0), (0, 0), (0, MAX_CHUNKS - nc), (0, 0)))
    return g.reshape(b, MLSTM_HEADS, 4 * MAX_CHUNKS, MCHUNK)


def kernel(x, c, ctx, c_ctx, w_ada, b_ada, norm_mix, norm_ffn, w_in, gate_b, qk_conv_w, qk_conv_b, mlstm_norm,
           attn_sink, w_out, w_up, ffn_conv_w, ffn_conv_b, w_down, final_norm):
    b, l, d = x.shape
    n_ctx = ctx.shape[1]
    assert w_ada.shape[0] == 1, "single-layer stack"
    assert l % ROW_TILE == 0 and l % MCHUNK == 0 and n_ctx % MCHUNK == 0 and l >= BAND
    assert l % (ATTN_BLOCKS_PER_STEP * ATTN_BLOCK) == 0 and MCHUNK == V7X_LANES
    assert (l + n_ctx) // MCHUNK <= MAX_CHUNKS

    rows = -(-(b + 1) // V7X_SUBLANES) * V7X_SUBLANES
    cvec = jnp.zeros((rows, d), F32).at[:b].set(c).at[b].set(c_ctx)
    mod = _ada(cvec, w_ada[0], b_ada[0][None, :])
    mod_x = mod[:b].reshape(b, 1, 6 * d)
    mod_c = mod[b:b + 1].reshape(1, 1, 6 * d)

    w_in = jnp.swapaxes(w_in, 1, 2)
    o0, o1, o2 = 2 * MLSTM_W, 3 * MLSTM_W, 4 * MLSTM_W
    o3 = o2 + N_GATES
    o4 = o3 + ATTN_W
    gb = gate_b[0][:, None]
    gain_mix = norm_mix[0][None, :]
    conv = (qk_conv_w[0], qk_conv_b[0][None, :])
    k_mc, v_mc, gates_c, kv_ac = _inproj(
        ctx, gain_mix, mod_c, 0, 1, False, _identity_rope(n_ctx), gb, *conv, w_in,
        [("conv_k", MLSTM_W, MLSTM_W), ("heads", o0, MLSTM_W), ("gates", o2, N_GATES), ("plain", o4, 2 * KV_W)],
        "inproj_ctx")
    qk_m, vo_m, gates_x, q_a, kv_a = _inproj(
        x, gain_mix, mod_x, 0, 1, True, _rope_tables(l), gb, *conv, w_in,
        [("conv_qk", 0, o0), ("heads", o0, o0), ("gates", o2, N_GATES), ("rope_q", o3, ATTN_W),
         ("rope_kv", o4, 2 * KV_W)], "inproj_x")

    y_m = _mlstm(qk_m, vo_m, k_mc, v_mc, _gate_table(gates_c, gates_x), mlstm_norm[0][None, :])
    y_a = _attn(attn_sink[0], q_a, kv_a, kv_ac)

    return _mixffn(y_m, y_a, x, mod_x, w_out[0].astype(BF16), norm_ffn[0][None, :], w_up,
                   ffn_conv_w[0], ffn_conv_b[0][None, :], w_down, final_norm[None, :])
```

```python
import functools

import jax
import jax.numpy as jnp
from jax import lax
from jax.experimental import pallas as pl
from jax.experimental.pallas import tpu as pltpu

F32 = jnp.float32
BF16 = jnp.bfloat16

D_MODEL = 1024
GRID_W = 64
MLSTM_HEADS = 4
MLSTM_DH = 128
MLSTM_W = MLSTM_HEADS * MLSTM_DH
N_GATES = 4 * MLSTM_HEADS
ATTN_HEADS = 8
KV_HEADS = 2
ATTN_DH = 64
ATTN_W = ATTN_HEADS * ATTN_DH
KV_W = KV_HEADS * ATTN_DH
GQA_GROUP = ATTN_HEADS // KV_HEADS
WINDOW = 128
ROPE_BASE = 10000.0
ROPE_AXIS_PAIRS = ATTN_DH // 4
D_FF = 2816
EPS = 1e-6
LOG2E = 1.4426950408889634

V7X_LANES = 128
V7X_SUBLANES = 8
V7X_VMEM_BYTES = 64 * 1024 * 1024

ROW_TILE = 512
ADA_COL_TILE = 1536
PROJ_CHUNK = 512
MCHUNK = 128
MAX_CHUNKS = 32
ATTN_BLOCK = 128
ATTN_BLOCKS_PER_STEP = 4
BAND = 3 * ATTN_BLOCK
FF_CHUNK = 512
HALO = 2 * V7X_SUBLANES


def _vmem_limit(block_bytes):
    return int(min(V7X_VMEM_BYTES * 7 // 8, 2 * block_bytes + 16 * 1024 * 1024))


def _params(block_bytes, n_axes):
    return pltpu.CompilerParams(dimension_semantics=("arbitrary",) * n_axes,
                                vmem_limit_bytes=_vmem_limit(block_bytes))


def _nbytes(shape, dtype):
    n = 1
    for s in shape:
        n *= s
    return n * jnp.dtype(dtype).itemsize


def _ada_body(c_ref, w_ref, b_ref, o_ref):
    c = c_ref[...]
    s = (c * jax.nn.sigmoid(c)).astype(BF16)
    o_ref[...] = jnp.dot(s, w_ref[...].astype(BF16), preferred_element_type=F32) + b_ref[...]


def _ada(cvec, w_ada, b_ada):
    rows, d = cvec.shape
    n = w_ada.shape[1]
    blk = _nbytes((d, ADA_COL_TILE), F32) + _nbytes((rows, d), F32) + 2 * _nbytes((rows, ADA_COL_TILE), F32)
    return pl.pallas_call(
        _ada_body,
        grid=(n // ADA_COL_TILE,),
        in_specs=[pl.BlockSpec((rows, d), lambda j: (0, 0)),
                  pl.BlockSpec((d, ADA_COL_TILE), lambda j: (0, j)),
                  pl.BlockSpec((1, ADA_COL_TILE), lambda j: (0, j))],
        out_specs=pl.BlockSpec((rows, ADA_COL_TILE), lambda j: (0, j)),
        out_shape=jax.ShapeDtypeStruct((rows, n), F32),
        compiler_params=_params(blk, 1),
        name="ada",
    )(cvec, w_ada, b_ada)


def _norm_modulate(x, gain, shift, scale):
    ms = jnp.mean(x * x, axis=-1, keepdims=True)
    return (x * lax.rsqrt(ms + EPS) * gain) * (1.0 + scale) + shift


def _silu(y):
    h = 0.5 * y
    return h + h * jnp.tanh(h)


def _inproj_body(x_ref, xp_ref, xn_ref, gain_ref, sh_ref, sc_ref, cos_ref, sina_ref, sinb_ref, gb_ref,
                 cw_ref, cb_ref, win_ref, *refs, groups):
    n = len(groups)
    o_refs, hs, w_refs = refs[:n], refs[n], refs[n + 1:]
    kinds = [g[0] for g in groups]

    @pl.when((pl.program_id(0) == 0) & (pl.program_id(1) == 0))
    def _():
        for (_, col0, width), w_bf in zip(groups, w_refs):
            w_bf[...] = win_ref[pl.ds(col0, width), :].astype(BF16)

    proj = lambda lhs, w_bf: lax.dot_general(lhs, w_bf[...], (((1,), (1,)), ((), ())), preferred_element_type=F32)

    i = pl.program_id(1)
    tm = x_ref.shape[1]
    ext = tm + 2 * HALO
    norm = lambda rows: _norm_modulate(rows, gain_ref[...], sh_ref[0], sc_ref[0]).astype(BF16)
    half = tm // 2

    def heads_half(r0):
        for kind, w_ref, o_ref in zip(kinds, w_refs, o_refs):
            if kind == "heads":
                for s0 in range(0, w_ref.shape[0], PROJ_CHUNK):
                    u = proj(hs[pl.ds(HALO + r0, half), :], w_ref.at[pl.ds(s0, PROJ_CHUNK)])
                    for j in range(PROJ_CHUNK // V7X_LANES):
                        o_ref[0, s0 // V7X_LANES + j, pl.ds(r0, half), :] = (
                            u[:, j * V7X_LANES:(j + 1) * V7X_LANES].astype(BF16))

    hs[pl.ds(HALO, half), :] = norm(x_ref[0, pl.ds(0, half), :])
    heads_half(0)
    hs[pl.ds(HALO + half, half), :] = norm(x_ref[0, pl.ds(half, half), :])
    h_lo, h_hi = norm(xp_ref[0]), norm(xn_ref[0])
    hs[pl.ds(0, HALO), :] = jnp.where(i > 0, h_lo, jnp.zeros_like(h_lo))
    hs[pl.ds(HALO + tm, HALO), :] = jnp.where(i < pl.num_programs(1) - 1, h_hi, jnp.zeros_like(h_hi))
    heads_half(half)
    hb = hs[pl.ds(HALO, tm), :]
    def conv_item(w_ref, o_ref, s0, c0, first_k):
        def epilogue(a):
            cc = pl.ds(c0 + s0, PROJ_CHUNK)
            a_prev = pltpu.roll(a, 1, 0)[HALO:HALO + tm]
            a_next = pltpu.roll(a, ext - 1, 0)[HALO:HALO + tm]
            y = _silu(cw_ref[0:1, cc] * a_prev + cw_ref[1:2, cc] * a[HALO:HALO + tm]
                      + cw_ref[2:3, cc] * a_next + cb_ref[:, cc])
            for j in range(PROJ_CHUNK // V7X_LANES):
                slab = s0 // V7X_LANES + j
                u = y[:, j * V7X_LANES:(j + 1) * V7X_LANES]
                o_ref[0, slab] = (u * MLSTM_DH ** -0.5 if slab >= first_k else u).astype(BF16)
        return (lambda: proj(hs[...], w_ref.at[pl.ds(s0, PROJ_CHUNK)])), epilogue

    def gates_item(w_ref, o_ref):
        def epilogue(u):
            o_ref[0] = u + gb_ref[...]
        return (lambda: lax.dot_general(w_ref[...], hb, (((1,), (1,)), ((), ())), preferred_element_type=F32)), epilogue

    def rope_item(w_ref, o_ref, kind):
        def epilogue(u_all):
            n_groups = u_all.shape[1] // V7X_LANES
            n_rot = {"rope_q": n_groups, "rope_kv": KV_W // V7X_LANES, "plain": 0}[kind]
            for j in range(n_groups):
                u = u_all[:, j * V7X_LANES:(j + 1) * V7X_LANES]
                if j < n_rot:
                    u = (u * cos_ref[...]
                         + pltpu.roll(u, V7X_LANES - ROPE_AXIS_PAIRS, 1) * sina_ref[...]
                         + pltpu.roll(u, ROPE_AXIS_PAIRS, 1) * sinb_ref[...])
                    if kind == "rope_q":
                        u = u * (ATTN_DH ** -0.5 * LOG2E)
                o_ref[0, :, pl.ds(j * V7X_LANES, V7X_LANES)] = u.astype(BF16)
        return (lambda: proj(hb, w_ref)), epilogue

    items = []
    for kind, w_ref, o_ref in zip(kinds, w_refs, o_refs):
        if kind in ("conv_qk", "conv_k"):
            width = w_ref.shape[0]
            c0 = cw_ref.shape[1] - width
            items += [conv_item(w_ref, o_ref, s0, c0, width // V7X_LANES - MLSTM_HEADS)
                      for s0 in range(0, width, PROJ_CHUNK)]
        elif kind == "heads":
            pass
        elif kind == "gates":
            items.append(gates_item(w_ref, o_ref))
        else:
            items.append(rope_item(w_ref, o_ref, kind))
    for matmul, epilogue in items:
        epilogue(matmul())


def _inproj(x, gain, mod3, shift_idx, scale_idx, per_batch_mod, rope, gate_b, conv_w, conv_b, w_in, groups, name):
    b, n, d = x.shape
    tm = min(ROW_TILE, n)
    per = tm // HALO
    last = n // HALO - 1
    cos, sina, sinb = rope
    mod_spec = lambda piece: pl.BlockSpec(
        (1, 1, d), (lambda bi, i: (bi, 0, piece)) if per_batch_mod else (lambda bi, i: (0, 0, piece)))
    in_specs = [pl.BlockSpec((1, tm, d), lambda bi, i: (bi, i, 0)),
                pl.BlockSpec((1, HALO, d), lambda bi, i: (bi, jnp.maximum(i * per - 1, 0), 0)),
                pl.BlockSpec((1, HALO, d), lambda bi, i: (bi, jnp.minimum((i + 1) * per, last), 0)),
                pl.BlockSpec((1, d), lambda bi, i: (0, 0)),
                mod_spec(shift_idx), mod_spec(scale_idx),
                pl.BlockSpec((tm, V7X_LANES), lambda bi, i: (i, 0)),
                pl.BlockSpec((tm, V7X_LANES), lambda bi, i: (i, 0)),
                pl.BlockSpec((tm, V7X_LANES), lambda bi, i: (i, 0)),
                pl.BlockSpec((N_GATES, 1), lambda bi, i: (0, 0)),
                pl.BlockSpec(conv_w.shape, lambda bi, i: (0, 0)),
                pl.BlockSpec(conv_b.shape, lambda bi, i: (0, 0)),
                pl.BlockSpec((None,) + w_in.shape[1:], lambda bi, i: (0, 0, 0), pipeline_mode=pl.Buffered(1))]
    out_specs, out_shapes = [], []
    scratch = [((tm + 2 * HALO, d), BF16)]
    blk = (_nbytes((tm, d), F32) + 3 * _nbytes((tm, V7X_LANES), F32) + 4 * _nbytes((tm + 2 * HALO, 2 * MLSTM_W), F32)
           + _nbytes(w_in.shape, F32) // 2)
    for kind, _, width in groups:
        if kind == "gates":
            scratch.append(((width, d), BF16))
            out_specs.append(pl.BlockSpec((1, N_GATES, tm), lambda bi, i: (bi, 0, i)))
            out_shapes.append(jax.ShapeDtypeStruct((b, N_GATES, n), F32))
            continue
        scratch.append(((width, d), BF16))
        if kind in ("heads", "conv_qk", "conv_k"):
            slabs = width // V7X_LANES
            out_specs.append(pl.BlockSpec((1, slabs, tm, V7X_LANES), lambda bi, i: (bi, 0, i, 0)))
            out_shapes.append(jax.ShapeDtypeStruct((b, slabs, n, V7X_LANES), BF16))
        else:
            out_specs.append(pl.BlockSpec((1, tm, width), lambda bi, i: (bi, i, 0)))
            out_shapes.append(jax.ShapeDtypeStruct((b, n, width), BF16))
        blk += _nbytes((tm, width), BF16) + _nbytes((tm, width), F32)
    blk += sum(_nbytes(s, dt) for s, dt in scratch) // 2
    return pl.pallas_call(
        functools.partial(_inproj_body, groups=tuple(groups)),
        grid=(b, n // tm),
        in_specs=in_specs,
        out_specs=out_specs,
        out_shape=out_shapes,
        scratch_shapes=[pltpu.VMEM(s, dt) for s, dt in scratch],
        compiler_params=_params(blk, 2),
        name=name,
    )(x, x, x, gain, mod3, mod3, cos, sina, sinb, gate_b, conv_w, conv_b, w_in)


def _sigmoid(y):
    return 0.5 + 0.5 * jnp.tanh(0.5 * y)


def _log_sigmoid(x):
    return jnp.minimum(x, 0.0) - jnp.log1p(jnp.exp(-jnp.abs(x)))


def _mlstm_body(qs, ks, v_ref, o_ref, kcs, vc_ref, gt_ref, gain_ref,
                y_ref, gtab, rtab, cumtab, kts, cnl, c0n, cumc, sqk, sds, inter_s, em_s, svq,
                *, n_ctx_chunks, n_lat_chunks):
    t = MCHUNK
    dh = MLSTM_DH
    n_chunks = n_ctx_chunks + n_lat_chunks
    nt = (((1,), (1,)), ((), ()))

    tbl = gt_ref[0, 0]
    rid = lax.broadcasted_iota(jnp.int32, tbl.shape, 0) // MAX_CHUNKS
    lane = lax.broadcasted_iota(jnp.int32, tbl.shape, 1)
    lf = _log_sigmoid(tbl)
    pre = jnp.where(rid == 1, lf, 0.0)
    suf = jnp.where(rid == 3, lf, 0.0)
    s = 1
    while s < t:
        pre = pre + jnp.where(lane >= s, pltpu.roll(pre, s, 1), 0.0)
        suf = suf + jnp.where(lane < t - s, pltpu.roll(suf, t - s, 1), 0.0)
        s *= 2
    li = (tbl[0:MAX_CHUNKS] * LOG2E, tbl[2 * MAX_CHUNKS:3 * MAX_CHUNKS] * LOG2E)
    cum = (pre[MAX_CHUNKS:2 * MAX_CHUNKS] * LOG2E, suf[3 * MAX_CHUNKS:4 * MAX_CHUNKS] * LOG2E)
    total = (cum[0][:, t - 1:t], cum[1][:, 0:1])
    mloc = []
    for d in (0, 1):
        g = total[d] - cum[d] + li[d]
        mloc.append(jnp.max(g, axis=1, keepdims=True))
        gtab[d] = g
        rtab[d] = cum[d] - li[d]
        cumtab[d] = cum[d]

    orders = (list(range(n_chunks)),
              list(range(n_ctx_chunks - 1, -1, -1)) + list(range(n_chunks - 1, n_ctx_chunks - 1, -1)))

    a_sc, m0, m1 = ({}, {}), ({}, {}), ({}, {})
    for d in (0, 1):
        m_st = jnp.zeros((1, 1), F32)
        for c in orders[d][:-1]:
            m0[d][c] = m_st
            tot_c = total[d][c:c + 1, :]
            m1[d][c] = jnp.maximum(tot_c + m_st, mloc[d][c:c + 1, :])
            a_sc[d][c] = jnp.exp2(tot_c + m_st - m1[d][c])
            m_st = m1[d][c]
        m0[d][orders[d][-1]] = m_st

    def kv_chunk(c):
        if c < n_ctx_chunks:
            rows = pl.ds(c * t, t)
            return kcs[rows, :], vc_ref[rows, :]
        rows = pl.ds((c - n_ctx_chunks) * t, t)
        return ks[rows, :], v_ref[rows, :]

    ri = lax.broadcasted_iota(jnp.int32, (t, t), 0)
    ci = lax.broadcasted_iota(jnp.int32, (t, t), 1)
    eye = jnp.where(ri == ci, 1.0, 0.0).astype(BF16)
    ones = jnp.ones((t, dh), BF16)

    for c in range(n_chunks):
        kts[c] = lax.dot_general(eye, kv_chunk(c)[0], nt, preferred_element_type=F32)
    for c in range(n_chunks):
        vo = jnp.concatenate([kv_chunk(c)[1], ones], axis=1)
        for d in (0, 1):
            if c != orders[d][-1]:
                w = jnp.exp2(gtab[d, pl.ds(c, 1), :] - m1[d][c])
                cnl[d, c] = jnp.dot((kts[c] * w).astype(BF16), vo, preferred_element_type=F32)

    for d in (0, 1):
        cn = jnp.zeros((dh, 2 * dh), F32)
        for c in orders[d]:
            if c >= n_ctx_chunks:
                c0n[d, c - n_ctx_chunks] = cn.astype(BF16)
            if c != orders[d][-1]:
                cn = a_sc[d][c] * cn + cnl[d, c]

    for j in range(n_lat_chunks):
        c = j + n_ctx_chunks
        rows = pl.ds(j * t, t)
        sqk[rows, :] = lax.dot_general(qs[rows, :], ks[rows, :], nt, preferred_element_type=F32)
        for d in (0, 1):
            cumc[d, rows, :] = jnp.broadcast_to(cumtab[d, pl.ds(c, 1), :], (t, t)).T

    grp = 2
    gi = lax.broadcasted_iota(jnp.int32, (grp * t, t), 0) & (t - 1)
    gs = lax.broadcasted_iota(jnp.int32, (grp * t, t), 1)
    for d in (0, 1):
        ok = (gs <= gi) if d == 0 else (gs >= gi)
        for j0 in range(0, n_lat_chunks, grp):
            rows = pl.ds(j0 * t, grp * t)
            cs = [j + n_ctx_chunks for j in range(j0, j0 + grp)]
            rb = jnp.concatenate([jnp.broadcast_to(rtab[d, pl.ds(c, 1), :], (t, t)) for c in cs], axis=0)
            m0b = jnp.concatenate([jnp.broadcast_to(m0[d][c], (t, t)) for c in cs], axis=0)
            cum_c = cumc[d, rows, :]
            d_log = jnp.where(ok, cum_c - rb, -jnp.inf)
            a_log = cum_c + m0b
            m_t = jnp.maximum(a_log, jnp.max(d_log, axis=1, keepdims=True))
            sds[d, rows, :] = sqk[rows, :].astype(BF16) * jnp.exp2((d_log - m_t).astype(BF16))
            inter_s[d, rows, :] = jnp.exp2(a_log - m_t)
            em_s[d, rows, :] = jnp.exp2(-m_t)

    for j in range(n_lat_chunks):
        rows = pl.ds(j * t, t)
        qf = qs[rows, :].astype(F32)
        vo = jnp.concatenate([v_ref[rows, :], ones], axis=1)
        for d in (0, 1):
            lhs = jnp.concatenate([sds[d, rows, :], (qf * inter_s[d, rows, :]).astype(BF16)], axis=1)
            rhs = jnp.concatenate([vo, c0n[d, j]], axis=0)
            svq[d, rows, :] = jnp.dot(lhs, rhs, preferred_element_type=F32)

    gain = gain_ref[...]
    grp = 4
    for j0 in range(0, n_lat_chunks, grp):
        rows = pl.ds(j0 * t, grp * t)
        hid = None
        for d in (0, 1):
            h_d = svq[d, rows, pl.ds(0, dh)] / jnp.maximum(jnp.abs(svq[d, rows, pl.ds(dh, dh)]), em_s[d, rows, :])
            hid = h_d if hid is None else hid + h_d
        hn = hid * lax.rsqrt(jnp.mean(hid * hid, axis=-1, keepdims=True) + EPS) * gain
        y_ref[rows, :] = (hn * _sigmoid(o_ref[rows, :].astype(F32))).astype(BF16)


def _mlstm(qk, vo, k_ctx, v_ctx, gate_tbl, gain):
    b, _, l, _ = qk.shape
    n_ctx = k_ctx.shape[2]
    dh = MLSTM_DH
    nh = MLSTM_HEADS
    n_ctx_chunks, n_lat_chunks = n_ctx // MCHUNK, l // MCHUNK
    seq = lambda slab0: pl.BlockSpec((None, None, l, dh), lambda bi, h: (bi, slab0 + h, 0, 0))
    ctx = lambda slab0: pl.BlockSpec((None, None, n_ctx, dh), lambda bi, h: (bi, slab0 + h, 0, 0))
    n_chunks = n_ctx_chunks + n_lat_chunks
    scratch = [((2, MAX_CHUNKS, MCHUNK), F32), ((2, MAX_CHUNKS, MCHUNK), F32),
               ((2, MAX_CHUNKS, MCHUNK), F32), ((n_chunks, dh, MCHUNK), F32),
               ((2, n_chunks, dh, 2 * dh), F32), ((2, n_lat_chunks, dh, 2 * dh), BF16),
               ((2, l, MCHUNK), F32), ((l, MCHUNK), F32), ((2, l, MCHUNK), BF16),
               ((2, l, MCHUNK), F32), ((2, l, MCHUNK), F32), ((2, l, 2 * dh), F32)]
    blk = (5 * _nbytes((l, dh), BF16) + 2 * _nbytes((n_ctx, dh), BF16) + _nbytes((V7X_LANES, V7X_LANES), F32)
           + sum(_nbytes(s, dt) for s, dt in scratch) // 2)
    return pl.pallas_call(
        functools.partial(_mlstm_body, n_ctx_chunks=n_ctx_chunks, n_lat_chunks=n_lat_chunks),
        grid=(b, nh),
        in_specs=[seq(0), seq(nh), seq(0), seq(nh), ctx(0), ctx(0),
                  pl.BlockSpec((1, 1, 4 * MAX_CHUNKS, MCHUNK), lambda bi, h: (bi, h, 0, 0)),
                  pl.BlockSpec((1, dh), lambda bi, h: (0, h))],
        out_specs=pl.BlockSpec((None, None, l, dh), lambda bi, h: (bi, h, 0, 0)),
        out_shape=jax.ShapeDtypeStruct((b, nh, l, dh), BF16),
        scratch_shapes=[pltpu.VMEM(s, dt) for s, dt in scratch],
        compiler_params=_params(blk, 2),
        name="mlstm",
    )(qk, qk, vo, vo, k_ctx, v_ctx, gate_tbl, gain)


def _attn_body(sink_ref, bias_ref, q_ref, kv_ref, kvc_ref, o_ref, s_scr, p_scr, e_scr, *, seq_len):
    blk = ATTN_BLOCK
    n_blocks = seq_len // blk
    n_ctx = kvc_ref.shape[1]
    rows = GQA_GROUP * blk
    nt = (((1,), (1,)), ((), ()))
    chains = [(qb, g) for qb in range(ATTN_BLOCKS_PER_STEP) for g in range(KV_HEADS)]

    def block_start(qb):
        i = pl.program_id(1) * ATTN_BLOCKS_PER_STEP + qb
        return i, pl.multiple_of(jnp.clip((i - 1) * blk, 0, seq_len - BAND), blk)

    for ci, (qb, g) in enumerate(chains):
        i, start = block_start(qb)
        k_cols = pl.ds(g * ATTN_DH, ATTN_DH)
        q = jnp.concatenate([q_ref[0, pl.ds(qb * blk, blk), pl.ds((g * GQA_GROUP + j) * ATTN_DH, ATTN_DH)]
                             for j in range(GQA_GROUP)], axis=0)
        bias = bias_ref[jnp.where(i == 0, 0, jnp.where(i == n_blocks - 1, 2, 1))]
        s_scr[ci, :, pl.ds(0, BAND)] = (
            lax.dot_general(q, kv_ref[0, pl.ds(start, BAND), k_cols], nt, preferred_element_type=F32) + bias)
        s_scr[ci, :, pl.ds(BAND, n_ctx)] = lax.dot_general(q, kvc_ref[0, :, k_cols], nt, preferred_element_type=F32)

    for ci, (qb, g) in enumerate(chains):
        s = s_scr[ci]
        sink = jnp.concatenate([jnp.full((blk, V7X_LANES), sink_ref[g * GQA_GROUP + j] * LOG2E, F32)
                                for j in range(GQA_GROUP)], axis=0)
        m = jnp.maximum(jnp.broadcast_to(jnp.max(s, axis=1, keepdims=True), (rows, V7X_LANES)), sink)
        p_scr[ci] = jnp.exp2((s - jnp.tile(m, (1, (BAND + n_ctx) // V7X_LANES))).astype(BF16))
        e_scr[ci] = jnp.exp2(sink - m)

    ones_b = jnp.ones((BAND, ATTN_DH), BF16)
    ones_c = jnp.ones((n_ctx, ATTN_DH), BF16)
    for ci, (qb, g) in enumerate(chains):
        _, start = block_start(qb)
        v_cols = pl.ds(KV_W + g * ATTN_DH, ATTN_DH)
        vb = jnp.concatenate([kv_ref[0, pl.ds(start, BAND), v_cols], ones_b], axis=1)
        vc = jnp.concatenate([kvc_ref[0, :, v_cols], ones_c], axis=1)
        acc = (jnp.dot(p_scr[ci, :, pl.ds(0, BAND)], vb, preferred_element_type=F32)
               + jnp.dot(p_scr[ci, :, pl.ds(BAND, n_ctx)], vc, preferred_element_type=F32))
        out = (acc / (pltpu.roll(acc, ATTN_DH, 1) + e_scr[ci]))[:, :ATTN_DH]
        for j in range(GQA_GROUP):
            h = g * GQA_GROUP + j
            o_ref[0, pl.ds(qb * blk, blk), pl.ds(h * ATTN_DH, ATTN_DH)] = out[j * blk:(j + 1) * blk].astype(BF16)


def _band_bias(seq_len):
    row = jnp.arange(ATTN_BLOCK)[:, None]
    col = jnp.arange(BAND)[None, :]
    n_blocks = seq_len // ATTN_BLOCK
    tables = []
    for i in (0, 1, n_blocks - 1):
        start = min(max((i - 1) * ATTN_BLOCK, 0), seq_len - BAND)
        ok = jnp.abs(start + col - (i * ATTN_BLOCK + row)) <= WINDOW
        tables.append(jnp.tile(jnp.where(ok, 0.0, -jnp.inf).astype(F32), (GQA_GROUP, 1)))
    return jnp.stack(tables)


def _attn(sink, q, kv, kvc):
    b, l, _ = q.shape
    n_ctx = kvc.shape[1]
    rows = ATTN_BLOCKS_PER_STEP * ATTN_BLOCK
    bias = _band_bias(l)
    n_chains = KV_HEADS * ATTN_BLOCKS_PER_STEP
    stacked = GQA_GROUP * ATTN_BLOCK
    scratch = [((n_chains, stacked, BAND + n_ctx), F32), ((n_chains, stacked, BAND + n_ctx), BF16),
               ((n_chains, stacked, V7X_LANES), F32)]
    blk = (2 * _nbytes((rows, ATTN_W), BF16) + _nbytes((l, 2 * KV_W), BF16) + _nbytes((n_ctx, 2 * KV_W), BF16)
           + _nbytes(bias.shape, F32) + sum(_nbytes(s, dt) for s, dt in scratch) // 2)
    return pl.pallas_call(
        functools.partial(_attn_body, seq_len=l),
        grid=(b, l // rows),
        in_specs=[pl.BlockSpec(memory_space=pltpu.SMEM),
                  pl.BlockSpec(bias.shape, lambda bi, i: (0, 0, 0)),
                  pl.BlockSpec((1, rows, ATTN_W), lambda bi, i: (bi, i, 0)),
                  pl.BlockSpec((1, l, 2 * KV_W), lambda bi, i: (bi, 0, 0)),
                  pl.BlockSpec((1, n_ctx, 2 * KV_W), lambda bi, i: (bi, 0, 0))],
        out_specs=pl.BlockSpec((1, rows, ATTN_W), lambda bi, i: (bi, i, 0)),
        out_shape=jax.ShapeDtypeStruct((b, l, ATTN_W), BF16),
        scratch_shapes=[pltpu.VMEM(s, dt) for s, dt in scratch],
        compiler_params=_params(blk, 2),
        name="attn",
    )(sink, bias, q, kv, kvc)


GELU_C1 = 0.7978845608028654
GELU_C2 = GELU_C1 * 0.044715


def _twice_gelu(x):
    return x + x * jnp.tanh(x * (GELU_C1 + GELU_C2 * (x * x)))


def _mixffn_body(ym_ref, ymp_ref, ymn_ref, ya_ref, yap_ref, yan_ref, x_ref, xp_ref, xn_ref,
                 wo_ref, ga_ref, gain_ref, sh_ref, sc_ref, gf_ref, wup_ref, cw_ref, cb_ref, wdn_ref, fn_ref,
                 o_ref, ys, hs, x1s, gated):
    i = pl.program_id(1)
    tm = x_ref.shape[1]
    ext = tm + 2 * HALO
    lo, mid, hi = pl.ds(0, HALO), pl.ds(HALO, tm), pl.ds(HALO + tm, HALO)
    for rows, m_ref, a_ref in ((lo, ymp_ref, yap_ref), (mid, ym_ref, ya_ref), (hi, ymn_ref, yan_ref)):
        for h in range(MLSTM_HEADS):
            ys[rows, pl.ds(h * MLSTM_DH, MLSTM_DH)] = m_ref[0, h]
        ys[rows, pl.ds(MLSTM_W, ATTN_W)] = a_ref[0]
    ga = ga_ref[0]
    half = tm // 2

    def mixed(x_rows, proj_rows):
        x1 = x_rows + ga * proj_rows
        return x1, _norm_modulate(x1, gain_ref[...], sh_ref[0], sc_ref[0]).astype(BF16)

    proj = jnp.dot(ys[pl.ds(0, HALO + half), :], wo_ref[...], preferred_element_type=F32)
    x1, h_top = mixed(x_ref[0, pl.ds(0, half), :], proj[HALO:])
    x1s[pl.ds(0, half), :] = x1
    hs[pl.ds(HALO, half), :] = h_top
    h_lo = mixed(xp_ref[0], proj[:HALO])[1]
    hs[lo, :] = jnp.where(i > 0, h_lo, jnp.zeros_like(h_lo))
    proj = jnp.dot(ys[pl.ds(HALO + half, half + HALO), :], wo_ref[...], preferred_element_type=F32)
    x1, h_bot = mixed(x_ref[0, pl.ds(half, half), :], proj[:half])
    x1s[pl.ds(half, half), :] = x1
    hs[pl.ds(HALO + half, half), :] = h_bot
    h_hi = mixed(xn_ref[0], proj[half:])[1]
    hs[hi, :] = jnp.where(i < pl.num_programs(1) - 1, h_hi, jnp.zeros_like(h_hi))
    starts = list(range(0, D_FF, FF_CHUNK))

    def up_a(col0):
        return jnp.dot(hs[...], wup_ref[:, pl.ds(col0, min(FF_CHUNK, D_FF - col0))], preferred_element_type=F32)

    nxt = up_a(starts[0])
    for c, col0 in enumerate(starts):
        width = min(FF_CHUNK, D_FF - col0)
        cols = pl.ds(col0, width)
        a = nxt
        if c + 1 < len(starts):
            nxt = up_a(starts[c + 1])
        a_prev = pltpu.roll(a, 1, 0)[HALO:HALO + tm]
        a_next = pltpu.roll(a, ext - 1, 0)[HALO:HALO + tm]
        conv = (cw_ref[0:1, cols] * a_prev + cw_ref[1:2, cols] * a[HALO:HALO + tm]
                + cw_ref[2:3, cols] * a_next + cb_ref[:, cols])
        g2 = _twice_gelu(conv)
        val = jnp.dot(hs[mid, :], wup_ref[:, pl.ds(D_FF + col0, width)], preferred_element_type=F32)
        gated[:, cols] = (g2 * val).astype(BF16)
    for r0 in (0, half):
        rows = pl.ds(r0, half)
        x2 = x1s[rows, :] + (0.5 * gf_ref[0]) * jnp.dot(gated[rows, :], wdn_ref[...], preferred_element_type=F32)
        ms = jnp.mean(x2 * x2, axis=-1, keepdims=True)
        o_ref[0, rows, :] = x2 * lax.rsqrt(ms + EPS) * fn_ref[...]


def _mixffn(ym, ya, x, mod3, w_out, norm_gain, w_up, conv_w, conv_b, w_down, final_norm):
    b, l, d = x.shape
    tm = ROW_TILE
    per = tm // HALO
    last = l // HALO - 1
    ext = tm + 2 * HALO
    tile, before, after = (lambda bi, i: i), (lambda bi, i: jnp.maximum(i * per - 1, 0)), \
        (lambda bi, i: jnp.minimum((i + 1) * per, last))

    def rows3(make):
        return [make(tm, tile), make(HALO, before), make(HALO, after)]

    ym_spec = lambda n, r: pl.BlockSpec((1, MLSTM_HEADS, n, MLSTM_DH), lambda bi, i: (bi, 0, r(bi, i), 0))
    ya_spec = lambda n, r: pl.BlockSpec((1, n, ATTN_W), lambda bi, i: (bi, r(bi, i), 0))
    x_spec = lambda n, r: pl.BlockSpec((1, n, d), lambda bi, i: (bi, r(bi, i), 0))
    mod_spec = lambda piece: pl.BlockSpec((1, 1, d), lambda bi, i: (bi, 0, piece))
    const = lambda arr: pl.BlockSpec(arr.shape, lambda bi, i: (0,) * arr.ndim)
    resident = lambda arr: pl.BlockSpec(arr.shape, lambda bi, i: (0,) * arr.ndim, pipeline_mode=pl.Buffered(1))
    scratch = [((ext, d), BF16), ((ext, d), BF16), ((tm, d), F32), ((tm, D_FF), BF16)]
    weights = _nbytes(w_out.shape, BF16) + _nbytes(w_up.shape, BF16) + _nbytes(w_down.shape, BF16)
    blk = (2 * _nbytes((tm, d), F32) + 2 * _nbytes((tm, d), BF16) + 4 * _nbytes((ext, FF_CHUNK), F32)
           + (weights + sum(_nbytes(s, dt) for s, dt in scratch)) // 2)
    return pl.pallas_call(
        _mixffn_body,
        grid=(b, l // tm),
        in_specs=(rows3(ym_spec) + rows3(ya_spec) + rows3(x_spec)
                  + [resident(w_out), mod_spec(2), const(norm_gain), mod_spec(3), mod_spec(4), mod_spec(5),
                     resident(w_up), const(conv_w), const(conv_b), resident(w_down), const(final_norm)]),
        out_specs=pl.BlockSpec((1, tm, d), lambda bi, i: (bi, i, 0)),
        out_shape=jax.ShapeDtypeStruct((b, l, d), F32),
        scratch_shapes=[pltpu.VMEM(s, dt) for s, dt in scratch],
        compiler_params=_params(blk, 2),
        name="mixffn",
    )(ym, ym, ym, ya, ya, ya, x, x, x, w_out, mod3, norm_gain, mod3, mod3, mod3,
      w_up, conv_w, conv_b, w_down, final_norm)


def _rope_tables(n_tokens):
    pos = jnp.arange(n_tokens)
    r = (pos // GRID_W).astype(F32)
    c = (pos % GRID_W).astype(F32)
    inv = ROPE_BASE ** (-jnp.arange(ROPE_AXIS_PAIRS, dtype=F32) / ROPE_AXIS_PAIRS)
    ar, ac = r[:, None] * inv, c[:, None] * inv
    zero = jnp.zeros_like(ar)
    cos = jnp.concatenate([jnp.cos(ar), jnp.cos(ar), jnp.cos(ac), jnp.cos(ac)], axis=1)
    sina = jnp.concatenate([-jnp.sin(ar), zero, -jnp.sin(ac), zero], axis=1)
    sinb = jnp.concatenate([zero, jnp.sin(ar), zero, jnp.sin(ac)], axis=1)
    rep = V7X_LANES // ATTN_DH
    return tuple(jnp.tile(tb, (1, rep)) for tb in (cos, sina, sinb))


def _identity_rope(n_tokens):
    return (jnp.ones((n_tokens, V7X_LANES), F32), jnp.zeros((n_tokens, V7X_LANES), F32),
            jnp.zeros((n_tokens, V7X_LANES), F32))


def _gate_table(gates_ctx, gates_x):
    g = jnp.concatenate([gates_ctx, gates_x], axis=2)
    b, _, n = g.shape
    nc = n // MCHUNK
    g = g.reshape(b, 4, MLSTM_HEADS, nc, MCHUNK).transpose(0, 2, 1, 3, 4)
    g = jnp.pad(g, ((0, 0), (0, 0), (0, 0), (0, MAX_CHUNKS - nc), (0, 0)))
    return g.reshape(b, MLSTM_HEADS, 4 * MAX_CHUNKS, MCHUNK)


def kernel(x, c, ctx, c_ctx, w_ada, b_ada, norm_mix, norm_ffn, w_in, gate_b, qk_conv_w, qk_conv_b, mlstm_norm,
           attn_sink, w_out, w_up, ffn_conv_w, ffn_conv_b, w_down, final_norm):
    b, l, d = x.shape
    n_ctx = ctx.shape[1]
    assert w_ada.shape[0] == 1, "single-layer stack"
    assert l % ROW_TILE == 0 and l % MCHUNK == 0 and n_ctx % MCHUNK == 0 and l >= BAND
    assert l % (ATTN_BLOCKS_PER_STEP * ATTN_BLOCK) == 0 and MCHUNK == V7X_LANES
    assert (l + n_ctx) // MCHUNK <= MAX_CHUNKS

    rows = -(-(b + 1) // V7X_SUBLANES) * V7X_SUBLANES
    cvec = jnp.zeros((rows, d), F32).at[:b].set(c).at[b].set(c_ctx)
    mod = _ada(cvec, w_ada[0], b_ada[0][None, :])
    mod_x = mod[:b].reshape(b, 1, 6 * d)
    mod_c = mod[b:b + 1].reshape(1, 1, 6 * d)

    w_in = jnp.swapaxes(w_in, 1, 2)
    o0, o1, o2 = 2 * MLSTM_W, 3 * MLSTM_W, 4 * MLSTM_W
    o3 = o2 + N_GATES
    o4 = o3 + ATTN_W
    gb = gate_b[0][:, None]
    gain_mix = norm_mix[0][None, :]
    conv = (qk_conv_w[0], qk_conv_b[0][None, :])
    k_mc, v_mc, gates_c, kv_ac = _inproj(
        ctx, gain_mix, mod_c, 0, 1, False, _identity_rope(n_ctx), gb, *conv, w_in,
        [("conv_k", MLSTM_W, MLSTM_W), ("heads", o0, MLSTM_W), ("gates", o2, N_GATES), ("plain", o4, 2 * KV_W)],
        "inproj_ctx")
    qk_m, vo_m, gates_x, q_a, kv_a = _inproj(
        x, gain_mix, mod_x, 0, 1, True, _rope_tables(l), gb, *conv, w_in,
        [("conv_qk", 0, o0), ("heads", o0, o0), ("gates", o2, N_GATES), ("rope_q", o3, ATTN_W),
         ("rope_kv", o4, 2 * KV_W)], "inproj_x")

    y_m = _mlstm(qk_m, vo_m, k_mc, v_mc, _gate_table(gates_c, gates_x), mlstm_norm[0][None, :])
    y_a = _attn(attn_sink[0], q_a, kv_a, kv_ac)

    return _mixffn(y_m, y_a, x, mod_x, w_out[0].astype(BF16), norm_ffn[0][None, :], w_up[0].astype(BF16),
                   ffn_conv_w[0], ffn_conv_b[0][None, :], w_down[0].astype(BF16), final_norm[None, :])
```

```python
import functools

import jax
import jax.numpy as jnp
from jax import lax
from jax.experimental import pallas as pl
from jax.experimental.pallas import tpu as pltpu

F32 = jnp.float32
BF16 = jnp.bfloat16

D_MODEL = 1024
GRID_W = 64
MLSTM_HEADS = 4
MLSTM_DH = 128
MLSTM_W = MLSTM_HEADS * MLSTM_DH
N_GATES = 4 * MLSTM_HEADS
ATTN_HEADS = 8
KV_HEADS = 2
ATTN_DH = 64
ATTN_W = ATTN_HEADS * ATTN_DH
KV_W = KV_HEADS * ATTN_DH
GQA_GROUP = ATTN_HEADS // KV_HEADS
WINDOW = 128
ROPE_BASE = 10000.0
ROPE_AXIS_PAIRS = ATTN_DH // 4
D_FF = 2816
EPS = 1e-6
LOG2E = 1.4426950408889634

V7X_LANES = 128
V7X_SUBLANES = 8
V7X_VMEM_BYTES = 64 * 1024 * 1024

ROW_TILE = 512
ADA_COL_TILE = 1536
PROJ_CHUNK = 512
MCHUNK = 128
MAX_CHUNKS = 32
ATTN_BLOCK = 128
ATTN_BLOCKS_PER_STEP = 4
BAND = 3 * ATTN_BLOCK
FF_CHUNK = 512
HALO = 2 * V7X_SUBLANES


def _vmem_limit(block_bytes):
    return int(min(V7X_VMEM_BYTES * 7 // 8, 2 * block_bytes + 16 * 1024 * 1024))


def _params(block_bytes, n_axes):
    return pltpu.CompilerParams(dimension_semantics=("arbitrary",) * n_axes,
                                vmem_limit_bytes=_vmem_limit(block_bytes))


def _nbytes(shape, dtype):
    n = 1
    for s in shape:
        n *= s
    return n * jnp.dtype(dtype).itemsize


def _ada_body(c_ref, w_ref, b_ref, o_ref):
    c = c_ref[...]
    s = (c * jax.nn.sigmoid(c)).astype(BF16)
    o_ref[...] = jnp.dot(s, w_ref[...].astype(BF16), preferred_element_type=F32) + b_ref[...]


def _ada(cvec, w_ada, b_ada):
    rows, d = cvec.shape
    n = w_ada.shape[1]
    blk = _nbytes((d, ADA_COL_TILE), F32) + _nbytes((rows, d), F32) + 2 * _nbytes((rows, ADA_COL_TILE), F32)
    return pl.pallas_call(
        _ada_body,
        grid=(n // ADA_COL_TILE,),
        in_specs=[pl.BlockSpec((rows, d), lambda j: (0, 0)),
                  pl.BlockSpec((d, ADA_COL_TILE), lambda j: (0, j)),
                  pl.BlockSpec((1, ADA_COL_TILE), lambda j: (0, j))],
        out_specs=pl.BlockSpec((rows, ADA_COL_TILE), lambda j: (0, j)),
        out_shape=jax.ShapeDtypeStruct((rows, n), F32),
        compiler_params=_params(blk, 1),
        name="ada",
    )(cvec, w_ada, b_ada)


def _norm_modulate(x, gain, shift, scale):
    ms = jnp.mean(x * x, axis=-1, keepdims=True)
    return (x * lax.rsqrt(ms + EPS) * gain) * (1.0 + scale) + shift


def _silu(y):
    h = 0.5 * y
    return h + h * jnp.tanh(h)


def _inproj_body(x_ref, xp_ref, xn_ref, gain_ref, sh_ref, sc_ref, cos_ref, sina_ref, sinb_ref, gb_ref,
                 cw_ref, cb_ref, win_ref, *refs, groups):
    n = len(groups)
    o_refs, hs, w_refs = refs[:n], refs[n], refs[n + 1:]
    kinds = [g[0] for g in groups]

    @pl.when((pl.program_id(0) == 0) & (pl.program_id(1) == 0))
    def _():
        for (_, col0, width), w_bf in zip(groups, w_refs):
            w_bf[...] = win_ref[pl.ds(col0, width), :].astype(BF16)

    proj = lambda lhs, w_bf: lax.dot_general(lhs, w_bf[...], (((1,), (1,)), ((), ())), preferred_element_type=F32)

    i = pl.program_id(1)
    tm = x_ref.shape[1]
    ext = tm + 2 * HALO
    norm = lambda rows: _norm_modulate(rows, gain_ref[...], sh_ref[0], sc_ref[0]).astype(BF16)
    half = tm // 2 if tm >= ROW_TILE else tm

    def heads_half(r0):
        for kind, w_ref, o_ref in zip(kinds, w_refs, o_refs):
            if kind == "heads":
                for s0 in range(0, w_ref.shape[0], PROJ_CHUNK):
                    u = proj(hs[pl.ds(HALO + r0, half), :], w_ref.at[pl.ds(s0, PROJ_CHUNK)])
                    for j in range(PROJ_CHUNK // V7X_LANES):
                        o_ref[0, s0 // V7X_LANES + j, pl.ds(r0, half), :] = (
                            u[:, j * V7X_LANES:(j + 1) * V7X_LANES].astype(BF16))

    hs[pl.ds(HALO, half), :] = norm(x_ref[0, pl.ds(0, half), :])
    heads_half(0)
    if half < tm:
        hs[pl.ds(HALO + half, half), :] = norm(x_ref[0, pl.ds(half, half), :])
    h_lo, h_hi = norm(xp_ref[0]), norm(xn_ref[0])
    hs[pl.ds(0, HALO), :] = jnp.where(i > 0, h_lo, jnp.zeros_like(h_lo))
    hs[pl.ds(HALO + tm, HALO), :] = jnp.where(i < pl.num_programs(1) - 1, h_hi, jnp.zeros_like(h_hi))
    if half < tm:
        heads_half(half)
    hb = hs[pl.ds(HALO, tm), :]
    def conv_item(w_ref, o_ref, s0, c0, first_k):
        def epilogue(a):
            cc = pl.ds(c0 + s0, PROJ_CHUNK)
            a_prev = pltpu.roll(a, 1, 0)[HALO:HALO + tm]
            a_next = pltpu.roll(a, ext - 1, 0)[HALO:HALO + tm]
            y = _silu(cw_ref[0:1, cc] * a_prev + cw_ref[1:2, cc] * a[HALO:HALO + tm]
                      + cw_ref[2:3, cc] * a_next + cb_ref[:, cc])
            for j in range(PROJ_CHUNK // V7X_LANES):
                slab = s0 // V7X_LANES + j
                u = y[:, j * V7X_LANES:(j + 1) * V7X_LANES]
                o_ref[0, slab] = (u * MLSTM_DH ** -0.5 if slab >= first_k else u).astype(BF16)
        return (lambda: proj(hs[...], w_ref.at[pl.ds(s0, PROJ_CHUNK)])), epilogue

    def gates_item(w_ref, o_ref):
        def epilogue(u):
            o_ref[0] = u + gb_ref[...]
        return (lambda: lax.dot_general(w_ref[...], hb, (((1,), (1,)), ((), ())), preferred_element_type=F32)), epilogue

    def rope_item(w_ref, o_ref, kind):
        def epilogue(u_all):
            n_groups = u_all.shape[1] // V7X_LANES
            n_rot = {"rope_q": n_groups, "rope_kv": KV_W // V7X_LANES, "plain": 0}[kind]
            for j in range(n_groups):
                u = u_all[:, j * V7X_LANES:(j + 1) * V7X_LANES]
                if j < n_rot:
                    u = (u * cos_ref[...]
                         + pltpu.roll(u, V7X_LANES - ROPE_AXIS_PAIRS, 1) * sina_ref[...]
                         + pltpu.roll(u, ROPE_AXIS_PAIRS, 1) * sinb_ref[...])
                    if kind == "rope_q":
                        u = u * (ATTN_DH ** -0.5 * LOG2E)
                o_ref[0, :, pl.ds(j * V7X_LANES, V7X_LANES)] = u.astype(BF16)
        return (lambda: proj(hb, w_ref)), epilogue

    items = []
    for kind, w_ref, o_ref in zip(kinds, w_refs, o_refs):
        if kind in ("conv_qk", "conv_k"):
            width = w_ref.shape[0]
            c0 = cw_ref.shape[1] - width
            items += [conv_item(w_ref, o_ref, s0, c0, width // V7X_LANES - MLSTM_HEADS)
                      for s0 in range(0, width, PROJ_CHUNK)]
        elif kind == "heads":
            pass
        elif kind == "gates":
            items.append(gates_item(w_ref, o_ref))
        else:
            items.append(rope_item(w_ref, o_ref, kind))
    for matmul, epilogue in items:
        epilogue(matmul())


def _inproj(x, gain, mod3, shift_idx, scale_idx, per_batch_mod, rope, gate_b, conv_w, conv_b, w_in, groups, name):
    b, n, d = x.shape
    tm = min(ROW_TILE, n)
    per = tm // HALO
    last = n // HALO - 1
    cos, sina, sinb = rope
    mod_spec = lambda piece: pl.BlockSpec(
        (1, 1, d), (lambda bi, i: (bi, 0, piece)) if per_batch_mod else (lambda bi, i: (0, 0, piece)))
    in_specs = [pl.BlockSpec((1, tm, d), lambda bi, i: (bi, i, 0)),
                pl.BlockSpec((1, HALO, d), lambda bi, i: (bi, jnp.maximum(i * per - 1, 0), 0)),
                pl.BlockSpec((1, HALO, d), lambda bi, i: (bi, jnp.minimum((i + 1) * per, last), 0)),
                pl.BlockSpec((1, d), lambda bi, i: (0, 0)),
                mod_spec(shift_idx), mod_spec(scale_idx),
                pl.BlockSpec((tm, V7X_LANES), lambda bi, i: (i, 0)),
                pl.BlockSpec((tm, V7X_LANES), lambda bi, i: (i, 0)),
                pl.BlockSpec((tm, V7X_LANES), lambda bi, i: (i, 0)),
                pl.BlockSpec((N_GATES, 1), lambda bi, i: (0, 0)),
                pl.BlockSpec(conv_w.shape, lambda bi, i: (0, 0)),
                pl.BlockSpec(conv_b.shape, lambda bi, i: (0, 0)),
                pl.BlockSpec((None,) + w_in.shape[1:], lambda bi, i: (0, 0, 0), pipeline_mode=pl.Buffered(1))]
    out_specs, out_shapes = [], []
    scratch = [((tm + 2 * HALO, d), BF16)]
    blk = (_nbytes((tm, d), F32) + 3 * _nbytes((tm, V7X_LANES), F32) + 4 * _nbytes((tm + 2 * HALO, 2 * MLSTM_W), F32)
           + _nbytes(w_in.shape, F32) // 2)
    for kind, _, width in groups:
        if kind == "gates":
            scratch.append(((width, d), BF16))
            out_specs.append(pl.BlockSpec((1, N_GATES, tm), lambda bi, i: (bi, 0, i)))
            out_shapes.append(jax.ShapeDtypeStruct((b, N_GATES, n), F32))
            continue
        scratch.append(((width, d), BF16))
        if kind in ("heads", "conv_qk", "conv_k"):
            slabs = width // V7X_LANES
            out_specs.append(pl.BlockSpec((1, slabs, tm, V7X_LANES), lambda bi, i: (bi, 0, i, 0)))
            out_shapes.append(jax.ShapeDtypeStruct((b, slabs, n, V7X_LANES), BF16))
        else:
            out_specs.append(pl.BlockSpec((1, tm, width), lambda bi, i: (bi, i, 0)))
            out_shapes.append(jax.ShapeDtypeStruct((b, n, width), BF16))
        blk += _nbytes((tm, width), BF16) + _nbytes((tm, width), F32)
    blk += sum(_nbytes(s, dt) for s, dt in scratch) // 2
    return pl.pallas_call(
        functools.partial(_inproj_body, groups=tuple(groups)),
        grid=(b, n // tm),
        in_specs=in_specs,
        out_specs=out_specs,
        out_shape=out_shapes,
        scratch_shapes=[pltpu.VMEM(s, dt) for s, dt in scratch],
        compiler_params=_params(blk, 2),
        name=name,
    )(x, x, x, gain, mod3, mod3, cos, sina, sinb, gate_b, conv_w, conv_b, w_in)


def _sigmoid(y):
    return 0.5 + 0.5 * jnp.tanh(0.5 * y)


def _log_sigmoid(x):
    return jnp.minimum(x, 0.0) - jnp.log1p(jnp.exp(-jnp.abs(x)))


def _mlstm_body(qs, ks, v_ref, o_ref, kcs, vc_ref, gt_ref, gain_ref, wout_ref, wup_ref, wdn_ref,
                y_ref, wout_bf, wup_bf, wdn_bf,
                gtab, rtab, cumtab, kts, cnl, c0n, cumc, sqk, sds, inter_s, em_s, svq,
                *, n_ctx_chunks, n_lat_chunks):
    t = MCHUNK
    dh = MLSTM_DH
    n_chunks = n_ctx_chunks + n_lat_chunks
    nt = (((1,), (1,)), ((), ()))

    wout_bf[...] = wout_ref[...].astype(BF16)
    wup_bf[...] = wup_ref[...].astype(BF16)
    wdn_bf[...] = wdn_ref[...].astype(BF16)

    tbl = gt_ref[0, 0]
    rid = lax.broadcasted_iota(jnp.int32, tbl.shape, 0) // MAX_CHUNKS
    lane = lax.broadcasted_iota(jnp.int32, tbl.shape, 1)
    lf = _log_sigmoid(tbl)
    pre = jnp.where(rid == 1, lf, 0.0)
    suf = jnp.where(rid == 3, lf, 0.0)
    s = 1
    while s < t:
        pre = pre + jnp.where(lane >= s, pltpu.roll(pre, s, 1), 0.0)
        suf = suf + jnp.where(lane < t - s, pltpu.roll(suf, t - s, 1), 0.0)
        s *= 2
    li = (tbl[0:MAX_CHUNKS] * LOG2E, tbl[2 * MAX_CHUNKS:3 * MAX_CHUNKS] * LOG2E)
    cum = (pre[MAX_CHUNKS:2 * MAX_CHUNKS] * LOG2E, suf[3 * MAX_CHUNKS:4 * MAX_CHUNKS] * LOG2E)
    total = (cum[0][:, t - 1:t], cum[1][:, 0:1])
    mloc = []
    for d in (0, 1):
        g = total[d] - cum[d] + li[d]
        mloc.append(jnp.max(g, axis=1, keepdims=True))
        gtab[d] = g
        rtab[d] = cum[d] - li[d]
        cumtab[d] = cum[d]

    orders = (list(range(n_chunks)),
              list(range(n_ctx_chunks - 1, -1, -1)) + list(range(n_chunks - 1, n_ctx_chunks - 1, -1)))

    a_sc, m0, m1 = ({}, {}), ({}, {}), ({}, {})
    for d in (0, 1):
        m_st = jnp.zeros((1, 1), F32)
        for c in orders[d][:-1]:
            m0[d][c] = m_st
            tot_c = total[d][c:c + 1, :]
            m1[d][c] = jnp.maximum(tot_c + m_st, mloc[d][c:c + 1, :])
            a_sc[d][c] = jnp.exp2(tot_c + m_st - m1[d][c])
            m_st = m1[d][c]
        m0[d][orders[d][-1]] = m_st

    def kv_chunk(c):
        if c < n_ctx_chunks:
            rows = pl.ds(c * t, t)
            return kcs[rows, :], vc_ref[rows, :]
        rows = pl.ds((c - n_ctx_chunks) * t, t)
        return ks[rows, :], v_ref[rows, :]

    ri = lax.broadcasted_iota(jnp.int32, (t, t), 0)
    ci = lax.broadcasted_iota(jnp.int32, (t, t), 1)
    eye = jnp.where(ri == ci, 1.0, 0.0).astype(BF16)
    ones = jnp.ones((t, dh), BF16)

    for c in range(n_chunks):
        kts[c] = lax.dot_general(eye, kv_chunk(c)[0], nt, preferred_element_type=F32)
    for c in range(n_chunks):
        vo = jnp.concatenate([kv_chunk(c)[1], ones], axis=1)
        for d in (0, 1):
            if c != orders[d][-1]:
                w = jnp.exp2(gtab[d, pl.ds(c, 1), :] - m1[d][c])
                cnl[d, c] = jnp.dot((kts[c] * w).astype(BF16), vo, preferred_element_type=F32)

    for d in (0, 1):
        cn = jnp.zeros((dh, 2 * dh), F32)
        for c in orders[d]:
            if c >= n_ctx_chunks:
                c0n[d, c - n_ctx_chunks] = cn.astype(BF16)
            if c != orders[d][-1]:
                cn = a_sc[d][c] * cn + cnl[d, c]

    for j in range(n_lat_chunks):
        c = j + n_ctx_chunks
        rows = pl.ds(j * t, t)
        sqk[rows, :] = lax.dot_general(qs[rows, :], ks[rows, :], nt, preferred_element_type=F32)
        for d in (0, 1):
            cumc[d, rows, :] = jnp.broadcast_to(cumtab[d, pl.ds(c, 1), :], (t, t)).T

    grp = 2
    gi = lax.broadcasted_iota(jnp.int32, (grp * t, t), 0) & (t - 1)
    gs = lax.broadcasted_iota(jnp.int32, (grp * t, t), 1)
    for d in (0, 1):
        ok = (gs <= gi) if d == 0 else (gs >= gi)
        for j0 in range(0, n_lat_chunks, grp):
            rows = pl.ds(j0 * t, grp * t)
            cs = [j + n_ctx_chunks for j in range(j0, j0 + grp)]
            rb = jnp.concatenate([jnp.broadcast_to(rtab[d, pl.ds(c, 1), :], (t, t)) for c in cs], axis=0)
            m0b = jnp.concatenate([jnp.broadcast_to(m0[d][c], (t, t)) for c in cs], axis=0)
            cum_c = cumc[d, rows, :]
            d_log = jnp.where(ok, cum_c - rb, -jnp.inf)
            a_log = cum_c + m0b
            m_t = jnp.maximum(a_log, jnp.max(d_log, axis=1, keepdims=True))
            sds[d, rows, :] = sqk[rows, :].astype(BF16) * jnp.exp2((d_log - m_t).astype(BF16))
            inter_s[d, rows, :] = jnp.exp2(a_log - m_t)
            em_s[d, rows, :] = jnp.exp2(-m_t)

    for j in range(n_lat_chunks):
        rows = pl.ds(j * t, t)
        qf = qs[rows, :].astype(F32)
        vo = jnp.concatenate([v_ref[rows, :], ones], axis=1)
        for d in (0, 1):
            lhs = jnp.concatenate([sds[d, rows, :], (qf * inter_s[d, rows, :]).astype(BF16)], axis=1)
            rhs = jnp.concatenate([vo, c0n[d, j]], axis=0)
            svq[d, rows, :] = jnp.dot(lhs, rhs, preferred_element_type=F32)

    gain = gain_ref[...]
    grp = 4
    for j0 in range(0, n_lat_chunks, grp):
        rows = pl.ds(j0 * t, grp * t)
        hid = None
        for d in (0, 1):
            h_d = svq[d, rows, pl.ds(0, dh)] / jnp.maximum(jnp.abs(svq[d, rows, pl.ds(dh, dh)]), em_s[d, rows, :])
            hid = h_d if hid is None else hid + h_d
        hn = hid * lax.rsqrt(jnp.mean(hid * hid, axis=-1, keepdims=True) + EPS) * gain
        y_ref[rows, :] = (hn * _sigmoid(o_ref[rows, :].astype(F32))).astype(BF16)


def _mlstm(qk, vo, k_ctx, v_ctx, gate_tbl, gain, cast_weights):
    b, _, l, _ = qk.shape
    n_ctx = k_ctx.shape[2]
    dh = MLSTM_DH
    nh = MLSTM_HEADS
    n_ctx_chunks, n_lat_chunks = n_ctx // MCHUNK, l // MCHUNK
    seq = lambda slab0: pl.BlockSpec((None, None, l, dh), lambda bi, h: (bi, slab0 + h, 0, 0))
    ctx = lambda slab0: pl.BlockSpec((None, None, n_ctx, dh), lambda bi, h: (bi, slab0 + h, 0, 0))
    n_chunks = n_ctx_chunks + n_lat_chunks
    scratch = [((2, MAX_CHUNKS, MCHUNK), F32), ((2, MAX_CHUNKS, MCHUNK), F32),
               ((2, MAX_CHUNKS, MCHUNK), F32), ((n_chunks, dh, MCHUNK), F32),
               ((2, n_chunks, dh, 2 * dh), F32), ((2, n_lat_chunks, dh, 2 * dh), BF16),
               ((2, l, MCHUNK), F32), ((l, MCHUNK), F32), ((2, l, MCHUNK), BF16),
               ((2, l, MCHUNK), F32), ((2, l, MCHUNK), F32), ((2, l, 2 * dh), F32)]
    blk = (5 * _nbytes((l, dh), BF16) + 2 * _nbytes((n_ctx, dh), BF16) + _nbytes((V7X_LANES, V7X_LANES), F32)
           + sum(_nbytes(s, dt) for s, dt in scratch) // 2)
    steps = b * nh
    w_in_specs, w_out_specs, w_out_shapes = [], [], []
    for w in cast_weights:
        _, r, c = w.shape
        n_slabs = max(n for n in range(1, steps + 1) if r % n == 0 and (r // n) % (2 * V7X_SUBLANES) == 0)
        slab = lambda bi, h, n_slabs=n_slabs: jnp.minimum(bi * nh + h, n_slabs - 1)
        w_in_specs.append(pl.BlockSpec((None, r // n_slabs, c), lambda bi, h, slab=slab: (0, slab(bi, h), 0)))
        w_out_specs.append(pl.BlockSpec((r // n_slabs, c), lambda bi, h, slab=slab: (slab(bi, h), 0)))
        w_out_shapes.append(jax.ShapeDtypeStruct((r, c), BF16))
        blk += _nbytes((r // n_slabs, c), F32) + _nbytes((r // n_slabs, c), BF16)
    return pl.pallas_call(
        functools.partial(_mlstm_body, n_ctx_chunks=n_ctx_chunks, n_lat_chunks=n_lat_chunks),
        grid=(b, nh),
        in_specs=[seq(0), seq(nh), seq(0), seq(nh), ctx(0), ctx(0),
                  pl.BlockSpec((1, 1, 4 * MAX_CHUNKS, MCHUNK), lambda bi, h: (bi, h, 0, 0)),
                  pl.BlockSpec((1, dh), lambda bi, h: (0, h))] + w_in_specs,
        out_specs=[pl.BlockSpec((None, None, l, dh), lambda bi, h: (bi, h, 0, 0))] + w_out_specs,
        out_shape=[jax.ShapeDtypeStruct((b, nh, l, dh), BF16)] + w_out_shapes,
        scratch_shapes=[pltpu.VMEM(s, dt) for s, dt in scratch],
        compiler_params=_params(blk, 2),
        name="mlstm",
    )(qk, qk, vo, vo, k_ctx, v_ctx, gate_tbl, gain, *cast_weights)


def _attn_body(sink_ref, bias_ref, q_ref, kv_ref, kvc_ref, o_ref, s_scr, p_scr, e_scr, *, seq_len):
    blk = ATTN_BLOCK
    n_blocks = seq_len // blk
    n_ctx = kvc_ref.shape[1]
    rows = GQA_GROUP * blk
    nt = (((1,), (1,)), ((), ()))
    chains = [(qb, g) for qb in range(ATTN_BLOCKS_PER_STEP) for g in range(KV_HEADS)]

    def block_start(qb):
        i = pl.program_id(1) * ATTN_BLOCKS_PER_STEP + qb
        return i, pl.multiple_of(jnp.clip((i - 1) * blk, 0, seq_len - BAND), blk)

    for ci, (qb, g) in enumerate(chains):
        i, start = block_start(qb)
        k_cols = pl.ds(g * ATTN_DH, ATTN_DH)
        q = jnp.concatenate([q_ref[0, pl.ds(qb * blk, blk), pl.ds((g * GQA_GROUP + j) * ATTN_DH, ATTN_DH)]
                             for j in range(GQA_GROUP)], axis=0)
        bias = bias_ref[jnp.where(i == 0, 0, jnp.where(i == n_blocks - 1, 2, 1))]
        s_scr[ci, :, pl.ds(0, BAND)] = (
            lax.dot_general(q, kv_ref[0, pl.ds(start, BAND), k_cols], nt, preferred_element_type=F32) + bias)
        s_scr[ci, :, pl.ds(BAND, n_ctx)] = lax.dot_general(q, kvc_ref[0, :, k_cols], nt, preferred_element_type=F32)

    for ci, (qb, g) in enumerate(chains):
        s = s_scr[ci]
        sink = jnp.concatenate([jnp.full((blk, V7X_LANES), sink_ref[g * GQA_GROUP + j] * LOG2E, F32)
                                for j in range(GQA_GROUP)], axis=0)
        m = jnp.maximum(jnp.broadcast_to(jnp.max(s, axis=1, keepdims=True), (rows, V7X_LANES)), sink)
        p_scr[ci] = jnp.exp2((s - jnp.tile(m, (1, (BAND + n_ctx) // V7X_LANES))).astype(BF16))
        e_scr[ci] = jnp.exp2(sink - m)

    ones_b = jnp.ones((BAND, ATTN_DH), BF16)
    ones_c = jnp.ones((n_ctx, ATTN_DH), BF16)
    for ci, (qb, g) in enumerate(chains):
        _, start = block_start(qb)
        v_cols = pl.ds(KV_W + g * ATTN_DH, ATTN_DH)
        vb = jnp.concatenate([kv_ref[0, pl.ds(start, BAND), v_cols], ones_b], axis=1)
        vc = jnp.concatenate([kvc_ref[0, :, v_cols], ones_c], axis=1)
        acc = (jnp.dot(p_scr[ci, :, pl.ds(0, BAND)], vb, preferred_element_type=F32)
               + jnp.dot(p_scr[ci, :, pl.ds(BAND, n_ctx)], vc, preferred_element_type=F32))
        out = (acc / (pltpu.roll(acc, ATTN_DH, 1) + e_scr[ci]))[:, :ATTN_DH]
        for j in range(GQA_GROUP):
            h = g * GQA_GROUP + j
            o_ref[0, pl.ds(qb * blk, blk), pl.ds(h * ATTN_DH, ATTN_DH)] = out[j * blk:(j + 1) * blk].astype(BF16)


def _band_bias(seq_len):
    row = jnp.arange(ATTN_BLOCK)[:, None]
    col = jnp.arange(BAND)[None, :]
    n_blocks = seq_len // ATTN_BLOCK
    tables = []
    for i in (0, 1, n_blocks - 1):
        start = min(max((i - 1) * ATTN_BLOCK, 0), seq_len - BAND)
        ok = jnp.abs(start + col - (i * ATTN_BLOCK + row)) <= WINDOW
        tables.append(jnp.tile(jnp.where(ok, 0.0, -jnp.inf).astype(F32), (GQA_GROUP, 1)))
    return jnp.stack(tables)


def _attn(sink, q, kv, kvc):
    b, l, _ = q.shape
    n_ctx = kvc.shape[1]
    rows = ATTN_BLOCKS_PER_STEP * ATTN_BLOCK
    bias = _band_bias(l)
    n_chains = KV_HEADS * ATTN_BLOCKS_PER_STEP
    stacked = GQA_GROUP * ATTN_BLOCK
    scratch = [((n_chains, stacked, BAND + n_ctx), F32), ((n_chains, stacked, BAND + n_ctx), BF16),
               ((n_chains, stacked, V7X_LANES), F32)]
    blk = (2 * _nbytes((rows, ATTN_W), BF16) + _nbytes((l, 2 * KV_W), BF16) + _nbytes((n_ctx, 2 * KV_W), BF16)
           + _nbytes(bias.shape, F32) + sum(_nbytes(s, dt) for s, dt in scratch) // 2)
    return pl.pallas_call(
        functools.partial(_attn_body, seq_len=l),
        grid=(b, l // rows),
        in_specs=[pl.BlockSpec(memory_space=pltpu.SMEM),
                  pl.BlockSpec(bias.shape, lambda bi, i: (0, 0, 0)),
                  pl.BlockSpec((1, rows, ATTN_W), lambda bi, i: (bi, i, 0)),
                  pl.BlockSpec((1, l, 2 * KV_W), lambda bi, i: (bi, 0, 0)),
                  pl.BlockSpec((1, n_ctx, 2 * KV_W), lambda bi, i: (bi, 0, 0))],
        out_specs=pl.BlockSpec((1, rows, ATTN_W), lambda bi, i: (bi, i, 0)),
        out_shape=jax.ShapeDtypeStruct((b, l, ATTN_W), BF16),
        scratch_shapes=[pltpu.VMEM(s, dt) for s, dt in scratch],
        compiler_params=_params(blk, 2),
        name="attn",
    )(sink, bias, q, kv, kvc)


GELU_C1 = 0.7978845608028654
GELU_C2 = GELU_C1 * 0.044715


def _twice_gelu(x):
    return x + x * jnp.tanh(x * (GELU_C1 + GELU_C2 * (x * x)))


def _mixffn_body(ym_ref, ymp_ref, ymn_ref, ya_ref, yap_ref, yan_ref, x_ref, xp_ref, xn_ref,
                 wo_ref, ga_ref, gain_ref, sh_ref, sc_ref, gf_ref, wup_ref, cw_ref, cb_ref, wdn_ref, fn_ref,
                 o_ref, ys, hs, x1s, gated):
    i = pl.program_id(1)
    tm = x_ref.shape[1]
    ext = tm + 2 * HALO
    lo, mid, hi = pl.ds(0, HALO), pl.ds(HALO, tm), pl.ds(HALO + tm, HALO)
    for rows, m_ref, a_ref in ((lo, ymp_ref, yap_ref), (mid, ym_ref, ya_ref), (hi, ymn_ref, yan_ref)):
        for h in range(MLSTM_HEADS):
            ys[rows, pl.ds(h * MLSTM_DH, MLSTM_DH)] = m_ref[0, h]
        ys[rows, pl.ds(MLSTM_W, ATTN_W)] = a_ref[0]
    ga = ga_ref[0]
    half = tm // 2

    def mixed(x_rows, proj_rows):
        x1 = x_rows + ga * proj_rows
        return x1, _norm_modulate(x1, gain_ref[...], sh_ref[0], sc_ref[0]).astype(BF16)

    proj = jnp.dot(ys[pl.ds(0, HALO + half), :], wo_ref[...], preferred_element_type=F32)
    x1, h_top = mixed(x_ref[0, pl.ds(0, half), :], proj[HALO:])
    x1s[pl.ds(0, half), :] = x1
    hs[pl.ds(HALO, half), :] = h_top
    h_lo = mixed(xp_ref[0], proj[:HALO])[1]
    hs[lo, :] = jnp.where(i > 0, h_lo, jnp.zeros_like(h_lo))
    proj = jnp.dot(ys[pl.ds(HALO + half, half + HALO), :], wo_ref[...], preferred_element_type=F32)
    x1, h_bot = mixed(x_ref[0, pl.ds(half, half), :], proj[:half])
    x1s[pl.ds(half, half), :] = x1
    hs[pl.ds(HALO + half, half), :] = h_bot
    h_hi = mixed(xn_ref[0], proj[half:])[1]
    hs[hi, :] = jnp.where(i < pl.num_programs(1) - 1, h_hi, jnp.zeros_like(h_hi))
    starts = list(range(0, D_FF, FF_CHUNK))

    def up_a(col0):
        return jnp.dot(hs[...], wup_ref[:, pl.ds(col0, min(FF_CHUNK, D_FF - col0))], preferred_element_type=F32)

    nxt = up_a(starts[0])
    for c, col0 in enumerate(starts):
        width = min(FF_CHUNK, D_FF - col0)
        cols = pl.ds(col0, width)
        a = nxt
        if c + 1 < len(starts):
            nxt = up_a(starts[c + 1])
        a_prev = pltpu.roll(a, 1, 0)[HALO:HALO + tm]
        a_next = pltpu.roll(a, ext - 1, 0)[HALO:HALO + tm]
        conv = (cw_ref[0:1, cols] * a_prev + cw_ref[1:2, cols] * a[HALO:HALO + tm]
                + cw_ref[2:3, cols] * a_next + cb_ref[:, cols])
        g2 = _twice_gelu(conv)
        val = jnp.dot(hs[mid, :], wup_ref[:, pl.ds(D_FF + col0, width)], preferred_element_type=F32)
        gated[:, cols] = (g2 * val).astype(BF16)
    for r0 in (0, half):
        rows = pl.ds(r0, half)
        x2 = x1s[rows, :] + (0.5 * gf_ref[0]) * jnp.dot(gated[rows, :], wdn_ref[...], preferred_element_type=F32)
        ms = jnp.mean(x2 * x2, axis=-1, keepdims=True)
        o_ref[0, rows, :] = x2 * lax.rsqrt(ms + EPS) * fn_ref[...]


def _mixffn(ym, ya, x, mod3, w_out, norm_gain, w_up, conv_w, conv_b, w_down, final_norm):
    b, l, d = x.shape
    tm = ROW_TILE
    per = tm // HALO
    last = l // HALO - 1
    ext = tm + 2 * HALO
    tile, before, after = (lambda bi, i: i), (lambda bi, i: jnp.maximum(i * per - 1, 0)), \
        (lambda bi, i: jnp.minimum((i + 1) * per, last))

    def rows3(make):
        return [make(tm, tile), make(HALO, before), make(HALO, after)]

    ym_spec = lambda n, r: pl.BlockSpec((1, MLSTM_HEADS, n, MLSTM_DH), lambda bi, i: (bi, 0, r(bi, i), 0))
    ya_spec = lambda n, r: pl.BlockSpec((1, n, ATTN_W), lambda bi, i: (bi, r(bi, i), 0))
    x_spec = lambda n, r: pl.BlockSpec((1, n, d), lambda bi, i: (bi, r(bi, i), 0))
    mod_spec = lambda piece: pl.BlockSpec((1, 1, d), lambda bi, i: (bi, 0, piece))
    const = lambda arr: pl.BlockSpec(arr.shape, lambda bi, i: (0,) * arr.ndim)
    resident = lambda arr: pl.BlockSpec(arr.shape, lambda bi, i: (0,) * arr.ndim, pipeline_mode=pl.Buffered(1))
    scratch = [((ext, d), BF16), ((ext, d), BF16), ((tm, d), F32), ((tm, D_FF), BF16)]
    weights = _nbytes(w_out.shape, BF16) + _nbytes(w_up.shape, BF16) + _nbytes(w_down.shape, BF16)
    blk = (2 * _nbytes((tm, d), F32) + 2 * _nbytes((tm, d), BF16) + 4 * _nbytes((ext, FF_CHUNK), F32)
           + (weights + sum(_nbytes(s, dt) for s, dt in scratch)) // 2)
    return pl.pallas_call(
        _mixffn_body,
        grid=(b, l // tm),
        in_specs=(rows3(ym_spec) + rows3(ya_spec) + rows3(x_spec)
                  + [resident(w_out), mod_spec(2), const(norm_gain), mod_spec(3), mod_spec(4), mod_spec(5),
                     resident(w_up), const(conv_w), const(conv_b), resident(w_down), const(final_norm)]),
        out_specs=pl.BlockSpec((1, tm, d), lambda bi, i: (bi, i, 0)),
        out_shape=jax.ShapeDtypeStruct((b, l, d), F32),
        scratch_shapes=[pltpu.VMEM(s, dt) for s, dt in scratch],
        compiler_params=_params(blk, 2),
        name="mixffn",
    )(ym, ym, ym, ya, ya, ya, x, x, x, w_out, mod3, norm_gain, mod3, mod3, mod3,
      w_up, conv_w, conv_b, w_down, final_norm)


def _rope_tables(n_tokens):
    pos = jnp.arange(n_tokens)
    r = (pos // GRID_W).astype(F32)
    c = (pos % GRID_W).astype(F32)
    inv = ROPE_BASE ** (-jnp.arange(ROPE_AXIS_PAIRS, dtype=F32) / ROPE_AXIS_PAIRS)
    ar, ac = r[:, None] * inv, c[:, None] * inv
    zero = jnp.zeros_like(ar)
    cos = jnp.concatenate([jnp.cos(ar), jnp.cos(ar), jnp.cos(ac), jnp.cos(ac)], axis=1)
    sina = jnp.concatenate([-jnp.sin(ar), zero, -jnp.sin(ac), zero], axis=1)
    sinb = jnp.concatenate([zero, jnp.sin(ar), zero, jnp.sin(ac)], axis=1)
    rep = V7X_LANES // ATTN_DH
    return tuple(jnp.tile(tb, (1, rep)) for tb in (cos, sina, sinb))


def _identity_rope(n_tokens):
    return (jnp.ones((n_tokens, V7X_LANES), F32), jnp.zeros((n_tokens, V7X_LANES), F32),
            jnp.zeros((n_tokens, V7X_LANES), F32))


def _gate_table(gates_ctx, gates_x):
    g = jnp.concatenate([gates_ctx, gates_x], axis=2)
    b, _, n = g.shape
    nc = n // MCHUNK
    g = g.reshape(b, 4, MLSTM_HEADS, nc, MCHUNK).transpose(0, 2, 1, 3, 4)
    g = jnp.pad(g, ((0, 0), (0, 0), (0, 0), (0, MAX_CHUNKS - nc), (0, 0)))
    return g.reshape(b, MLSTM_HEADS, 4 * MAX_CHUNKS, MCHUNK)


def kernel(x, c, ctx, c_ctx, w_ada, b_ada, norm_mix, norm_ffn, w_in, gate_b, qk_conv_w, qk_conv_b, mlstm_norm,
           attn_sink, w_out, w_up, ffn_conv_w, ffn_conv_b, w_down, final_norm):
    b, l, d = x.shape
    n_ctx = ctx.shape[1]
    assert w_ada.shape[0] == 1, "single-layer stack"
    assert l % ROW_TILE == 0 and l % MCHUNK == 0 and n_ctx % MCHUNK == 0 and l >= BAND
    assert l % (ATTN_BLOCKS_PER_STEP * ATTN_BLOCK) == 0 and MCHUNK == V7X_LANES
    assert (l + n_ctx) // MCHUNK <= MAX_CHUNKS

    rows = -(-(b + 1) // V7X_SUBLANES) * V7X_SUBLANES
    cvec = jnp.zeros((rows, d), F32).at[:b].set(c).at[b].set(c_ctx)
    mod = _ada(cvec, w_ada[0], b_ada[0][None, :])
    mod_x = mod[:b].reshape(b, 1, 6 * d)
    mod_c = mod[b:b + 1].reshape(1, 1, 6 * d)

    w_in = jnp.swapaxes(w_in, 1, 2)
    o0, o1, o2 = 2 * MLSTM_W, 3 * MLSTM_W, 4 * MLSTM_W
    o3 = o2 + N_GATES
    o4 = o3 + ATTN_W
    gb = gate_b[0][:, None]
    gain_mix = norm_mix[0][None, :]
    conv = (qk_conv_w[0], qk_conv_b[0][None, :])
    k_mc, v_mc, gates_c, kv_ac = _inproj(
        ctx, gain_mix, mod_c, 0, 1, False, _identity_rope(n_ctx), gb, *conv, w_in,
        [("conv_k", MLSTM_W, MLSTM_W), ("heads", o0, MLSTM_W), ("gates", o2, N_GATES), ("plain", o4, 2 * KV_W)],
        "inproj_ctx")
    qk_m, vo_m, gates_x, q_a, kv_a = _inproj(
        x, gain_mix, mod_x, 0, 1, True, _rope_tables(l), gb, *conv, w_in,
        [("conv_qk", 0, o0), ("heads", o0, o0), ("gates", o2, N_GATES), ("rope_q", o3, ATTN_W),
         ("rope_kv", o4, 2 * KV_W)], "inproj_x")

    y_m, w_out_bf, w_up_bf, w_down_bf = _mlstm(qk_m, vo_m, k_mc, v_mc, _gate_table(gates_c, gates_x),
                                               mlstm_norm[0][None, :], [w_out, w_up, w_down])
    y_a = _attn(attn_sink[0], q_a, kv_a, kv_ac)

    return _mixffn(y_m, y_a, x, mod_x, w_out_bf, norm_ffn[0][None, :], w_up_bf,
                   ffn_conv_w[0], ffn_conv_b[0][None, :], w_down_bf, final_norm[None, :])
```

```python
import functools

import jax
import jax.numpy as jnp
from jax import lax
from jax.experimental import pallas as pl
from jax.experimental.pallas import tpu as pltpu

F32 = jnp.float32
BF16 = jnp.bfloat16

D_MODEL = 1024
GRID_W = 64
MLSTM_HEADS = 4
MLSTM_DH = 128
MLSTM_W = MLSTM_HEADS * MLSTM_DH
N_GATES = 4 * MLSTM_HEADS
ATTN_HEADS = 8
KV_HEADS = 2
ATTN_DH = 64
ATTN_W = ATTN_HEADS * ATTN_DH
KV_W = KV_HEADS * ATTN_DH
GQA_GROUP = ATTN_HEADS // KV_HEADS
WINDOW = 128
ROPE_BASE = 10000.0
ROPE_AXIS_PAIRS = ATTN_DH // 4
D_FF = 2816
EPS = 1e-6
LOG2E = 1.4426950408889634

V7X_LANES = 128
V7X_SUBLANES = 8
V7X_VMEM_BYTES = 64 * 1024 * 1024

ROW_TILE = 512
ADA_COL_TILE = 1536
PROJ_CHUNK = 512
MCHUNK = 128
MAX_CHUNKS = 32
ATTN_BLOCK = 128
ATTN_BLOCKS_PER_STEP = 4
BAND = 3 * ATTN_BLOCK
FF_CHUNK = 256
HALO = 2 * V7X_SUBLANES


def _vmem_limit(block_bytes):
    return int(min(V7X_VMEM_BYTES * 7 // 8, 2 * block_bytes + 16 * 1024 * 1024))


def _params(block_bytes, n_axes):
    return pltpu.CompilerParams(dimension_semantics=("arbitrary",) * n_axes,
                                vmem_limit_bytes=_vmem_limit(block_bytes))


def _nbytes(shape, dtype):
    n = 1
    for s in shape:
        n *= s
    return n * jnp.dtype(dtype).itemsize


def _ada_body(c_ref, w_ref, b_ref, o_ref):
    c = c_ref[...]
    s = (c * jax.nn.sigmoid(c)).astype(BF16)
    o_ref[...] = jnp.dot(s, w_ref[...].astype(BF16), preferred_element_type=F32) + b_ref[...]


def _ada(cvec, w_ada, b_ada):
    rows, d = cvec.shape
    n = w_ada.shape[1]
    blk = _nbytes((d, ADA_COL_TILE), F32) + _nbytes((rows, d), F32) + 2 * _nbytes((rows, ADA_COL_TILE), F32)
    return pl.pallas_call(
        _ada_body,
        grid=(n // ADA_COL_TILE,),
        in_specs=[pl.BlockSpec((rows, d), lambda j: (0, 0)),
                  pl.BlockSpec((d, ADA_COL_TILE), lambda j: (0, j)),
                  pl.BlockSpec((1, ADA_COL_TILE), lambda j: (0, j))],
        out_specs=pl.BlockSpec((rows, ADA_COL_TILE), lambda j: (0, j)),
        out_shape=jax.ShapeDtypeStruct((rows, n), F32),
        compiler_params=_params(blk, 1),
        name="ada",
    )(cvec, w_ada, b_ada)


def _norm_modulate(x, gain, shift, scale):
    ms = jnp.mean(x * x, axis=-1, keepdims=True)
    return (x * lax.rsqrt(ms + EPS) * gain) * (1.0 + scale) + shift


def _silu(y):
    h = 0.5 * y
    return h + h * jnp.tanh(h)


def _inproj_body(x_ref, xp_ref, xn_ref, gain_ref, sh_ref, sc_ref, cos_ref, sina_ref, sinb_ref, gb_ref,
                 cw_ref, cb_ref, win_ref, *refs, groups):
    n = len(groups)
    o_refs, hs, w_refs = refs[:n], refs[n], refs[n + 1:]
    kinds = [g[0] for g in groups]

    @pl.when((pl.program_id(0) == 0) & (pl.program_id(1) == 0))
    def _():
        for (_, col0, width), w_bf in zip(groups, w_refs):
            w_bf[...] = win_ref[pl.ds(col0, width), :].astype(BF16)

    proj = lambda lhs, w_bf: lax.dot_general(lhs, w_bf[...], (((1,), (1,)), ((), ())), preferred_element_type=F32)

    i = pl.program_id(1)
    tm = x_ref.shape[1]
    ext = tm + 2 * HALO
    norm = lambda rows: _norm_modulate(rows, gain_ref[...], sh_ref[0], sc_ref[0]).astype(BF16)
    half = tm // 2 if tm >= ROW_TILE else tm

    def heads_half(r0):
        for kind, w_ref, o_ref in zip(kinds, w_refs, o_refs):
            if kind == "heads":
                for s0 in range(0, w_ref.shape[0], PROJ_CHUNK):
                    u = proj(hs[pl.ds(HALO + r0, half), :], w_ref.at[pl.ds(s0, PROJ_CHUNK)])
                    for j in range(PROJ_CHUNK // V7X_LANES):
                        o_ref[0, s0 // V7X_LANES + j, pl.ds(r0, half), :] = (
                            u[:, j * V7X_LANES:(j + 1) * V7X_LANES].astype(BF16))

    hs[pl.ds(HALO, half), :] = norm(x_ref[0, pl.ds(0, half), :])
    heads_half(0)
    if half < tm:
        hs[pl.ds(HALO + half, half), :] = norm(x_ref[0, pl.ds(half, half), :])
    h_lo, h_hi = norm(xp_ref[0]), norm(xn_ref[0])
    hs[pl.ds(0, HALO), :] = jnp.where(i > 0, h_lo, jnp.zeros_like(h_lo))
    hs[pl.ds(HALO + tm, HALO), :] = jnp.where(i < pl.num_programs(1) - 1, h_hi, jnp.zeros_like(h_hi))
    if half < tm:
        heads_half(half)
    hb = hs[pl.ds(HALO, tm), :]
    def conv_item(w_ref, o_ref, s0, c0, first_k):
        def epilogue(a):
            cc = pl.ds(c0 + s0, PROJ_CHUNK)
            a_prev = pltpu.roll(a, 1, 0)[HALO:HALO + tm]
            a_next = pltpu.roll(a, ext - 1, 0)[HALO:HALO + tm]
            y = _silu(cw_ref[0:1, cc] * a_prev + cw_ref[1:2, cc] * a[HALO:HALO + tm]
                      + cw_ref[2:3, cc] * a_next + cb_ref[:, cc])
            for j in range(PROJ_CHUNK // V7X_LANES):
                slab = s0 // V7X_LANES + j
                u = y[:, j * V7X_LANES:(j + 1) * V7X_LANES]
                o_ref[0, slab] = (u * MLSTM_DH ** -0.5 if slab >= first_k else u).astype(BF16)
        return (lambda: proj(hs[...], w_ref.at[pl.ds(s0, PROJ_CHUNK)])), epilogue

    def gates_item(w_ref, o_ref):
        def epilogue(u):
            o_ref[0] = u + gb_ref[...]
        return (lambda: lax.dot_general(w_ref[...], hb, (((1,), (1,)), ((), ())), preferred_element_type=F32)), epilogue

    def rope_item(w_ref, o_ref, kind):
        def epilogue(u_all):
            n_groups = u_all.shape[1] // V7X_LANES
            n_rot = {"rope_q": n_groups, "rope_kv": KV_W // V7X_LANES, "plain": 0}[kind]
            for j in range(n_groups):
                u = u_all[:, j * V7X_LANES:(j + 1) * V7X_LANES]
                if j < n_rot:
                    u = (u * cos_ref[...]
                         + pltpu.roll(u, V7X_LANES - ROPE_AXIS_PAIRS, 1) * sina_ref[...]
                         + pltpu.roll(u, ROPE_AXIS_PAIRS, 1) * sinb_ref[...])
                    if kind == "rope_q":
                        u = u * (ATTN_DH ** -0.5 * LOG2E)
                o_ref[0, :, pl.ds(j * V7X_LANES, V7X_LANES)] = u.astype(BF16)
        return (lambda: proj(hb, w_ref)), epilogue

    items = []
    for kind, w_ref, o_ref in zip(kinds, w_refs, o_refs):
        if kind in ("conv_qk", "conv_k"):
            width = w_ref.shape[0]
            c0 = cw_ref.shape[1] - width
            items += [conv_item(w_ref, o_ref, s0, c0, width // V7X_LANES - MLSTM_HEADS)
                      for s0 in range(0, width, PROJ_CHUNK)]
        elif kind == "heads":
            pass
        elif kind == "gates":
            items.append(gates_item(w_ref, o_ref))
        else:
            items.append(rope_item(w_ref, o_ref, kind))
    for matmul, epilogue in items:
        epilogue(matmul())


def _inproj(x, gain, mod3, shift_idx, scale_idx, per_batch_mod, rope, gate_b, conv_w, conv_b, w_in, groups, name):
    b, n, d = x.shape
    tm = min(ROW_TILE, n)
    per = tm // HALO
    last = n // HALO - 1
    cos, sina, sinb = rope
    mod_spec = lambda piece: pl.BlockSpec(
        (1, 1, d), (lambda bi, i: (bi, 0, piece)) if per_batch_mod else (lambda bi, i: (0, 0, piece)))
    in_specs = [pl.BlockSpec((1, tm, d), lambda bi, i: (bi, i, 0)),
                pl.BlockSpec((1, HALO, d), lambda bi, i: (bi, jnp.maximum(i * per - 1, 0), 0)),
                pl.BlockSpec((1, HALO, d), lambda bi, i: (bi, jnp.minimum((i + 1) * per, last), 0)),
                pl.BlockSpec((1, d), lambda bi, i: (0, 0)),
                mod_spec(shift_idx), mod_spec(scale_idx),
                pl.BlockSpec((tm, V7X_LANES), lambda bi, i: (i, 0)),
                pl.BlockSpec((tm, V7X_LANES), lambda bi, i: (i, 0)),
                pl.BlockSpec((tm, V7X_LANES), lambda bi, i: (i, 0)),
                pl.BlockSpec((N_GATES, 1), lambda bi, i: (0, 0)),
                pl.BlockSpec(conv_w.shape, lambda bi, i: (0, 0)),
                pl.BlockSpec(conv_b.shape, lambda bi, i: (0, 0)),
                pl.BlockSpec((None,) + w_in.shape[1:], lambda bi, i: (0, 0, 0), pipeline_mode=pl.Buffered(1))]
    out_specs, out_shapes = [], []
    scratch = [((tm + 2 * HALO, d), BF16)]
    blk = (_nbytes((tm, d), F32) + 3 * _nbytes((tm, V7X_LANES), F32) + 4 * _nbytes((tm + 2 * HALO, 2 * MLSTM_W), F32)
           + _nbytes(w_in.shape, F32) // 2)
    for kind, _, width in groups:
        if kind == "gates":
            scratch.append(((width, d), BF16))
            out_specs.append(pl.BlockSpec((1, N_GATES, tm), lambda bi, i: (bi, 0, i)))
            out_shapes.append(jax.ShapeDtypeStruct((b, N_GATES, n), F32))
            continue
        scratch.append(((width, d), BF16))
        if kind in ("heads", "conv_qk", "conv_k"):
            slabs = width // V7X_LANES
            out_specs.append(pl.BlockSpec((1, slabs, tm, V7X_LANES), lambda bi, i: (bi, 0, i, 0)))
            out_shapes.append(jax.ShapeDtypeStruct((b, slabs, n, V7X_LANES), BF16))
        else:
            out_specs.append(pl.BlockSpec((1, tm, width), lambda bi, i: (bi, i, 0)))
            out_shapes.append(jax.ShapeDtypeStruct((b, n, width), BF16))
        blk += _nbytes((tm, width), BF16) + _nbytes((tm, width), F32)
    blk += sum(_nbytes(s, dt) for s, dt in scratch) // 2
    return pl.pallas_call(
        functools.partial(_inproj_body, groups=tuple(groups)),
        grid=(b, n // tm),
        in_specs=in_specs,
        out_specs=out_specs,
        out_shape=out_shapes,
        scratch_shapes=[pltpu.VMEM(s, dt) for s, dt in scratch],
        compiler_params=_params(blk, 2),
        name=name,
    )(x, x, x, gain, mod3, mod3, cos, sina, sinb, gate_b, conv_w, conv_b, w_in)


def _sigmoid(y):
    return 0.5 + 0.5 * jnp.tanh(0.5 * y)


def _log_sigmoid(x):
    return jnp.minimum(x, 0.0) - jnp.log1p(jnp.exp(-jnp.abs(x)))


def _mlstm_body(qs, ks, v_ref, o_ref, kcs, vc_ref, gt_ref, gain_ref, wout_ref, wup_ref, wdn_ref,
                y_ref, wout_bf, wup_bf, wdn_bf,
                gtab, rtab, cumtab, kts, cnl, c0n, cumc, sqk, sds, inter_s, em_s, svq,
                *, n_ctx_chunks, n_lat_chunks):
    t = MCHUNK
    dh = MLSTM_DH
    n_chunks = n_ctx_chunks + n_lat_chunks
    nt = (((1,), (1,)), ((), ()))

    wout_bf[...] = wout_ref[...].astype(BF16)
    wup_bf[...] = wup_ref[...].astype(BF16)
    wdn_bf[...] = wdn_ref[...].astype(BF16)

    tbl = gt_ref[0, 0]
    rid = lax.broadcasted_iota(jnp.int32, tbl.shape, 0) // MAX_CHUNKS
    lane = lax.broadcasted_iota(jnp.int32, tbl.shape, 1)
    lf = _log_sigmoid(tbl)
    pre = jnp.where(rid == 1, lf, 0.0)
    suf = jnp.where(rid == 3, lf, 0.0)
    s = 1
    while s < t:
        pre = pre + jnp.where(lane >= s, pltpu.roll(pre, s, 1), 0.0)
        suf = suf + jnp.where(lane < t - s, pltpu.roll(suf, t - s, 1), 0.0)
        s *= 2
    li = (tbl[0:MAX_CHUNKS] * LOG2E, tbl[2 * MAX_CHUNKS:3 * MAX_CHUNKS] * LOG2E)
    cum = (pre[MAX_CHUNKS:2 * MAX_CHUNKS] * LOG2E, suf[3 * MAX_CHUNKS:4 * MAX_CHUNKS] * LOG2E)
    total = (cum[0][:, t - 1:t], cum[1][:, 0:1])
    mloc = []
    for d in (0, 1):
        g = total[d] - cum[d] + li[d]
        mloc.append(jnp.max(g, axis=1, keepdims=True))
        gtab[d] = g
        rtab[d] = cum[d] - li[d]
        cumtab[d] = cum[d]

    orders = (list(range(n_chunks)),
              list(range(n_ctx_chunks - 1, -1, -1)) + list(range(n_chunks - 1, n_ctx_chunks - 1, -1)))

    a_sc, m0, m1 = ({}, {}), ({}, {}), ({}, {})
    for d in (0, 1):
        m_st = jnp.zeros((1, 1), F32)
        for c in orders[d][:-1]:
            m0[d][c] = m_st
            tot_c = total[d][c:c + 1, :]
            m1[d][c] = jnp.maximum(tot_c + m_st, mloc[d][c:c + 1, :])
            a_sc[d][c] = jnp.exp2(tot_c + m_st - m1[d][c])
            m_st = m1[d][c]
        m0[d][orders[d][-1]] = m_st

    def kv_chunk(c):
        if c < n_ctx_chunks:
            rows = pl.ds(c * t, t)
            return kcs[rows, :], vc_ref[rows, :]
        rows = pl.ds((c - n_ctx_chunks) * t, t)
        return ks[rows, :], v_ref[rows, :]

    ri = lax.broadcasted_iota(jnp.int32, (t, t), 0)
    ci = lax.broadcasted_iota(jnp.int32, (t, t), 1)
    eye = jnp.where(ri == ci, 1.0, 0.0).astype(BF16)
    ones = jnp.ones((t, dh), BF16)

    for c in range(n_chunks):
        kts[c] = lax.dot_general(eye, kv_chunk(c)[0], nt, preferred_element_type=F32)
    for c in range(n_chunks):
        vo = jnp.concatenate([kv_chunk(c)[1], ones], axis=1)
        for d in (0, 1):
            if c != orders[d][-1]:
                w = jnp.exp2(gtab[d, pl.ds(c, 1), :] - m1[d][c])
                cnl[d, c] = jnp.dot((kts[c] * w).astype(BF16), vo, preferred_element_type=F32)

    for d in (0, 1):
        cn = jnp.zeros((dh, 2 * dh), F32)
        for c in orders[d]:
            if c >= n_ctx_chunks:
                c0n[d, c - n_ctx_chunks] = cn.astype(BF16)
            if c != orders[d][-1]:
                cn = a_sc[d][c] * cn + cnl[d, c]

    for j in range(n_lat_chunks):
        c = j + n_ctx_chunks
        rows = pl.ds(j * t, t)
        sqk[rows, :] = lax.dot_general(qs[rows, :], ks[rows, :], nt, preferred_element_type=F32)
        for d in (0, 1):
            cumc[d, rows, :] = jnp.broadcast_to(cumtab[d, pl.ds(c, 1), :], (t, t)).T

    grp = 2
    gi = lax.broadcasted_iota(jnp.int32, (grp * t, t), 0) & (t - 1)
    gs = lax.broadcasted_iota(jnp.int32, (grp * t, t), 1)
    for d in (0, 1):
        ok = (gs <= gi) if d == 0 else (gs >= gi)
        for j0 in range(0, n_lat_chunks, grp):
            rows = pl.ds(j0 * t, grp * t)
            cs = [j + n_ctx_chunks for j in range(j0, j0 + grp)]
            rb = jnp.concatenate([jnp.broadcast_to(rtab[d, pl.ds(c, 1), :], (t, t)) for c in cs], axis=0)
            m0b = jnp.concatenate([jnp.broadcast_to(m0[d][c], (t, t)) for c in cs], axis=0)
            cum_c = cumc[d, rows, :]
            d_log = jnp.where(ok, cum_c - rb, -jnp.inf)
            a_log = cum_c + m0b
            m_t = jnp.maximum(a_log, jnp.max(d_log, axis=1, keepdims=True))
            sds[d, rows, :] = sqk[rows, :].astype(BF16) * jnp.exp2((d_log - m_t).astype(BF16))
            inter_s[d, rows, :] = jnp.exp2(a_log - m_t)
            em_s[d, rows, :] = jnp.exp2(-m_t)

    for j in range(n_lat_chunks):
        rows = pl.ds(j * t, t)
        qf = qs[rows, :].astype(F32)
        vo = jnp.concatenate([v_ref[rows, :], ones], axis=1)
        for d in (0, 1):
            lhs = jnp.concatenate([sds[d, rows, :], (qf * inter_s[d, rows, :]).astype(BF16)], axis=1)
            rhs = jnp.concatenate([vo, c0n[d, j]], axis=0)
            svq[d, rows, :] = jnp.dot(lhs, rhs, preferred_element_type=F32)

    gain = gain_ref[...]
    grp = 4
    for j0 in range(0, n_lat_chunks, grp):
        rows = pl.ds(j0 * t, grp * t)
        hid = None
        for d in (0, 1):
            h_d = svq[d, rows, pl.ds(0, dh)] / jnp.maximum(jnp.abs(svq[d, rows, pl.ds(dh, dh)]), em_s[d, rows, :])
            hid = h_d if hid is None else hid + h_d
        hn = hid * lax.rsqrt(jnp.mean(hid * hid, axis=-1, keepdims=True) + EPS) * gain
        y_ref[rows, :] = (hn * _sigmoid(o_ref[rows, :].astype(F32))).astype(BF16)


def _mlstm(qk, vo, k_ctx, v_ctx, gate_tbl, gain, cast_weights):
    b, _, l, _ = qk.shape
    n_ctx = k_ctx.shape[2]
    dh = MLSTM_DH
    nh = MLSTM_HEADS
    n_ctx_chunks, n_lat_chunks = n_ctx // MCHUNK, l // MCHUNK
    seq = lambda slab0: pl.BlockSpec((None, None, l, dh), lambda bi, h: (bi, slab0 + h, 0, 0))
    ctx = lambda slab0: pl.BlockSpec((None, None, n_ctx, dh), lambda bi, h: (bi, slab0 + h, 0, 0))
    n_chunks = n_ctx_chunks + n_lat_chunks
    scratch = [((2, MAX_CHUNKS, MCHUNK), F32), ((2, MAX_CHUNKS, MCHUNK), F32),
               ((2, MAX_CHUNKS, MCHUNK), F32), ((n_chunks, dh, MCHUNK), F32),
               ((2, n_chunks, dh, 2 * dh), F32), ((2, n_lat_chunks, dh, 2 * dh), BF16),
               ((2, l, MCHUNK), F32), ((l, MCHUNK), F32), ((2, l, MCHUNK), BF16),
               ((2, l, MCHUNK), F32), ((2, l, MCHUNK), F32), ((2, l, 2 * dh), F32)]
    blk = (5 * _nbytes((l, dh), BF16) + 2 * _nbytes((n_ctx, dh), BF16) + _nbytes((V7X_LANES, V7X_LANES), F32)
           + sum(_nbytes(s, dt) for s, dt in scratch) // 2)
    steps = b * nh
    w_in_specs, w_out_specs, w_out_shapes = [], [], []
    for w in cast_weights:
        _, r, c = w.shape
        n_slabs = max(n for n in range(1, steps + 1) if r % n == 0 and (r // n) % (2 * V7X_SUBLANES) == 0)
        slab = lambda bi, h, n_slabs=n_slabs: jnp.minimum(bi * nh + h, n_slabs - 1)
        w_in_specs.append(pl.BlockSpec((None, r // n_slabs, c), lambda bi, h, slab=slab: (0, slab(bi, h), 0)))
        w_out_specs.append(pl.BlockSpec((r // n_slabs, c), lambda bi, h, slab=slab: (slab(bi, h), 0)))
        w_out_shapes.append(jax.ShapeDtypeStruct((r, c), BF16))
        blk += _nbytes((r // n_slabs, c), F32) + _nbytes((r // n_slabs, c), BF16)
    return pl.pallas_call(
        functools.partial(_mlstm_body, n_ctx_chunks=n_ctx_chunks, n_lat_chunks=n_lat_chunks),
        grid=(b, nh),
        in_specs=[seq(0), seq(nh), seq(0), seq(nh), ctx(0), ctx(0),
                  pl.BlockSpec((1, 1, 4 * MAX_CHUNKS, MCHUNK), lambda bi, h: (bi, h, 0, 0)),
                  pl.BlockSpec((1, dh), lambda bi, h: (0, h))] + w_in_specs,
        out_specs=[pl.BlockSpec((None, None, l, dh), lambda bi, h: (bi, h, 0, 0))] + w_out_specs,
        out_shape=[jax.ShapeDtypeStruct((b, nh, l, dh), BF16)] + w_out_shapes,
        scratch_shapes=[pltpu.VMEM(s, dt) for s, dt in scratch],
        compiler_params=_params(blk, 2),
        name="mlstm",
    )(qk, qk, vo, vo, k_ctx, v_ctx, gate_tbl, gain, *cast_weights)


def _attn_body(sink_ref, bias_ref, q_ref, kv_ref, kvc_ref, o_ref, s_scr, p_scr, e_scr, *, seq_len):
    blk = ATTN_BLOCK
    n_blocks = seq_len // blk
    n_ctx = kvc_ref.shape[1]
    rows = GQA_GROUP * blk
    nt = (((1,), (1,)), ((), ()))
    chains = [(qb, g) for qb in range(ATTN_BLOCKS_PER_STEP) for g in range(KV_HEADS)]

    def block_start(qb):
        i = pl.program_id(1) * ATTN_BLOCKS_PER_STEP + qb
        return i, pl.multiple_of(jnp.clip((i - 1) * blk, 0, seq_len - BAND), blk)

    for ci, (qb, g) in enumerate(chains):
        i, start = block_start(qb)
        k_cols = pl.ds(g * ATTN_DH, ATTN_DH)
        q = jnp.concatenate([q_ref[0, pl.ds(qb * blk, blk), pl.ds((g * GQA_GROUP + j) * ATTN_DH, ATTN_DH)]
                             for j in range(GQA_GROUP)], axis=0)
        bias = bias_ref[jnp.where(i == 0, 0, jnp.where(i == n_blocks - 1, 2, 1))]
        s_scr[ci, :, pl.ds(0, BAND)] = (
            lax.dot_general(q, kv_ref[0, pl.ds(start, BAND), k_cols], nt, preferred_element_type=F32) + bias)
        s_scr[ci, :, pl.ds(BAND, n_ctx)] = lax.dot_general(q, kvc_ref[0, :, k_cols], nt, preferred_element_type=F32)

    for ci, (qb, g) in enumerate(chains):
        s = s_scr[ci]
        sink = jnp.concatenate([jnp.full((blk, V7X_LANES), sink_ref[g * GQA_GROUP + j] * LOG2E, F32)
                                for j in range(GQA_GROUP)], axis=0)
        m = jnp.maximum(jnp.broadcast_to(jnp.max(s, axis=1, keepdims=True), (rows, V7X_LANES)), sink)
        p_scr[ci] = jnp.exp2((s - jnp.tile(m, (1, (BAND + n_ctx) // V7X_LANES))).astype(BF16))
        e_scr[ci] = jnp.exp2(sink - m)

    ones_b = jnp.ones((BAND, ATTN_DH), BF16)
    ones_c = jnp.ones((n_ctx, ATTN_DH), BF16)
    for ci, (qb, g) in enumerate(chains):
        _, start = block_start(qb)
        v_cols = pl.ds(KV_W + g * ATTN_DH, ATTN_DH)
        vb = jnp.concatenate([kv_ref[0, pl.ds(start, BAND), v_cols], ones_b], axis=1)
        vc = jnp.concatenate([kvc_ref[0, :, v_cols], ones_c], axis=1)
        acc = (jnp.dot(p_scr[ci, :, pl.ds(0, BAND)], vb, preferred_element_type=F32)
               + jnp.dot(p_scr[ci, :, pl.ds(BAND, n_ctx)], vc, preferred_element_type=F32))
        out = (acc / (pltpu.roll(acc, ATTN_DH, 1) + e_scr[ci]))[:, :ATTN_DH]
        for j in range(GQA_GROUP):
            h = g * GQA_GROUP + j
            o_ref[0, pl.ds(qb * blk, blk), pl.ds(h * ATTN_DH, ATTN_DH)] = out[j * blk:(j + 1) * blk].astype(BF16)


def _band_bias(seq_len):
    row = jnp.arange(ATTN_BLOCK)[:, None]
    col = jnp.arange(BAND)[None, :]
    n_blocks = seq_len // ATTN_BLOCK
    tables = []
    for i in (0, 1, n_blocks - 1):
        start = min(max((i - 1) * ATTN_BLOCK, 0), seq_len - BAND)
        ok = jnp.abs(start + col - (i * ATTN_BLOCK + row)) <= WINDOW
        tables.append(jnp.tile(jnp.where(ok, 0.0, -jnp.inf).astype(F32), (GQA_GROUP, 1)))
    return jnp.stack(tables)


def _attn(sink, q, kv, kvc):
    b, l, _ = q.shape
    n_ctx = kvc.shape[1]
    rows = ATTN_BLOCKS_PER_STEP * ATTN_BLOCK
    bias = _band_bias(l)
    n_chains = KV_HEADS * ATTN_BLOCKS_PER_STEP
    stacked = GQA_GROUP * ATTN_BLOCK
    scratch = [((n_chains, stacked, BAND + n_ctx), F32), ((n_chains, stacked, BAND + n_ctx), BF16),
               ((n_chains, stacked, V7X_LANES), F32)]
    blk = (2 * _nbytes((rows, ATTN_W), BF16) + _nbytes((l, 2 * KV_W), BF16) + _nbytes((n_ctx, 2 * KV_W), BF16)
           + _nbytes(bias.shape, F32) + sum(_nbytes(s, dt) for s, dt in scratch) // 2)
    return pl.pallas_call(
        functools.partial(_attn_body, seq_len=l),
        grid=(b, l // rows),
        in_specs=[pl.BlockSpec(memory_space=pltpu.SMEM),
                  pl.BlockSpec(bias.shape, lambda bi, i: (0, 0, 0)),
                  pl.BlockSpec((1, rows, ATTN_W), lambda bi, i: (bi, i, 0)),
                  pl.BlockSpec((1, l, 2 * KV_W), lambda bi, i: (bi, 0, 0)),
                  pl.BlockSpec((1, n_ctx, 2 * KV_W), lambda bi, i: (bi, 0, 0))],
        out_specs=pl.BlockSpec((1, rows, ATTN_W), lambda bi, i: (bi, i, 0)),
        out_shape=jax.ShapeDtypeStruct((b, l, ATTN_W), BF16),
        scratch_shapes=[pltpu.VMEM(s, dt) for s, dt in scratch],
        compiler_params=_params(blk, 2),
        name="attn",
    )(sink, bias, q, kv, kvc)


GELU_C1 = 0.7978845608028654
GELU_C2 = GELU_C1 * 0.044715


def _twice_gelu(x):
    return x + x * jnp.tanh(x * (GELU_C1 + GELU_C2 * (x * x)))


def _mixffn_body(ym_ref, ymp_ref, ymn_ref, ya_ref, yap_ref, yan_ref, x_ref, xp_ref, xn_ref,
                 wo_ref, ga_ref, gain_ref, sh_ref, sc_ref, gf_ref, wup_ref, cw_ref, cb_ref, wdn_ref, fn_ref,
                 o_ref, ys, hs, x1s, gated):
    i = pl.program_id(1)
    tm = x_ref.shape[1]
    ext = tm + 2 * HALO
    lo, mid, hi = pl.ds(0, HALO), pl.ds(HALO, tm), pl.ds(HALO + tm, HALO)
    for rows, m_ref, a_ref in ((lo, ymp_ref, yap_ref), (mid, ym_ref, ya_ref), (hi, ymn_ref, yan_ref)):
        for h in range(MLSTM_HEADS):
            ys[rows, pl.ds(h * MLSTM_DH, MLSTM_DH)] = m_ref[0, h]
        ys[rows, pl.ds(MLSTM_W, ATTN_W)] = a_ref[0]
    ga = ga_ref[0]
    half = tm // 2

    def mixed(x_rows, proj_rows):
        x1 = x_rows + ga * proj_rows
        return x1, _norm_modulate(x1, gain_ref[...], sh_ref[0], sc_ref[0]).astype(BF16)

    proj = jnp.dot(ys[pl.ds(0, HALO + half), :], wo_ref[...], preferred_element_type=F32)
    x1, h_top = mixed(x_ref[0, pl.ds(0, half), :], proj[HALO:])
    x1s[pl.ds(0, half), :] = x1
    hs[pl.ds(HALO, half), :] = h_top
    h_lo = mixed(xp_ref[0], proj[:HALO])[1]
    hs[lo, :] = jnp.where(i > 0, h_lo, jnp.zeros_like(h_lo))
    proj = jnp.dot(ys[pl.ds(HALO + half, half + HALO), :], wo_ref[...], preferred_element_type=F32)
    x1, h_bot = mixed(x_ref[0, pl.ds(half, half), :], proj[:half])
    x1s[pl.ds(half, half), :] = x1
    hs[pl.ds(HALO + half, half), :] = h_bot
    h_hi = mixed(xn_ref[0], proj[half:])[1]
    hs[hi, :] = jnp.where(i < pl.num_programs(1) - 1, h_hi, jnp.zeros_like(h_hi))
    starts = list(range(0, D_FF, FF_CHUNK))

    def up_a(col0):
        return jnp.dot(hs[...], wup_ref[:, pl.ds(col0, min(FF_CHUNK, D_FF - col0))], preferred_element_type=F32)

    nxt = up_a(starts[0])
    for c, col0 in enumerate(starts):
        width = min(FF_CHUNK, D_FF - col0)
        cols = pl.ds(col0, width)
        a = nxt
        if c + 1 < len(starts):
            nxt = up_a(starts[c + 1])
        a_prev = pltpu.roll(a, 1, 0)[HALO:HALO + tm]
        a_next = pltpu.roll(a, ext - 1, 0)[HALO:HALO + tm]
        conv = (cw_ref[0:1, cols] * a_prev + cw_ref[1:2, cols] * a[HALO:HALO + tm]
                + cw_ref[2:3, cols] * a_next + cb_ref[:, cols])
        g2 = _twice_gelu(conv)
        val = jnp.dot(hs[mid, :], wup_ref[:, pl.ds(D_FF + col0, width)], preferred_element_type=F32)
        gated[:, cols] = (g2 * val).astype(BF16)
    for r0 in (0, half):
        rows = pl.ds(r0, half)
        x2 = x1s[rows, :] + (0.5 * gf_ref[0]) * jnp.dot(gated[rows, :], wdn_ref[...], preferred_element_type=F32)
        ms = jnp.mean(x2 * x2, axis=-1, keepdims=True)
        o_ref[0, rows, :] = x2 * lax.rsqrt(ms + EPS) * fn_ref[...]


def _mixffn(ym, ya, x, mod3, w_out, norm_gain, w_up, conv_w, conv_b, w_down, final_norm):
    b, l, d = x.shape
    tm = ROW_TILE
    per = tm // HALO
    last = l // HALO - 1
    ext = tm + 2 * HALO
    tile, before, after = (lambda bi, i: i), (lambda bi, i: jnp.maximum(i * per - 1, 0)), \
        (lambda bi, i: jnp.minimum((i + 1) * per, last))

    def rows3(make):
        return [make(tm, tile), make(HALO, before), make(HALO, after)]

    ym_spec = lambda n, r: pl.BlockSpec((1, MLSTM_HEADS, n, MLSTM_DH), lambda bi, i: (bi, 0, r(bi, i), 0))
    ya_spec = lambda n, r: pl.BlockSpec((1, n, ATTN_W), lambda bi, i: (bi, r(bi, i), 0))
    x_spec = lambda n, r: pl.BlockSpec((1, n, d), lambda bi, i: (bi, r(bi, i), 0))
    mod_spec = lambda piece: pl.BlockSpec((1, 1, d), lambda bi, i: (bi, 0, piece))
    const = lambda arr: pl.BlockSpec(arr.shape, lambda bi, i: (0,) * arr.ndim)
    resident = lambda arr: pl.BlockSpec(arr.shape, lambda bi, i: (0,) * arr.ndim, pipeline_mode=pl.Buffered(1))
    scratch = [((ext, d), BF16), ((ext, d), BF16), ((tm, d), F32), ((tm, D_FF), BF16)]
    weights = _nbytes(w_out.shape, BF16) + _nbytes(w_up.shape, BF16) + _nbytes(w_down.shape, BF16)
    blk = (2 * _nbytes((tm, d), F32) + 2 * _nbytes((tm, d), BF16) + 4 * _nbytes((ext, FF_CHUNK), F32)
           + (weights + sum(_nbytes(s, dt) for s, dt in scratch)) // 2)
    return pl.pallas_call(
        _mixffn_body,
        grid=(b, l // tm),
        in_specs=(rows3(ym_spec) + rows3(ya_spec) + rows3(x_spec)
                  + [resident(w_out), mod_spec(2), const(norm_gain), mod_spec(3), mod_spec(4), mod_spec(5),
                     resident(w_up), const(conv_w), const(conv_b), resident(w_down), const(final_norm)]),
        out_specs=pl.BlockSpec((1, tm, d), lambda bi, i: (bi, i, 0)),
        out_shape=jax.ShapeDtypeStruct((b, l, d), F32),
        scratch_shapes=[pltpu.VMEM(s, dt) for s, dt in scratch],
        compiler_params=_params(blk, 2),
        name="mixffn",
    )(ym, ym, ym, ya, ya, ya, x, x, x, w_out, mod3, norm_gain, mod3, mod3, mod3,
      w_up, conv_w, conv_b, w_down, final_norm)


def _rope_tables(n_tokens):
    pos = jnp.arange(n_tokens)
    r = (pos // GRID_W).astype(F32)
    c = (pos % GRID_W).astype(F32)
    inv = ROPE_BASE ** (-jnp.arange(ROPE_AXIS_PAIRS, dtype=F32) / ROPE_AXIS_PAIRS)
    ar, ac = r[:, None] * inv, c[:, None] * inv
    zero = jnp.zeros_like(ar)
    cos = jnp.concatenate([jnp.cos(ar), jnp.cos(ar), jnp.cos(ac), jnp.cos(ac)], axis=1)
    sina = jnp.concatenate([-jnp.sin(ar), zero, -jnp.sin(ac), zero], axis=1)
    sinb = jnp.concatenate([zero, jnp.sin(ar), zero, jnp.sin(ac)], axis=1)
    rep = V7X_LANES // ATTN_DH
    return tuple(jnp.tile(tb, (1, rep)) for tb in (cos, sina, sinb))


def _identity_rope(n_tokens):
    return (jnp.ones((n_tokens, V7X_LANES), F32), jnp.zeros((n_tokens, V7X_LANES), F32),
            jnp.zeros((n_tokens, V7X_LANES), F32))


def _gate_table(gates_ctx, gates_x):
    g = jnp.concatenate([gates_ctx, gates_x], axis=2)
    b, _, n = g.shape
    nc = n // MCHUNK
    g = g.reshape(b, 4, MLSTM_HEADS, nc, MCHUNK).transpose(0, 2, 1, 3, 4)
    g = jnp.pad(g, ((0, 0), (0, 0), (0, 0), (0, MAX_CHUNKS - nc), (0, 0)))
    return g.reshape(b, MLSTM_HEADS, 4 * MAX_CHUNKS, MCHUNK)


def kernel(x, c, ctx, c_ctx, w_ada, b_ada, norm_mix, norm_ffn, w_in, gate_b, qk_conv_w, qk_conv_b, mlstm_norm,
           attn_sink, w_out, w_up, ffn_conv_w, ffn_conv_b, w_down, final_norm):
    b, l, d = x.shape
    n_ctx = ctx.shape[1]
    assert w_ada.shape[0] == 1, "single-layer stack"
    assert l % ROW_TILE == 0 and l % MCHUNK == 0 and n_ctx % MCHUNK == 0 and l >= BAND
    assert l % (ATTN_BLOCKS_PER_STEP * ATTN_BLOCK) == 0 and MCHUNK == V7X_LANES
    assert (l + n_ctx) // MCHUNK <= MAX_CHUNKS

    rows = -(-(b + 1) // V7X_SUBLANES) * V7X_SUBLANES
    cvec = jnp.zeros((rows, d), F32).at[:b].set(c).at[b].set(c_ctx)
    mod = _ada(cvec, w_ada[0], b_ada[0][None, :])
    mod_x = mod[:b].reshape(b, 1, 6 * d)
    mod_c = mod[b:b + 1].reshape(1, 1, 6 * d)

    w_in = jnp.swapaxes(w_in, 1, 2)
    o0, o1, o2 = 2 * MLSTM_W, 3 * MLSTM_W, 4 * MLSTM_W
    o3 = o2 + N_GATES
    o4 = o3 + ATTN_W
    gb = gate_b[0][:, None]
    gain_mix = norm_mix[0][None, :]
    conv = (qk_conv_w[0], qk_conv_b[0][None, :])
    k_mc, v_mc, gates_c, kv_ac = _inproj(
        ctx, gain_mix, mod_c, 0, 1, False, _identity_rope(n_ctx), gb, *conv, w_in,
        [("conv_k", MLSTM_W, MLSTM_W), ("heads", o0, MLSTM_W), ("gates", o2, N_GATES), ("plain", o4, 2 * KV_W)],
        "inproj_ctx")
    qk_m, vo_m, gates_x, q_a, kv_a = _inproj(
        x, gain_mix, mod_x, 0, 1, True, _rope_tables(l), gb, *conv, w_in,
        [("conv_qk", 0, o0), ("heads", o0, o0), ("gates", o2, N_GATES), ("rope_q", o3, ATTN_W),
         ("rope_kv", o4, 2 * KV_W)], "inproj_x")

    y_m, w_out_bf, w_up_bf, w_down_bf = _mlstm(qk_m, vo_m, k_mc, v_mc, _gate_table(gates_c, gates_x),
                                               mlstm_norm[0][None, :], [w_out, w_up, w_down])
    y_a = _attn(attn_sink[0], q_a, kv_a, kv_ac)

    return _mixffn(y_m, y_a, x, mod_x, w_out_bf, norm_ffn[0][None, :], w_up_bf,
                   ffn_conv_w[0], ffn_conv_b[0][None, :], w_down_bf, final_norm[None, :])
```

```python
import functools

import jax
import jax.numpy as jnp
from jax import lax
from jax.experimental import pallas as pl
from jax.experimental.pallas import tpu as pltpu

F32 = jnp.float32
BF16 = jnp.bfloat16

D_MODEL = 1024
GRID_W = 64
MLSTM_HEADS = 4
MLSTM_DH = 128
MLSTM_W = MLSTM_HEADS * MLSTM_DH
N_GATES = 4 * MLSTM_HEADS
ATTN_HEADS = 8
KV_HEADS = 2
ATTN_DH = 64
ATTN_W = ATTN_HEADS * ATTN_DH
KV_W = KV_HEADS * ATTN_DH
GQA_GROUP = ATTN_HEADS // KV_HEADS
WINDOW = 128
ROPE_BASE = 10000.0
ROPE_AXIS_PAIRS = ATTN_DH // 4
D_FF = 2816
EPS = 1e-6
LOG2E = 1.4426950408889634

V7X_LANES = 128
V7X_SUBLANES = 8
V7X_VMEM_BYTES = 64 * 1024 * 1024

ROW_TILE = 512
ADA_COL_TILE = 1536
PROJ_CHUNK = 512
MCHUNK = 128
MAX_CHUNKS = 32
ATTN_BLOCK = 128
ATTN_BLOCKS_PER_STEP = 4
BAND = 3 * ATTN_BLOCK
FF_CHUNK = 256
HALO = 2 * V7X_SUBLANES


def _vmem_limit(block_bytes):
    return int(min(V7X_VMEM_BYTES * 7 // 8, 2 * block_bytes + 16 * 1024 * 1024))


def _params(block_bytes, n_axes):
    return pltpu.CompilerParams(dimension_semantics=("arbitrary",) * n_axes,
                                vmem_limit_bytes=_vmem_limit(block_bytes))


def _nbytes(shape, dtype):
    n = 1
    for s in shape:
        n *= s
    return n * jnp.dtype(dtype).itemsize


def _ada_body(c_ref, w_ref, b_ref, o_ref):
    c = c_ref[...]
    s = (c * jax.nn.sigmoid(c)).astype(BF16)
    o_ref[...] = jnp.dot(s, w_ref[...].astype(BF16), preferred_element_type=F32) + b_ref[...]


def _ada(cvec, w_ada, b_ada):
    rows, d = cvec.shape
    n = w_ada.shape[1]
    blk = _nbytes((d, ADA_COL_TILE), F32) + _nbytes((rows, d), F32) + 2 * _nbytes((rows, ADA_COL_TILE), F32)
    return pl.pallas_call(
        _ada_body,
        grid=(n // ADA_COL_TILE,),
        in_specs=[pl.BlockSpec((rows, d), lambda j: (0, 0)),
                  pl.BlockSpec((d, ADA_COL_TILE), lambda j: (0, j)),
                  pl.BlockSpec((1, ADA_COL_TILE), lambda j: (0, j))],
        out_specs=pl.BlockSpec((rows, ADA_COL_TILE), lambda j: (0, j)),
        out_shape=jax.ShapeDtypeStruct((rows, n), F32),
        compiler_params=_params(blk, 1),
        name="ada",
    )(cvec, w_ada, b_ada)


def _norm_modulate(x, gain, shift, scale):
    ms = jnp.mean(x * x, axis=-1, keepdims=True)
    return (x * lax.rsqrt(ms + EPS) * gain) * (1.0 + scale) + shift


def _silu(y):
    h = 0.5 * y
    return h + h * jnp.tanh(h)


def _inproj_body(x_ref, xp_ref, xn_ref, gain_ref, sh_ref, sc_ref, cos_ref, sina_ref, sinb_ref, gb_ref,
                 cw_ref, cb_ref, win_ref, *refs, groups):
    n = len(groups)
    o_refs, hs, w_refs = refs[:n], refs[n], refs[n + 1:]
    kinds = [g[0] for g in groups]

    @pl.when((pl.program_id(0) == 0) & (pl.program_id(1) == 0))
    def _():
        for (_, col0, width), w_bf in zip(groups, w_refs):
            w_bf[...] = win_ref[pl.ds(col0, width), :].astype(BF16)

    proj = lambda lhs, w_bf: lax.dot_general(lhs, w_bf[...], (((1,), (1,)), ((), ())), preferred_element_type=F32)

    i = pl.program_id(1)
    tm = x_ref.shape[1]
    ext = tm + 2 * HALO
    norm = lambda rows: _norm_modulate(rows, gain_ref[...], sh_ref[0], sc_ref[0]).astype(BF16)
    half = tm // 2 if tm >= ROW_TILE else tm

    def heads_half(r0):
        for kind, w_ref, o_ref in zip(kinds, w_refs, o_refs):
            if kind == "heads":
                for s0 in range(0, w_ref.shape[0], PROJ_CHUNK):
                    u = proj(hs[pl.ds(HALO + r0, half), :], w_ref.at[pl.ds(s0, PROJ_CHUNK)])
                    for j in range(PROJ_CHUNK // V7X_LANES):
                        o_ref[0, s0 // V7X_LANES + j, pl.ds(r0, half), :] = (
                            u[:, j * V7X_LANES:(j + 1) * V7X_LANES].astype(BF16))

    hs[pl.ds(HALO, half), :] = norm(x_ref[0, pl.ds(0, half), :])
    heads_half(0)
    if half < tm:
        hs[pl.ds(HALO + half, half), :] = norm(x_ref[0, pl.ds(half, half), :])
    h_lo, h_hi = norm(xp_ref[0]), norm(xn_ref[0])
    hs[pl.ds(0, HALO), :] = jnp.where(i > 0, h_lo, jnp.zeros_like(h_lo))
    hs[pl.ds(HALO + tm, HALO), :] = jnp.where(i < pl.num_programs(1) - 1, h_hi, jnp.zeros_like(h_hi))
    if half < tm:
        heads_half(half)
    hb = hs[pl.ds(HALO, tm), :]
    def conv_item(w_ref, o_ref, s0, c0, first_k):
        def epilogue(a):
            cc = pl.ds(c0 + s0, PROJ_CHUNK)
            a_prev = pltpu.roll(a, 1, 0)[HALO:HALO + tm]
            a_next = pltpu.roll(a, ext - 1, 0)[HALO:HALO + tm]
            y = _silu(cw_ref[0:1, cc] * a_prev + cw_ref[1:2, cc] * a[HALO:HALO + tm]
                      + cw_ref[2:3, cc] * a_next + cb_ref[:, cc])
            for j in range(PROJ_CHUNK // V7X_LANES):
                slab = s0 // V7X_LANES + j
                u = y[:, j * V7X_LANES:(j + 1) * V7X_LANES]
                o_ref[0, slab] = (u * MLSTM_DH ** -0.5 if slab >= first_k else u).astype(BF16)
        return (lambda: proj(hs[...], w_ref.at[pl.ds(s0, PROJ_CHUNK)])), epilogue

    def gates_item(w_ref, o_ref):
        def epilogue(u):
            o_ref[0] = u + gb_ref[...]
        return (lambda: lax.dot_general(w_ref[...], hb, (((1,), (1,)), ((), ())), preferred_element_type=F32)), epilogue

    def rope_item(w_ref, o_ref, kind):
        def epilogue(u_all):
            n_groups = u_all.shape[1] // V7X_LANES
            n_rot = {"rope_q": n_groups, "rope_kv": KV_W // V7X_LANES, "plain": 0}[kind]
            for j in range(n_groups):
                u = u_all[:, j * V7X_LANES:(j + 1) * V7X_LANES]
                if j < n_rot:
                    u = (u * cos_ref[...]
                         + pltpu.roll(u, V7X_LANES - ROPE_AXIS_PAIRS, 1) * sina_ref[...]
                         + pltpu.roll(u, ROPE_AXIS_PAIRS, 1) * sinb_ref[...])
                    if kind == "rope_q":
                        u = u * (ATTN_DH ** -0.5 * LOG2E)
                o_ref[0, :, pl.ds(j * V7X_LANES, V7X_LANES)] = u.astype(BF16)
        return (lambda: proj(hb, w_ref)), epilogue

    items = []
    for kind, w_ref, o_ref in zip(kinds, w_refs, o_refs):
        if kind in ("conv_qk", "conv_k"):
            width = w_ref.shape[0]
            c0 = cw_ref.shape[1] - width
            items += [conv_item(w_ref, o_ref, s0, c0, width // V7X_LANES - MLSTM_HEADS)
                      for s0 in range(0, width, PROJ_CHUNK)]
        elif kind == "heads":
            pass
        elif kind == "gates":
            items.append(gates_item(w_ref, o_ref))
        else:
            items.append(rope_item(w_ref, o_ref, kind))
    for matmul, epilogue in items:
        epilogue(matmul())


def _inproj(x, gain, mod3, shift_idx, scale_idx, per_batch_mod, rope, gate_b, conv_w, conv_b, w_in, groups, name):
    b, n, d = x.shape
    tm = min(ROW_TILE, n)
    per = tm // HALO
    last = n // HALO - 1
    cos, sina, sinb = rope
    mod_spec = lambda piece: pl.BlockSpec(
        (1, 1, d), (lambda bi, i: (bi, 0, piece)) if per_batch_mod else (lambda bi, i: (0, 0, piece)))
    in_specs = [pl.BlockSpec((1, tm, d), lambda bi, i: (bi, i, 0)),
                pl.BlockSpec((1, HALO, d), lambda bi, i: (bi, jnp.maximum(i * per - 1, 0), 0)),
                pl.BlockSpec((1, HALO, d), lambda bi, i: (bi, jnp.minimum((i + 1) * per, last), 0)),
                pl.BlockSpec((1, d), lambda bi, i: (0, 0)),
                mod_spec(shift_idx), mod_spec(scale_idx),
                pl.BlockSpec((tm, V7X_LANES), lambda bi, i: (i, 0)),
                pl.BlockSpec((tm, V7X_LANES), lambda bi, i: (i, 0)),
                pl.BlockSpec((tm, V7X_LANES), lambda bi, i: (i, 0)),
                pl.BlockSpec((N_GATES, 1), lambda bi, i: (0, 0)),
                pl.BlockSpec(conv_w.shape, lambda bi, i: (0, 0)),
                pl.BlockSpec(conv_b.shape, lambda bi, i: (0, 0)),
                pl.BlockSpec((None,) + w_in.shape[1:], lambda bi, i: (0, 0, 0), pipeline_mode=pl.Buffered(1))]
    out_specs, out_shapes = [], []
    scratch = [((tm + 2 * HALO, d), BF16)]
    blk = (_nbytes((tm, d), F32) + 3 * _nbytes((tm, V7X_LANES), F32) + 4 * _nbytes((tm + 2 * HALO, 2 * MLSTM_W), F32)
           + _nbytes(w_in.shape, F32) // 2)
    for kind, _, width in groups:
        if kind == "gates":
            scratch.append(((width, d), BF16))
            out_specs.append(pl.BlockSpec((1, N_GATES, tm), lambda bi, i: (bi, 0, i)))
            out_shapes.append(jax.ShapeDtypeStruct((b, N_GATES, n), F32))
            continue
        scratch.append(((width, d), BF16))
        if kind in ("heads", "conv_qk", "conv_k"):
            slabs = width // V7X_LANES
            out_specs.append(pl.BlockSpec((1, slabs, tm, V7X_LANES), lambda bi, i: (bi, 0, i, 0)))
            out_shapes.append(jax.ShapeDtypeStruct((b, slabs, n, V7X_LANES), BF16))
        else:
            out_specs.append(pl.BlockSpec((1, tm, width), lambda bi, i: (bi, i, 0)))
            out_shapes.append(jax.ShapeDtypeStruct((b, n, width), BF16))
        blk += _nbytes((tm, width), BF16) + _nbytes((tm, width), F32)
    blk += sum(_nbytes(s, dt) for s, dt in scratch) // 2
    return pl.pallas_call(
        functools.partial(_inproj_body, groups=tuple(groups)),
        grid=(b, n // tm),
        in_specs=in_specs,
        out_specs=out_specs,
        out_shape=out_shapes,
        scratch_shapes=[pltpu.VMEM(s, dt) for s, dt in scratch],
        compiler_params=_params(blk, 2),
        name=name,
    )(x, x, x, gain, mod3, mod3, cos, sina, sinb, gate_b, conv_w, conv_b, w_in)


def _sigmoid(y):
    return 0.5 + 0.5 * jnp.tanh(0.5 * y)


def _log_sigmoid(x):
    return jnp.minimum(x, 0.0) - jnp.log1p(jnp.exp(-jnp.abs(x)))


def _mlstm_body(qs, ks, v_ref, o_ref, kcs, vc_ref, gt_ref, gain_ref, wout_ref, wup_ref, wdn_ref,
                y_ref, wout_bf, wup_bf, wdn_bf,
                gtab, rtab, cumtab, kts, cnl, c0n, cumc, sqk, sds, inter_s, em_s, svq,
                *, n_ctx_chunks, n_lat_chunks):
    t = MCHUNK
    dh = MLSTM_DH
    n_chunks = n_ctx_chunks + n_lat_chunks
    nt = (((1,), (1,)), ((), ()))

    wout_bf[...] = wout_ref[...].astype(BF16)
    wup_bf[...] = wup_ref[...].astype(BF16)
    wdn_bf[...] = wdn_ref[...].astype(BF16)

    tbl = gt_ref[0, 0]
    rid = lax.broadcasted_iota(jnp.int32, tbl.shape, 0) // MAX_CHUNKS
    lane = lax.broadcasted_iota(jnp.int32, tbl.shape, 1)
    lf = _log_sigmoid(tbl)
    pre = jnp.where(rid == 1, lf, 0.0)
    suf = jnp.where(rid == 3, lf, 0.0)
    s = 1
    while s < t:
        pre = pre + jnp.where(lane >= s, pltpu.roll(pre, s, 1), 0.0)
        suf = suf + jnp.where(lane < t - s, pltpu.roll(suf, t - s, 1), 0.0)
        s *= 2
    li = (tbl[0:MAX_CHUNKS] * LOG2E, tbl[2 * MAX_CHUNKS:3 * MAX_CHUNKS] * LOG2E)
    cum = (pre[MAX_CHUNKS:2 * MAX_CHUNKS] * LOG2E, suf[3 * MAX_CHUNKS:4 * MAX_CHUNKS] * LOG2E)
    total = (cum[0][:, t - 1:t], cum[1][:, 0:1])
    mloc = []
    for d in (0, 1):
        g = total[d] - cum[d] + li[d]
        mloc.append(jnp.max(g, axis=1, keepdims=True))
        gtab[d] = g
        rtab[d] = cum[d] - li[d]
        cumtab[d] = cum[d]

    orders = (list(range(n_chunks)),
              list(range(n_ctx_chunks - 1, -1, -1)) + list(range(n_chunks - 1, n_ctx_chunks - 1, -1)))

    a_sc, m0, m1 = ({}, {}), ({}, {}), ({}, {})
    for d in (0, 1):
        m_st = jnp.zeros((1, 1), F32)
        for c in orders[d][:-1]:
            m0[d][c] = m_st
            tot_c = total[d][c:c + 1, :]
            m1[d][c] = jnp.maximum(tot_c + m_st, mloc[d][c:c + 1, :])
            a_sc[d][c] = jnp.exp2(tot_c + m_st - m1[d][c])
            m_st = m1[d][c]
        m0[d][orders[d][-1]] = m_st

    def kv_chunk(c):
        if c < n_ctx_chunks:
            rows = pl.ds(c * t, t)
            return kcs[rows, :], vc_ref[rows, :]
        rows = pl.ds((c - n_ctx_chunks) * t, t)
        return ks[rows, :], v_ref[rows, :]

    ri = lax.broadcasted_iota(jnp.int32, (t, t), 0)
    ci = lax.broadcasted_iota(jnp.int32, (t, t), 1)
    eye = jnp.where(ri == ci, 1.0, 0.0).astype(BF16)
    ones = jnp.ones((t, dh), BF16)

    for c in range(n_chunks):
        kts[c] = lax.dot_general(eye, kv_chunk(c)[0], nt, preferred_element_type=F32)
    for c in range(n_chunks):
        vo = jnp.concatenate([kv_chunk(c)[1], ones], axis=1)
        for d in (0, 1):
            if c != orders[d][-1]:
                w = jnp.exp2(gtab[d, pl.ds(c, 1), :] - m1[d][c])
                cnl[d, c] = jnp.dot((kts[c] * w).astype(BF16), vo, preferred_element_type=F32)

    for d in (0, 1):
        cn = jnp.zeros((dh, 2 * dh), F32)
        for c in orders[d]:
            if c >= n_ctx_chunks:
                c0n[d, c - n_ctx_chunks] = cn.astype(BF16)
            if c != orders[d][-1]:
                cn = a_sc[d][c] * cn + cnl[d, c]

    for j in range(n_lat_chunks):
        c = j + n_ctx_chunks
        rows = pl.ds(j * t, t)
        sqk[rows, :] = lax.dot_general(qs[rows, :], ks[rows, :], nt, preferred_element_type=F32)
        for d in (0, 1):
            cumc[d, rows, :] = jnp.broadcast_to(cumtab[d, pl.ds(c, 1), :], (t, t)).T

    grp = 1
    gi = lax.broadcasted_iota(jnp.int32, (grp * t, t), 0) & (t - 1)
    gs = lax.broadcasted_iota(jnp.int32, (grp * t, t), 1)
    for d in (0, 1):
        ok = (gs <= gi) if d == 0 else (gs >= gi)
        for j0 in range(0, n_lat_chunks, grp):
            rows = pl.ds(j0 * t, grp * t)
            cs = [j + n_ctx_chunks for j in range(j0, j0 + grp)]
            rb = jnp.concatenate([jnp.broadcast_to(rtab[d, pl.ds(c, 1), :], (t, t)) for c in cs], axis=0)
            m0b = jnp.concatenate([jnp.broadcast_to(m0[d][c], (t, t)) for c in cs], axis=0)
            cum_c = cumc[d, rows, :]
            d_log = jnp.where(ok, cum_c - rb, -jnp.inf)
            a_log = cum_c + m0b
            m_t = jnp.maximum(a_log, jnp.max(d_log, axis=1, keepdims=True))
            sds[d, rows, :] = sqk[rows, :].astype(BF16) * jnp.exp2((d_log - m_t).astype(BF16))
            inter_s[d, rows, :] = jnp.exp2(a_log - m_t)
            em_s[d, rows, :] = jnp.exp2(-m_t)

    for j in range(n_lat_chunks):
        rows = pl.ds(j * t, t)
        qf = qs[rows, :].astype(F32)
        vo = jnp.concatenate([v_ref[rows, :], ones], axis=1)
        for d in (0, 1):
            lhs = jnp.concatenate([sds[d, rows, :], (qf * inter_s[d, rows, :]).astype(BF16)], axis=1)
            rhs = jnp.concatenate([vo, c0n[d, j]], axis=0)
            svq[d, rows, :] = jnp.dot(lhs, rhs, preferred_element_type=F32)

    gain = gain_ref[...]
    grp = 2
    for j0 in range(0, n_lat_chunks, grp):
        rows = pl.ds(j0 * t, grp * t)
        hid = None
        for d in (0, 1):
            h_d = svq[d, rows, pl.ds(0, dh)] / jnp.maximum(jnp.abs(svq[d, rows, pl.ds(dh, dh)]), em_s[d, rows, :])
            hid = h_d if hid is None else hid + h_d
        hn = hid * lax.rsqrt(jnp.mean(hid * hid, axis=-1, keepdims=True) + EPS) * gain
        y_ref[rows, :] = (hn * _sigmoid(o_ref[rows, :].astype(F32))).astype(BF16)


def _mlstm(qk, vo, k_ctx, v_ctx, gate_tbl, gain, cast_weights):
    b, _, l, _ = qk.shape
    n_ctx = k_ctx.shape[2]
    dh = MLSTM_DH
    nh = MLSTM_HEADS
    n_ctx_chunks, n_lat_chunks = n_ctx // MCHUNK, l // MCHUNK
    seq = lambda slab0: pl.BlockSpec((None, None, l, dh), lambda bi, h: (bi, slab0 + h, 0, 0))
    ctx = lambda slab0: pl.BlockSpec((None, None, n_ctx, dh), lambda bi, h: (bi, slab0 + h, 0, 0))
    n_chunks = n_ctx_chunks + n_lat_chunks
    scratch = [((2, MAX_CHUNKS, MCHUNK), F32), ((2, MAX_CHUNKS, MCHUNK), F32),
               ((2, MAX_CHUNKS, MCHUNK), F32), ((n_chunks, dh, MCHUNK), F32),
               ((2, n_chunks, dh, 2 * dh), F32), ((2, n_lat_chunks, dh, 2 * dh), BF16),
               ((2, l, MCHUNK), F32), ((l, MCHUNK), F32), ((2, l, MCHUNK), BF16),
               ((2, l, MCHUNK), F32), ((2, l, MCHUNK), F32), ((2, l, 2 * dh), F32)]
    blk = (5 * _nbytes((l, dh), BF16) + 2 * _nbytes((n_ctx, dh), BF16) + _nbytes((V7X_LANES, V7X_LANES), F32)
           + sum(_nbytes(s, dt) for s, dt in scratch) // 2)
    steps = b * nh
    w_in_specs, w_out_specs, w_out_shapes = [], [], []
    for w in cast_weights:
        _, r, c = w.shape
        n_slabs = max(n for n in range(1, steps + 1) if r % n == 0 and (r // n) % (2 * V7X_SUBLANES) == 0)
        slab = lambda bi, h, n_slabs=n_slabs: jnp.minimum(bi * nh + h, n_slabs - 1)
        w_in_specs.append(pl.BlockSpec((None, r // n_slabs, c), lambda bi, h, slab=slab: (0, slab(bi, h), 0)))
        w_out_specs.append(pl.BlockSpec((r // n_slabs, c), lambda bi, h, slab=slab: (slab(bi, h), 0)))
        w_out_shapes.append(jax.ShapeDtypeStruct((r, c), BF16))
        blk += _nbytes((r // n_slabs, c), F32) + _nbytes((r // n_slabs, c), BF16)
    return pl.pallas_call(
        functools.partial(_mlstm_body, n_ctx_chunks=n_ctx_chunks, n_lat_chunks=n_lat_chunks),
        grid=(b, nh),
        in_specs=[seq(0), seq(nh), seq(0), seq(nh), ctx(0), ctx(0),
                  pl.BlockSpec((1, 1, 4 * MAX_CHUNKS, MCHUNK), lambda bi, h: (bi, h, 0, 0)),
                  pl.BlockSpec((1, dh), lambda bi, h: (0, h))] + w_in_specs,
        out_specs=[pl.BlockSpec((None, None, l, dh), lambda bi, h: (bi, h, 0, 0))] + w_out_specs,
        out_shape=[jax.ShapeDtypeStruct((b, nh, l, dh), BF16)] + w_out_shapes,
        scratch_shapes=[pltpu.VMEM(s, dt) for s, dt in scratch],
        compiler_params=_params(blk, 2),
        name="mlstm",
    )(qk, qk, vo, vo, k_ctx, v_ctx, gate_tbl, gain, *cast_weights)


def _attn_body(sink_ref, bias_ref, q_ref, kv_ref, kvc_ref, o_ref, s_scr, p_scr, e_scr, *, seq_len):
    blk = ATTN_BLOCK
    n_blocks = seq_len // blk
    n_ctx = kvc_ref.shape[1]
    rows = GQA_GROUP * blk
    nt = (((1,), (1,)), ((), ()))
    chains = [(qb, g) for qb in range(ATTN_BLOCKS_PER_STEP) for g in range(KV_HEADS)]

    def block_start(qb):
        i = pl.program_id(1) * ATTN_BLOCKS_PER_STEP + qb
        return i, pl.multiple_of(jnp.clip((i - 1) * blk, 0, seq_len - BAND), blk)

    for ci, (qb, g) in enumerate(chains):
        i, start = block_start(qb)
        k_cols = pl.ds(g * ATTN_DH, ATTN_DH)
        q = jnp.concatenate([q_ref[0, pl.ds(qb * blk, blk), pl.ds((g * GQA_GROUP + j) * ATTN_DH, ATTN_DH)]
                             for j in range(GQA_GROUP)], axis=0)
        bias = bias_ref[jnp.where(i == 0, 0, jnp.where(i == n_blocks - 1, 2, 1))]
        s_scr[ci, :, pl.ds(0, BAND)] = (
            lax.dot_general(q, kv_ref[0, pl.ds(start, BAND), k_cols], nt, preferred_element_type=F32) + bias)
        s_scr[ci, :, pl.ds(BAND, n_ctx)] = lax.dot_general(q, kvc_ref[0, :, k_cols], nt, preferred_element_type=F32)

    for ci, (qb, g) in enumerate(chains):
        s = s_scr[ci]
        sink = jnp.concatenate([jnp.full((blk, V7X_LANES), sink_ref[g * GQA_GROUP + j] * LOG2E, F32)
                                for j in range(GQA_GROUP)], axis=0)
        m = jnp.maximum(jnp.broadcast_to(jnp.max(s, axis=1, keepdims=True), (rows, V7X_LANES)), sink)
        p_scr[ci] = jnp.exp2((s - jnp.tile(m, (1, (BAND + n_ctx) // V7X_LANES))).astype(BF16))
        e_scr[ci] = jnp.exp2(sink - m)

    ones_b = jnp.ones((BAND, ATTN_DH), BF16)
    ones_c = jnp.ones((n_ctx, ATTN_DH), BF16)
    for ci, (qb, g) in enumerate(chains):
        _, start = block_start(qb)
        v_cols = pl.ds(KV_W + g * ATTN_DH, ATTN_DH)
        vb = jnp.concatenate([kv_ref[0, pl.ds(start, BAND), v_cols], ones_b], axis=1)
        vc = jnp.concatenate([kvc_ref[0, :, v_cols], ones_c], axis=1)
        acc = (jnp.dot(p_scr[ci, :, pl.ds(0, BAND)], vb, preferred_element_type=F32)
               + jnp.dot(p_scr[ci, :, pl.ds(BAND, n_ctx)], vc, preferred_element_type=F32))
        out = (acc / (pltpu.roll(acc, ATTN_DH, 1) + e_scr[ci]))[:, :ATTN_DH]
        for j in range(GQA_GROUP):
            h = g * GQA_GROUP + j
            o_ref[0, pl.ds(qb * blk, blk), pl.ds(h * ATTN_DH, ATTN_DH)] = out[j * blk:(j + 1) * blk].astype(BF16)


def _band_bias(seq_len):
    row = jnp.arange(ATTN_BLOCK)[:, None]
    col = jnp.arange(BAND)[None, :]
    n_blocks = seq_len // ATTN_BLOCK
    tables = []
    for i in (0, 1, n_blocks - 1):
        start = min(max((i - 1) * ATTN_BLOCK, 0), seq_len - BAND)
        ok = jnp.abs(start + col - (i * ATTN_BLOCK + row)) <= WINDOW
        tables.append(jnp.tile(jnp.where(ok, 0.0, -jnp.inf).astype(F32), (GQA_GROUP, 1)))
    return jnp.stack(tables)


def _attn(sink, q, kv, kvc):
    b, l, _ = q.shape
    n_ctx = kvc.shape[1]
    rows = ATTN_BLOCKS_PER_STEP * ATTN_BLOCK
    bias = _band_bias(l)
    n_chains = KV_HEADS * ATTN_BLOCKS_PER_STEP
    stacked = GQA_GROUP * ATTN_BLOCK
    scratch = [((n_chains, stacked, BAND + n_ctx), F32), ((n_chains, stacked, BAND + n_ctx), BF16),
               ((n_chains, stacked, V7X_LANES), F32)]
    blk = (2 * _nbytes((rows, ATTN_W), BF16) + _nbytes((l, 2 * KV_W), BF16) + _nbytes((n_ctx, 2 * KV_W), BF16)
           + _nbytes(bias.shape, F32) + sum(_nbytes(s, dt) for s, dt in scratch) // 2)
    return pl.pallas_call(
        functools.partial(_attn_body, seq_len=l),
        grid=(b, l // rows),
        in_specs=[pl.BlockSpec(memory_space=pltpu.SMEM),
                  pl.BlockSpec(bias.shape, lambda bi, i: (0, 0, 0)),
                  pl.BlockSpec((1, rows, ATTN_W), lambda bi, i: (bi, i, 0)),
                  pl.BlockSpec((1, l, 2 * KV_W), lambda bi, i: (bi, 0, 0)),
                  pl.BlockSpec((1, n_ctx, 2 * KV_W), lambda bi, i: (bi, 0, 0))],
        out_specs=pl.BlockSpec((1, rows, ATTN_W), lambda bi, i: (bi, i, 0)),
        out_shape=jax.ShapeDtypeStruct((b, l, ATTN_W), BF16),
        scratch_shapes=[pltpu.VMEM(s, dt) for s, dt in scratch],
        compiler_params=_params(blk, 2),
        name="attn",
    )(sink, bias, q, kv, kvc)


GELU_C1 = 0.7978845608028654
GELU_C2 = GELU_C1 * 0.044715


def _twice_gelu(x):
    return x + x * jnp.tanh(x * (GELU_C1 + GELU_C2 * (x * x)))


def _mixffn_body(ym_ref, ymp_ref, ymn_ref, ya_ref, yap_ref, yan_ref, x_ref, xp_ref, xn_ref,
                 wo_ref, ga_ref, gain_ref, sh_ref, sc_ref, gf_ref, wup_ref, cw_ref, cb_ref, wdn_ref, fn_ref,
                 o_ref, ys, hs, x1s, gated):
    i = pl.program_id(1)
    tm = x_ref.shape[1]
    ext = tm + 2 * HALO
    lo, mid, hi = pl.ds(0, HALO), pl.ds(HALO, tm), pl.ds(HALO + tm, HALO)
    for rows, m_ref, a_ref in ((lo, ymp_ref, yap_ref), (mid, ym_ref, ya_ref), (hi, ymn_ref, yan_ref)):
        for h in range(MLSTM_HEADS):
            ys[rows, pl.ds(h * MLSTM_DH, MLSTM_DH)] = m_ref[0, h]
        ys[rows, pl.ds(MLSTM_W, ATTN_W)] = a_ref[0]
    ga = ga_ref[0]
    half = tm // 2

    def mixed(x_rows, proj_rows):
        x1 = x_rows + ga * proj_rows
        return x1, _norm_modulate(x1, gain_ref[...], sh_ref[0], sc_ref[0]).astype(BF16)

    proj = jnp.dot(ys[pl.ds(0, HALO + half), :], wo_ref[...], preferred_element_type=F32)
    x1, h_top = mixed(x_ref[0, pl.ds(0, half), :], proj[HALO:])
    x1s[pl.ds(0, half), :] = x1
    hs[pl.ds(HALO, half), :] = h_top
    h_lo = mixed(xp_ref[0], proj[:HALO])[1]
    hs[lo, :] = jnp.where(i > 0, h_lo, jnp.zeros_like(h_lo))
    proj = jnp.dot(ys[pl.ds(HALO + half, half + HALO), :], wo_ref[...], preferred_element_type=F32)
    x1, h_bot = mixed(x_ref[0, pl.ds(half, half), :], proj[:half])
    x1s[pl.ds(half, half), :] = x1
    hs[pl.ds(HALO + half, half), :] = h_bot
    h_hi = mixed(xn_ref[0], proj[half:])[1]
    hs[hi, :] = jnp.where(i < pl.num_programs(1) - 1, h_hi, jnp.zeros_like(h_hi))
    starts = list(range(0, D_FF, FF_CHUNK))

    def up_a(col0):
        return jnp.dot(hs[...], wup_ref[:, pl.ds(col0, min(FF_CHUNK, D_FF - col0))], preferred_element_type=F32)

    nxt = up_a(starts[0])
    for c, col0 in enumerate(starts):
        width = min(FF_CHUNK, D_FF - col0)
        cols = pl.ds(col0, width)
        a = nxt
        if c + 1 < len(starts):
            nxt = up_a(starts[c + 1])
        a_prev = pltpu.roll(a, 1, 0)[HALO:HALO + tm]
        a_next = pltpu.roll(a, ext - 1, 0)[HALO:HALO + tm]
        conv = (cw_ref[0:1, cols] * a_prev + cw_ref[1:2, cols] * a[HALO:HALO + tm]
                + cw_ref[2:3, cols] * a_next + cb_ref[:, cols])
        g2 = _twice_gelu(conv)
        val = jnp.dot(hs[mid, :], wup_ref[:, pl.ds(D_FF + col0, width)], preferred_element_type=F32)
        gated[:, cols] = (g2 * val).astype(BF16)
    for r0 in (0, half):
        rows = pl.ds(r0, half)
        x2 = x1s[rows, :] + (0.5 * gf_ref[0]) * jnp.dot(gated[rows, :], wdn_ref[...], preferred_element_type=F32)
        ms = jnp.mean(x2 * x2, axis=-1, keepdims=True)
        o_ref[0, rows, :] = x2 * lax.rsqrt(ms + EPS) * fn_ref[...]


def _mixffn(ym, ya, x, mod3, w_out, norm_gain, w_up, conv_w, conv_b, w_down, final_norm):
    b, l, d = x.shape
    tm = ROW_TILE
    per = tm // HALO
    last = l // HALO - 1
    ext = tm + 2 * HALO
    tile, before, after = (lambda bi, i: i), (lambda bi, i: jnp.maximum(i * per - 1, 0)), \
        (lambda bi, i: jnp.minimum((i + 1) * per, last))

    def rows3(make):
        return [make(tm, tile), make(HALO, before), make(HALO, after)]

    ym_spec = lambda n, r: pl.BlockSpec((1, MLSTM_HEADS, n, MLSTM_DH), lambda bi, i: (bi, 0, r(bi, i), 0))
    ya_spec = lambda n, r: pl.BlockSpec((1, n, ATTN_W), lambda bi, i: (bi, r(bi, i), 0))
    x_spec = lambda n, r: pl.BlockSpec((1, n, d), lambda bi, i: (bi, r(bi, i), 0))
    mod_spec = lambda piece: pl.BlockSpec((1, 1, d), lambda bi, i: (bi, 0, piece))
    const = lambda arr: pl.BlockSpec(arr.shape, lambda bi, i: (0,) * arr.ndim)
    resident = lambda arr: pl.BlockSpec(arr.shape, lambda bi, i: (0,) * arr.ndim, pipeline_mode=pl.Buffered(1))
    scratch = [((ext, d), BF16), ((ext, d), BF16), ((tm, d), F32), ((tm, D_FF), BF16)]
    weights = _nbytes(w_out.shape, BF16) + _nbytes(w_up.shape, BF16) + _nbytes(w_down.shape, BF16)
    blk = (2 * _nbytes((tm, d), F32) + 2 * _nbytes((tm, d), BF16) + 4 * _nbytes((ext, FF_CHUNK), F32)
           + (weights + sum(_nbytes(s, dt) for s, dt in scratch)) // 2)
    return pl.pallas_call(
        _mixffn_body,
        grid=(b, l // tm),
        in_specs=(rows3(ym_spec) + rows3(ya_spec) + rows3(x_spec)
                  + [resident(w_out), mod_spec(2), const(norm_gain), mod_spec(3), mod_spec(4), mod_spec(5),
                     resident(w_up), const(conv_w), const(conv_b), resident(w_down), const(final_norm)]),
        out_specs=pl.BlockSpec((1, tm, d), lambda bi, i: (bi, i, 0)),
        out_shape=jax.ShapeDtypeStruct((b, l, d), F32),
        scratch_shapes=[pltpu.VMEM(s, dt) for s, dt in scratch],
        compiler_params=_params(blk, 2),
        name="mixffn",
    )(ym, ym, ym, ya, ya, ya, x, x, x, w_out, mod3, norm_gain, mod3, mod3, mod3,
      w_up, conv_w, conv_b, w_down, final_norm)


def _rope_tables(n_tokens):
    pos = jnp.arange(n_tokens)
    r = (pos // GRID_W).astype(F32)
    c = (pos % GRID_W).astype(F32)
    inv = ROPE_BASE ** (-jnp.arange(ROPE_AXIS_PAIRS, dtype=F32) / ROPE_AXIS_PAIRS)
    ar, ac = r[:, None] * inv, c[:, None] * inv
    zero = jnp.zeros_like(ar)
    cos = jnp.concatenate([jnp.cos(ar), jnp.cos(ar), jnp.cos(ac), jnp.cos(ac)], axis=1)
    sina = jnp.concatenate([-jnp.sin(ar), zero, -jnp.sin(ac), zero], axis=1)
    sinb = jnp.concatenate([zero, jnp.sin(ar), zero, jnp.sin(ac)], axis=1)
    rep = V7X_LANES // ATTN_DH
    return tuple(jnp.tile(tb, (1, rep)) for tb in (cos, sina, sinb))


def _identity_rope(n_tokens):
    return (jnp.ones((n_tokens, V7X_LANES), F32), jnp.zeros((n_tokens, V7X_LANES), F32),
            jnp.zeros((n_tokens, V7X_LANES), F32))


def _gate_table(gates_ctx, gates_x):
    g = jnp.concatenate([gates_ctx, gates_x], axis=2)
    b, _, n = g.shape
    nc = n // MCHUNK
    g = g.reshape(b, 4, MLSTM_HEADS, nc, MCHUNK).transpose(0, 2, 1, 3, 4)
    g = jnp.pad(g, ((0, 0), (0, 0), (0, 0), (0, MAX_CHUNKS - nc), (0, 0)))
    return g.reshape(b, MLSTM_HEADS, 4 * MAX_CHUNKS, MCHUNK)


def kernel(x, c, ctx, c_ctx, w_ada, b_ada, norm_mix, norm_ffn, w_in, gate_b, qk_conv_w, qk_conv_b, mlstm_norm,
           attn_sink, w_out, w_up, ffn_conv_w, ffn_conv_b, w_down, final_norm):
    b, l, d = x.shape
    n_ctx = ctx.shape[1]
    assert w_ada.shape[0] == 1, "single-layer stack"
    assert l % ROW_TILE == 0 and l % MCHUNK == 0 and n_ctx % MCHUNK == 0 and l >= BAND
    assert l % (ATTN_BLOCKS_PER_STEP * ATTN_BLOCK) == 0 and MCHUNK == V7X_LANES
    assert (l + n_ctx) // MCHUNK <= MAX_CHUNKS

    rows = -(-(b + 1) // V7X_SUBLANES) * V7X_SUBLANES
    cvec = jnp.zeros((rows, d), F32).at[:b].set(c).at[b].set(c_ctx)
    mod = _ada(cvec, w_ada[0], b_ada[0][None, :])
    mod_x = mod[:b].reshape(b, 1, 6 * d)
    mod_c = mod[b:b + 1].reshape(1, 1, 6 * d)

    w_in = jnp.swapaxes(w_in, 1, 2)
    o0, o1, o2 = 2 * MLSTM_W, 3 * MLSTM_W, 4 * MLSTM_W
    o3 = o2 + N_GATES
    o4 = o3 + ATTN_W
    gb = gate_b[0][:, None]
    gain_mix = norm_mix[0][None, :]
    conv = (qk_conv_w[0], qk_conv_b[0][None, :])
    k_mc, v_mc, gates_c, kv_ac = _inproj(
        ctx, gain_mix, mod_c, 0, 1, False, _identity_rope(n_ctx), gb, *conv, w_in,
        [("conv_k", MLSTM_W, MLSTM_W), ("heads", o0, MLSTM_W), ("gates", o2, N_GATES), ("plain", o4, 2 * KV_W)],
        "inproj_ctx")
    qk_m, vo_m, gates_x, q_a, kv_a = _inproj(
        x, gain_mix, mod_x, 0, 1, True, _rope_tables(l), gb, *conv, w_in,
        [("conv_qk", 0, o0), ("heads", o0, o0), ("gates", o2, N_GATES), ("rope_q", o3, ATTN_W),
         ("rope_kv", o4, 2 * KV_W)], "inproj_x")

    y_m, w_out_bf, w_up_bf, w_down_bf = _mlstm(qk_m, vo_m, k_mc, v_mc, _gate_table(gates_c, gates_x),
                                               mlstm_norm[0][None, :], [w_out, w_up, w_down])
    y_a = _attn(attn_sink[0], q_a, kv_a, kv_ac)

    return _mixffn(y_m, y_a, x, mod_x, w_out_bf, norm_ffn[0][None, :], w_up_bf,
                   ffn_conv_w[0], ffn_conv_b[0][None, :], w_down_bf, final_norm[None, :])
```
